```python
import jax, jax.numpy as jnp
from jax import lax
import numpy as np

D_MODEL = 1024
BATCH = 8
SEQ = 2048
DEPTH = 1
DEC_BATCH = 32
DEC_SEQ = 4
PAST_LEN = 16384
PAGE_SIZE = 128

A_WIDTH = D_MODEL // 2
A_GROUPS = 4
A_GROUP_DIM = A_WIDTH // A_GROUPS
CHUNK = 128
B_HEADS = 8
HEAD_DIM = 64
B_WIDTH = B_HEADS * HEAD_DIM
KV_HEADS = 2
GQA = B_HEADS // KV_HEADS
BLOCK = 64
N_SELECT = 16
WINDOW = 512
CMP_HIDDEN = HEAD_DIM
N_KV_SLOTS = 6
N_PAGED_SLOTS = 4
QBLOCK = 128
D_FF = ((8 * D_MODEL // 3) + 127) // 128 * 128
CONV_W = 3

EPS = 1e-6
NEG = -1e30
FORCE_SCORE = 1e4
PAD_POS = -(2 ** 30)
KV_COLS = N_KV_SLOTS * KV_HEADS * HEAD_DIM
IN_COLS = 2 * A_WIDTH + B_WIDTH + KV_COLS + 3 * B_HEADS + 2 * D_MODEL
SPLIT_IDX = (2 * A_WIDTH, 2 * A_WIDTH + B_WIDTH, 2 * A_WIDTH + B_WIDTH + KV_COLS,
             2 * A_WIDTH + B_WIDTH + KV_COLS + 3 * B_HEADS)

kernel_name = "hybrid_gmlp_nsa_convffn_adaln_step"


def rmsnorm(x, g):
    xf = x.astype(jnp.float32)
    y = xf * lax.rsqrt(jnp.mean(xf * xf, -1, keepdims=True) + EPS)
    return (y * g.astype(jnp.float32)).astype(x.dtype)


def layernorm(x, g, b):
    xf = x.astype(jnp.float32)
    xc = xf - jnp.mean(xf, -1, keepdims=True)
    y = xc * lax.rsqrt(jnp.mean(xc * xc, -1, keepdims=True) + EPS)
    return (y * g.astype(jnp.float32) + b.astype(jnp.float32)).astype(x.dtype)


def masked_softmax(s, mask):
    s = jnp.where(mask, s.astype(jnp.float32), NEG)
    m = jnp.max(s, -1, keepdims=True)
    e = jnp.where(mask, jnp.exp(s - m), 0.0)
    return e / jnp.maximum(jnp.sum(e, -1, keepdims=True), 1e-30)


def chunk_spatial_gating(u, vn, w_s, b_s):
    B, T, _ = u.shape
    tc = CHUNK if T % CHUNK == 0 else T
    nc = T // tc
    w = jnp.tril(w_s[:, :tc, :tc])
    vr = vn.reshape(B, nc, tc, A_GROUPS, A_GROUP_DIM)
    s = jnp.einsum('hts,bcshd->bcthd', w, vr) + b_s[:, :tc].T[None, None, :, :, None]
    return u * s.reshape(B, T, A_WIDTH)


def compress_blocks(rows, pe, w1, b1, w2, b2):
    x = rows + pe[None, None, :, None, :]
    h = jax.nn.gelu(jnp.einsum('bnikd,idh->bnkh', x, w1) + b1)
    return jnp.einsum('bnkh,hd->bnkd', h, w2) + b2


def nsa_attention(q, kv_rows, win_rows, win_pos, q_pos, gate_logits, cmp_pe, cmp_w1, cmp_b1, cmp_w2, cmp_b2):
    B, T = q.shape[:2]
    L = kv_rows.shape[1]
    nb = -(-L // BLOCK)
    rows = jnp.pad(kv_rows, ((0, 0), (0, nb * BLOCK - L), (0, 0), (0, 0), (0, 0)))
    rows = rows.reshape(B, nb, BLOCK, N_PAGED_SLOTS, KV_HEADS, HEAD_DIM)
    qg = q.reshape(B, T, KV_HEADS, GQA, HEAD_DIM) * (HEAD_DIM ** -0.5)
    blk = jnp.arange(nb, dtype=jnp.int32)

    kc = compress_blocks(rows[:, :, :, 0], cmp_pe[0], cmp_w1[0], cmp_b1[0], cmp_w2[0], cmp_b2[0])
    vc = compress_blocks(rows[:, :, :, 1], cmp_pe[1], cmp_w1[1], cmp_b1[1], cmp_w2[1], cmp_b2[1])
    avail = (blk[None, :] + 1) * BLOCK <= q_pos[:, None] + 1
    p_cmp = masked_softmax(jnp.einsum('btkgd,bnkd->btkgn', qg, kc), avail[None, :, None, None, :])
    o_cmp = jnp.einsum('btkgn,bnkd->btkgd', p_cmp.astype(vc.dtype), vc)

    cur = (q_pos // BLOCK)[:, None]
    imp = jnp.sum(p_cmp, axis=3)
    forced = (blk[None, :] == 0) | (blk[None, :] == cur) | (blk[None, :] == cur - 1)
    future = blk[None, :] > cur
    imp = jnp.where(forced[None, :, None, :], FORCE_SCORE, jnp.where(future[None, :, None, :], -1.0, imp))
    n_sel = min(N_SELECT, nb)
    _, sel_idx = lax.top_k(imp, n_sel)
    ks = jnp.moveaxis(rows[:, :, :, 2], 3, 1)
    vs = jnp.moveaxis(rows[:, :, :, 3], 3, 1)
    qb = QBLOCK if T % QBLOCK == 0 else T
    nq = T // qb
    head_ix = jnp.arange(KV_HEADS)[None, :, None]
    offs = jnp.arange(BLOCK, dtype=jnp.int32)

    def sel_block(args):
        qi, ii, pi, bi = args
        kg = ks[bi][head_ix, ii]
        vg = vs[bi][head_ix, ii]
        kpos = ii[..., None] * BLOCK + offs
        mask = (kpos <= pi[:, None, None, None]).reshape(qb, KV_HEADS, 1, n_sel * BLOCK)
        s = jnp.einsum('qkgd,qksid->qkgsi', qi, kg).reshape(qb, KV_HEADS, GQA, n_sel * BLOCK)
        p = masked_softmax(s, mask).reshape(qb, KV_HEADS, GQA, n_sel, BLOCK)
        return jnp.einsum('qkgsi,qksid->qkgd', p.astype(vg.dtype), vg)

    o_sel = lax.map(sel_block, (qg.reshape(B * nq, qb, KV_HEADS, GQA, HEAD_DIM),
                                sel_idx.reshape(B * nq, qb, KV_HEADS, n_sel),
                                jnp.tile(q_pos.reshape(nq, qb), (B, 1)),
                                jnp.repeat(jnp.arange(B, dtype=jnp.int32), nq)))
    o_sel = o_sel.reshape(B, T, KV_HEADS, GQA, HEAD_DIM)

    lw = win_rows.shape[1]
    q_off = lw - T
    span = WINDOW + qb
    wpad = jnp.pad(win_rows, ((0, 0), (WINDOW, 0), (0, 0), (0, 0), (0, 0)))
    ppad = jnp.concatenate([jnp.full((WINDOW,), PAD_POS, jnp.int32), win_pos])
    kidx = q_off + qb * jnp.arange(nq)[:, None] + jnp.arange(span)[None, :]
    kw = wpad[:, kidx]
    kpos = ppad[kidx]
    diff = q_pos.reshape(nq, qb)[:, :, None] - kpos[:, None, :]
    wmask = (diff >= 0) & (diff < WINDOW)
    s = jnp.einsum('bnqkgd,bnjkd->bnqkgj', qg.reshape(B, nq, qb, KV_HEADS, GQA, HEAD_DIM), kw[:, :, :, 0])
    p = masked_softmax(s, wmask[None, :, :, None, None, :])
    o_win = jnp.einsum('bnqkgj,bnjkd->bnqkgd', p.astype(kw.dtype), kw[:, :, :, 1]).reshape(B, T, KV_HEADS, GQA, HEAD_DIM)

    g = jax.nn.sigmoid(gate_logits.astype(jnp.float32)).reshape(B, T, 3, KV_HEADS, GQA, 1).astype(q.dtype)
    o = g[:, :, 0] * o_cmp + g[:, :, 1] * o_sel + g[:, :, 2] * o_win
    return o.reshape(B, T, B_WIDTH)


def decoder_layer(x, c, q_pos, past_rows, win_prev, win_prev_pos, conv_prev, params):
    (w_ada, b_ada, g_norm1, w_in, ln_v_g, ln_v_b, w_spatial, b_spatial,
     cmp_pe, cmp_w1, cmp_b1, cmp_w2, cmp_b2, w_branch_a, w_branch_b, w_out,
     g_norm2, w_up, w_conv, b_conv, w_down) = params
    B, T, _ = x.shape
    mod = jax.nn.silu(c) @ w_ada + b_ada
    shift1, scale1, gate1, shift2, scale2, gate2 = jnp.split(mod[:, None, :], 6, axis=-1)

    h = rmsnorm(x, g_norm1) * (1 + scale1) + shift1
    za, zq, zkv, zg, zm = jnp.split(h @ w_in, SPLIT_IDX, axis=-1)

    u, v = jnp.split(jax.nn.gelu(za), 2, axis=-1)
    vn = layernorm(v, ln_v_g, ln_v_b)
    ya = chunk_spatial_gating(u, vn, w_spatial, b_spatial)

    q = zq.reshape(B, T, B_HEADS, HEAD_DIM)
    kv_new = zkv.reshape(B, T, N_KV_SLOTS, KV_HEADS, HEAD_DIM)
    paged_new = kv_new[:, :, :N_PAGED_SLOTS]
    win_new = kv_new[:, :, N_PAGED_SLOTS:]
    if past_rows is None:
        kv_rows, win_rows, win_pos = paged_new, win_new, q_pos
    else:
        kv_rows = jnp.concatenate([past_rows, paged_new], axis=1)
        win_rows = jnp.concatenate([win_prev, win_new], axis=1)
        win_pos = jnp.concatenate([win_prev_pos, q_pos])
    yb = nsa_attention(q, kv_rows, win_rows, win_pos, q_pos, zg, cmp_pe, cmp_w1, cmp_b1, cmp_w2, cmp_b2)

    gate_a, gate_b = jnp.split(jax.nn.sigmoid(zm), 2, axis=-1)
    mixed = (gate_a * (ya @ w_branch_a) + gate_b * (yb @ w_branch_b)) @ w_out
    x = x + gate1 * mixed

    h2 = rmsnorm(x, g_norm2) * (1 + scale2) + shift2
    up = h2 @ w_up
    prev = jnp.zeros((B, CONV_W - 1, 2 * D_FF), up.dtype) if conv_prev is None else conv_prev
    upc = jnp.concatenate([prev, up], axis=1)
    conv = b_conv
    for k in range(CONV_W):
        conv = conv + w_conv[k] * upc[:, k:k + T]
    a, gv = jnp.split(conv, 2, axis=-1)
    x = x + gate2 * ((jax.nn.gelu(a) * gv) @ w_down)

    new_win = win_rows[:, -min(WINDOW, win_rows.shape[1]):]
    new_conv = upc[:, -(CONV_W - 1):]
    return x, paged_new, new_win, vn, new_conv


def setup_inputs(seed: int = 0) -> dict:
    key = jax.random.key(seed)
    ks = jax.random.split(key, 32)
    n_pages = PAST_LEN // PAGE_SIZE
    n_pool = (5 * DEC_BATCH * n_pages + 3) // 4
    lbuf = min(WINDOW, PAST_LEN)

    def nrm(k, shape, s):
        return s * jax.random.normal(k, shape, jnp.float32)

    page_table = jax.random.permutation(ks[5], n_pool)[:DEC_BATCH * n_pages].reshape(DEC_BATCH, n_pages).astype(jnp.int32)
    return {
        'x_prompt': nrm(ks[0], (BATCH, SEQ, D_MODEL), 1.0),
        'x_sample': nrm(ks[1], (DEC_BATCH, DEC_SEQ, D_MODEL), 1.0),
        'cache_kv': nrm(ks[2], (DEPTH, n_pool, PAGE_SIZE, N_PAGED_SLOTS, KV_HEADS, HEAD_DIM), 1.0),
        'state_kv_win': nrm(ks[3], (DEPTH, DEC_BATCH, lbuf, 2, KV_HEADS, HEAD_DIM), 1.0),
        'state_ffn_conv': nrm(ks[4], (DEPTH, DEC_BATCH, CONV_W - 1, 2 * D_FF), 1.0),
        'page_table': page_table,
        'c_prompt': nrm(ks[6], (BATCH, D_MODEL), 1.0),
        'c_sample': nrm(ks[7], (DEC_BATCH, D_MODEL), 1.0),
        'w_ada': nrm(ks[8], (DEPTH, D_MODEL, 6 * D_MODEL), 0.5 * D_MODEL ** -0.5),
        'b_ada': nrm(ks[9], (DEPTH, 6 * D_MODEL), 0.01),
        'g_norm1': 1.0 + nrm(ks[10], (DEPTH, D_MODEL), 0.02),
        'w_in': nrm(ks[11], (DEPTH, D_MODEL, IN_COLS), D_MODEL ** -0.5),
        'ln_v_g': 1.0 + nrm(ks[12], (DEPTH, A_WIDTH), 0.02),
        'ln_v_b': nrm(ks[13], (DEPTH, A_WIDTH), 0.02),
        'w_spatial': nrm(ks[14], (DEPTH, A_GROUPS, CHUNK, CHUNK), 0.5 * CHUNK ** -0.5),
        'b_spatial': 1.0 + nrm(ks[15], (DEPTH, A_GROUPS, CHUNK), 0.1),
        'cmp_pe': nrm(ks[16], (DEPTH, 2, BLOCK, HEAD_DIM), 0.1),
        'cmp_w1': nrm(ks[17], (DEPTH, 2, BLOCK, HEAD_DIM, CMP_HIDDEN), (BLOCK * HEAD_DIM) ** -0.5),
        'cmp_b1': nrm(ks[18], (DEPTH, 2, CMP_HIDDEN), 0.01),
        'cmp_w2': nrm(ks[19], (DEPTH, 2, CMP_HIDDEN, HEAD_DIM), CMP_HIDDEN ** -0.5),
        'cmp_b2': nrm(ks[20], (DEPTH, 2, HEAD_DIM), 0.01),
        'w_branch_a': nrm(ks[21], (DEPTH, A_WIDTH, D_MODEL), A_WIDTH ** -0.5),
        'w_branch_b': nrm(ks[22], (DEPTH, B_WIDTH, D_MODEL), B_WIDTH ** -0.5),
        'w_out': nrm(ks[23], (DEPTH, D_MODEL, D_MODEL), D_MODEL ** -0.5),
        'g_norm2': 1.0 + nrm(ks[24], (DEPTH, D_MODEL), 0.02),
        'w_up': nrm(ks[25], (DEPTH, D_MODEL, 2 * D_FF), D_MODEL ** -0.5),
        'w_conv': nrm(ks[26], (DEPTH, CONV_W, 2 * D_FF), CONV_W ** -0.5),
        'b_conv': nrm(ks[27], (DEPTH, 2 * D_FF), 0.01),
        'w_down': nrm(ks[28], (DEPTH, D_FF, D_MODEL), D_FF ** -0.5),
        'g_final': 1.0 + nrm(ks[29], (D_MODEL,), 0.02),
    }


def reference(x_prompt, x_sample, cache_kv, state_kv_win, state_ffn_conv, page_table, c_prompt, c_sample,
              w_ada, b_ada, g_norm1, w_in, ln_v_g, ln_v_b, w_spatial, b_spatial,
              cmp_pe, cmp_w1, cmp_b1, cmp_w2, cmp_b2, w_branch_a, w_branch_b, w_out,
              g_norm2, w_up, w_conv, b_conv, w_down, g_final):
    t_p = x_prompt.shape[1]
    t_s = x_sample.shape[1]
    dec_b = x_sample.shape[0]
    past_len = page_table.shape[1] * cache_kv.shape[2]
    lbuf = state_kv_win.shape[2]
    pos_p = jnp.arange(t_p, dtype=jnp.int32)
    pos_s = past_len + jnp.arange(t_s, dtype=jnp.int32)
    win_prev_pos = past_len - lbuf + jnp.arange(lbuf, dtype=jnp.int32)

    xp, xs = x_prompt, x_sample
    kv_p, kv_s, win_p, win_s, v_s, conv_p, conv_s = [], [], [], [], [], [], []
    for l in range(DEPTH):
        params = (w_ada[l], b_ada[l], g_norm1[l], w_in[l], ln_v_g[l], ln_v_b[l], w_spatial[l], b_spatial[l],
                  cmp_pe[l], cmp_w1[l], cmp_b1[l], cmp_w2[l], cmp_b2[l], w_branch_a[l], w_branch_b[l], w_out[l],
                  g_norm2[l], w_up[l], w_conv[l], b_conv[l], w_down[l])
        past = cache_kv[l][page_table].reshape(dec_b, past_len, N_PAGED_SLOTS, KV_HEADS, HEAD_DIM)
        xp, kvp, wp, _, cp = decoder_layer(xp, c_prompt, pos_p, None, None, None, None, params)
        xs, kvs, ws, vs, cs = decoder_layer(xs, c_sample, pos_s, past, state_kv_win[l], win_prev_pos,
                                            state_ffn_conv[l], params)
        kv_p.append(kvp); kv_s.append(kvs); win_p.append(wp); win_s.append(ws)
        v_s.append(vs); conv_p.append(cp); conv_s.append(cs)

    y_prompt = rmsnorm(xp, g_final)
    y_sample = rmsnorm(xs, g_final)
    return (y_prompt, y_sample, jnp.stack(kv_p), jnp.stack(kv_s), jnp.stack(win_p), jnp.stack(win_s),
            jnp.stack(v_s), jnp.stack(conv_p), jnp.stack(conv_s))
```

```python
import functools

import jax
import jax.numpy as jnp
from jax import lax
from jax.experimental import pallas as pl
from jax.experimental.pallas import tpu as pltpu

F32 = jnp.float32
BF16 = jnp.bfloat16

D_MODEL = 1024
A_WIDTH = 512
A_GROUPS = 4
CHUNK = 128
B_HEADS = 8
HEAD_DIM = 64
B_WIDTH = 512
KV_HEADS = 2
GQA = 4
BLOCK = 64
N_SELECT = 16
WINDOW = 512
N_KV_SLOTS = 6
KV_COLS = 768
D_FF = 2816
F2 = 2 * D_FF
CONV_W = 3
EPS = 1e-6
NEG = -1e30
FORCE_SCORE = 1e4

FF_CHUNK = 256
N_FF_CHUNKS = D_FF // FF_CHUNK
KEY_CHUNK = 256
VMEM_LIMIT = 56 * 1024 * 1024


def _dot(a, b):
    return jnp.dot(a, b, preferred_element_type=F32)


def _dot_nt(a, b):
    return lax.dot_general(a, b, (((1,), (1,)), ((), ())), preferred_element_type=F32)


def _rms_mod(x, g, scale, shift):
    y = x * lax.rsqrt(jnp.mean(x * x, axis=-1, keepdims=True) + EPS)
    return (y * g) * (1.0 + scale) + shift


def _mod_kernel(c_ref, w_ref, b_ref, o_ref):
    c = c_ref[...]
    s = c * jax.nn.sigmoid(c)
    o_ref[...] = _dot(s.astype(BF16), w_ref[...].astype(BF16)) + b_ref[...]


def _mod_call(c_all, w_ada, b_ada):
    n = c_all.shape[0]
    tn = 1536
    return pl.pallas_call(
        _mod_kernel,
        out_shape=jax.ShapeDtypeStruct((n, 6 * D_MODEL), F32),
        grid=(6 * D_MODEL // tn,),
        in_specs=[pl.BlockSpec((n, D_MODEL), lambda j: (0, 0)),
                  pl.BlockSpec((D_MODEL, tn), lambda j: (0, j)),
                  pl.BlockSpec((1, tn), lambda j: (0, j))],
        out_specs=pl.BlockSpec((n, tn), lambda j: (0, j)),
        compiler_params=pltpu.CompilerParams(dimension_semantics=("arbitrary",), vmem_limit_bytes=VMEM_LIMIT),
        name="mod",
    )(c_all, w_ada, b_ada)


def _inproj_kernel(x_ref, mod_ref, g1_ref, wa_ref, wq_ref, wkvT_ref, wkvc_ref, wg_ref, lng_ref, lnb_ref,
                   wsp_ref, bsp_ref, *out_refs, tm, emit_att, emit_vn):
    it = iter(out_refs)
    ya_ref, q_ref, kvT_ref, kvc_ref, gz_ref = next(it), next(it), next(it), next(it), next(it)
    att_ref = next(it) if emit_att else None
    vn_ref = next(it) if emit_vn else None

    x = x_ref[0]
    h = _rms_mod(x, g1_ref[...], mod_ref[0, 1], mod_ref[0, 0])
    hb = h.astype(BF16)

    ga = jax.nn.gelu(_dot(hb, wa_ref[...]))
    u = ga[:, :A_WIDTH]
    v = ga[:, A_WIDTH:]
    vc = v - jnp.mean(v, axis=-1, keepdims=True)
    vn = vc * lax.rsqrt(jnp.mean(vc * vc, axis=-1, keepdims=True) + EPS) * lng_ref[...] + lnb_ref[...]
    if emit_vn:
        vn_ref[0] = vn
    vnb = vn.astype(BF16)
    for c in range(tm // CHUNK):
        rows = slice(c * CHUNK, (c + 1) * CHUNK)
        s = jnp.concatenate(
            [_dot(wsp_ref[g], vnb[rows, g * 128:(g + 1) * 128]) for g in range(A_GROUPS)], axis=1)
        ya_ref[0, rows, :] = (u[rows] * (s + bsp_ref[...])).astype(BF16)

    q_ref[0] = (_dot(hb, wq_ref[...]) * (HEAD_DIM ** -0.5)).astype(BF16)
    kvT = _dot_nt(wkvT_ref[...], hb)
    kvT_ref[0] = kvT
    if emit_att:
        attb = kvT[2 * 128:].astype(BF16)
        for cc in range(tm // KEY_CHUNK):
            att_ref[0, cc] = attb[:, cc * KEY_CHUNK:(cc + 1) * KEY_CHUNK]
    kvc_ref[0] = _dot(hb, wkvc_ref[...])
    gz_ref[0] = jax.nn.sigmoid(_dot(hb, wg_ref[...]))


def _inproj_call(x, mod, g1, wa, wq, wkvT, wkvc, wg, lng, lnb, wsp, bsp, *, tm, emit_att, emit_vn):
    G, T, _ = x.shape
    R = mod.shape[2]
    nt = T // tm
    const2 = lambda b, i: (0, 0)
    const3 = lambda b, i: (0, 0, 0)
    out_shape = [jax.ShapeDtypeStruct((G, T, A_WIDTH), BF16),
                 jax.ShapeDtypeStruct((G, T, B_WIDTH), BF16),
                 jax.ShapeDtypeStruct((G, KV_COLS, T), F32),
                 jax.ShapeDtypeStruct((G, T, 256), F32),
                 jax.ShapeDtypeStruct((G, T, 128), F32)]
    out_specs = [pl.BlockSpec((1, tm, A_WIDTH), lambda b, i: (b, i, 0)),
                 pl.BlockSpec((1, tm, B_WIDTH), lambda b, i: (b, i, 0)),
                 pl.BlockSpec((1, KV_COLS, tm), lambda b, i: (b, 0, i)),
                 pl.BlockSpec((1, tm, 256), lambda b, i: (b, i, 0)),
                 pl.BlockSpec((1, tm, 128), lambda b, i: (b, i, 0))]
    if emit_att:
        out_shape.append(jax.ShapeDtypeStruct((G, T // KEY_CHUNK, 512, KEY_CHUNK), BF16))
        out_specs.append(pl.BlockSpec((1, tm // KEY_CHUNK, 512, KEY_CHUNK), lambda b, i: (b, i, 0, 0)))
    if emit_vn:
        out_shape.append(jax.ShapeDtypeStruct((G, T, A_WIDTH), F32))
        out_specs.append(pl.BlockSpec((1, tm, A_WIDTH), lambda b, i: (b, i, 0)))
    return pl.pallas_call(
        functools.partial(_inproj_kernel, tm=tm, emit_att=emit_att, emit_vn=emit_vn),
        out_shape=out_shape,
        grid=(G, nt),
        in_specs=[pl.BlockSpec((1, tm, D_MODEL), lambda b, i: (b, i, 0)),
                  pl.BlockSpec((1, 6, R, D_MODEL), lambda b, i: (b, 0, 0, 0)),
                  pl.BlockSpec((1, D_MODEL), const2),
                  pl.BlockSpec(wa.shape, const2),
                  pl.BlockSpec(wq.shape, const2),
                  pl.BlockSpec(wkvT.shape, const2),
                  pl.BlockSpec(wkvc.shape, const2),
                  pl.BlockSpec(wg.shape, const2),
                  pl.BlockSpec((1, A_WIDTH), const2),
                  pl.BlockSpec((1, A_WIDTH), const2),
                  pl.BlockSpec(wsp.shape, const3),
                  pl.BlockSpec(bsp.shape, const2)],
        out_specs=out_specs,
        compiler_params=pltpu.CompilerParams(dimension_semantics=("arbitrary", "arbitrary"),
                                             vmem_limit_bytes=VMEM_LIMIT),
        name="inproj",
    )(x, mod, g1, wa, wq, wkvT, wkvc, wg, lng, lnb, wsp, bsp)


def _cmp_tail(acc, b1, w2, b2):
    return _dot(jax.nn.gelu(acc + b1).astype(BF16), w2) + b2


def _cmp_rows_kernel(x_ref, pe_ref, w1_ref, b1_ref, w2_ref, b2_ref, o_ref, *, R):
    acc = jnp.zeros((R, 128), F32)
    for j in range(BLOCK):
        xj = x_ref[pl.ds(j, R, stride=BLOCK), :] + pe_ref[0, j:j + 1, :]
        acc = acc + _dot(xj.astype(BF16), w1_ref[0, j])
    o_ref[0] = _cmp_tail(acc, b1_ref[0], w2_ref[0], b2_ref[0])


def _cmp_rows_call(kvc2d, pe, w1, b1, w2, b2, *, R):
    nrows = kvc2d.shape[0] // BLOCK
    return pl.pallas_call(
        functools.partial(_cmp_rows_kernel, R=R),
        out_shape=jax.ShapeDtypeStruct((2, nrows, 128), F32),
        grid=(2, nrows // R),
        in_specs=[pl.BlockSpec((R * BLOCK, 128), lambda s, r: (r, s)),
                  pl.BlockSpec((1, BLOCK, 128), lambda s, r: (s, 0, 0)),
                  pl.BlockSpec((1, BLOCK, 128, 128), lambda s, r: (s, 0, 0, 0)),
                  pl.BlockSpec((1, 1, 128), lambda s, r: (s, 0, 0)),
                  pl.BlockSpec((1, 128, 128), lambda s, r: (s, 0, 0)),
                  pl.BlockSpec((1, 1, 128), lambda s, r: (s, 0, 0))],
        out_specs=pl.BlockSpec((1, R, 128), lambda s, r: (s, r, 0)),
        compiler_params=pltpu.CompilerParams(dimension_semantics=("arbitrary", "arbitrary"),
                                             vmem_limit_bytes=VMEM_LIMIT),
        name="cmp_rows",
    )(kvc2d, pe, w1, b1, w2, b2)


def _cmp_pages_kernel(x_ref, pe_ref, w1_ref, b1_ref, w2_ref, b2_ref, o_ref, *, Pt):
    for s in range(2):
        acc = jnp.zeros((2 * Pt, 128), F32)
        for d in range(HEAD_DIM):
            xd = jnp.concatenate([x_ref[:, s * 128 + d, :], x_ref[:, s * 128 + HEAD_DIM + d, :]], axis=0)
            xd = xd + pe_ref[s, d:d + 1, :]
            acc = acc + _dot(xd.astype(BF16), w1_ref[s, d])
        r = _cmp_tail(acc, b1_ref[s], w2_ref[s], b2_ref[s])
        o_ref[:, (2 * s) * 128:(2 * s + 1) * 128] = r[:Pt]
        o_ref[:, (2 * s + 1) * 128:(2 * s + 2) * 128] = r[Pt:]


def _cmp_pages_call(pages, pe, w1, b1, w2, b2, *, Pt):
    P = pages.shape[0]
    full = lambda i: (0, 0, 0)
    return pl.pallas_call(
        functools.partial(_cmp_pages_kernel, Pt=Pt),
        out_shape=jax.ShapeDtypeStruct((P, 512), F32),
        grid=(P // Pt,),
        in_specs=[pl.BlockSpec((Pt, 256, 128), lambda i: (i, 0, 0)),
                  pl.BlockSpec(pe.shape, full),
                  pl.BlockSpec(w1.shape, lambda i: (0, 0, 0, 0)),
                  pl.BlockSpec(b1.shape, full),
                  pl.BlockSpec(w2.shape, full),
                  pl.BlockSpec(b2.shape, full)],
        out_specs=pl.BlockSpec((Pt, 512), lambda i: (i, 0)),
        compiler_params=pltpu.CompilerParams(dimension_semantics=("arbitrary",), vmem_limit_bytes=VMEM_LIMIT),
        name="cmp_pages",
    )(pages, pe, w1, b1, w2, b2)


def _p_attn_kernel(q_ref, att_ref, kc_ref, vc_ref, gz_ref, yb_ref, m_s, l_s, acc_s, *, tq, T):
    j = pl.program_id(1)
    nb = T // BLOCK
    q0 = j * tq
    tok = q0 + lax.broadcasted_iota(jnp.int32, (tq, 1), 0)
    gz = gz_ref[0]

    for h in range(KV_HEADS):
        qs = jnp.concatenate([q_ref[0, :, (h * GQA + g) * HEAD_DIM:(h * GQA + g + 1) * HEAD_DIM]
                              for g in range(GQA)], axis=0)
        kc = kc_ref[0, :, h * HEAD_DIM:(h + 1) * HEAD_DIM].astype(BF16)
        vc = vc_ref[0, :, h * HEAD_DIM:(h + 1) * HEAD_DIM].astype(BF16)

        tok4 = jnp.concatenate([tok] * GQA, axis=0)
        blk = lax.broadcasted_iota(jnp.int32, (1, nb), 1)
        avail = (blk + 1) * BLOCK <= tok4 + 1
        s = jnp.where(avail, _dot_nt(qs, kc), NEG)
        m = jnp.max(s, axis=-1, keepdims=True)
        e = jnp.where(avail, jnp.exp(s - m), 0.0)
        p = e / jnp.maximum(jnp.sum(e, axis=-1, keepdims=True), 1e-30)
        o_cmp = _dot(p.astype(BF16), vc)

        tokT = q0 + lax.broadcasted_iota(jnp.int32, (1, tq), 1)
        tokT4 = jnp.concatenate([tokT] * GQA, axis=1)
        blkT = lax.broadcasted_iota(jnp.int32, (nb, 1), 0)
        availT = (blkT + 1) * BLOCK <= tokT4 + 1
        sT = jnp.where(availT, _dot_nt(kc, qs), NEG)
        mT = jnp.max(sT, axis=0, keepdims=True)
        eT = jnp.where(availT, jnp.exp(sT - mT), 0.0)
        pT = eT / jnp.maximum(jnp.sum(eT, axis=0, keepdims=True), 1e-30)
        imp = pT[:, 0:tq]
        for g in range(1, GQA):
            imp = imp + pT[:, g * tq:(g + 1) * tq]
        cur = tokT >> 6
        forced = (blkT == 0) | (blkT == cur) | (blkT == cur - 1)
        imp = jnp.where(forced, FORCE_SCORE, jnp.where(blkT > cur, -1.0, imp))
        sel_rows = []
        for i in range(nb):
            vi = imp[i:i + 1, :]
            ahead = (imp > vi) | ((imp == vi) & (blkT < i))
            cnt = jnp.sum(ahead.astype(F32), axis=0, keepdims=True)
            sel_rows.append((cnt < float(N_SELECT)).astype(F32))
        selT = jnp.concatenate(sel_rows + [jnp.zeros((128 - nb, tq), F32)], axis=0)
        selB = selT.T.astype(BF16)

        def flash(k_row, v_row, lo, hi, mask_fn):
            m_s[...] = jnp.full(m_s.shape, NEG, F32)
            l_s[...] = jnp.zeros(l_s.shape, F32)
            acc_s[...] = jnp.zeros(acc_s.shape, F32)

            def body(c, carry):
                kT = att_ref[0, c, k_row * 64:(k_row + 1) * 64, :]
                vT = att_ref[0, c, v_row * 64:(v_row + 1) * 64, :]
                kpos = c * KEY_CHUNK + lax.broadcasted_iota(jnp.int32, (1, KEY_CHUNK), 1)
                mask4 = mask_fn(c, kpos)
                sc = jnp.where(mask4, _dot(qs, kT), NEG)
                m_old = m_s[...]
                m_new = jnp.maximum(m_old, jnp.max(sc, axis=-1, keepdims=True))
                alpha = jnp.exp(m_old - m_new)
                pe = jnp.where(mask4, jnp.exp(sc - m_new), 0.0)
                l_s[...] = alpha * l_s[...] + jnp.sum(pe, axis=-1, keepdims=True)
                acc_s[...] = alpha * acc_s[...] + _dot_nt(pe.astype(BF16), vT)
                m_s[...] = m_new
                return carry

            lax.fori_loop(lo, hi, body, 0)
            return acc_s[...] / jnp.maximum(l_s[...], 1e-30)

        n_chunks = (q0 + tq + KEY_CHUNK - 1) // KEY_CHUNK

        def sel_mask(c, kpos):
            nrow = lax.broadcasted_iota(jnp.int32, (128, KEY_CHUNK), 0)
            kblk = c * (KEY_CHUNK // BLOCK) + (lax.broadcasted_iota(jnp.int32, (128, KEY_CHUNK), 1) >> 6)
            expand = (nrow == kblk).astype(BF16)
            hit = _dot(selB, expand)
            return (jnp.concatenate([hit] * GQA, axis=0) > 0.5) & (kpos <= tok4)

        def win_mask(c, kpos):
            diff = tok4 - kpos
            return (diff >= 0) & (diff < WINDOW)

        o_sel = flash(h, 2 + h, 0, n_chunks, sel_mask)
        w_lo = jnp.maximum(q0 - WINDOW + 1, 0) // KEY_CHUNK
        o_win = flash(4 + h, 6 + h, w_lo, n_chunks, win_mask)

        for g in range(GQA):
            rows = slice(g * tq, (g + 1) * tq)
            col = h * GQA + g
            o = (gz[:, col:col + 1] * o_cmp[rows]
                 + gz[:, 8 + col:8 + col + 1] * o_sel[rows]
                 + gz[:, 16 + col:16 + col + 1] * o_win[rows])
            yb_ref[0, :, col * HEAD_DIM:(col + 1) * HEAD_DIM] = o.astype(BF16)


def _p_attn_call(q, att, cmp_out, gz, *, tq):
    B, T, _ = q.shape
    nb = T // BLOCK
    return pl.pallas_call(
        functools.partial(_p_attn_kernel, tq=tq, T=T),
        out_shape=jax.ShapeDtypeStruct((B, T, B_WIDTH), BF16),
        grid=(B, T // tq),
        in_specs=[pl.BlockSpec((1, tq, B_WIDTH), lambda b, j: (b, j, 0)),
                  pl.BlockSpec((1, T // KEY_CHUNK, 512, KEY_CHUNK), lambda b, j: (b, 0, 0, 0)),
                  pl.BlockSpec((1, nb, 128), lambda b, j: (0, b, 0)),
                  pl.BlockSpec((1, nb, 128), lambda b, j: (1, b, 0)),
                  pl.BlockSpec((1, tq, 128), lambda b, j: (b, j, 0))],
        out_specs=pl.BlockSpec((1, tq, B_WIDTH), lambda b, j: (b, j, 0)),
        scratch_shapes=[pltpu.VMEM((GQA * tq, 1), F32),
                        pltpu.VMEM((GQA * tq, 1), F32),
                        pltpu.VMEM((GQA * tq, HEAD_DIM), F32)],
        compiler_params=pltpu.CompilerParams(dimension_semantics=("arbitrary", "arbitrary"),
                                             vmem_limit_bytes=VMEM_LIMIT),
        name="p_attn",
    )(q, att, cmp_out, cmp_out, gz)


def _s_cmp_kernel(pt_ref, q_ref, pool_ref, newc_ref, ocmp_ref, imp_ref, gath, *, n_pages, past_len):
    b = pl.program_id(0)

    def gather(jp, carry):
        gath[pl.ds(jp, 1), :] = pool_ref[pl.ds(pt_ref[b, jp], 1), :]
        return carry

    lax.fori_loop(0, n_pages, gather, 0)

    rows = 8 * GQA
    t_row = lax.broadcasted_iota(jnp.int32, (rows, 1), 0) & 7
    pos = past_len + t_row
    lane = lax.broadcasted_iota(jnp.int32, (1, 2 * n_pages), 1)
    blk = 2 * (lane & (n_pages - 1)) + (lane >> 7)
    new_blk = past_len // BLOCK
    avail = (blk + 1) * BLOCK <= pos + 1
    avail_new = (new_blk + 1) * BLOCK <= pos + 1
    for h in range(KV_HEADS):
        qh = q_ref[0, h]
        kc = [gath[:, (h * 2 + k) * 64:(h * 2 + k + 1) * 64].astype(BF16) for k in range(2)]
        vc = [gath[:, (2 + h) * 128 + k * 64:(2 + h) * 128 + (k + 1) * 64].astype(BF16) for k in range(2)]
        kc_new = newc_ref[0, :, h * 128:h * 128 + 64].astype(BF16).astype(F32)
        vc_new = newc_ref[0, :, (2 + h) * 128:(2 + h) * 128 + 64].astype(BF16).astype(F32)
        s = jnp.where(avail, jnp.concatenate([_dot_nt(qh, kc[0]), _dot_nt(qh, kc[1])], axis=1), NEG)
        s_new = jnp.where(avail_new, jnp.sum(qh.astype(F32) * kc_new, axis=-1, keepdims=True), NEG)
        m = jnp.maximum(jnp.max(s, axis=-1, keepdims=True), s_new)
        e = jnp.where(avail, jnp.exp(s - m), 0.0)
        e_new = jnp.where(avail_new, jnp.exp(s_new - m), 0.0)
        den = jnp.maximum(jnp.sum(e, axis=-1, keepdims=True) + e_new, 1e-30)
        p = e / den
        p_new = e_new / den
        pb = p.astype(BF16)
        ocmp_ref[0, h] = (_dot(pb[:, :n_pages], vc[0]) + _dot(pb[:, n_pages:], vc[1])
                          + p_new.astype(BF16).astype(F32) * vc_new)
        imp = p[0:8] + p[8:16] + p[16:24] + p[24:32]
        cur = (past_len + lax.broadcasted_iota(jnp.int32, (8, 1), 0)) >> 6
        forced = (blk == 0) | (blk == cur) | (blk == cur - 1)
        imp_ref[0, h] = jnp.where(forced, FORCE_SCORE, jnp.where(blk > cur, -1.0, imp))


def _s_cmp_call(page_table, q_r, pool, newc, *, past_len):
    nbatch, n_pages = page_table.shape
    P = pool.shape[0]
    return pl.pallas_call(
        functools.partial(_s_cmp_kernel, n_pages=n_pages, past_len=past_len),
        out_shape=[jax.ShapeDtypeStruct((nbatch, KV_HEADS, 32, HEAD_DIM), F32),
                   jax.ShapeDtypeStruct((nbatch, KV_HEADS, 8, 2 * n_pages), F32)],
        grid_spec=pltpu.PrefetchScalarGridSpec(
            num_scalar_prefetch=1,
            grid=(nbatch,),
            in_specs=[pl.BlockSpec((1, KV_HEADS, 32, HEAD_DIM), lambda b, pt: (b, 0, 0, 0)),
                      pl.BlockSpec((P, 512), lambda b, pt: (0, 0)),
                      pl.BlockSpec((1, 1, 512), lambda b, pt: (b, 0, 0))],
            out_specs=[pl.BlockSpec((1, KV_HEADS, 32, HEAD_DIM), lambda b, pt: (b, 0, 0, 0)),
                       pl.BlockSpec((1, KV_HEADS, 8, 2 * n_pages), lambda b, pt: (b, 0, 0, 0))],
            scratch_shapes=[pltpu.VMEM((n_pages, 512), F32)]),
        compiler_params=pltpu.CompilerParams(dimension_semantics=("arbitrary",), vmem_limit_bytes=VMEM_LIMIT),
        name="s_cmp",
    )(page_table, q_r, pool, newc)


def _s_select_kernel(imp_ref, idx_ref, *, n_pages, n_pick):
    v = imp_ref[...]
    rows = v.shape[0]
    lane = lax.broadcasted_iota(jnp.int32, (1, 2 * n_pages), 1)
    blk = (2 * (lane & (n_pages - 1)) + (lane >> 7)).astype(F32)
    out_lane = lax.broadcasted_iota(jnp.int32, (1, 128), 1)
    out = jnp.full((rows, 128), float(2 * n_pages), F32)
    for k in range(n_pick):
        m = jnp.max(v, axis=-1, keepdims=True)
        pick = jnp.min(jnp.where(v == m, blk, 1e9), axis=-1, keepdims=True)
        out = jnp.where(out_lane == k, pick, out)
        v = jnp.where(blk == pick, NEG, v)
    idx_ref[...] = out.astype(jnp.int32)


def _s_select_call(imp2d, *, n_pages, n_pick):
    rows = imp2d.shape[0]
    return pl.pallas_call(
        functools.partial(_s_select_kernel, n_pages=n_pages, n_pick=n_pick),
        out_shape=jax.ShapeDtypeStruct((rows, 128), jnp.int32),
        grid=(1,),
        in_specs=[pl.BlockSpec(imp2d.shape, lambda i: (0, 0))],
        out_specs=pl.BlockSpec((rows, 128), lambda i: (0, 0)),
        compiler_params=pltpu.CompilerParams(dimension_semantics=("arbitrary",), vmem_limit_bytes=VMEM_LIMIT),
        name="s_select",
    )(imp2d)


def _s_attn_kernel(pt_ref, idx_ref, q_ref, kvn_ref, win_ref, gz_ref, ocmp_ref, cache_ref, yb_ref,
                   kbuf, vbuf, sem, *, n_tok, n_gather, past_len):
    b = pl.program_id(0)
    span = n_gather * 128

    def tile_copies(t, h, jj):
        n = idx_ref[((b * KV_HEADS + h) * n_tok + t) * N_SELECT + jj]
        page = pt_ref[b, n >> 1]
        kc = pltpu.make_async_copy(cache_ref.at[page, pl.ds(2 * 128 + h * 64, 64), :],
                                   kbuf.at[h, t, :, pl.ds(jj * 128, 128)], sem.at[0])
        vc = pltpu.make_async_copy(cache_ref.at[page, pl.ds(3 * 128 + h * 64, 64), :],
                                   vbuf.at[h, t, :, pl.ds(jj * 128, 128)], sem.at[1])
        return kc, vc

    for h in range(KV_HEADS):
        for t in range(n_tok):
            for jj in range(n_gather):
                kc, vc = tile_copies(t, h, jj)
                kc.start()
                vc.start()

    rows = 8 * GQA
    t_row = lax.broadcasted_iota(jnp.int32, (rows, 1), 0) & 7
    gz = gz_ref[0]

    lane_n = lax.broadcasted_iota(jnp.int32, (1, 128), 1)
    tok_shift = n_tok.bit_length() - 1
    new_ok = ((lane_n >> tok_shift) == b) & ((lane_n & (n_tok - 1)) <= t_row)
    lane_w = lax.broadcasted_iota(jnp.int32, (1, WINDOW), 1)
    win_ok = lane_w > t_row

    def softmax_pv(parts):
        m = None
        for sc, mk, _ in parts:
            mm = jnp.max(jnp.where(mk, sc, NEG), axis=-1, keepdims=True)
            m = mm if m is None else jnp.maximum(m, mm)
        den = jnp.zeros((rows, 1), F32)
        o = jnp.zeros((rows, HEAD_DIM), F32)
        for sc, mk, vts in parts:
            e = jnp.where(mk, jnp.exp(jnp.where(mk, sc, NEG) - m), 0.0)
            den = den + jnp.sum(e, axis=-1, keepdims=True)
            for rmask, vt in vts:
                er = e if rmask is None else jnp.where(rmask, e, 0.0)
                o = o + _dot_nt(er.astype(BF16), vt)
        return o / jnp.maximum(den, 1e-30)

    win_out = []
    for h in range(KV_HEADS):
        qh = q_ref[0, h]
        kw = win_ref[0, h * 64:(h + 1) * 64, :].astype(BF16)
        vw = win_ref[0, 128 + h * 64:128 + (h + 1) * 64, :].astype(BF16)
        kwn = kvn_ref[4 * 128 + h * 64:4 * 128 + (h + 1) * 64, :].astype(BF16)
        vwn = kvn_ref[5 * 128 + h * 64:5 * 128 + (h + 1) * 64, :].astype(BF16)
        win_out.append(softmax_pv([(_dot(qh, kw), win_ok, [(None, vw)]),
                                   (_dot(qh, kwn), new_ok, [(None, vwn)])]))

    for h in range(KV_HEADS):
        for t in range(n_tok):
            for jj in range(n_gather):
                kc, vc = tile_copies(t, h, jj)
                kc.wait()
                vc.wait()

    lane_s = lax.broadcasted_iota(jnp.int32, (1, span), 1)
    for h in range(KV_HEADS):
        qh = q_ref[0, h]
        sc = jnp.zeros((rows, span), F32)
        half = jnp.zeros((rows, span), jnp.int32)
        for t in range(n_tok):
            st = _dot(qh, kbuf[h, t].astype(BF16))
            hrow = jnp.zeros((1, span), jnp.int32)
            for jj in range(n_gather):
                n = idx_ref[((b * KV_HEADS + h) * n_tok + t) * N_SELECT + jj]
                hrow = jnp.where((lane_s >> 7) == jj, n & 1, hrow)
            sc = jnp.where(t_row == t, st, sc)
            half = jnp.where(t_row == t, hrow, half)
        ok = ((lane_s >> 6) & 1) == half
        ksn = kvn_ref[2 * 128 + h * 64:2 * 128 + (h + 1) * 64, :].astype(BF16)
        vsn = kvn_ref[3 * 128 + h * 64:3 * 128 + (h + 1) * 64, :].astype(BF16)
        vts = [(t_row == t, vbuf[h, t].astype(BF16)) for t in range(n_tok)]
        o_sel = softmax_pv([(sc, ok, vts), (_dot(qh, ksn), new_ok, [(None, vsn)])])
        yb_ref[0, h] = (gz[h, :, 0:1] * ocmp_ref[0, h] + gz[h, :, 1:2] * o_sel + gz[h, :, 2:3] * win_out[h])


def _s_attn_call(page_table, idx_flat, q_r, kvT_new, winT, gz_r, ocmp, cache_pages, *, n_tok, past_len):
    nbatch = page_table.shape[0]
    n_gather = N_SELECT - 1
    span = n_gather * 128
    blk4 = lambda b, pt, ix: (b, 0, 0, 0)
    return pl.pallas_call(
        functools.partial(_s_attn_kernel, n_tok=n_tok, n_gather=n_gather, past_len=past_len),
        out_shape=jax.ShapeDtypeStruct((nbatch, KV_HEADS, 32, HEAD_DIM), F32),
        grid_spec=pltpu.PrefetchScalarGridSpec(
            num_scalar_prefetch=2,
            grid=(nbatch,),
            in_specs=[pl.BlockSpec((1, KV_HEADS, 32, HEAD_DIM), blk4),
                      pl.BlockSpec(kvT_new.shape, lambda b, pt, ix: (0, 0)),
                      pl.BlockSpec((1, 256, WINDOW), lambda b, pt, ix: (b, 0, 0)),
                      pl.BlockSpec((1, KV_HEADS, 32, 128), blk4),
                      pl.BlockSpec((1, KV_HEADS, 32, HEAD_DIM), blk4),
                      pl.BlockSpec(memory_space=pl.ANY)],
            out_specs=pl.BlockSpec((1, KV_HEADS, 32, HEAD_DIM), blk4),
            scratch_shapes=[pltpu.VMEM((KV_HEADS, n_tok, HEAD_DIM, span), F32),
                            pltpu.VMEM((KV_HEADS, n_tok, HEAD_DIM, span), F32),
                            pltpu.SemaphoreType.DMA((2,))]),
        compiler_params=pltpu.CompilerParams(dimension_semantics=("arbitrary",), vmem_limit_bytes=VMEM_LIMIT),
        name="s_attn",
    )(page_table, idx_flat, q_r, kvT_new, winT, gz_r, ocmp, cache_pages)


def _post_kernel(*refs, tm, seq_len, has_prev):
    it = iter(refs)
    x_ref, mod_ref, ya_ref, yb_ref = next(it), next(it), next(it), next(it)
    g1_ref, g2_ref, gf_ref = next(it), next(it), next(it)
    wm_ref, wba_ref, wbb_ref, wo_ref, wup_ref, wdn_ref, cw_ref = (next(it) for _ in range(7))
    p1_ref = next(it) if has_prev else None
    p2_ref = next(it) if has_prev else None
    y_ref, up_ref = next(it), next(it)
    carry, h2_s, acc_s = next(it), next(it), next(it)

    i = pl.program_id(1)
    x = x_ref[0]
    shift1, scale1, gate1 = mod_ref[0, 0], mod_ref[0, 1], mod_ref[0, 2]
    shift2, scale2, gate2 = mod_ref[0, 3], mod_ref[0, 4], mod_ref[0, 5]

    hb = _rms_mod(x, g1_ref[...], scale1, shift1).astype(BF16)
    gates = jax.nn.sigmoid(_dot(hb, wm_ref[...]))
    mix = (gates[:, :D_MODEL] * _dot(ya_ref[0], wba_ref[...])
           + gates[:, D_MODEL:] * _dot(yb_ref[0], wbb_ref[...]))
    x1 = x + gate1 * _dot(mix.astype(BF16), wo_ref[...])
    h2_s[...] = _rms_mod(x1, g2_ref[...], scale2, shift2).astype(BF16)
    acc_s[...] = jnp.zeros(acc_s.shape, F32)

    if not has_prev:
        @pl.when(i == 0)
        def _():
            carry[...] = jnp.zeros(carry.shape, F32)

    row = lax.broadcasted_iota(jnp.int32, (tm, 1), 0)
    row8 = lax.broadcasted_iota(jnp.int32, (8, 1), 0)

    def conv_half(idx):
        up = _dot(h2_s[...], wup_ref[idx])
        r1 = pltpu.roll(up, 1, 0)
        r2 = pltpu.roll(up, 2, 0)
        if has_prev:
            s1 = jnp.where((row & (seq_len - 1)) == 0, p1_ref[idx], r1)
            s2 = jnp.where((row & (seq_len - 1)) < 2, p2_ref[idx], r2)
            up_ref[idx] = up
        else:
            prev = carry[idx]
            t1 = jnp.where(row8 == 0, pltpu.roll(prev, 1, 0), r1[:8])
            t2 = jnp.where(row8 < 2, pltpu.roll(prev, 2, 0), r2[:8])
            s1 = jnp.concatenate([t1, r1[8:]], axis=0)
            s2 = jnp.concatenate([t2, r2[8:]], axis=0)
            carry[idx] = up[tm - 8:]
            up_ref[0, idx] = up[tm - 8:]
        cw = cw_ref[idx]
        return cw[3:4] + cw[0:1] * s2 + cw[1:2] * s1 + cw[2:3] * up

    def ff_body(c, cr):
        a = conv_half(c)
        gv = conv_half(N_FF_CHUNKS + c)
        act = (jax.nn.gelu(a) * gv).astype(BF16)
        acc_s[...] += _dot(act, wdn_ref[c])
        return cr

    lax.fori_loop(0, N_FF_CHUNKS, ff_body, 0)
    x2 = x1 + gate2 * acc_s[...]
    y_ref[0] = x2 * lax.rsqrt(jnp.mean(x2 * x2, axis=-1, keepdims=True) + EPS) * gf_ref[...]


def _post_call(x, mod, ya, yb, g1, g2, gf, wm, wba, wbb, wo, wup, wdn, cw, prev=None, *, tm, seq_len):
    G, T, _ = x.shape
    R = mod.shape[2]
    nt = T // tm
    has_prev = prev is not None
    single = pl.Buffered(1)
    c2 = lambda b, i: (0, 0)
    c3 = lambda b, i: (0, 0, 0)
    in_specs = [pl.BlockSpec((1, tm, D_MODEL), lambda b, i: (b, i, 0)),
                pl.BlockSpec((1, 6, R, D_MODEL), lambda b, i: (b, 0, 0, 0)),
                pl.BlockSpec((1, tm, A_WIDTH), lambda b, i: (b, i, 0)),
                pl.BlockSpec((1, tm, B_WIDTH), lambda b, i: (b, i, 0)),
                pl.BlockSpec((1, D_MODEL), c2),
                pl.BlockSpec((1, D_MODEL), c2),
                pl.BlockSpec((1, D_MODEL), c2),
                pl.BlockSpec(wm.shape, c2, pipeline_mode=single),
                pl.BlockSpec(wba.shape, c2, pipeline_mode=single),
                pl.BlockSpec(wbb.shape, c2, pipeline_mode=single),
                pl.BlockSpec(wo.shape, c2, pipeline_mode=single),
                pl.BlockSpec(wup.shape, c3, pipeline_mode=single),
                pl.BlockSpec(wdn.shape, c3, pipeline_mode=single),
                pl.BlockSpec(cw.shape, c3, pipeline_mode=single)]
    args = [x, mod, ya, yb, g1, g2, gf, wm, wba, wbb, wo, wup, wdn, cw]
    if has_prev:
        in_specs += [pl.BlockSpec(prev[0].shape, c3), pl.BlockSpec(prev[1].shape, c3)]
        args += list(prev)
        up_shape = jax.ShapeDtypeStruct((2 * N_FF_CHUNKS, T, FF_CHUNK), F32)
        up_spec = pl.BlockSpec((2 * N_FF_CHUNKS, tm, FF_CHUNK), lambda b, i: (0, i, 0))
    else:
        up_shape = jax.ShapeDtypeStruct((G, 2 * N_FF_CHUNKS, 8, FF_CHUNK), F32)
        up_spec = pl.BlockSpec((1, 2 * N_FF_CHUNKS, 8, FF_CHUNK), lambda b, i: (b, 0, 0, 0))
    return pl.pallas_call(
        functools.partial(_post_kernel, tm=tm, seq_len=seq_len, has_prev=has_prev),
        out_shape=[jax.ShapeDtypeStruct((G, T, D_MODEL), F32), up_shape],
        grid=(G, nt),
        in_specs=in_specs,
        out_specs=[pl.BlockSpec((1, tm, D_MODEL), lambda b, i: (b, i, 0)), up_spec],
        scratch_shapes=[pltpu.VMEM((2 * N_FF_CHUNKS, 8, FF_CHUNK), F32),
                        pltpu.VMEM((tm, D_MODEL), BF16),
                        pltpu.VMEM((tm, D_MODEL), F32)],
        compiler_params=pltpu.CompilerParams(dimension_semantics=("arbitrary", "arbitrary"),
                                             vmem_limit_bytes=VMEM_LIMIT),
        name="post",
    )(*args)


def _chunk_cols(a, n):
    return a.reshape(a.shape[0], n, FF_CHUNK).transpose(1, 0, 2)


def kernel(x_prompt, x_sample, cache_kv, state_kv_win, state_ffn_conv, page_table, c_prompt, c_sample, w_ada, b_ada, g_norm1, w_in, ln_v_g, ln_v_b, w_spatial, b_spatial, cmp_pe, cmp_w1, cmp_b1, cmp_w2, cmp_b2, w_branch_a, w_branch_b, w_out, g_norm2, w_up, w_conv, b_conv, w_down, g_final):
    B, T, _ = x_prompt.shape
    SB, ST, _ = x_sample.shape
    n_pool, page_size = cache_kv.shape[1], cache_kv.shape[2]
    n_pages = page_table.shape[1]
    past_len = n_pages * page_size
    lbuf = state_kv_win.shape[2]
    assert cache_kv.shape[0] == 1 and page_size == 128 and lbuf == WINDOW and SB * ST == 128 and ST <= 8
    assert past_len % BLOCK == 0 and n_pages == 128 and T % 512 == 0 and ST & (ST - 1) == 0 and ST >= 2

    win = w_in[0]
    wa = win[:, :2 * A_WIDTH].astype(BF16)
    o = 2 * A_WIDTH
    wq = win[:, o:o + B_WIDTH].astype(BF16)
    o += B_WIDTH
    wkv = win[:, o:o + KV_COLS]
    wkvT = wkv.T.astype(BF16)
    wkvc = wkv[:, :256].astype(BF16)
    o += KV_COLS
    wg = jnp.pad(win[:, o:o + 3 * B_HEADS], ((0, 0), (0, 128 - 3 * B_HEADS))).astype(BF16)
    o += 3 * B_HEADS
    wm = win[:, o:].astype(BF16)
    g1 = g_norm1[0][None]
    g2 = g_norm2[0][None]
    gf = g_final[None]
    lng = ln_v_g[0][None]
    lnb = ln_v_b[0][None]

    ws = w_spatial[0]
    bs = b_spatial[0]
    wsp_p = jnp.tril(ws).astype(BF16)
    bsp_p = jnp.repeat(bs.T, 128, axis=1)
    w4 = jnp.tril(ws[:, :ST, :ST])
    eye_s = jnp.eye(128 // ST, dtype=F32)
    wsp_s = jnp.einsum('ab,gts->gatbs', eye_s, w4).reshape(A_GROUPS, 128, 128).astype(BF16)
    bsp_s = jnp.repeat(jnp.tile(bs[:, :ST].T, (128 // ST, 1)), 128, axis=1)

    eye2 = jnp.eye(2, dtype=F32)
    w1, pe = cmp_w1[0], cmp_pe[0]
    w1_rows = jnp.einsum('kl,sidh->sikdlh', eye2, w1).reshape(2, BLOCK, 128, 128).astype(BF16)
    pe_rows = jnp.tile(pe, (1, 1, 2))
    w1_pages = jnp.einsum('bc,sidh->sdbich', eye2, w1).reshape(2, HEAD_DIM, 128, 128).astype(BF16)
    pe_pages = jnp.tile(pe.transpose(0, 2, 1), (1, 1, 2))
    b1t = jnp.tile(cmp_b1[0], (1, 2))[:, None, :]
    b2t = jnp.tile(cmp_b2[0], (1, 2))[:, None, :]
    w2bd = jnp.einsum('kl,shd->skhld', eye2, cmp_w2[0]).reshape(2, 128, 128).astype(BF16)

    wba = w_branch_a[0].astype(BF16)
    wbb = w_branch_b[0].astype(BF16)
    wo = w_out[0].astype(BF16)
    wup = _chunk_cols(w_up[0], 2 * N_FF_CHUNKS).astype(BF16)
    wdn = w_down[0].reshape(N_FF_CHUNKS, FF_CHUNK, D_MODEL).astype(BF16)
    cw = jnp.concatenate([w_conv[0], b_conv[0][None], jnp.zeros((4, F2), F32)], axis=0)
    cw = _chunk_cols(cw, 2 * N_FF_CHUNKS)

    mod = _mod_call(jnp.concatenate([c_prompt, c_sample], axis=0), w_ada[0], b_ada)
    mod_p = mod[:B].reshape(B, 6, 1, D_MODEL)
    mod_s = jnp.repeat(mod[B:], ST, axis=0).reshape(SB * ST, 6, D_MODEL).transpose(1, 0, 2)[None]

    ya_p, q_p, kvT_p, kvc_p, gz_p, att_p = _inproj_call(
        x_prompt, mod_p, g1, wa, wq, wkvT, wkvc, wg, lng, lnb, wsp_p, bsp_p, tm=512, emit_att=True, emit_vn=False)
    xs2 = x_sample.reshape(1, SB * ST, D_MODEL)
    ya_s, q_s, kvT_s, kvc_s, gz_s, vn_s = _inproj_call(
        xs2, mod_s, g1, wa, wq, wkvT, wkvc, wg, lng, lnb, wsp_s, bsp_s, tm=128, emit_att=False, emit_vn=True)

    cmp_p = _cmp_rows_call(kvc_p.reshape(B * T, 256), pe_rows, w1_rows, b1t, w2bd, b2t, R=128)
    yb_p = _p_attn_call(q_p, att_p, cmp_p, gz_p, tq=128)

    cache_pages = jnp.transpose(cache_kv[0], (0, 2, 3, 4, 1)).reshape(n_pool, 512, page_size)
    pool = _cmp_pages_call(cache_pages, pe_pages, w1_pages, b1t, w2bd, b2t, Pt=64)
    kv_rows_s = jnp.transpose(kvT_s[0], (1, 0)).reshape(SB, ST, KV_COLS)
    newblk = jnp.pad(jnp.transpose(kv_rows_s[:, :, :256], (0, 2, 1)), ((0, 0), (0, 0), (0, page_size - ST)))
    newc = _cmp_pages_call(newblk, pe_pages, w1_pages, b1t, w2bd, b2t, Pt=SB)

    q_r = jnp.pad(q_s.reshape(SB, ST, KV_HEADS, GQA, HEAD_DIM).transpose(0, 2, 3, 1, 4),
                  ((0, 0), (0, 0), (0, 0), (0, 8 - ST), (0, 0))).reshape(SB, KV_HEADS, 8 * GQA, HEAD_DIM)
    ocmp_s, imp_s = _s_cmp_call(page_table, q_r, pool, newc.reshape(SB, 1, 512), past_len=past_len)
    idx = _s_select_call(imp_s.reshape(SB * KV_HEADS * 8, 2 * n_pages), n_pages=n_pages, n_pick=N_SELECT - 1)
    idx_flat = idx.reshape(SB, KV_HEADS, 8, 128)[:, :, :ST, :N_SELECT].reshape(-1)
    winT = jnp.transpose(state_kv_win[0], (0, 2, 3, 4, 1)).reshape(SB, 256, lbuf)
    gz_r = jnp.pad(gz_s[0, :, :24].reshape(SB, ST, 3, KV_HEADS, GQA).transpose(0, 3, 4, 1, 2),
                   ((0, 0), (0, 0), (0, 0), (0, 8 - ST), (0, 128 - 3))).reshape(SB, KV_HEADS, 8 * GQA, 128)
    yb_r = _s_attn_call(page_table, idx_flat, q_r, kvT_s[0], winT, gz_r, ocmp_s, cache_pages,
                        n_tok=ST, past_len=past_len)
    yb_s = yb_r.reshape(SB, KV_HEADS, GQA, 8, HEAD_DIM)[:, :, :, :ST].transpose(0, 3, 1, 2, 4)
    yb_s = yb_s.reshape(1, SB * ST, B_WIDTH).astype(BF16)

    y_p, up_p = _post_call(x_prompt, mod_p, ya_p, yb_p, g1, g2, gf, wm, wba, wbb, wo, wup, wdn, cw,
                           tm=256, seq_len=T)
    st = state_ffn_conv[0]
    zrow = jnp.zeros((SB, 1, F2), F32)
    p1 = jnp.concatenate([st[:, 1:2], jnp.tile(zrow, (1, ST - 1, 1))], axis=1).reshape(SB * ST, F2)
    p2 = jnp.concatenate([st, jnp.tile(zrow, (1, ST - 2, 1))], axis=1).reshape(SB * ST, F2)
    y_s, up_s = _post_call(xs2, mod_s, ya_s, yb_s, g1, g2, gf, wm, wba, wbb, wo, wup, wdn, cw,
                           prev=(_chunk_cols(p1, 2 * N_FF_CHUNKS), _chunk_cols(p2, 2 * N_FF_CHUNKS)),
                           tm=SB * ST, seq_len=ST)

    kv_prompt = jnp.transpose(kvT_p[:, :512].reshape(B, 4, KV_HEADS, HEAD_DIM, T), (0, 4, 1, 2, 3))[None]
    win_prompt = jnp.transpose(kvT_p[:, 512:, T - WINDOW:].reshape(B, 2, KV_HEADS, HEAD_DIM, WINDOW),
                               (0, 4, 1, 2, 3))[None]
    kv_sample = kv_rows_s[:, :, :512].reshape(SB, ST, 4, KV_HEADS, HEAD_DIM)[None]
    win_new_s = kv_rows_s[:, :, 512:].reshape(SB, ST, 2, KV_HEADS, HEAD_DIM)
    win_sample = jnp.concatenate([state_kv_win[0][:, ST:], win_new_s], axis=1)[None]
    v_chunk = vn_s.reshape(SB, ST, A_WIDTH)[None]
    conv_prompt = up_p[:, :, 6:8, :].transpose(0, 2, 1, 3).reshape(B, 2, F2)[None]
    up_rows = up_s.transpose(1, 0, 2).reshape(SB, ST, F2)
    conv_sample = up_rows[:, ST - 2:][None]
    return (y_p, y_s.reshape(SB, ST, D_MODEL), kv_prompt, kv_sample, win_prompt, win_sample, v_chunk,
            conv_prompt, conv_sample)
```

```python
import functools

import jax
import jax.numpy as jnp
from jax import lax
from jax.experimental import pallas as pl
from jax.experimental.pallas import tpu as pltpu

F32 = jnp.float32
BF16 = jnp.bfloat16

D_MODEL = 1024
A_WIDTH = 512
A_GROUPS = 4
CHUNK = 128
B_HEADS = 8
HEAD_DIM = 64
B_WIDTH = 512
KV_HEADS = 2
GQA = 4
BLOCK = 64
N_SELECT = 16
WINDOW = 512
N_KV_SLOTS = 6
KV_COLS = 768
D_FF = 2816
F2 = 2 * D_FF
CONV_W = 3
EPS = 1e-6
NEG = -1e30
FORCE_SCORE = 1e4

FF_CHUNK = 256
N_FF_CHUNKS = D_FF // FF_CHUNK
VMEM_LIMIT = 56 * 1024 * 1024


def _dot(a, b):
    return jnp.dot(a, b, preferred_element_type=F32)


def _dot_nt(a, b):
    return lax.dot_general(a, b, (((1,), (1,)), ((), ())), preferred_element_type=F32)


def _rms_mod(x, g, scale, shift):
    y = x * lax.rsqrt(jnp.mean(x * x, axis=-1, keepdims=True) + EPS)
    return (y * g) * (1.0 + scale) + shift


def _mod_kernel(c_ref, w_ref, b_ref, o_ref):
    c = c_ref[...]
    s = c * jax.nn.sigmoid(c)
    o_ref[...] = _dot(s.astype(BF16), w_ref[...].astype(BF16)) + b_ref[...]


def _mod_call(c_all, w_ada, b_ada):
    n = c_all.shape[0]
    tn = 1536
    return pl.pallas_call(
        _mod_kernel,
        out_shape=jax.ShapeDtypeStruct((n, 6 * D_MODEL), F32),
        grid=(6 * D_MODEL // tn,),
        in_specs=[pl.BlockSpec((n, D_MODEL), lambda j: (0, 0)),
                  pl.BlockSpec((D_MODEL, tn), lambda j: (0, j)),
                  pl.BlockSpec((1, tn), lambda j: (0, j))],
        out_specs=pl.BlockSpec((n, tn), lambda j: (0, j)),
        compiler_params=pltpu.CompilerParams(dimension_semantics=("arbitrary",), vmem_limit_bytes=VMEM_LIMIT),
        name="mod",
    )(c_all, w_ada, b_ada)


def _inproj_kernel(x_ref, mod_ref, g1_ref, wa_ref, lng_ref, lnb_ref, wsp_ref, bsp_ref, *refs, tm, prompt):
    x = x_ref[0]
    hb = _rms_mod(x, g1_ref[...], mod_ref[0, 1], mod_ref[0, 0]).astype(BF16)

    ga = jax.nn.gelu(_dot(hb, wa_ref[...]))
    u = ga[:, :A_WIDTH]
    v = ga[:, A_WIDTH:]
    vc = v - jnp.mean(v, axis=-1, keepdims=True)
    vn = vc * lax.rsqrt(jnp.mean(vc * vc, axis=-1, keepdims=True) + EPS) * lng_ref[...] + lnb_ref[...]
    vnb = vn.astype(BF16)

    if prompt:
        (wqT_ref, wkvT_ref, wrow_ref, wgT_ref,
         ya_ref, qT_ref, kv4T_ref, winT_ref, vselT_ref, vwinT_ref, kvc_ref, ksel_ref, kwin_ref, gzT_ref) = refs
    else:
        (wq_ref, wkvT_ref, wkvc_ref, wg_ref, ya_ref, q_ref, kvT_ref, kvc_ref, gz_ref, vn_ref) = refs
        vn_ref[0] = vn

    for c in range(tm // CHUNK):
        rows = slice(c * CHUNK, (c + 1) * CHUNK)
        s = jnp.concatenate(
            [_dot(wsp_ref[g], vnb[rows, g * 128:(g + 1) * 128]) for g in range(A_GROUPS)], axis=1)
        ya_ref[0, rows, :] = (u[rows] * (s + bsp_ref[...])).astype(BF16)

    kvT = _dot_nt(wkvT_ref[...], hb)
    if prompt:
        qT_ref[0] = (_dot_nt(wqT_ref[...], hb) * (HEAD_DIM ** -0.5)).astype(BF16)
        kv4T_ref[0] = kvT[:4 * 128]
        winT_ref[0] = kvT[4 * 128:]
        vselT_ref[0] = kvT[3 * 128:4 * 128].astype(BF16)
        vwinT_ref[0] = kvT[5 * 128:].astype(BF16)
        row = _dot(hb, wrow_ref[...])
        kvc_ref[0] = row[:, :256]
        ksel_ref[0] = row[:, 256:384].astype(BF16)
        kwin_ref[0] = row[:, 384:].astype(BF16)
        gzT_ref[0] = jax.nn.sigmoid(_dot_nt(wgT_ref[...], hb))
    else:
        q_ref[0] = (_dot(hb, wq_ref[...]) * (HEAD_DIM ** -0.5)).astype(BF16)
        kvT_ref[0] = kvT
        kvc_ref[0] = _dot(hb, wkvc_ref[...])
        gz_ref[0] = jax.nn.sigmoid(_dot(hb, wg_ref[...]))


def _inproj_call(x, mod, g1, wa, lng, lnb, wsp, bsp, proj_w, *, tm, prompt):
    G, T, _ = x.shape
    R = mod.shape[2]
    nt = T // tm
    const2 = lambda b, i: (0, 0)
    const3 = lambda b, i: (0, 0, 0)
    rows = lambda w: pl.BlockSpec((1, tm, w), lambda b, i: (b, i, 0))
    cols = lambda h: pl.BlockSpec((1, h, tm), lambda b, i: (b, 0, i))
    if prompt:
        outs = [((G, T, A_WIDTH), BF16, rows(A_WIDTH)),
                ((G, B_WIDTH, T), BF16, cols(B_WIDTH)),
                ((G, 512, T), F32, cols(512)),
                ((G, 256, T), F32, cols(256)),
                ((G, 128, T), BF16, cols(128)),
                ((G, 128, T), BF16, cols(128)),
                ((G, T, 256), F32, rows(256)),
                ((G, T, 128), BF16, rows(128)),
                ((G, T, 128), BF16, rows(128)),
                ((G, 32, T), F32, cols(32))]
    else:
        outs = [((G, T, A_WIDTH), BF16, rows(A_WIDTH)),
                ((G, T, B_WIDTH), BF16, rows(B_WIDTH)),
                ((G, KV_COLS, T), F32, cols(KV_COLS)),
                ((G, T, 256), F32, rows(256)),
                ((G, T, 128), F32, rows(128)),
                ((G, T, A_WIDTH), F32, rows(A_WIDTH))]
    return pl.pallas_call(
        functools.partial(_inproj_kernel, tm=tm, prompt=prompt),
        out_shape=[jax.ShapeDtypeStruct(s, d) for s, d, _ in outs],
        grid=(G, nt),
        in_specs=[pl.BlockSpec((1, tm, D_MODEL), lambda b, i: (b, i, 0)),
                  pl.BlockSpec((1, 6, R, D_MODEL), lambda b, i: (b, 0, 0, 0)),
                  pl.BlockSpec((1, D_MODEL), const2),
                  pl.BlockSpec(wa.shape, const2),
                  pl.BlockSpec((1, A_WIDTH), const2),
                  pl.BlockSpec((1, A_WIDTH), const2),
                  pl.BlockSpec(wsp.shape, const3),
                  pl.BlockSpec(bsp.shape, const2)] + [pl.BlockSpec(w.shape, const2) for w in proj_w],
        out_specs=[sp for _, _, sp in outs],
        compiler_params=pltpu.CompilerParams(dimension_semantics=("arbitrary", "arbitrary"),
                                             vmem_limit_bytes=VMEM_LIMIT),
        name="inproj",
    )(x, mod, g1, wa, lng, lnb, wsp, bsp, *proj_w)


def _cmp_tail(acc, b1, w2, b2):
    return _dot(jax.nn.gelu(acc + b1).astype(BF16), w2) + b2


def _cmp_rows_kernel(x_ref, pe_ref, w1_ref, b1_ref, w2_ref, b2_ref, o_ref, *, R):
    acc = jnp.zeros((R, 128), F32)
    for j in range(BLOCK):
        xj = x_ref[pl.ds(j, R, stride=BLOCK), :] + pe_ref[0, j:j + 1, :]
        acc = acc + _dot(xj.astype(BF16), w1_ref[0, j])
    o_ref[0] = _cmp_tail(acc, b1_ref[0], w2_ref[0], b2_ref[0])


def _cmp_rows_call(kvc2d, pe, w1, b1, w2, b2, *, R):
    nrows = kvc2d.shape[0] // BLOCK
    return pl.pallas_call(
        functools.partial(_cmp_rows_kernel, R=R),
        out_shape=jax.ShapeDtypeStruct((2, nrows, 128), F32),
        grid=(2, nrows // R),
        in_specs=[pl.BlockSpec((R * BLOCK, 128), lambda s, r: (r, s)),
                  pl.BlockSpec((1, BLOCK, 128), lambda s, r: (s, 0, 0)),
                  pl.BlockSpec((1, BLOCK, 128, 128), lambda s, r: (s, 0, 0, 0)),
                  pl.BlockSpec((1, 1, 128), lambda s, r: (s, 0, 0)),
                  pl.BlockSpec((1, 128, 128), lambda s, r: (s, 0, 0)),
                  pl.BlockSpec((1, 1, 128), lambda s, r: (s, 0, 0))],
        out_specs=pl.BlockSpec((1, R, 128), lambda s, r: (s, r, 0)),
        compiler_params=pltpu.CompilerParams(dimension_semantics=("arbitrary", "arbitrary"),
                                             vmem_limit_bytes=VMEM_LIMIT),
        name="cmp_rows",
    )(kvc2d, pe, w1, b1, w2, b2)


def _cmp_pages_kernel(x_ref, pe_ref, w1_ref, b1_ref, w2_ref, b2_ref, o_ref, *, Pt):
    for s in range(2):
        acc = jnp.zeros((2 * Pt, 128), F32)
        for d in range(HEAD_DIM):
            xd = jnp.concatenate([x_ref[:, s * 128 + d, :], x_ref[:, s * 128 + HEAD_DIM + d, :]], axis=0)
            xd = xd + pe_ref[s, d:d + 1, :]
            acc = acc + _dot(xd.astype(BF16), w1_ref[s, d])
        r = _cmp_tail(acc, b1_ref[s], w2_ref[s], b2_ref[s])
        o_ref[:, (2 * s) * 128:(2 * s + 1) * 128] = r[:Pt]
        o_ref[:, (2 * s + 1) * 128:(2 * s + 2) * 128] = r[Pt:]


def _cmp_pages_call(pages, pe, w1, b1, w2, b2, *, Pt):
    P = pages.shape[0]
    full = lambda i: (0, 0, 0)
    return pl.pallas_call(
        functools.partial(_cmp_pages_kernel, Pt=Pt),
        out_shape=jax.ShapeDtypeStruct((P, 512), F32),
        grid=(P // Pt,),
        in_specs=[pl.BlockSpec((Pt, 256, 128), lambda i: (i, 0, 0)),
                  pl.BlockSpec(pe.shape, full),
                  pl.BlockSpec(w1.shape, lambda i: (0, 0, 0, 0)),
                  pl.BlockSpec(b1.shape, full),
                  pl.BlockSpec(w2.shape, full),
                  pl.BlockSpec(b2.shape, full)],
        out_specs=pl.BlockSpec((Pt, 512), lambda i: (i, 0)),
        compiler_params=pltpu.CompilerParams(dimension_semantics=("arbitrary",), vmem_limit_bytes=VMEM_LIMIT),
        name="cmp_pages",
    )(pages, pe, w1, b1, w2, b2)


def _pair_schedule(T, tq, kc):
    js, cs = [], []
    for j in range(T // tq):
        for c in range(((j + 1) * tq - 1) // kc + 1):
            js.append(j)
            cs.append(c)
    return js, cs


def _p_attn_kernel(jt_ref, ct_ref, qT_ref, ksel_ref, kwin_ref, vselT_ref, vwinT_ref, kc_ref, vcT_ref, gzT_ref,
                   yb_ref, sel_s, ocmp_s, m_s, l_s, acc_s, *, tq, kc, T):
    p = pl.program_id(1)
    j = jt_ref[p]
    c = ct_ref[p]
    nb = T // BLOCK
    N = GQA * tq
    q0 = j * tq
    k0 = c * kc
    c_last = ((j + 1) * tq - 1) // kc

    def q_pad(h):
        qT = jnp.concatenate([qT_ref[0, (h * GQA + g) * HEAD_DIM:(h * GQA + g + 1) * HEAD_DIM, :]
                              for g in range(GQA)], axis=1)
        z = jnp.zeros_like(qT)
        return jnp.concatenate([qT if k == h else z for k in range(KV_HEADS)], axis=0)

    @pl.when(c == 0)
    def _():
        m_s[...] = jnp.full(m_s.shape, NEG, F32)
        l_s[...] = jnp.zeros(l_s.shape, F32)
        acc_s[...] = jnp.zeros(acc_s.shape, F32)
        tok_n = q0 + (lax.broadcasted_iota(jnp.int32, (1, N), 1) & (tq - 1))
        tok_1 = q0 + lax.broadcasted_iota(jnp.int32, (1, tq), 1)
        blk = lax.broadcasted_iota(jnp.int32, (nb, 1), 0)
        avail = (blk + 1) * BLOCK <= tok_n + 1
        cur = tok_1 >> 6
        forced = (blk == 0) | (blk == cur) | (blk == cur - 1)
        future = blk > cur
        kcb = kc_ref[0].astype(BF16)
        for h in range(KV_HEADS):
            s = jnp.where(avail, _dot(kcb, q_pad(h)), NEG)
            m = jnp.max(s, axis=0, keepdims=True)
            e = jnp.where(avail, jnp.exp(s - m), 0.0)
            pr = e / jnp.maximum(jnp.sum(e, axis=0, keepdims=True), 1e-30)
            ocmp_s[h] = _dot(vcT_ref[0, h * HEAD_DIM:(h + 1) * HEAD_DIM, :].astype(BF16), pr.astype(BF16))
            imp = pr[:, 0:tq]
            for g in range(1, GQA):
                imp = imp + pr[:, g * tq:(g + 1) * tq]
            imp = jnp.where(forced, FORCE_SCORE, jnp.where(future, -1.0, imp))
            rows = []
            for i in range(nb):
                vi = imp[i:i + 1, :]
                ahead = (imp > vi) | ((imp == vi) & (blk < i))
                cnt = jnp.sum(ahead.astype(F32), axis=0, keepdims=True)
                rows.append((cnt < float(N_SELECT)).astype(F32))
            sel_s[h] = jnp.concatenate(rows + [jnp.zeros((128 - nb, tq), F32)], axis=0).astype(BF16)

    kpos = k0 + lax.broadcasted_iota(jnp.int32, (kc, tq), 0)
    tok = q0 + lax.broadcasted_iota(jnp.int32, (kc, tq), 1)

    def update(idx, s, vT):
        m_old = m_s[idx]
        m_new = jnp.maximum(m_old, jnp.max(s, axis=0, keepdims=True))
        alpha = jnp.exp(m_old - m_new)
        pe = jnp.exp(s - m_new)
        l_s[idx] = alpha * l_s[idx] + jnp.sum(pe, axis=0, keepdims=True)
        acc_s[idx] = alpha * acc_s[idx] + _dot(vT, pe.astype(BF16))
        m_s[idx] = m_new

    blk_of_key = (k0 >> 6) + (lax.broadcasted_iota(jnp.int32, (kc, 128), 0) >> 6)
    expand = (lax.broadcasted_iota(jnp.int32, (kc, 128), 1) == blk_of_key).astype(BF16)
    causal = kpos <= tok
    for h in range(KV_HEADS):
        hit = _dot(expand, sel_s[h])
        bias = jnp.where((hit > 0.5) & causal, 0.0, NEG)
        s = _dot(ksel_ref[0], q_pad(h)) + jnp.concatenate([bias] * GQA, axis=1)
        update(h, s, vselT_ref[0, h * HEAD_DIM:(h + 1) * HEAD_DIM, :])

    @pl.when((c + 1) * kc - 1 >= q0 - (WINDOW - 1))
    def _():
        diff = tok - kpos
        bias = jnp.where((diff >= 0) & (diff < WINDOW), 0.0, NEG)
        bias_n = jnp.concatenate([bias] * GQA, axis=1)
        for h in range(KV_HEADS):
            s = _dot(kwin_ref[0], q_pad(h)) + bias_n
            update(KV_HEADS + h, s, vwinT_ref[0, h * HEAD_DIM:(h + 1) * HEAD_DIM, :])

    @pl.when(c == c_last)
    def _():
        gz = gzT_ref[0]
        outs = []
        for h in range(KV_HEADS):
            o_sel = acc_s[h] / jnp.maximum(l_s[h], 1e-30)
            o_win = acc_s[KV_HEADS + h] / jnp.maximum(l_s[KV_HEADS + h], 1e-30)
            o_cmp = ocmp_s[h]
            for g in range(GQA):
                col = h * GQA + g
                ls = slice(g * tq, (g + 1) * tq)
                outs.append(gz[col:col + 1] * o_cmp[:, ls] + gz[B_HEADS + col:B_HEADS + col + 1] * o_sel[:, ls]
                            + gz[2 * B_HEADS + col:2 * B_HEADS + col + 1] * o_win[:, ls])
        yb_ref[0] = jnp.concatenate(outs, axis=0).T.astype(BF16)


def _p_attn_call(qT, ksel, kwin, vselT, vwinT, kc_all, vcT, gzT, *, tq, kc):
    B, _, T = qT.shape
    nb = T // BLOCK
    N = GQA * tq
    js, cs = _pair_schedule(T, tq, kc)
    jt = jnp.asarray(js, jnp.int32)
    ct = jnp.asarray(cs, jnp.int32)
    return pl.pallas_call(
        functools.partial(_p_attn_kernel, tq=tq, kc=kc, T=T),
        out_shape=jax.ShapeDtypeStruct((B, T, B_WIDTH), BF16),
        grid_spec=pltpu.PrefetchScalarGridSpec(
            num_scalar_prefetch=2,
            grid=(B, len(js)),
            in_specs=[pl.BlockSpec((1, B_WIDTH, tq), lambda b, p, jt, ct: (b, 0, jt[p])),
                      pl.BlockSpec((1, kc, 128), lambda b, p, jt, ct: (b, ct[p], 0)),
                      pl.BlockSpec((1, kc, 128), lambda b, p, jt, ct: (b, ct[p], 0)),
                      pl.BlockSpec((1, 128, kc), lambda b, p, jt, ct: (b, 0, ct[p])),
                      pl.BlockSpec((1, 128, kc), lambda b, p, jt, ct: (b, 0, ct[p])),
                      pl.BlockSpec((1, nb, 128), lambda b, p, jt, ct: (b, 0, 0)),
                      pl.BlockSpec((1, 128, nb), lambda b, p, jt, ct: (b, 0, 0)),
                      pl.BlockSpec((1, 32, tq), lambda b, p, jt, ct: (b, 0, jt[p]))],
            out_specs=pl.BlockSpec((1, tq, B_WIDTH), lambda b, p, jt, ct: (b, jt[p], 0)),
            scratch_shapes=[pltpu.VMEM((KV_HEADS, 128, tq), BF16),
                            pltpu.VMEM((KV_HEADS, HEAD_DIM, N), F32),
                            pltpu.VMEM((2 * KV_HEADS, 1, N), F32),
                            pltpu.VMEM((2 * KV_HEADS, 1, N), F32),
                            pltpu.VMEM((2 * KV_HEADS, HEAD_DIM, N), F32)]),
        compiler_params=pltpu.CompilerParams(dimension_semantics=("arbitrary", "arbitrary"),
                                             vmem_limit_bytes=VMEM_LIMIT),
        name="p_attn",
    )(jt, ct, qT, ksel, kwin, vselT, vwinT, kc_all, vcT, gzT)


def _s_cmp_kernel(pt_ref, q_ref, pool_ref, newc_ref, ocmp_ref, imp_ref, gath, *, n_pages, past_len):
    b = pl.program_id(0)

    def gather(jp, carry):
        gath[pl.ds(jp, 1), :] = pool_ref[pl.ds(pt_ref[b, jp], 1), :]
        return carry

    lax.fori_loop(0, n_pages, gather, 0)

    rows = 8 * GQA
    t_row = lax.broadcasted_iota(jnp.int32, (rows, 1), 0) & 7
    pos = past_len + t_row
    lane = lax.broadcasted_iota(jnp.int32, (1, 2 * n_pages), 1)
    blk = 2 * (lane & (n_pages - 1)) + (lane >> 7)
    new_blk = past_len // BLOCK
    avail = (blk + 1) * BLOCK <= pos + 1
    avail_new = (new_blk + 1) * BLOCK <= pos + 1
    for h in range(KV_HEADS):
        qh = q_ref[0, h]
        kc = [gath[:, (h * 2 + k) * 64:(h * 2 + k + 1) * 64].astype(BF16) for k in range(2)]
        vc = [gath[:, (2 + h) * 128 + k * 64:(2 + h) * 128 + (k + 1) * 64].astype(BF16) for k in range(2)]
        kc_new = newc_ref[0, :, h * 128:h * 128 + 64].astype(BF16).astype(F32)
        vc_new = newc_ref[0, :, (2 + h) * 128:(2 + h) * 128 + 64].astype(BF16).astype(F32)
        s = jnp.where(avail, jnp.concatenate([_dot_nt(qh, kc[0]), _dot_nt(qh, kc[1])], axis=1), NEG)
        s_new = jnp.where(avail_new, jnp.sum(qh.astype(F32) * kc_new, axis=-1, keepdims=True), NEG)
        m = jnp.maximum(jnp.max(s, axis=-1, keepdims=True), s_new)
        e = jnp.where(avail, jnp.exp(s - m), 0.0)
        e_new = jnp.where(avail_new, jnp.exp(s_new - m), 0.0)
        den = jnp.maximum(jnp.sum(e, axis=-1, keepdims=True) + e_new, 1e-30)
        p = e / den
        p_new = e_new / den
        pb = p.astype(BF16)
        ocmp_ref[0, h] = (_dot(pb[:, :n_pages], vc[0]) + _dot(pb[:, n_pages:], vc[1])
                          + p_new.astype(BF16).astype(F32) * vc_new)
        imp = p[0:8] + p[8:16] + p[16:24] + p[24:32]
        cur = (past_len + lax.broadcasted_iota(jnp.int32, (8, 1), 0)) >> 6
        forced = (blk == 0) | (blk == cur) | (blk == cur - 1)
        imp_ref[0, h] = jnp.where(forced, FORCE_SCORE, jnp.where(blk > cur, -1.0, imp))


def _s_cmp_call(page_table, q_r, pool, newc, *, past_len):
    nbatch, n_pages = page_table.shape
    P = pool.shape[0]
    return pl.pallas_call(
        functools.partial(_s_cmp_kernel, n_pages=n_pages, past_len=past_len),
        out_shape=[jax.ShapeDtypeStruct((nbatch, KV_HEADS, 32, HEAD_DIM), F32),
                   jax.ShapeDtypeStruct((nbatch, KV_HEADS, 8, 2 * n_pages), F32)],
        grid_spec=pltpu.PrefetchScalarGridSpec(
            num_scalar_prefetch=1,
            grid=(nbatch,),
            in_specs=[pl.BlockSpec((1, KV_HEADS, 32, HEAD_DIM), lambda b, pt: (b, 0, 0, 0)),
                      pl.BlockSpec((P, 512), lambda b, pt: (0, 0)),
                      pl.BlockSpec((1, 1, 512), lambda b, pt: (b, 0, 0))],
            out_specs=[pl.BlockSpec((1, KV_HEADS, 32, HEAD_DIM), lambda b, pt: (b, 0, 0, 0)),
                       pl.BlockSpec((1, KV_HEADS, 8, 2 * n_pages), lambda b, pt: (b, 0, 0, 0))],
            scratch_shapes=[pltpu.VMEM((n_pages, 512), F32)]),
        compiler_params=pltpu.CompilerParams(dimension_semantics=("arbitrary",), vmem_limit_bytes=VMEM_LIMIT),
        name="s_cmp",
    )(page_table, q_r, pool, newc)


def _s_select_kernel(imp_ref, idx_ref, *, n_pages, n_pick):
    v = imp_ref[...]
    rows = v.shape[0]
    lane = lax.broadcasted_iota(jnp.int32, (1, 2 * n_pages), 1)
    blk = (2 * (lane & (n_pages - 1)) + (lane >> 7)).astype(F32)
    out_lane = lax.broadcasted_iota(jnp.int32, (1, 128), 1)
    out = jnp.full((rows, 128), float(2 * n_pages), F32)
    for k in range(n_pick):
        m = jnp.max(v, axis=-1, keepdims=True)
        pick = jnp.min(jnp.where(v == m, blk, 1e9), axis=-1, keepdims=True)
        out = jnp.where(out_lane == k, pick, out)
        v = jnp.where(blk == pick, NEG, v)
    idx_ref[...] = out.astype(jnp.int32)


def _s_select_call(imp2d, *, n_pages, n_pick):
    rows = imp2d.shape[0]
    return pl.pallas_call(
        functools.partial(_s_select_kernel, n_pages=n_pages, n_pick=n_pick),
        out_shape=jax.ShapeDtypeStruct((rows, 128), jnp.int32),
        grid=(1,),
        in_specs=[pl.BlockSpec(imp2d.shape, lambda i: (0, 0))],
        out_specs=pl.BlockSpec((rows, 128), lambda i: (0, 0)),
        compiler_params=pltpu.CompilerParams(dimension_semantics=("arbitrary",), vmem_limit_bytes=VMEM_LIMIT),
        name="s_select",
    )(imp2d)


def _s_attn_kernel(pt_ref, idx_ref, q_ref, kvn_ref, win_ref, gz_ref, ocmp_ref, cache_ref, yb_ref,
                   kbuf, vbuf, sem, *, n_tok, n_gather, past_len):
    b = pl.program_id(0)
    span = n_gather * 128

    def tile_copies(t, h, jj):
        n = idx_ref[((b * KV_HEADS + h) * n_tok + t) * N_SELECT + jj]
        page = pt_ref[b, n >> 1]
        kc = pltpu.make_async_copy(cache_ref.at[page, pl.ds(2 * 128 + h * 64, 64), :],
                                   kbuf.at[h, t, :, pl.ds(jj * 128, 128)], sem.at[0])
        vc = pltpu.make_async_copy(cache_ref.at[page, pl.ds(3 * 128 + h * 64, 64), :],
                                   vbuf.at[h, t, :, pl.ds(jj * 128, 128)], sem.at[1])
        return kc, vc

    for h in range(KV_HEADS):
        for t in range(n_tok):
            for jj in range(n_gather):
                kc, vc = tile_copies(t, h, jj)
                kc.start()
                vc.start()

    rows = 8 * GQA
    t_row = lax.broadcasted_iota(jnp.int32, (rows, 1), 0) & 7
    gz = gz_ref[0]

    lane_n = lax.broadcasted_iota(jnp.int32, (1, 128), 1)
    tok_shift = n_tok.bit_length() - 1
    new_ok = ((lane_n >> tok_shift) == b) & ((lane_n & (n_tok - 1)) <= t_row)
    lane_w = lax.broadcasted_iota(jnp.int32, (1, WINDOW), 1)
    win_ok = lane_w > t_row

    def softmax_pv(parts):
        m = None
        for sc, mk, _ in parts:
            mm = jnp.max(jnp.where(mk, sc, NEG), axis=-1, keepdims=True)
            m = mm if m is None else jnp.maximum(m, mm)
        den = jnp.zeros((rows, 1), F32)
        o = jnp.zeros((rows, HEAD_DIM), F32)
        for sc, mk, vts in parts:
            e = jnp.where(mk, jnp.exp(jnp.where(mk, sc, NEG) - m), 0.0)
            den = den + jnp.sum(e, axis=-1, keepdims=True)
            for rmask, vt in vts:
                er = e if rmask is None else jnp.where(rmask, e, 0.0)
                o = o + _dot_nt(er.astype(BF16), vt)
        return o / jnp.maximum(den, 1e-30)

    win_out = []
    for h in range(KV_HEADS):
        qh = q_ref[0, h]
        kw = win_ref[0, h * 64:(h + 1) * 64, :].astype(BF16)
        vw = win_ref[0, 128 + h * 64:128 + (h + 1) * 64, :].astype(BF16)
        kwn = kvn_ref[4 * 128 + h * 64:4 * 128 + (h + 1) * 64, :].astype(BF16)
        vwn = kvn_ref[5 * 128 + h * 64:5 * 128 + (h + 1) * 64, :].astype(BF16)
        win_out.append(softmax_pv([(_dot(qh, kw), win_ok, [(None, vw)]),
                                   (_dot(qh, kwn), new_ok, [(None, vwn)])]))

    for h in range(KV_HEADS):
        for t in range(n_tok):
            for jj in range(n_gather):
                kc, vc = tile_copies(t, h, jj)
                kc.wait()
                vc.wait()

    lane_s = lax.broadcasted_iota(jnp.int32, (1, span), 1)
    for h in range(KV_HEADS):
        qh = q_ref[0, h]
        sc = jnp.zeros((rows, span), F32)
        half = jnp.zeros((rows, span), jnp.int32)
        for t in range(n_tok):
            st = _dot(qh, kbuf[h, t].astype(BF16))
            hrow = jnp.zeros((1, span), jnp.int32)
            for jj in range(n_gather):
                n = idx_ref[((b * KV_HEADS + h) * n_tok + t) * N_SELECT + jj]
                hrow = jnp.where((lane_s >> 7) == jj, n & 1, hrow)
            sc = jnp.where(t_row == t, st, sc)
            half = jnp.where(t_row == t, hrow, half)
        ok = ((lane_s >> 6) & 1) == half
        ksn = kvn_ref[2 * 128 + h * 64:2 * 128 + (h + 1) * 64, :].astype(BF16)
        vsn = kvn_ref[3 * 128 + h * 64:3 * 128 + (h + 1) * 64, :].astype(BF16)
        vts = [(t_row == t, vbuf[h, t].astype(BF16)) for t in range(n_tok)]
        o_sel = softmax_pv([(sc, ok, vts), (_dot(qh, ksn), new_ok, [(None, vsn)])])
        yb_ref[0, h] = (gz[h, :, 0:1] * ocmp_ref[0, h] + gz[h, :, 1:2] * o_sel + gz[h, :, 2:3] * win_out[h])


def _s_attn_call(page_table, idx_flat, q_r, kvT_new, winT, gz_r, ocmp, cache_pages, *, n_tok, past_len):
    nbatch = page_table.shape[0]
    n_gather = N_SELECT - 1
    span = n_gather * 128
    blk4 = lambda b, pt, ix: (b, 0, 0, 0)
    return pl.pallas_call(
        functools.partial(_s_attn_kernel, n_tok=n_tok, n_gather=n_gather, past_len=past_len),
        out_shape=jax.ShapeDtypeStruct((nbatch, KV_HEADS, 32, HEAD_DIM), F32),
        grid_spec=pltpu.PrefetchScalarGridSpec(
            num_scalar_prefetch=2,
            grid=(nbatch,),
            in_specs=[pl.BlockSpec((1, KV_HEADS, 32, HEAD_DIM), blk4),
                      pl.BlockSpec(kvT_new.shape, lambda b, pt, ix: (0, 0)),
                      pl.BlockSpec((1, 256, WINDOW), lambda b, pt, ix: (b, 0, 0)),
                      pl.BlockSpec((1, KV_HEADS, 32, 128), blk4),
                      pl.BlockSpec((1, KV_HEADS, 32, HEAD_DIM), blk4),
                      pl.BlockSpec(memory_space=pl.ANY)],
            out_specs=pl.BlockSpec((1, KV_HEADS, 32, HEAD_DIM), blk4),
            scratch_shapes=[pltpu.VMEM((KV_HEADS, n_tok, HEAD_DIM, span), F32),
                            pltpu.VMEM((KV_HEADS, n_tok, HEAD_DIM, span), F32),
                            pltpu.SemaphoreType.DMA((2,))]),
        compiler_params=pltpu.CompilerParams(dimension_semantics=("arbitrary",), vmem_limit_bytes=VMEM_LIMIT),
        name="s_attn",
    )(page_table, idx_flat, q_r, kvT_new, winT, gz_r, ocmp, cache_pages)


def _post_kernel(*refs, tm, seq_len, has_prev):
    it = iter(refs)
    x_ref, mod_ref, ya_ref, yb_ref = next(it), next(it), next(it), next(it)
    g1_ref, g2_ref, gf_ref = next(it), next(it), next(it)
    wm_ref, wba_ref, wbb_ref, wo_ref, wup_ref, wdn_ref, cw_ref = (next(it) for _ in range(7))
    p1_ref = next(it) if has_prev else None
    p2_ref = next(it) if has_prev else None
    y_ref, up_ref = next(it), next(it)
    carry, h2_s, acc_s = next(it), next(it), next(it)

    i = pl.program_id(1)
    x = x_ref[0]
    shift1, scale1, gate1 = mod_ref[0, 0], mod_ref[0, 1], mod_ref[0, 2]
    shift2, scale2, gate2 = mod_ref[0, 3], mod_ref[0, 4], mod_ref[0, 5]

    hb = _rms_mod(x, g1_ref[...], scale1, shift1).astype(BF16)
    gates = jax.nn.sigmoid(_dot(hb, wm_ref[...]))
    mix = (gates[:, :D_MODEL] * _dot(ya_ref[0], wba_ref[...])
           + gates[:, D_MODEL:] * _dot(yb_ref[0], wbb_ref[...]))
    x1 = x + gate1 * _dot(mix.astype(BF16), wo_ref[...])
    h2_s[...] = _rms_mod(x1, g2_ref[...], scale2, shift2).astype(BF16)
    acc_s[...] = jnp.zeros(acc_s.shape, F32)

    if not has_prev:
        @pl.when(i == 0)
        def _():
            carry[...] = jnp.zeros(carry.shape, F32)

    row = lax.broadcasted_iota(jnp.int32, (tm, 1), 0)
    row8 = lax.broadcasted_iota(jnp.int32, (8, 1), 0)

    def conv_half(idx):
        up = _dot(h2_s[...], wup_ref[idx])
        r1 = pltpu.roll(up, 1, 0)
        r2 = pltpu.roll(up, 2, 0)
        if has_prev:
            s1 = jnp.where((row & (seq_len - 1)) == 0, p1_ref[idx], r1)
            s2 = jnp.where((row & (seq_len - 1)) < 2, p2_ref[idx], r2)
            up_ref[idx] = up
        else:
            prev = carry[idx]
            t1 = jnp.where(row8 == 0, pltpu.roll(prev, 1, 0), r1[:8])
            t2 = jnp.where(row8 < 2, pltpu.roll(prev, 2, 0), r2[:8])
            s1 = jnp.concatenate([t1, r1[8:]], axis=0)
            s2 = jnp.concatenate([t2, r2[8:]], axis=0)
            carry[idx] = up[tm - 8:]
            up_ref[0, idx] = up[tm - 8:]
        cw = cw_ref[idx]
        return cw[3:4] + cw[0:1] * s2 + cw[1:2] * s1 + cw[2:3] * up

    def ff_body(c, cr):
        a = conv_half(c)
        gv = conv_half(N_FF_CHUNKS + c)
        act = (jax.nn.gelu(a) * gv).astype(BF16)
        acc_s[...] += _dot(act, wdn_ref[c])
        return cr

    lax.fori_loop(0, N_FF_CHUNKS, ff_body, 0)
    x2 = x1 + gate2 * acc_s[...]
    y_ref[0] = x2 * lax.rsqrt(jnp.mean(x2 * x2, axis=-1, keepdims=True) + EPS) * gf_ref[...]


def _post_call(x, mod, ya, yb, g1, g2, gf, wm, wba, wbb, wo, wup, wdn, cw, prev=None, *, tm, seq_len):
    G, T, _ = x.shape
    R = mod.shape[2]
    nt = T // tm
    has_prev = prev is not None
    single = pl.Buffered(1)
    c2 = lambda b, i: (0, 0)
    c3 = lambda b, i: (0, 0, 0)
    in_specs = [pl.BlockSpec((1, tm, D_MODEL), lambda b, i: (b, i, 0)),
                pl.BlockSpec((1, 6, R, D_MODEL), lambda b, i: (b, 0, 0, 0)),
                pl.BlockSpec((1, tm, A_WIDTH), lambda b, i: (b, i, 0)),
                pl.BlockSpec((1, tm, B_WIDTH), lambda b, i: (b, i, 0)),
                pl.BlockSpec((1, D_MODEL), c2),
                pl.BlockSpec((1, D_MODEL), c2),
                pl.BlockSpec((1, D_MODEL), c2),
                pl.BlockSpec(wm.shape, c2, pipeline_mode=single),
                pl.BlockSpec(wba.shape, c2, pipeline_mode=single),
                pl.BlockSpec(wbb.shape, c2, pipeline_mode=single),
                pl.BlockSpec(wo.shape, c2, pipeline_mode=single),
                pl.BlockSpec(wup.shape, c3, pipeline_mode=single),
                pl.BlockSpec(wdn.shape, c3, pipeline_mode=single),
                pl.BlockSpec(cw.shape, c3, pipeline_mode=single)]
    args = [x, mod, ya, yb, g1, g2, gf, wm, wba, wbb, wo, wup, wdn, cw]
    if has_prev:
        in_specs += [pl.BlockSpec(prev[0].shape, c3), pl.BlockSpec(prev[1].shape, c3)]
        args += list(prev)
        up_shape = jax.ShapeDtypeStruct((2 * N_FF_CHUNKS, T, FF_CHUNK), F32)
        up_spec = pl.BlockSpec((2 * N_FF_CHUNKS, tm, FF_CHUNK), lambda b, i: (0, i, 0))
    else:
        up_shape = jax.ShapeDtypeStruct((G, 2 * N_FF_CHUNKS, 8, FF_CHUNK), F32)
        up_spec = pl.BlockSpec((1, 2 * N_FF_CHUNKS, 8, FF_CHUNK), lambda b, i: (b, 0, 0, 0))
    return pl.pallas_call(
        functools.partial(_post_kernel, tm=tm, seq_len=seq_len, has_prev=has_prev),
        out_shape=[jax.ShapeDtypeStruct((G, T, D_MODEL), F32), up_shape],
        grid=(G, nt),
        in_specs=in_specs,
        out_specs=[pl.BlockSpec((1, tm, D_MODEL), lambda b, i: (b, i, 0)), up_spec],
        scratch_shapes=[pltpu.VMEM((2 * N_FF_CHUNKS, 8, FF_CHUNK), F32),
                        pltpu.VMEM((tm, D_MODEL), BF16),
                        pltpu.VMEM((tm, D_MODEL), F32)],
        compiler_params=pltpu.CompilerParams(dimension_semantics=("arbitrary", "arbitrary"),
                                             vmem_limit_bytes=VMEM_LIMIT),
        name="post",
    )(*args)


def _chunk_cols(a, n):
    return a.reshape(a.shape[0], n, FF_CHUNK).transpose(1, 0, 2)


def kernel(x_prompt, x_sample, cache_kv, state_kv_win, state_ffn_conv, page_table, c_prompt, c_sample, w_ada, b_ada, g_norm1, w_in, ln_v_g, ln_v_b, w_spatial, b_spatial, cmp_pe, cmp_w1, cmp_b1, cmp_w2, cmp_b2, w_branch_a, w_branch_b, w_out, g_norm2, w_up, w_conv, b_conv, w_down, g_final):
    B, T, _ = x_prompt.shape
    SB, ST, _ = x_sample.shape
    n_pool, page_size = cache_kv.shape[1], cache_kv.shape[2]
    n_pages = page_table.shape[1]
    past_len = n_pages * page_size
    lbuf = state_kv_win.shape[2]
    assert cache_kv.shape[0] == 1 and page_size == 128 and lbuf == WINDOW and SB * ST == 128 and ST <= 8
    assert past_len % BLOCK == 0 and n_pages == 128 and T % 512 == 0 and ST & (ST - 1) == 0 and ST >= 2

    win = w_in[0]
    wa = win[:, :2 * A_WIDTH].astype(BF16)
    o = 2 * A_WIDTH
    wq = win[:, o:o + B_WIDTH].astype(BF16)
    o += B_WIDTH
    wkv = win[:, o:o + KV_COLS]
    wkvT = wkv.T.astype(BF16)
    wkvc = wkv[:, :256].astype(BF16)
    o += KV_COLS
    wg = jnp.pad(win[:, o:o + 3 * B_HEADS], ((0, 0), (0, 128 - 3 * B_HEADS))).astype(BF16)
    wgT = jnp.pad(win[:, o:o + 3 * B_HEADS].T, ((0, 32 - 3 * B_HEADS), (0, 0))).astype(BF16)
    wqT = wq.T
    wrow = jnp.concatenate([wkv[:, :3 * 128], wkv[:, 4 * 128:5 * 128]], axis=1).astype(BF16)
    o += 3 * B_HEADS
    wm = win[:, o:].astype(BF16)
    g1 = g_norm1[0][None]
    g2 = g_norm2[0][None]
    gf = g_final[None]
    lng = ln_v_g[0][None]
    lnb = ln_v_b[0][None]

    ws = w_spatial[0]
    bs = b_spatial[0]
    wsp_p = jnp.tril(ws).astype(BF16)
    bsp_p = jnp.repeat(bs.T, 128, axis=1)
    w4 = jnp.tril(ws[:, :ST, :ST])
    eye_s = jnp.eye(128 // ST, dtype=F32)
    wsp_s = jnp.einsum('ab,gts->gatbs', eye_s, w4).reshape(A_GROUPS, 128, 128).astype(BF16)
    bsp_s = jnp.repeat(jnp.tile(bs[:, :ST].T, (128 // ST, 1)), 128, axis=1)

    eye2 = jnp.eye(2, dtype=F32)
    w1, pe = cmp_w1[0], cmp_pe[0]
    w1_rows = jnp.einsum('kl,sidh->sikdlh', eye2, w1).reshape(2, BLOCK, 128, 128).astype(BF16)
    pe_rows = jnp.tile(pe, (1, 1, 2))
    w1_pages = jnp.einsum('bc,sidh->sdbich', eye2, w1).reshape(2, HEAD_DIM, 128, 128).astype(BF16)
    pe_pages = jnp.tile(pe.transpose(0, 2, 1), (1, 1, 2))
    b1t = jnp.tile(cmp_b1[0], (1, 2))[:, None, :]
    b2t = jnp.tile(cmp_b2[0], (1, 2))[:, None, :]
    w2bd = jnp.einsum('kl,shd->skhld', eye2, cmp_w2[0]).reshape(2, 128, 128).astype(BF16)

    wba = w_branch_a[0].astype(BF16)
    wbb = w_branch_b[0].astype(BF16)
    wo = w_out[0].astype(BF16)
    wup = _chunk_cols(w_up[0], 2 * N_FF_CHUNKS).astype(BF16)
    wdn = w_down[0].reshape(N_FF_CHUNKS, FF_CHUNK, D_MODEL).astype(BF16)
    cw = jnp.concatenate([w_conv[0], b_conv[0][None], jnp.zeros((4, F2), F32)], axis=0)
    cw = _chunk_cols(cw, 2 * N_FF_CHUNKS)

    mod = _mod_call(jnp.concatenate([c_prompt, c_sample], axis=0), w_ada[0], b_ada)
    mod_p = mod[:B].reshape(B, 6, 1, D_MODEL)
    mod_s = jnp.repeat(mod[B:], ST, axis=0).reshape(SB * ST, 6, D_MODEL).transpose(1, 0, 2)[None]

    ya_p, qT_p, kv4T_p, winT_p, vselT_p, vwinT_p, kvc_p, ksel_p, kwin_p, gzT_p = _inproj_call(
        x_prompt, mod_p, g1, wa, lng, lnb, wsp_p, bsp_p, (wqT, wkvT, wrow, wgT), tm=512, prompt=True)
    xs2 = x_sample.reshape(1, SB * ST, D_MODEL)
    ya_s, q_s, kvT_s, kvc_s, gz_s, vn_s = _inproj_call(
        xs2, mod_s, g1, wa, lng, lnb, wsp_s, bsp_s, (wq, wkvT, wkvc, wg), tm=128, prompt=False)

    nb = T // BLOCK
    cmp_p = _cmp_rows_call(kvc_p.reshape(B * T, 256), pe_rows, w1_rows, b1t, w2bd, b2t, R=128)
    kc_p = cmp_p[0].reshape(B, nb, 128)
    vcT_p = cmp_p[1].reshape(B, nb, 128).transpose(0, 2, 1)
    yb_p = _p_attn_call(qT_p, ksel_p, kwin_p, vselT_p, vwinT_p, kc_p, vcT_p, gzT_p, tq=512, kc=512)

    cache_pages = jnp.transpose(cache_kv[0], (0, 2, 3, 4, 1)).reshape(n_pool, 512, page_size)
    pool = _cmp_pages_call(cache_pages, pe_pages, w1_pages, b1t, w2bd, b2t, Pt=64)
    kv_rows_s = jnp.transpose(kvT_s[0], (1, 0)).reshape(SB, ST, KV_COLS)
    newblk = jnp.pad(jnp.transpose(kv_rows_s[:, :, :256], (0, 2, 1)), ((0, 0), (0, 0), (0, page_size - ST)))
    newc = _cmp_pages_call(newblk, pe_pages, w1_pages, b1t, w2bd, b2t, Pt=SB)

    q_r = jnp.pad(q_s.reshape(SB, ST, KV_HEADS, GQA, HEAD_DIM).transpose(0, 2, 3, 1, 4),
                  ((0, 0), (0, 0), (0, 0), (0, 8 - ST), (0, 0))).reshape(SB, KV_HEADS, 8 * GQA, HEAD_DIM)
    ocmp_s, imp_s = _s_cmp_call(page_table, q_r, pool, newc.reshape(SB, 1, 512), past_len=past_len)
    idx = _s_select_call(imp_s.reshape(SB * KV_HEADS * 8, 2 * n_pages), n_pages=n_pages, n_pick=N_SELECT - 1)
    idx_flat = idx.reshape(SB, KV_HEADS, 8, 128)[:, :, :ST, :N_SELECT].reshape(-1)
    winT = jnp.transpose(state_kv_win[0], (0, 2, 3, 4, 1)).reshape(SB, 256, lbuf)
    gz_r = jnp.pad(gz_s[0, :, :24].reshape(SB, ST, 3, KV_HEADS, GQA).transpose(0, 3, 4, 1, 2),
                   ((0, 0), (0, 0), (0, 0), (0, 8 - ST), (0, 128 - 3))).reshape(SB, KV_HEADS, 8 * GQA, 128)
    yb_r = _s_attn_call(page_table, idx_flat, q_r, kvT_s[0], winT, gz_r, ocmp_s, cache_pages,
                        n_tok=ST, past_len=past_len)
    yb_s = yb_r.reshape(SB, KV_HEADS, GQA, 8, HEAD_DIM)[:, :, :, :ST].transpose(0, 3, 1, 2, 4)
    yb_s = yb_s.reshape(1, SB * ST, B_WIDTH).astype(BF16)

    y_p, up_p = _post_call(x_prompt, mod_p, ya_p, yb_p, g1, g2, gf, wm, wba, wbb, wo, wup, wdn, cw,
                           tm=256, seq_len=T)
    st = state_ffn_conv[0]
    zrow = jnp.zeros((SB, 1, F2), F32)
    p1 = jnp.concatenate([st[:, 1:2], jnp.tile(zrow, (1, ST - 1, 1))], axis=1).reshape(SB * ST, F2)
    p2 = jnp.concatenate([st, jnp.tile(zrow, (1, ST - 2, 1))], axis=1).reshape(SB * ST, F2)
    y_s, up_s = _post_call(xs2, mod_s, ya_s, yb_s, g1, g2, gf, wm, wba, wbb, wo, wup, wdn, cw,
                           prev=(_chunk_cols(p1, 2 * N_FF_CHUNKS), _chunk_cols(p2, 2 * N_FF_CHUNKS)),
                           tm=SB * ST, seq_len=ST)

    kv_prompt = jnp.transpose(kv4T_p.reshape(B, 4, KV_HEADS, HEAD_DIM, T), (0, 4, 1, 2, 3))[None]
    win_prompt = jnp.transpose(winT_p[:, :, T - WINDOW:].reshape(B, 2, KV_HEADS, HEAD_DIM, WINDOW),
                               (0, 4, 1, 2, 3))[None]
    kv_sample = kv_rows_s[:, :, :512].reshape(SB, ST, 4, KV_HEADS, HEAD_DIM)[None]
    win_new_s = kv_rows_s[:, :, 512:].reshape(SB, ST, 2, KV_HEADS, HEAD_DIM)
    win_sample = jnp.concatenate([state_kv_win[0][:, ST:], win_new_s], axis=1)[None]
    v_chunk = vn_s.reshape(SB, ST, A_WIDTH)[None]
    conv_prompt = up_p[:, :, 6:8, :].transpose(0, 2, 1, 3).reshape(B, 2, F2)[None]
    up_rows = up_s.transpose(1, 0, 2).reshape(SB, ST, F2)
    conv_sample = up_rows[:, ST - 2:][None]
    return (y_p, y_s.reshape(SB, ST, D_MODEL), kv_prompt, kv_sample, win_prompt, win_sample, v_chunk,
            conv_prompt, conv_sample)
```

```python
import functools

import jax
import jax.numpy as jnp
from jax import lax
from jax.experimental import pallas as pl
from jax.experimental.pallas import tpu as pltpu

F32 = jnp.float32
BF16 = jnp.bfloat16

D_MODEL = 1024
A_WIDTH = 512
A_GROUPS = 4
CHUNK = 128
B_HEADS = 8
HEAD_DIM = 64
B_WIDTH = 512
KV_HEADS = 2
GQA = 4
BLOCK = 64
N_SELECT = 16
WINDOW = 512
N_KV_SLOTS = 6
KV_COLS = 768
D_FF = 2816
F2 = 2 * D_FF
CONV_W = 3
EPS = 1e-6
NEG = -1e30
FORCE_SCORE = 1e4

FF_CHUNK = 256
N_FF_CHUNKS = D_FF // FF_CHUNK
ROW_BLOCK = 64
ACC_ROWS = HEAD_DIM + 16
LOG2E = 1.4426950408889634
VMEM_LIMIT = 56 * 1024 * 1024


def _dot(a, b):
    return jnp.dot(a, b, preferred_element_type=F32)


def _dot_nt(a, b):
    return lax.dot_general(a, b, (((1,), (1,)), ((), ())), preferred_element_type=F32)


def _rms_mod(x, g, scale, shift):
    y = x * lax.rsqrt(jnp.mean(x * x, axis=-1, keepdims=True) + EPS)
    return (y * g) * (1.0 + scale) + shift


def _mod_kernel(c_ref, w_ref, b_ref, o_ref):
    c = c_ref[...]
    s = c * jax.nn.sigmoid(c)
    o_ref[...] = _dot(s.astype(BF16), w_ref[...].astype(BF16)) + b_ref[...]


def _mod_call(c_all, w_ada, b_ada):
    n = c_all.shape[0]
    tn = 1536
    return pl.pallas_call(
        _mod_kernel,
        out_shape=jax.ShapeDtypeStruct((n, 6 * D_MODEL), F32),
        grid=(6 * D_MODEL // tn,),
        in_specs=[pl.BlockSpec((n, D_MODEL), lambda j: (0, 0)),
                  pl.BlockSpec((D_MODEL, tn), lambda j: (0, j)),
                  pl.BlockSpec((1, tn), lambda j: (0, j))],
        out_specs=pl.BlockSpec((n, tn), lambda j: (0, j)),
        compiler_params=pltpu.CompilerParams(dimension_semantics=("arbitrary",), vmem_limit_bytes=VMEM_LIMIT),
        name="mod",
    )(c_all, w_ada, b_ada)


def _inproj_kernel(x_ref, mod_ref, g1_ref, wa_ref, lng_ref, lnb_ref, wsp_ref, bsp_ref, *refs, tm, prompt):
    x = x_ref[0]
    hb = _rms_mod(x, g1_ref[...], mod_ref[0, 1], mod_ref[0, 0]).astype(BF16)

    ga = jax.nn.gelu(_dot(hb, wa_ref[...]))
    u = ga[:, :A_WIDTH]
    v = ga[:, A_WIDTH:]
    vc = v - jnp.mean(v, axis=-1, keepdims=True)
    vn = vc * lax.rsqrt(jnp.mean(vc * vc, axis=-1, keepdims=True) + EPS) * lng_ref[...] + lnb_ref[...]
    vnb = vn.astype(BF16)

    if prompt:
        (wqT_ref, wkvT_ref, wrow_ref, wgT_ref,
         ya_ref, qT_ref, kv4T_ref, winT_ref, vselT_ref, vwinT_ref, kvc_ref, ksel_ref, kwin_ref, gzT_ref) = refs
    else:
        (wq_ref, wkvT_ref, wkvc_ref, wg_ref, ya_ref, q_ref, kvT_ref, kvc_ref, gz_ref, vn_ref) = refs
        vn_ref[0] = vn

    for c in range(tm // CHUNK):
        rows = slice(c * CHUNK, (c + 1) * CHUNK)
        s = jnp.concatenate(
            [_dot(wsp_ref[g], vnb[rows, g * 128:(g + 1) * 128]) for g in range(A_GROUPS)], axis=1)
        ya_ref[0, rows, :] = (u[rows] * (s + bsp_ref[...])).astype(BF16)

    kvT = _dot_nt(wkvT_ref[...], hb)
    if prompt:
        qT_ref[0] = (_dot_nt(wqT_ref[...], hb) * (HEAD_DIM ** -0.5 * LOG2E)).astype(BF16)
        kv4T_ref[0] = kvT[:4 * 128]
        winT_ref[0] = kvT[4 * 128:]
        vselT_ref[0] = kvT[3 * 128:4 * 128].astype(BF16)
        vwinT_ref[0] = kvT[5 * 128:].astype(BF16)
        row = _dot(hb, wrow_ref[...])
        kvc_ref[0] = row[:, :256]
        ksel_ref[0] = row[:, 256:384].astype(BF16)
        kwin_ref[0] = row[:, 384:].astype(BF16)
        gzT_ref[0] = jax.nn.sigmoid(_dot_nt(wgT_ref[...], hb))
    else:
        q_ref[0] = (_dot(hb, wq_ref[...]) * (HEAD_DIM ** -0.5)).astype(BF16)
        kvT_ref[0] = kvT
        kvc_ref[0] = _dot(hb, wkvc_ref[...])
        gz_ref[0] = jax.nn.sigmoid(_dot(hb, wg_ref[...]))


def _inproj_call(x, mod, g1, wa, lng, lnb, wsp, bsp, proj_w, *, tm, prompt):
    G, T, _ = x.shape
    R = mod.shape[2]
    nt = T // tm
    const2 = lambda b, i: (0, 0)
    const3 = lambda b, i: (0, 0, 0)
    rows = lambda w: pl.BlockSpec((1, tm, w), lambda b, i: (b, i, 0))
    cols = lambda h: pl.BlockSpec((1, h, tm), lambda b, i: (b, 0, i))
    if prompt:
        outs = [((G, T, A_WIDTH), BF16, rows(A_WIDTH)),
                ((G, B_WIDTH, T), BF16, cols(B_WIDTH)),
                ((G, 512, T), F32, cols(512)),
                ((G, 256, T), F32, cols(256)),
                ((G, 128, T), BF16, cols(128)),
                ((G, 128, T), BF16, cols(128)),
                ((G, T, 256), F32, rows(256)),
                ((G, T, 128), BF16, rows(128)),
                ((G, T, 128), BF16, rows(128)),
                ((G, 32, T), F32, cols(32))]
    else:
        outs = [((G, T, A_WIDTH), BF16, rows(A_WIDTH)),
                ((G, T, B_WIDTH), BF16, rows(B_WIDTH)),
                ((G, KV_COLS, T), F32, cols(KV_COLS)),
                ((G, T, 256), F32, rows(256)),
                ((G, T, 128), F32, rows(128)),
                ((G, T, A_WIDTH), F32, rows(A_WIDTH))]
    return pl.pallas_call(
        functools.partial(_inproj_kernel, tm=tm, prompt=prompt),
        out_shape=[jax.ShapeDtypeStruct(s, d) for s, d, _ in outs],
        grid=(G, nt),
        in_specs=[pl.BlockSpec((1, tm, D_MODEL), lambda b, i: (b, i, 0)),
                  pl.BlockSpec((1, 6, R, D_MODEL), lambda b, i: (b, 0, 0, 0)),
                  pl.BlockSpec((1, D_MODEL), const2),
                  pl.BlockSpec(wa.shape, const2),
                  pl.BlockSpec((1, A_WIDTH), const2),
                  pl.BlockSpec((1, A_WIDTH), const2),
                  pl.BlockSpec(wsp.shape, const3),
                  pl.BlockSpec(bsp.shape, const2)] + [pl.BlockSpec(w.shape, const2) for w in proj_w],
        out_specs=[sp for _, _, sp in outs],
        compiler_params=pltpu.CompilerParams(dimension_semantics=("arbitrary", "arbitrary"),
                                             vmem_limit_bytes=VMEM_LIMIT),
        name="inproj",
    )(x, mod, g1, wa, lng, lnb, wsp, bsp, *proj_w)


def _cmp_tail(acc, b1, w2, b2):
    return _dot(jax.nn.gelu(acc + b1).astype(BF16), w2) + b2


def _cmp_rows_kernel(x_ref, pe_ref, w1_ref, b1_ref, w2_ref, b2_ref, o_ref, *, R):
    acc = jnp.zeros((R, 128), F32)
    for j in range(BLOCK):
        xj = x_ref[pl.ds(j, R, stride=BLOCK), :] + pe_ref[0, j:j + 1, :]
        acc = acc + _dot(xj.astype(BF16), w1_ref[0, j])
    o_ref[0] = _cmp_tail(acc, b1_ref[0], w2_ref[0], b2_ref[0])


def _cmp_rows_call(kvc2d, pe, w1, b1, w2, b2, *, R):
    nrows = kvc2d.shape[0] // BLOCK
    return pl.pallas_call(
        functools.partial(_cmp_rows_kernel, R=R),
        out_shape=jax.ShapeDtypeStruct((2, nrows, 128), F32),
        grid=(2, nrows // R),
        in_specs=[pl.BlockSpec((R * BLOCK, 128), lambda s, r: (r, s)),
                  pl.BlockSpec((1, BLOCK, 128), lambda s, r: (s, 0, 0)),
                  pl.BlockSpec((1, BLOCK, 128, 128), lambda s, r: (s, 0, 0, 0)),
                  pl.BlockSpec((1, 1, 128), lambda s, r: (s, 0, 0)),
                  pl.BlockSpec((1, 128, 128), lambda s, r: (s, 0, 0)),
                  pl.BlockSpec((1, 1, 128), lambda s, r: (s, 0, 0))],
        out_specs=pl.BlockSpec((1, R, 128), lambda s, r: (s, r, 0)),
        compiler_params=pltpu.CompilerParams(dimension_semantics=("arbitrary", "arbitrary"),
                                             vmem_limit_bytes=VMEM_LIMIT),
        name="cmp_rows",
    )(kvc2d, pe, w1, b1, w2, b2)


def _cmp_pages_kernel(x_ref, pe_ref, w1_ref, b1_ref, w2_ref, b2_ref, o_ref, *, Pt):
    for s in range(2):
        acc = jnp.zeros((2 * Pt, 128), F32)
        for dg in range(HEAD_DIM // 8):
            r0 = s * 128 + dg * 8
            xa = jnp.swapaxes(x_ref[:, r0:r0 + 8, :], 0, 1)
            xb = jnp.swapaxes(x_ref[:, HEAD_DIM + r0:HEAD_DIM + r0 + 8, :], 0, 1)
            for dd in range(0, 8, 2):
                dp = dg * 4 + dd // 2
                xd = jnp.concatenate([jnp.concatenate([xa[dd], xb[dd]], axis=0),
                                      jnp.concatenate([xa[dd + 1], xb[dd + 1]], axis=0)], axis=1)
                xd = xd + pe_ref[s, dp:dp + 1, :]
                acc = acc + _dot(xd.astype(BF16), w1_ref[s, dp])
        r = _cmp_tail(acc, b1_ref[s], w2_ref[s], b2_ref[s])
        o_ref[:, (2 * s) * 128:(2 * s + 1) * 128] = r[:Pt]
        o_ref[:, (2 * s + 1) * 128:(2 * s + 2) * 128] = r[Pt:]


def _cmp_pages_call(pages, pe, w1, b1, w2, b2, *, Pt):
    P = pages.shape[0]
    full = lambda i: (0, 0, 0)
    return pl.pallas_call(
        functools.partial(_cmp_pages_kernel, Pt=Pt),
        out_shape=jax.ShapeDtypeStruct((P, 512), F32),
        grid=(P // Pt,),
        in_specs=[pl.BlockSpec((Pt, 256, 128), lambda i: (i, 0, 0)),
                  pl.BlockSpec(pe.shape, full),
                  pl.BlockSpec(w1.shape, lambda i: (0, 0, 0, 0)),
                  pl.BlockSpec(b1.shape, full),
                  pl.BlockSpec(w2.shape, full),
                  pl.BlockSpec(b2.shape, full)],
        out_specs=pl.BlockSpec((Pt, 512), lambda i: (i, 0)),
        compiler_params=pltpu.CompilerParams(dimension_semantics=("arbitrary",), vmem_limit_bytes=VMEM_LIMIT),
        name="cmp_pages",
    )(pages, pe, w1, b1, w2, b2)


def _pair_schedule(T, tq, kc):
    js, cs = [], []
    for j in range(T // tq):
        for c in range(((j + 1) * tq - 1) // kc + 1):
            js.append(j)
            cs.append(c)
    return js, cs


def _p_attn_kernel(jt_ref, ct_ref, qT_ref, ksel_ref, kwin_ref, vselT_ref, vwinT_ref, kc_ref, vcT_ref, gzT_ref,
                   yb_ref, sel_s, ocmp_s, m_s, acc_s, s_scr, p_scr, b_scr, *, tq, kc, T, strip):
    p = pl.program_id(1)
    j = jt_ref[p]
    c = ct_ref[p]
    nb = T // BLOCK
    N = GQA * tq
    q0 = j * tq
    k0 = c * kc
    c_last = ((j + 1) * tq - 1) // kc

    def q_pad(h):
        qT = jnp.concatenate([qT_ref[0, (h * GQA + g) * HEAD_DIM:(h * GQA + g + 1) * HEAD_DIM, :]
                              for g in range(GQA)], axis=1)
        z = jnp.zeros_like(qT)
        return jnp.concatenate([qT if k == h else z for k in range(KV_HEADS)], axis=0)

    @pl.when(c == 0)
    def _():
        m_s[...] = jnp.full(m_s.shape, NEG, F32)
        acc_s[...] = jnp.zeros(acc_s.shape, F32)
        tok_n = q0 + (lax.broadcasted_iota(jnp.int32, (1, N), 1) & (tq - 1))
        tok_1 = q0 + lax.broadcasted_iota(jnp.int32, (1, tq), 1)
        blk = lax.broadcasted_iota(jnp.int32, (nb, 1), 0)
        avail = (blk + 1) * BLOCK <= tok_n + 1
        cur = tok_1 >> 6
        forced = (blk == 0) | (blk == cur) | (blk == cur - 1)
        future = blk > cur
        kcb = kc_ref[0].astype(BF16)
        for h in range(KV_HEADS):
            s = jnp.where(avail, _dot(kcb, q_pad(h)), NEG)
            m = jnp.max(s, axis=0, keepdims=True)
            e = jnp.where(avail, jnp.exp2(s - m), 0.0)
            pr = e / jnp.maximum(jnp.sum(e, axis=0, keepdims=True), 1e-30)
            ocmp_s[h] = _dot(vcT_ref[0, h * HEAD_DIM:(h + 1) * HEAD_DIM, :].astype(BF16), pr.astype(BF16))
            imp = pr[:, 0:tq]
            for g in range(1, GQA):
                imp = imp + pr[:, g * tq:(g + 1) * tq]
            imp = jnp.where(forced, FORCE_SCORE, jnp.where(future, -1.0, imp))
            rows = []
            for i in range(nb):
                vi = imp[i:i + 1, :]
                ahead = (imp > vi) | ((imp == vi) & (blk < i))
                cnt = jnp.sum(ahead.astype(F32), axis=0, keepdims=True)
                rows.append((cnt < float(N_SELECT)).astype(F32))
            sel_s[h] = jnp.concatenate(rows + [jnp.zeros((128 - nb, tq), F32)], axis=0).astype(BF16)

    kpos = k0 + lax.broadcasted_iota(jnp.int32, (kc, tq), 0)
    tok = q0 + lax.broadcasted_iota(jnp.int32, (kc, tq), 1)

    strips = [slice(st * strip, (st + 1) * strip) for st in range(N // strip)]

    row_blocks = [slice(r, r + ROW_BLOCK) for r in range(0, kc, ROW_BLOCK)]

    def scores(h, keys, bh):
        mcur = []
        for st, ls in enumerate(strips):
            g, t0 = divmod(st * strip, tq)
            r0 = (h * GQA + g) * HEAD_DIM
            qT = qT_ref[0, r0:r0 + HEAD_DIM, t0:t0 + strip]
            z = jnp.zeros_like(qT)
            qp = jnp.concatenate([qT if k == h else z for k in range(KV_HEADS)], axis=0)
            s_scr[h, :, ls] = _dot(keys, qp)
        for st, ls in enumerate(strips):
            t0 = (st * strip) % tq
            m = None
            for rb in row_blocks:
                t = s_scr[h, rb, ls] + b_scr[bh, rb, t0:t0 + strip]
                s_scr[h, rb, ls] = t
                m = t if m is None else jnp.maximum(m, t)
            mcur.append(jnp.max(m, axis=0, keepdims=True))
        return jnp.concatenate(mcur, axis=1)

    def probs(idx, h, mcur):
        m_old = m_s[idx]
        m_new = jnp.maximum(m_old, mcur)
        alpha = jnp.exp2(m_old - m_new)
        m_s[idx] = m_new
        for ls in strips:
            mb = m_new[:, ls]
            for rb in row_blocks:
                p_scr[h, rb, ls] = jnp.exp2(s_scr[h, rb, ls] - mb).astype(BF16)
        return alpha

    ones_rows = (lax.broadcasted_iota(jnp.int32, (ACC_ROWS - HEAD_DIM, kc), 0) == 0).astype(BF16)

    def weighted_values(idx, h, vT, alpha):
        vT1 = jnp.concatenate([vT, ones_rows], axis=0)
        for ls in strips:
            acc_s[idx, :, ls] = alpha[:, ls] * acc_s[idx, :, ls] + _dot(vT1, p_scr[h, :, ls])

    def online_softmax(first_idx, keys, vT_ref, bias_slots):
        heads = range(KV_HEADS)
        mcur = [scores(h, keys, bias_slots[h]) for h in heads]
        alpha = [probs(first_idx + h, h, mcur[h]) for h in heads]
        for h in heads:
            weighted_values(first_idx + h, h, vT_ref[0, h * HEAD_DIM:(h + 1) * HEAD_DIM, :], alpha[h])

    blk_of_key = (k0 >> 6) + (lax.broadcasted_iota(jnp.int32, (kc, 128), 0) >> 6)
    expand = (lax.broadcasted_iota(jnp.int32, (kc, 128), 1) == blk_of_key).astype(BF16)
    causal = kpos <= tok
    for h in range(KV_HEADS):
        b_scr[h] = jnp.where((_dot(expand, sel_s[h]) > 0.5) & causal, 0.0, NEG)
    online_softmax(0, ksel_ref[0], vselT_ref, list(range(KV_HEADS)))

    @pl.when((c + 1) * kc - 1 >= q0 - (WINDOW - 1))
    def _():
        diff = tok - kpos
        b_scr[0] = jnp.where((diff >= 0) & (diff < WINDOW), 0.0, NEG)
        online_softmax(KV_HEADS, kwin_ref[0], vwinT_ref, [0] * KV_HEADS)

    @pl.when(c == c_last)
    def _():
        gz = gzT_ref[0]
        outs = []
        for h in range(KV_HEADS):
            a_sel = acc_s[h]
            a_win = acc_s[KV_HEADS + h]
            o_sel = a_sel[:HEAD_DIM] / jnp.maximum(a_sel[HEAD_DIM:HEAD_DIM + 1], 1e-30)
            o_win = a_win[:HEAD_DIM] / jnp.maximum(a_win[HEAD_DIM:HEAD_DIM + 1], 1e-30)
            o_cmp = ocmp_s[h]
            for g in range(GQA):
                col = h * GQA + g
                ls = slice(g * tq, (g + 1) * tq)
                outs.append(gz[col:col + 1] * o_cmp[:, ls] + gz[B_HEADS + col:B_HEADS + col + 1] * o_sel[:, ls]
                            + gz[2 * B_HEADS + col:2 * B_HEADS + col + 1] * o_win[:, ls])
        yb_ref[0] = jnp.concatenate(outs, axis=0).T.astype(BF16)


def _p_attn_call(qT, ksel, kwin, vselT, vwinT, kc_all, vcT, gzT, *, tq, kc):
    B, _, T = qT.shape
    nb = T // BLOCK
    N = GQA * tq
    js, cs = _pair_schedule(T, tq, kc)
    jt = jnp.asarray(js, jnp.int32)
    ct = jnp.asarray(cs, jnp.int32)
    return pl.pallas_call(
        functools.partial(_p_attn_kernel, tq=tq, kc=kc, T=T, strip=256),
        out_shape=jax.ShapeDtypeStruct((B, T, B_WIDTH), BF16),
        grid_spec=pltpu.PrefetchScalarGridSpec(
            num_scalar_prefetch=2,
            grid=(B, len(js)),
            in_specs=[pl.BlockSpec((1, B_WIDTH, tq), lambda b, p, jt, ct: (b, 0, jt[p])),
                      pl.BlockSpec((1, kc, 128), lambda b, p, jt, ct: (b, ct[p], 0)),
                      pl.BlockSpec((1, kc, 128), lambda b, p, jt, ct: (b, ct[p], 0)),
                      pl.BlockSpec((1, 128, kc), lambda b, p, jt, ct: (b, 0, ct[p])),
                      pl.BlockSpec((1, 128, kc), lambda b, p, jt, ct: (b, 0, ct[p])),
                      pl.BlockSpec((1, nb, 128), lambda b, p, jt, ct: (b, 0, 0)),
                      pl.BlockSpec((1, 128, nb), lambda b, p, jt, ct: (b, 0, 0)),
                      pl.BlockSpec((1, 32, tq), lambda b, p, jt, ct: (b, 0, jt[p]))],
            out_specs=pl.BlockSpec((1, tq, B_WIDTH), lambda b, p, jt, ct: (b, jt[p], 0)),
            scratch_shapes=[pltpu.VMEM((KV_HEADS, 128, tq), BF16),
                            pltpu.VMEM((KV_HEADS, HEAD_DIM, N), F32),
                            pltpu.VMEM((2 * KV_HEADS, 1, N), F32),
                            pltpu.VMEM((2 * KV_HEADS, ACC_ROWS, N), F32),
                            pltpu.VMEM((KV_HEADS, kc, N), F32),
                            pltpu.VMEM((KV_HEADS, kc, N), BF16),
                            pltpu.VMEM((KV_HEADS, kc, tq), F32)]),
        compiler_params=pltpu.CompilerParams(dimension_semantics=("arbitrary", "arbitrary"),
                                             vmem_limit_bytes=VMEM_LIMIT),
        name="p_attn",
    )(jt, ct, qT, ksel, kwin, vselT, vwinT, kc_all, vcT, gzT)


def _s_cmp_kernel(pt_ref, q_ref, pool_ref, newc_ref, ocmp_ref, imp_ref, gath, *, n_pages, past_len):
    b = pl.program_id(0)

    def gather(jp, carry):
        gath[pl.ds(jp, 1), :] = pool_ref[pl.ds(pt_ref[b, jp], 1), :]
        return carry

    lax.fori_loop(0, n_pages, gather, 0)

    rows = 8 * GQA
    t_row = lax.broadcasted_iota(jnp.int32, (rows, 1), 0) & 7
    pos = past_len + t_row
    lane = lax.broadcasted_iota(jnp.int32, (1, 2 * n_pages), 1)
    blk = 2 * (lane & (n_pages - 1)) + (lane >> 7)
    new_blk = past_len // BLOCK
    avail = (blk + 1) * BLOCK <= pos + 1
    avail_new = (new_blk + 1) * BLOCK <= pos + 1
    for h in range(KV_HEADS):
        qh = q_ref[0, h]
        kc = [gath[:, (h * 2 + k) * 64:(h * 2 + k + 1) * 64].astype(BF16) for k in range(2)]
        vc = [gath[:, (2 + h) * 128 + k * 64:(2 + h) * 128 + (k + 1) * 64].astype(BF16) for k in range(2)]
        kc_new = newc_ref[0, :, h * 128:h * 128 + 64].astype(BF16).astype(F32)
        vc_new = newc_ref[0, :, (2 + h) * 128:(2 + h) * 128 + 64].astype(BF16).astype(F32)
        s = jnp.where(avail, jnp.concatenate([_dot_nt(qh, kc[0]), _dot_nt(qh, kc[1])], axis=1), NEG)
        s_new = jnp.where(avail_new, jnp.sum(qh.astype(F32) * kc_new, axis=-1, keepdims=True), NEG)
        m = jnp.maximum(jnp.max(s, axis=-1, keepdims=True), s_new)
        e = jnp.where(avail, jnp.exp(s - m), 0.0)
        e_new = jnp.where(avail_new, jnp.exp(s_new - m), 0.0)
        den = jnp.maximum(jnp.sum(e, axis=-1, keepdims=True) + e_new, 1e-30)
        p = e / den
        p_new = e_new / den
        pb = p.astype(BF16)
        ocmp_ref[0, h] = (_dot(pb[:, :n_pages], vc[0]) + _dot(pb[:, n_pages:], vc[1])
                          + p_new.astype(BF16).astype(F32) * vc_new)
        imp = p[0:8] + p[8:16] + p[16:24] + p[24:32]
        cur = (past_len + lax.broadcasted_iota(jnp.int32, (8, 1), 0)) >> 6
        forced = (blk == 0) | (blk == cur) | (blk == cur - 1)
        imp_ref[0, h] = jnp.where(forced, FORCE_SCORE, jnp.where(blk > cur, -1.0, imp))


def _s_cmp_call(page_table, q_r, pool, newc, *, past_len):
    nbatch, n_pages = page_table.shape
    P = pool.shape[0]
    return pl.pallas_call(
        functools.partial(_s_cmp_kernel, n_pages=n_pages, past_len=past_len),
        out_shape=[jax.ShapeDtypeStruct((nbatch, KV_HEADS, 32, HEAD_DIM), F32),
                   jax.ShapeDtypeStruct((nbatch, KV_HEADS, 8, 2 * n_pages), F32)],
        grid_spec=pltpu.PrefetchScalarGridSpec(
            num_scalar_prefetch=1,
            grid=(nbatch,),
            in_specs=[pl.BlockSpec((1, KV_HEADS, 32, HEAD_DIM), lambda b, pt: (b, 0, 0, 0)),
                      pl.BlockSpec((P, 512), lambda b, pt: (0, 0)),
                      pl.BlockSpec((1, 1, 512), lambda b, pt: (b, 0, 0))],
            out_specs=[pl.BlockSpec((1, KV_HEADS, 32, HEAD_DIM), lambda b, pt: (b, 0, 0, 0)),
                       pl.BlockSpec((1, KV_HEADS, 8, 2 * n_pages), lambda b, pt: (b, 0, 0, 0))],
            scratch_shapes=[pltpu.VMEM((n_pages, 512), F32)]),
        compiler_params=pltpu.CompilerParams(dimension_semantics=("arbitrary",), vmem_limit_bytes=VMEM_LIMIT),
        name="s_cmp",
    )(page_table, q_r, pool, newc)


def _s_select_kernel(imp_ref, idx_ref, *, n_pages, n_pick):
    v = imp_ref[...]
    rows = v.shape[0]
    lane = lax.broadcasted_iota(jnp.int32, (1, 2 * n_pages), 1)
    blk = (2 * (lane & (n_pages - 1)) + (lane >> 7)).astype(F32)
    out_lane = lax.broadcasted_iota(jnp.int32, (1, 128), 1)
    out = jnp.full((rows, 128), float(2 * n_pages), F32)
    for k in range(n_pick):
        m = jnp.max(v, axis=-1, keepdims=True)
        pick = jnp.min(jnp.where(v == m, blk, 1e9), axis=-1, keepdims=True)
        out = jnp.where(out_lane == k, pick, out)
        v = jnp.where(blk == pick, NEG, v)
    idx_ref[...] = out.astype(jnp.int32)


def _s_select_call(imp2d, *, n_pages, n_pick):
    rows = imp2d.shape[0]
    return pl.pallas_call(
        functools.partial(_s_select_kernel, n_pages=n_pages, n_pick=n_pick),
        out_shape=jax.ShapeDtypeStruct((rows, 128), jnp.int32),
        grid=(1,),
        in_specs=[pl.BlockSpec(imp2d.shape, lambda i: (0, 0))],
        out_specs=pl.BlockSpec((rows, 128), lambda i: (0, 0)),
        compiler_params=pltpu.CompilerParams(dimension_semantics=("arbitrary",), vmem_limit_bytes=VMEM_LIMIT),
        name="s_select",
    )(imp2d)


def _s_attn_kernel(pt_ref, idx_ref, q_ref, kvn_ref, win_ref, gz_ref, ocmp_ref, cache_ref, yb_ref, wnext_ref,
                   kbuf, vbuf, sem, *, n_tok, n_gather, past_len):
    b = pl.program_id(0)
    span = n_gather * 128

    def tile_copies(t, h, jj):
        n = idx_ref[((b * KV_HEADS + h) * n_tok + t) * N_SELECT + jj]
        page = pt_ref[b, n >> 1]
        kc = pltpu.make_async_copy(cache_ref.at[page, pl.ds(2 * 128 + h * 64, 64), :],
                                   kbuf.at[h, t, :, pl.ds(jj * 128, 128)], sem.at[0])
        vc = pltpu.make_async_copy(cache_ref.at[page, pl.ds(3 * 128 + h * 64, 64), :],
                                   vbuf.at[h, t, :, pl.ds(jj * 128, 128)], sem.at[1])
        return kc, vc

    for h in range(KV_HEADS):
        for t in range(n_tok):
            for jj in range(n_gather):
                kc, vc = tile_copies(t, h, jj)
                kc.start()
                vc.start()

    rows = 8 * GQA
    t_row = lax.broadcasted_iota(jnp.int32, (rows, 1), 0) & 7
    gz = gz_ref[0]

    lane_n = lax.broadcasted_iota(jnp.int32, (1, 128), 1)
    tok_shift = n_tok.bit_length() - 1
    new_ok = ((lane_n >> tok_shift) == b) & ((lane_n & (n_tok - 1)) <= t_row)
    lane_w = lax.broadcasted_iota(jnp.int32, (1, WINDOW), 1)
    win_ok = lane_w > t_row

    shifted = pltpu.roll(win_ref[0], WINDOW - n_tok, 1)
    moved = pltpu.roll(kvn_ref[4 * 128:, :], (128 - n_tok) - b * n_tok, 1)
    wnext_ref[0, :, :WINDOW - 128] = shifted[:, :WINDOW - 128]
    wnext_ref[0, :, WINDOW - 128:] = jnp.where(lane_n >= 128 - n_tok, moved, shifted[:, WINDOW - 128:])

    def softmax_pv(parts):
        m = None
        for sc, mk, _ in parts:
            mm = jnp.max(jnp.where(mk, sc, NEG), axis=-1, keepdims=True)
            m = mm if m is None else jnp.maximum(m, mm)
        den = jnp.zeros((rows, 1), F32)
        o = jnp.zeros((rows, HEAD_DIM), F32)
        for sc, mk, vts in parts:
            e = jnp.where(mk, jnp.exp(jnp.where(mk, sc, NEG) - m), 0.0)
            den = den + jnp.sum(e, axis=-1, keepdims=True)
            for rmask, vt in vts:
                er = e if rmask is None else jnp.where(rmask, e, 0.0)
                o = o + _dot_nt(er.astype(BF16), vt)
        return o / jnp.maximum(den, 1e-30)

    win_out = []
    for h in range(KV_HEADS):
        qh = q_ref[0, h]
        kw = win_ref[0, h * 64:(h + 1) * 64, :].astype(BF16)
        vw = win_ref[0, 128 + h * 64:128 + (h + 1) * 64, :].astype(BF16)
        kwn = kvn_ref[4 * 128 + h * 64:4 * 128 + (h + 1) * 64, :].astype(BF16)
        vwn = kvn_ref[5 * 128 + h * 64:5 * 128 + (h + 1) * 64, :].astype(BF16)
        win_out.append(softmax_pv([(_dot(qh, kw), win_ok, [(None, vw)]),
                                   (_dot(qh, kwn), new_ok, [(None, vwn)])]))

    for h in range(KV_HEADS):
        for t in range(n_tok):
            for jj in range(n_gather):
                kc, vc = tile_copies(t, h, jj)
                kc.wait()
                vc.wait()

    lane_s = lax.broadcasted_iota(jnp.int32, (1, span), 1)
    for h in range(KV_HEADS):
        qh = q_ref[0, h]
        sc = jnp.zeros((rows, span), F32)
        half = jnp.zeros((rows, span), jnp.int32)
        for t in range(n_tok):
            st = _dot(qh, kbuf[h, t].astype(BF16))
            hrow = jnp.zeros((1, span), jnp.int32)
            for jj in range(n_gather):
                n = idx_ref[((b * KV_HEADS + h) * n_tok + t) * N_SELECT + jj]
                hrow = jnp.where((lane_s >> 7) == jj, n & 1, hrow)
            sc = jnp.where(t_row == t, st, sc)
            half = jnp.where(t_row == t, hrow, half)
        ok = ((lane_s >> 6) & 1) == half
        ksn = kvn_ref[2 * 128 + h * 64:2 * 128 + (h + 1) * 64, :].astype(BF16)
        vsn = kvn_ref[3 * 128 + h * 64:3 * 128 + (h + 1) * 64, :].astype(BF16)
        vts = [(t_row == t, vbuf[h, t].astype(BF16)) for t in range(n_tok)]
        o_sel = softmax_pv([(sc, ok, vts), (_dot(qh, ksn), new_ok, [(None, vsn)])])
        yb_ref[0, h] = (gz[h, :, 0:1] * ocmp_ref[0, h] + gz[h, :, 1:2] * o_sel + gz[h, :, 2:3] * win_out[h])


def _s_attn_call(page_table, idx_flat, q_r, kvT_new, winT, gz_r, ocmp, cache_pages, *, n_tok, past_len):
    nbatch = page_table.shape[0]
    n_gather = N_SELECT - 1
    span = n_gather * 128
    blk4 = lambda b, pt, ix: (b, 0, 0, 0)
    return pl.pallas_call(
        functools.partial(_s_attn_kernel, n_tok=n_tok, n_gather=n_gather, past_len=past_len),
        out_shape=[jax.ShapeDtypeStruct((nbatch, KV_HEADS, 32, HEAD_DIM), F32),
                   jax.ShapeDtypeStruct((nbatch, 256, WINDOW), F32)],
        grid_spec=pltpu.PrefetchScalarGridSpec(
            num_scalar_prefetch=2,
            grid=(nbatch,),
            in_specs=[pl.BlockSpec((1, KV_HEADS, 32, HEAD_DIM), blk4),
                      pl.BlockSpec(kvT_new.shape, lambda b, pt, ix: (0, 0)),
                      pl.BlockSpec((1, 256, WINDOW), lambda b, pt, ix: (b, 0, 0)),
                      pl.BlockSpec((1, KV_HEADS, 32, 128), blk4),
                      pl.BlockSpec((1, KV_HEADS, 32, HEAD_DIM), blk4),
                      pl.BlockSpec(memory_space=pl.ANY)],
            out_specs=[pl.BlockSpec((1, KV_HEADS, 32, HEAD_DIM), blk4),
                       pl.BlockSpec((1, 256, WINDOW), lambda b, pt, ix: (b, 0, 0))],
            scratch_shapes=[pltpu.VMEM((KV_HEADS, n_tok, HEAD_DIM, span), F32),
                            pltpu.VMEM((KV_HEADS, n_tok, HEAD_DIM, span), F32),
                            pltpu.SemaphoreType.DMA((2,))]),
        compiler_params=pltpu.CompilerParams(dimension_semantics=("arbitrary",), vmem_limit_bytes=VMEM_LIMIT),
        name="s_attn",
    )(page_table, idx_flat, q_r, kvT_new, winT, gz_r, ocmp, cache_pages)


def _post_kernel(*refs, tm, seq_len, has_prev):
    it = iter(refs)
    x_ref, mod_ref, ya_ref, yb_ref = next(it), next(it), next(it), next(it)
    g1_ref, g2_ref, gf_ref = next(it), next(it), next(it)
    wm_ref, wba_ref, wbb_ref, wo_ref, wup_ref, wdn_ref, cw_ref = (next(it) for _ in range(7))
    p1_ref = next(it) if has_prev else None
    p2_ref = next(it) if has_prev else None
    y_ref, up_ref = next(it), next(it)
    carry, h2_s, acc_s = next(it), next(it), next(it)

    i = pl.program_id(1)
    x = x_ref[0]
    shift1, scale1, gate1 = mod_ref[0, 0], mod_ref[0, 1], mod_ref[0, 2]
    shift2, scale2, gate2 = mod_ref[0, 3], mod_ref[0, 4], mod_ref[0, 5]

    hb = _rms_mod(x, g1_ref[...], scale1, shift1).astype(BF16)
    gates = jax.nn.sigmoid(_dot(hb, wm_ref[...]))
    mix = (gates[:, :D_MODEL] * _dot(ya_ref[0], wba_ref[...])
           + gates[:, D_MODEL:] * _dot(yb_ref[0], wbb_ref[...]))
    x1 = x + gate1 * _dot(mix.astype(BF16), wo_ref[...])
    h2_s[...] = _rms_mod(x1, g2_ref[...], scale2, shift2).astype(BF16)
    acc_s[...] = jnp.zeros(acc_s.shape, F32)

    if not has_prev:
        @pl.when(i == 0)
        def _():
            carry[...] = jnp.zeros(carry.shape, F32)

    row = lax.broadcasted_iota(jnp.int32, (tm, 1), 0)
    row8 = lax.broadcasted_iota(jnp.int32, (8, 1), 0)

    def conv_half(idx):
        up = _dot(h2_s[...], wup_ref[idx])
        r1 = pltpu.roll(up, 1, 0)
        r2 = pltpu.roll(up, 2, 0)
        if has_prev:
            s1 = jnp.where((row & (seq_len - 1)) == 0, p1_ref[idx], r1)
            s2 = jnp.where((row & (seq_len - 1)) < 2, p2_ref[idx], r2)
            up_ref[idx] = up
        else:
            prev = carry[idx]
            t1 = jnp.where(row8 == 0, pltpu.roll(prev, 1, 0), r1[:8])
            t2 = jnp.where(row8 < 2, pltpu.roll(prev, 2, 0), r2[:8])
            s1 = jnp.concatenate([t1, r1[8:]], axis=0)
            s2 = jnp.concatenate([t2, r2[8:]], axis=0)
            carry[idx] = up[tm - 8:]
            up_ref[0, idx] = up[tm - 8:]
        cw = cw_ref[idx]
        return cw[3:4] + cw[0:1] * s2 + cw[1:2] * s1 + cw[2:3] * up

    def ff_body(c, cr):
        a = conv_half(c)
        gv = conv_half(N_FF_CHUNKS + c)
        act = (jax.nn.gelu(a) * gv).astype(BF16)
        acc_s[...] += _dot(act, wdn_ref[c])
        return cr

    for c in range(N_FF_CHUNKS):
        ff_body(c, 0)
    x2 = x1 + gate2 * acc_s[...]
    y_ref[0] = x2 * lax.rsqrt(jnp.mean(x2 * x2, axis=-1, keepdims=True) + EPS) * gf_ref[...]


def _post_call(x, mod, ya, yb, g1, g2, gf, wm, wba, wbb, wo, wup, wdn, cw, prev=None, *, tm, seq_len):
    G, T, _ = x.shape
    R = mod.shape[2]
    nt = T // tm
    has_prev = prev is not None
    single = pl.Buffered(1)
    c2 = lambda b, i: (0, 0)
    c3 = lambda b, i: (0, 0, 0)
    in_specs = [pl.BlockSpec((1, tm, D_MODEL), lambda b, i: (b, i, 0)),
                pl.BlockSpec((1, 6, R, D_MODEL), lambda b, i: (b, 0, 0, 0)),
                pl.BlockSpec((1, tm, A_WIDTH), lambda b, i: (b, i, 0)),
                pl.BlockSpec((1, tm, B_WIDTH), lambda b, i: (b, i, 0)),
                pl.BlockSpec((1, D_MODEL), c2),
                pl.BlockSpec((1, D_MODEL), c2),
                pl.BlockSpec((1, D_MODEL), c2),
                pl.BlockSpec(wm.shape, c2, pipeline_mode=single),
                pl.BlockSpec(wba.shape, c2, pipeline_mode=single),
                pl.BlockSpec(wbb.shape, c2, pipeline_mode=single),
                pl.BlockSpec(wo.shape, c2, pipeline_mode=single),
                pl.BlockSpec(wup.shape, c3, pipeline_mode=single),
                pl.BlockSpec(wdn.shape, c3, pipeline_mode=single),
                pl.BlockSpec(cw.shape, c3, pipeline_mode=single)]
    args = [x, mod, ya, yb, g1, g2, gf, wm, wba, wbb, wo, wup, wdn, cw]
    if has_prev:
        in_specs += [pl.BlockSpec(prev[0].shape, c3), pl.BlockSpec(prev[1].shape, c3)]
        args += list(prev)
        up_shape = jax.ShapeDtypeStruct((2 * N_FF_CHUNKS, T, FF_CHUNK), F32)
        up_spec = pl.BlockSpec((2 * N_FF_CHUNKS, tm, FF_CHUNK), lambda b, i: (0, i, 0))
    else:
        up_shape = jax.ShapeDtypeStruct((G, 2 * N_FF_CHUNKS, 8, FF_CHUNK), F32)
        up_spec = pl.BlockSpec((1, 2 * N_FF_CHUNKS, 8, FF_CHUNK), lambda b, i: (b, 0, 0, 0))
    return pl.pallas_call(
        functools.partial(_post_kernel, tm=tm, seq_len=seq_len, has_prev=has_prev),
        out_shape=[jax.ShapeDtypeStruct((G, T, D_MODEL), F32), up_shape],
        grid=(G, nt),
        in_specs=in_specs,
        out_specs=[pl.BlockSpec((1, tm, D_MODEL), lambda b, i: (b, i, 0)), up_spec],
        scratch_shapes=[pltpu.VMEM((2 * N_FF_CHUNKS, 8, FF_CHUNK), F32),
                        pltpu.VMEM((tm, D_MODEL), BF16),
                        pltpu.VMEM((tm, D_MODEL), F32)],
        compiler_params=pltpu.CompilerParams(dimension_semantics=("arbitrary", "arbitrary"),
                                             vmem_limit_bytes=VMEM_LIMIT),
        name="post",
    )(*args)


def _chunk_cols(a, n):
    return a.reshape(a.shape[0], n, FF_CHUNK).transpose(1, 0, 2)


def _block_diag2(m):
    z = jnp.zeros_like(m)
    return jnp.concatenate([jnp.concatenate([m, z], axis=-1), jnp.concatenate([z, m], axis=-1)], axis=-2)


def kernel(x_prompt, x_sample, cache_kv, state_kv_win, state_ffn_conv, page_table, c_prompt, c_sample, w_ada, b_ada, g_norm1, w_in, ln_v_g, ln_v_b, w_spatial, b_spatial, cmp_pe, cmp_w1, cmp_b1, cmp_w2, cmp_b2, w_branch_a, w_branch_b, w_out, g_norm2, w_up, w_conv, b_conv, w_down, g_final):
    B, T, _ = x_prompt.shape
    SB, ST, _ = x_sample.shape
    n_pool, page_size = cache_kv.shape[1], cache_kv.shape[2]
    n_pages = page_table.shape[1]
    past_len = n_pages * page_size
    lbuf = state_kv_win.shape[2]
    assert cache_kv.shape[0] == 1 and page_size == 128 and lbuf == WINDOW and SB * ST == 128 and ST <= 8
    assert past_len % BLOCK == 0 and n_pages == 128 and T % 512 == 0 and ST & (ST - 1) == 0 and ST >= 2

    win = w_in[0]
    wa = win[:, :2 * A_WIDTH].astype(BF16)
    o = 2 * A_WIDTH
    wq = win[:, o:o + B_WIDTH].astype(BF16)
    o += B_WIDTH
    wkv = win[:, o:o + KV_COLS]
    wkvT = wkv.T.astype(BF16)
    wkvc = wkv[:, :256].astype(BF16)
    o += KV_COLS
    wg = jnp.pad(win[:, o:o + 3 * B_HEADS], ((0, 0), (0, 128 - 3 * B_HEADS))).astype(BF16)
    wgT = jnp.pad(win[:, o:o + 3 * B_HEADS].T, ((0, 32 - 3 * B_HEADS), (0, 0))).astype(BF16)
    wqT = wq.T
    wrow = jnp.concatenate([wkv[:, :3 * 128], wkv[:, 4 * 128:5 * 128]], axis=1).astype(BF16)
    o += 3 * B_HEADS
    wm = win[:, o:].astype(BF16)
    g1 = g_norm1[0][None]
    g2 = g_norm2[0][None]
    gf = g_final[None]
    lng = ln_v_g[0][None]
    lnb = ln_v_b[0][None]

    ws = w_spatial[0]
    bs = b_spatial[0]
    wsp_p = jnp.tril(ws).astype(BF16)
    bsp_p = jnp.repeat(bs.T, 128, axis=1)
    w4 = jnp.tril(ws[:, :ST, :ST])
    eye_s = jnp.eye(128 // ST, dtype=F32)
    wsp_s = jnp.einsum('ab,gts->gatbs', eye_s, w4).reshape(A_GROUPS, 128, 128).astype(BF16)
    bsp_s = jnp.repeat(jnp.tile(bs[:, :ST].T, (128 // ST, 1)), 128, axis=1)

    w1, pe = cmp_w1[0].astype(BF16), cmp_pe[0]
    w1_rows = _block_diag2(w1)
    pe_rows = jnp.tile(pe, (1, 1, 2))
    w1_pages = _block_diag2(w1.transpose(0, 2, 1, 3)).reshape(2, HEAD_DIM // 2, 256, 128)
    pe_pages = jnp.tile(pe.transpose(0, 2, 1), (1, 1, 2)).reshape(2, HEAD_DIM // 2, 256)
    b1t = jnp.tile(cmp_b1[0], (1, 2))[:, None, :]
    b2t = jnp.tile(cmp_b2[0], (1, 2))[:, None, :]
    w2bd = _block_diag2(cmp_w2[0].astype(BF16))

    wba = w_branch_a[0].astype(BF16)
    wbb = w_branch_b[0].astype(BF16)
    wo = w_out[0].astype(BF16)
    wup = _chunk_cols(w_up[0], 2 * N_FF_CHUNKS).astype(BF16)
    wdn = w_down[0].reshape(N_FF_CHUNKS, FF_CHUNK, D_MODEL).astype(BF16)
    cw = jnp.concatenate([w_conv[0], b_conv[0][None], jnp.zeros((4, F2), F32)], axis=0)
    cw = _chunk_cols(cw, 2 * N_FF_CHUNKS)

    mod = _mod_call(jnp.concatenate([c_prompt, c_sample], axis=0), w_ada[0], b_ada)
    mod_p = mod[:B].reshape(B, 6, 1, D_MODEL)
    mod_s = jnp.repeat(mod[B:], ST, axis=0).reshape(SB * ST, 6, D_MODEL).transpose(1, 0, 2)[None]

    ya_p, qT_p, kv4T_p, winT_p, vselT_p, vwinT_p, kvc_p, ksel_p, kwin_p, gzT_p = _inproj_call(
        x_prompt, mod_p, g1, wa, lng, lnb, wsp_p, bsp_p, (wqT, wkvT, wrow, wgT), tm=512, prompt=True)
    xs2 = x_sample.reshape(1, SB * ST, D_MODEL)
    ya_s, q_s, kvT_s, kvc_s, gz_s, vn_s = _inproj_call(
        xs2, mod_s, g1, wa, lng, lnb, wsp_s, bsp_s, (wq, wkvT, wkvc, wg), tm=128, prompt=False)

    nb = T // BLOCK
    cmp_p = _cmp_rows_call(kvc_p.reshape(B * T, 256), pe_rows, w1_rows, b1t, w2bd, b2t, R=128)
    kc_p = cmp_p[0].reshape(B, nb, 128)
    vcT_p = cmp_p[1].reshape(B, nb, 128).transpose(0, 2, 1)
    yb_p = _p_attn_call(qT_p, ksel_p, kwin_p, vselT_p, vwinT_p, kc_p, vcT_p, gzT_p, tq=512, kc=512)

    cache_pages = jnp.transpose(cache_kv[0], (0, 2, 3, 4, 1)).reshape(n_pool, 512, page_size)
    pool = _cmp_pages_call(cache_pages, pe_pages, w1_pages, b1t, w2bd, b2t, Pt=64)
    kv_rows_s = jnp.transpose(kvT_s[0], (1, 0)).reshape(SB, ST, KV_COLS)
    newblk = jnp.pad(jnp.transpose(kv_rows_s[:, :, :256], (0, 2, 1)), ((0, 0), (0, 0), (0, page_size - ST)))
    newc = _cmp_pages_call(newblk, pe_pages, w1_pages, b1t, w2bd, b2t, Pt=SB)

    q_r = jnp.pad(q_s.reshape(SB, ST, KV_HEADS, GQA, HEAD_DIM).transpose(0, 2, 3, 1, 4),
                  ((0, 0), (0, 0), (0, 0), (0, 8 - ST), (0, 0))).reshape(SB, KV_HEADS, 8 * GQA, HEAD_DIM)
    ocmp_s, imp_s = _s_cmp_call(page_table, q_r, pool, newc.reshape(SB, 1, 512), past_len=past_len)
    idx = _s_select_call(imp_s.reshape(SB * KV_HEADS * 8, 2 * n_pages), n_pages=n_pages, n_pick=N_SELECT - 1)
    idx_flat = idx.reshape(SB, KV_HEADS, 8, 128)[:, :, :ST, :N_SELECT].reshape(-1)
    winT = jnp.transpose(state_kv_win[0], (0, 2, 3, 4, 1)).reshape(SB, 256, lbuf)
    gz_r = jnp.pad(gz_s[0, :, :24].reshape(SB, ST, 3, KV_HEADS, GQA).transpose(0, 3, 4, 1, 2),
                   ((0, 0), (0, 0), (0, 0), (0, 8 - ST), (0, 128 - 3))).reshape(SB, KV_HEADS, 8 * GQA, 128)
    yb_r, wnextT = _s_attn_call(page_table, idx_flat, q_r, kvT_s[0], winT, gz_r, ocmp_s, cache_pages,
                        n_tok=ST, past_len=past_len)
    yb_s = yb_r.reshape(SB, KV_HEADS, GQA, 8, HEAD_DIM)[:, :, :, :ST].transpose(0, 3, 1, 2, 4)
    yb_s = yb_s.reshape(1, SB * ST, B_WIDTH).astype(BF16)

    y_p, up_p = _post_call(x_prompt, mod_p, ya_p, yb_p, g1, g2, gf, wm, wba, wbb, wo, wup, wdn, cw,
                           tm=256, seq_len=T)
    st = state_ffn_conv[0]
    zrow = jnp.zeros((SB, 1, F2), F32)
    p1 = jnp.concatenate([st[:, 1:2], jnp.tile(zrow, (1, ST - 1, 1))], axis=1).reshape(SB * ST, F2)
    p2 = jnp.concatenate([st, jnp.tile(zrow, (1, ST - 2, 1))], axis=1).reshape(SB * ST, F2)
    y_s, up_s = _post_call(xs2, mod_s, ya_s, yb_s, g1, g2, gf, wm, wba, wbb, wo, wup, wdn, cw,
                           prev=(_chunk_cols(p1, 2 * N_FF_CHUNKS), _chunk_cols(p2, 2 * N_FF_CHUNKS)),
                           tm=SB * ST, seq_len=ST)

    kv_prompt = jnp.transpose(kv4T_p.reshape(B, 4, KV_HEADS, HEAD_DIM, T), (0, 4, 1, 2, 3))[None]
    win_prompt = jnp.transpose(winT_p[:, :, T - WINDOW:].reshape(B, 2, KV_HEADS, HEAD_DIM, WINDOW),
                               (0, 4, 1, 2, 3))[None]
    kv_sample = kv_rows_s[:, :, :512].reshape(SB, ST, 4, KV_HEADS, HEAD_DIM)[None]
    win_sample = jnp.transpose(wnextT.reshape(SB, 2, KV_HEADS, HEAD_DIM, lbuf), (0, 4, 1, 2, 3))[None]
    v_chunk = vn_s.reshape(SB, ST, A_WIDTH)[None]
    conv_prompt = up_p[:, :, 6:8, :].transpose(0, 2, 1, 3).reshape(B, 2, F2)[None]
    up_rows = up_s.transpose(1, 0, 2).reshape(SB, ST, F2)
    conv_sample = up_rows[:, ST - 2:][None]
    return (y_p, y_s.reshape(SB, ST, D_MODEL), kv_prompt, kv_sample, win_prompt, win_sample, v_chunk,
            conv_prompt, conv_sample)
```

```python
import functools

import jax
import jax.numpy as jnp
from jax import lax
from jax.experimental import pallas as pl
from jax.experimental.pallas import tpu as pltpu

F32 = jnp.float32
BF16 = jnp.bfloat16

D_MODEL = 1024
A_WIDTH = 512
A_GROUPS = 4
CHUNK = 128
B_HEADS = 8
HEAD_DIM = 64
B_WIDTH = 512
KV_HEADS = 2
GQA = 4
BLOCK = 64
N_SELECT = 16
WINDOW = 512
N_KV_SLOTS = 6
KV_COLS = 768
D_FF = 2816
F2 = 2 * D_FF
CONV_W = 3
EPS = 1e-6
NEG = -1e30
FORCE_SCORE = 1e4

FF_CHUNK = 256
N_FF_CHUNKS = D_FF // FF_CHUNK
ROW_BLOCK = 64
ACC_ROWS = HEAD_DIM + 16
LOG2E = 1.4426950408889634
VMEM_LIMIT = 56 * 1024 * 1024


def _dot(a, b):
    return jnp.dot(a, b, preferred_element_type=F32)


def _dot_nt(a, b):
    return lax.dot_general(a, b, (((1,), (1,)), ((), ())), preferred_element_type=F32)


def _rms_mod(x, g, scale, shift):
    y = x * lax.rsqrt(jnp.mean(x * x, axis=-1, keepdims=True) + EPS)
    return (y * g) * (1.0 + scale) + shift


def _mod_kernel(c_ref, w_ref, b_ref, o_ref):
    c = c_ref[...]
    s = c * jax.nn.sigmoid(c)
    o_ref[...] = _dot(s.astype(BF16), w_ref[...].astype(BF16)) + b_ref[...]


def _mod_call(c_all, w_ada, b_ada):
    n = c_all.shape[0]
    tn = 1536
    return pl.pallas_call(
        _mod_kernel,
        out_shape=jax.ShapeDtypeStruct((n, 6 * D_MODEL), F32),
        grid=(6 * D_MODEL // tn,),
        in_specs=[pl.BlockSpec((n, D_MODEL), lambda j: (0, 0)),
                  pl.BlockSpec((D_MODEL, tn), lambda j: (0, j)),
                  pl.BlockSpec((1, tn), lambda j: (0, j))],
        out_specs=pl.BlockSpec((n, tn), lambda j: (0, j)),
        compiler_params=pltpu.CompilerParams(dimension_semantics=("arbitrary",), vmem_limit_bytes=VMEM_LIMIT),
        name="mod",
    )(c_all, w_ada, b_ada)


def _inproj_kernel(x_ref, mod_ref, g1_ref, wa_ref, lng_ref, lnb_ref, wsp_ref, bsp_ref, *refs, tm, prompt):
    x = x_ref[0]
    hb = _rms_mod(x, g1_ref[...], mod_ref[0, 1], mod_ref[0, 0]).astype(BF16)

    ga = jax.nn.gelu(_dot(hb, wa_ref[...]))
    u = ga[:, :A_WIDTH]
    v = ga[:, A_WIDTH:]
    vc = v - jnp.mean(v, axis=-1, keepdims=True)
    vn = vc * lax.rsqrt(jnp.mean(vc * vc, axis=-1, keepdims=True) + EPS) * lng_ref[...] + lnb_ref[...]
    vnb = vn.astype(BF16)

    if prompt:
        (wqT_ref, wkvT_ref, wrow_ref, wgT_ref,
         ya_ref, qT_ref, kv4T_ref, winT_ref, vselT_ref, vwinT_ref, kvc_ref, ksel_ref, kwin_ref, gzT_ref) = refs
    else:
        (wq_ref, wkvT_ref, wkvc_ref, wg_ref, ya_ref, q_ref, kvT_ref, kvc_ref, gz_ref, vn_ref) = refs
        vn_ref[0] = vn

    for c in range(tm // CHUNK):
        rows = slice(c * CHUNK, (c + 1) * CHUNK)
        s = jnp.concatenate(
            [_dot(wsp_ref[g], vnb[rows, g * 128:(g + 1) * 128]) for g in range(A_GROUPS)], axis=1)
        ya_ref[0, rows, :] = (u[rows] * (s + bsp_ref[...])).astype(BF16)

    kvT = _dot_nt(wkvT_ref[...], hb)
    if prompt:
        qT_ref[0] = (_dot_nt(wqT_ref[...], hb) * (HEAD_DIM ** -0.5 * LOG2E)).astype(BF16)
        kv4T_ref[0] = kvT[:4 * 128]
        winT_ref[0] = kvT[4 * 128:]
        vselT_ref[0] = kvT[3 * 128:4 * 128].astype(BF16)
        vwinT_ref[0] = kvT[5 * 128:].astype(BF16)
        row = _dot(hb, wrow_ref[...])
        kvc_ref[0] = row[:, :256]
        ksel_ref[0] = row[:, 256:384].astype(BF16)
        kwin_ref[0] = row[:, 384:].astype(BF16)
        gzT_ref[0] = jax.nn.sigmoid(_dot_nt(wgT_ref[...], hb))
    else:
        q_ref[0] = (_dot(hb, wq_ref[...]) * (HEAD_DIM ** -0.5)).astype(BF16)
        kvT_ref[0] = kvT
        kvc_ref[0] = _dot(hb, wkvc_ref[...])
        gz_ref[0] = jax.nn.sigmoid(_dot(hb, wg_ref[...]))


def _inproj_call(x, mod, g1, wa, lng, lnb, wsp, bsp, proj_w, *, tm, prompt):
    G, T, _ = x.shape
    R = mod.shape[2]
    nt = T // tm
    const2 = lambda b, i: (0, 0)
    const3 = lambda b, i: (0, 0, 0)
    rows = lambda w: pl.BlockSpec((1, tm, w), lambda b, i: (b, i, 0))
    cols = lambda h: pl.BlockSpec((1, h, tm), lambda b, i: (b, 0, i))
    if prompt:
        outs = [((G, T, A_WIDTH), BF16, rows(A_WIDTH)),
                ((G, B_WIDTH, T), BF16, cols(B_WIDTH)),
                ((G, 512, T), F32, cols(512)),
                ((G, 256, T), F32, cols(256)),
                ((G, 128, T), BF16, cols(128)),
                ((G, 128, T), BF16, cols(128)),
                ((G, T, 256), F32, rows(256)),
                ((G, T, 128), BF16, rows(128)),
                ((G, T, 128), BF16, rows(128)),
                ((G, 32, T), F32, cols(32))]
    else:
        outs = [((G, T, A_WIDTH), BF16, rows(A_WIDTH)),
                ((G, T, B_WIDTH), BF16, rows(B_WIDTH)),
                ((G, KV_COLS, T), F32, cols(KV_COLS)),
                ((G, T, 256), F32, rows(256)),
                ((G, T, 128), F32, rows(128)),
                ((G, T, A_WIDTH), F32, rows(A_WIDTH))]
    return pl.pallas_call(
        functools.partial(_inproj_kernel, tm=tm, prompt=prompt),
        out_shape=[jax.ShapeDtypeStruct(s, d) for s, d, _ in outs],
        grid=(G, nt),
        in_specs=[pl.BlockSpec((1, tm, D_MODEL), lambda b, i: (b, i, 0)),
                  pl.BlockSpec((1, 6, R, D_MODEL), lambda b, i: (b, 0, 0, 0)),
                  pl.BlockSpec((1, D_MODEL), const2),
                  pl.BlockSpec(wa.shape, const2),
                  pl.BlockSpec((1, A_WIDTH), const2),
                  pl.BlockSpec((1, A_WIDTH), const2),
                  pl.BlockSpec(wsp.shape, const3),
                  pl.BlockSpec(bsp.shape, const2)] + [pl.BlockSpec(w.shape, const2) for w in proj_w],
        out_specs=[sp for _, _, sp in outs],
        compiler_params=pltpu.CompilerParams(dimension_semantics=("arbitrary", "arbitrary"),
                                             vmem_limit_bytes=VMEM_LIMIT),
        name="inproj",
    )(x, mod, g1, wa, lng, lnb, wsp, bsp, *proj_w)


def _cmp_tail(acc, b1, w2, b2):
    return _dot(jax.nn.gelu(acc + b1).astype(BF16), w2) + b2


def _cmp_rows_kernel(x_ref, pe_ref, w1_ref, b1_ref, w2_ref, b2_ref, o_ref, *, R):
    acc = jnp.zeros((R, 128), F32)
    for j in range(BLOCK):
        xj = x_ref[pl.ds(j, R, stride=BLOCK), :] + pe_ref[0, j:j + 1, :]
        acc = acc + _dot(xj.astype(BF16), w1_ref[0, j])
    o_ref[0] = _cmp_tail(acc, b1_ref[0], w2_ref[0], b2_ref[0])


def _cmp_rows_call(kvc2d, pe, w1, b1, w2, b2, *, R):
    nrows = kvc2d.shape[0] // BLOCK
    return pl.pallas_call(
        functools.partial(_cmp_rows_kernel, R=R),
        out_shape=jax.ShapeDtypeStruct((2, nrows, 128), F32),
        grid=(2, nrows // R),
        in_specs=[pl.BlockSpec((R * BLOCK, 128), lambda s, r: (r, s)),
                  pl.BlockSpec((1, BLOCK, 128), lambda s, r: (s, 0, 0)),
                  pl.BlockSpec((1, BLOCK, 128, 128), lambda s, r: (s, 0, 0, 0)),
                  pl.BlockSpec((1, 1, 128), lambda s, r: (s, 0, 0)),
                  pl.BlockSpec((1, 128, 128), lambda s, r: (s, 0, 0)),
                  pl.BlockSpec((1, 1, 128), lambda s, r: (s, 0, 0))],
        out_specs=pl.BlockSpec((1, R, 128), lambda s, r: (s, r, 0)),
        compiler_params=pltpu.CompilerParams(dimension_semantics=("arbitrary", "arbitrary"),
                                             vmem_limit_bytes=VMEM_LIMIT),
        name="cmp_rows",
    )(kvc2d, pe, w1, b1, w2, b2)


def _cmp_pages_kernel(x_ref, pe_ref, w1_ref, b1_ref, w2_ref, b2_ref, o_ref, *, Pt):
    for s in range(2):
        acc = jnp.zeros((2 * Pt, 128), F32)
        for dg in range(HEAD_DIM // 8):
            r0 = s * 128 + dg * 8
            xa = jnp.swapaxes(x_ref[:, r0:r0 + 8, :], 0, 1)
            xb = jnp.swapaxes(x_ref[:, HEAD_DIM + r0:HEAD_DIM + r0 + 8, :], 0, 1)
            for dd in range(0, 8, 2):
                dp = dg * 4 + dd // 2
                xd = jnp.concatenate([jnp.concatenate([xa[dd], xb[dd]], axis=0),
                                      jnp.concatenate([xa[dd + 1], xb[dd + 1]], axis=0)], axis=1)
                xd = xd + pe_ref[s, dp:dp + 1, :]
                acc = acc + _dot(xd.astype(BF16), w1_ref[s, dp])
        r = _cmp_tail(acc, b1_ref[s], w2_ref[s], b2_ref[s])
        o_ref[:, (2 * s) * 128:(2 * s + 1) * 128] = r[:Pt]
        o_ref[:, (2 * s + 1) * 128:(2 * s + 2) * 128] = r[Pt:]


def _cmp_pages_call(pages, pe, w1, b1, w2, b2, *, Pt):
    P = pages.shape[0]
    full = lambda i: (0, 0, 0)
    return pl.pallas_call(
        functools.partial(_cmp_pages_kernel, Pt=Pt),
        out_shape=jax.ShapeDtypeStruct((P, 512), F32),
        grid=(P // Pt,),
        in_specs=[pl.BlockSpec((Pt, 256, 128), lambda i: (i, 0, 0)),
                  pl.BlockSpec(pe.shape, full),
                  pl.BlockSpec(w1.shape, lambda i: (0, 0, 0, 0)),
                  pl.BlockSpec(b1.shape, full),
                  pl.BlockSpec(w2.shape, full),
                  pl.BlockSpec(b2.shape, full)],
        out_specs=pl.BlockSpec((Pt, 512), lambda i: (i, 0)),
        compiler_params=pltpu.CompilerParams(dimension_semantics=("arbitrary",), vmem_limit_bytes=VMEM_LIMIT),
        name="cmp_pages",
    )(pages, pe, w1, b1, w2, b2)


def _pair_schedule(T, tq, kc):
    js, cs = [], []
    for j in range(T // tq):
        for c in range(((j + 1) * tq - 1) // kc + 1):
            js.append(j)
            cs.append(c)
    return js, cs


def _p_attn_kernel(jt_ref, ct_ref, qT_ref, ksel_ref, kwin_ref, vselT_ref, vwinT_ref, kc_ref, vcT_ref, gzT_ref,
                   yb_ref, sel_s, ocmp_s, m_s, acc_s, s_scr, p_scr, b_scr, *, tq, kc, T, strip):
    p = pl.program_id(1)
    j = jt_ref[p]
    c = ct_ref[p]
    nb = T // BLOCK
    N = GQA * tq
    q0 = j * tq
    k0 = c * kc
    c_last = ((j + 1) * tq - 1) // kc

    def q_pad(h):
        qT = jnp.concatenate([qT_ref[0, (h * GQA + g) * HEAD_DIM:(h * GQA + g + 1) * HEAD_DIM, :]
                              for g in range(GQA)], axis=1)
        z = jnp.zeros_like(qT)
        return jnp.concatenate([qT if k == h else z for k in range(KV_HEADS)], axis=0)

    @pl.when(c == 0)
    def _():
        m_s[...] = jnp.full(m_s.shape, NEG, F32)
        acc_s[...] = jnp.zeros(acc_s.shape, F32)
        tok_n = q0 + (lax.broadcasted_iota(jnp.int32, (1, N), 1) & (tq - 1))
        tok_1 = q0 + lax.broadcasted_iota(jnp.int32, (1, tq), 1)
        blk = lax.broadcasted_iota(jnp.int32, (nb, 1), 0)
        avail = (blk + 1) * BLOCK <= tok_n + 1
        cur = tok_1 >> 6
        forced = (blk == 0) | (blk == cur) | (blk == cur - 1)
        future = blk > cur
        kcb = kc_ref[0].astype(BF16)
        for h in range(KV_HEADS):
            s = jnp.where(avail, _dot(kcb, q_pad(h)), NEG)
            m = jnp.max(s, axis=0, keepdims=True)
            e = jnp.where(avail, jnp.exp2(s - m), 0.0)
            pr = e / jnp.maximum(jnp.sum(e, axis=0, keepdims=True), 1e-30)
            ocmp_s[h] = _dot(vcT_ref[0, h * HEAD_DIM:(h + 1) * HEAD_DIM, :].astype(BF16), pr.astype(BF16))
            imp = pr[:, 0:tq]
            for g in range(1, GQA):
                imp = imp + pr[:, g * tq:(g + 1) * tq]
            imp = jnp.where(forced, FORCE_SCORE, jnp.where(future, -1.0, imp))
            rows = []
            for i in range(nb):
                vi = imp[i:i + 1, :]
                ahead = (imp > vi) | ((imp == vi) & (blk < i))
                cnt = jnp.sum(ahead.astype(F32), axis=0, keepdims=True)
                rows.append((cnt < float(N_SELECT)).astype(F32))
            sel_s[h] = jnp.concatenate(rows + [jnp.zeros((128 - nb, tq), F32)], axis=0).astype(BF16)

    kpos = k0 + lax.broadcasted_iota(jnp.int32, (kc, tq), 0)
    tok = q0 + lax.broadcasted_iota(jnp.int32, (kc, tq), 1)

    strips = [slice(st * strip, (st + 1) * strip) for st in range(N // strip)]

    row_blocks = [slice(r, r + ROW_BLOCK) for r in range(0, kc, ROW_BLOCK)]

    def scores(h, keys, bh):
        mcur = []
        for st, ls in enumerate(strips):
            g, t0 = divmod(st * strip, tq)
            r0 = (h * GQA + g) * HEAD_DIM
            qT = qT_ref[0, r0:r0 + HEAD_DIM, t0:t0 + strip]
            z = jnp.zeros_like(qT)
            qp = jnp.concatenate([qT if k == h else z for k in range(KV_HEADS)], axis=0)
            s_scr[h, :, ls] = _dot(keys, qp)
        for st, ls in enumerate(strips):
            t0 = (st * strip) % tq
            m = None
            for rb in row_blocks:
                t = s_scr[h, rb, ls] + b_scr[bh, rb, t0:t0 + strip]
                s_scr[h, rb, ls] = t
                m = t if m is None else jnp.maximum(m, t)
            mcur.append(jnp.max(m, axis=0, keepdims=True))
        return jnp.concatenate(mcur, axis=1)

    def probs(idx, h, mcur):
        m_old = m_s[idx]
        m_new = jnp.maximum(m_old, mcur)
        alpha = jnp.exp2(m_old - m_new)
        m_s[idx] = m_new
        for ls in strips:
            mb = m_new[:, ls]
            for rb in row_blocks:
                p_scr[h, rb, ls] = jnp.exp2(s_scr[h, rb, ls] - mb).astype(BF16)
        return alpha

    ones_rows = (lax.broadcasted_iota(jnp.int32, (ACC_ROWS - HEAD_DIM, kc), 0) == 0).astype(BF16)

    def weighted_values(idx, h, vT, alpha):
        vT1 = jnp.concatenate([vT, ones_rows], axis=0)
        for ls in strips:
            acc_s[idx, :, ls] = alpha[:, ls] * acc_s[idx, :, ls] + _dot(vT1, p_scr[h, :, ls])

    def online_softmax(first_idx, keys, vT_ref, bias_slots):
        heads = range(KV_HEADS)
        mcur = [scores(h, keys, bias_slots[h]) for h in heads]
        alpha = [probs(first_idx + h, h, mcur[h]) for h in heads]
        for h in heads:
            weighted_values(first_idx + h, h, vT_ref[0, h * HEAD_DIM:(h + 1) * HEAD_DIM, :], alpha[h])

    blk_of_key = (k0 >> 6) + (lax.broadcasted_iota(jnp.int32, (kc, 128), 0) >> 6)
    expand = (lax.broadcasted_iota(jnp.int32, (kc, 128), 1) == blk_of_key).astype(BF16)
    causal = kpos <= tok
    for h in range(KV_HEADS):
        b_scr[h] = jnp.where((_dot(expand, sel_s[h]) > 0.5) & causal, 0.0, NEG)
    online_softmax(0, ksel_ref[0], vselT_ref, list(range(KV_HEADS)))

    @pl.when((c + 1) * kc - 1 >= q0 - (WINDOW - 1))
    def _():
        diff = tok - kpos
        b_scr[0] = jnp.where((diff >= 0) & (diff < WINDOW), 0.0, NEG)
        online_softmax(KV_HEADS, kwin_ref[0], vwinT_ref, [0] * KV_HEADS)

    @pl.when(c == c_last)
    def _():
        gz = gzT_ref[0]
        outs = []
        for h in range(KV_HEADS):
            a_sel = acc_s[h]
            a_win = acc_s[KV_HEADS + h]
            o_sel = a_sel[:HEAD_DIM] / jnp.maximum(a_sel[HEAD_DIM:HEAD_DIM + 1], 1e-30)
            o_win = a_win[:HEAD_DIM] / jnp.maximum(a_win[HEAD_DIM:HEAD_DIM + 1], 1e-30)
            o_cmp = ocmp_s[h]
            for g in range(GQA):
                col = h * GQA + g
                ls = slice(g * tq, (g + 1) * tq)
                outs.append(gz[col:col + 1] * o_cmp[:, ls] + gz[B_HEADS + col:B_HEADS + col + 1] * o_sel[:, ls]
                            + gz[2 * B_HEADS + col:2 * B_HEADS + col + 1] * o_win[:, ls])
        yb_ref[0] = jnp.concatenate(outs, axis=0).T.astype(BF16)


def _p_attn_call(qT, ksel, kwin, vselT, vwinT, kc_all, vcT, gzT, *, tq, kc):
    B, _, T = qT.shape
    nb = T // BLOCK
    N = GQA * tq
    js, cs = _pair_schedule(T, tq, kc)
    jt = jnp.asarray(js, jnp.int32)
    ct = jnp.asarray(cs, jnp.int32)
    return pl.pallas_call(
        functools.partial(_p_attn_kernel, tq=tq, kc=kc, T=T, strip=256),
        out_shape=jax.ShapeDtypeStruct((B, T, B_WIDTH), BF16),
        grid_spec=pltpu.PrefetchScalarGridSpec(
            num_scalar_prefetch=2,
            grid=(B, len(js)),
            in_specs=[pl.BlockSpec((1, B_WIDTH, tq), lambda b, p, jt, ct: (b, 0, jt[p])),
                      pl.BlockSpec((1, kc, 128), lambda b, p, jt, ct: (b, ct[p], 0)),
                      pl.BlockSpec((1, kc, 128), lambda b, p, jt, ct: (b, ct[p], 0)),
                      pl.BlockSpec((1, 128, kc), lambda b, p, jt, ct: (b, 0, ct[p])),
                      pl.BlockSpec((1, 128, kc), lambda b, p, jt, ct: (b, 0, ct[p])),
                      pl.BlockSpec((1, nb, 128), lambda b, p, jt, ct: (b, 0, 0)),
                      pl.BlockSpec((1, 128, nb), lambda b, p, jt, ct: (b, 0, 0)),
                      pl.BlockSpec((1, 32, tq), lambda b, p, jt, ct: (b, 0, jt[p]))],
            out_specs=pl.BlockSpec((1, tq, B_WIDTH), lambda b, p, jt, ct: (b, jt[p], 0)),
            scratch_shapes=[pltpu.VMEM((KV_HEADS, 128, tq), BF16),
                            pltpu.VMEM((KV_HEADS, HEAD_DIM, N), F32),
                            pltpu.VMEM((2 * KV_HEADS, 1, N), F32),
                            pltpu.VMEM((2 * KV_HEADS, ACC_ROWS, N), F32),
                            pltpu.VMEM((KV_HEADS, kc, N), F32),
                            pltpu.VMEM((KV_HEADS, kc, N), BF16),
                            pltpu.VMEM((KV_HEADS, kc, tq), F32)]),
        compiler_params=pltpu.CompilerParams(dimension_semantics=("arbitrary", "arbitrary"),
                                             vmem_limit_bytes=VMEM_LIMIT),
        name="p_attn",
    )(jt, ct, qT, ksel, kwin, vselT, vwinT, kc_all, vcT, gzT)


def _s_cmp_kernel(pt_ref, q_ref, pool_ref, newc_ref, ocmp_ref, imp_ref, gath, *, n_pages, past_len):
    b = pl.program_id(0)

    def gather(jp, carry):
        gath[pl.ds(jp, 1), :] = pool_ref[pl.ds(pt_ref[b, jp], 1), :]
        return carry

    lax.fori_loop(0, n_pages, gather, 0)

    rows = 8 * GQA
    t_row = lax.broadcasted_iota(jnp.int32, (rows, 1), 0) & 7
    pos = past_len + t_row
    lane = lax.broadcasted_iota(jnp.int32, (1, 2 * n_pages), 1)
    blk = 2 * (lane & (n_pages - 1)) + (lane >> 7)
    new_blk = past_len // BLOCK
    avail = (blk + 1) * BLOCK <= pos + 1
    avail_new = (new_blk + 1) * BLOCK <= pos + 1
    for h in range(KV_HEADS):
        qh = q_ref[0, h]
        kc = [gath[:, (h * 2 + k) * 64:(h * 2 + k + 1) * 64].astype(BF16) for k in range(2)]
        vc = [gath[:, (2 + h) * 128 + k * 64:(2 + h) * 128 + (k + 1) * 64].astype(BF16) for k in range(2)]
        kc_new = newc_ref[0, :, h * 128:h * 128 + 64].astype(BF16).astype(F32)
        vc_new = newc_ref[0, :, (2 + h) * 128:(2 + h) * 128 + 64].astype(BF16).astype(F32)
        s = jnp.where(avail, jnp.concatenate([_dot_nt(qh, kc[0]), _dot_nt(qh, kc[1])], axis=1), NEG)
        s_new = jnp.where(avail_new, jnp.sum(qh.astype(F32) * kc_new, axis=-1, keepdims=True), NEG)
        m = jnp.maximum(jnp.max(s, axis=-1, keepdims=True), s_new)
        e = jnp.where(avail, jnp.exp(s - m), 0.0)
        e_new = jnp.where(avail_new, jnp.exp(s_new - m), 0.0)
        den = jnp.maximum(jnp.sum(e, axis=-1, keepdims=True) + e_new, 1e-30)
        p = e / den
        p_new = e_new / den
        pb = p.astype(BF16)
        ocmp_ref[0, h] = (_dot(pb[:, :n_pages], vc[0]) + _dot(pb[:, n_pages:], vc[1])
                          + p_new.astype(BF16).astype(F32) * vc_new)
        imp = p[0:8] + p[8:16] + p[16:24] + p[24:32]
        cur = (past_len + lax.broadcasted_iota(jnp.int32, (8, 1), 0)) >> 6
        forced = (blk == 0) | (blk == cur) | (blk == cur - 1)
        imp_ref[0, h] = jnp.where(forced, FORCE_SCORE, jnp.where(blk > cur, -1.0, imp))


def _s_cmp_call(page_table, q_r, pool, newc, *, past_len):
    nbatch, n_pages = page_table.shape
    P = pool.shape[0]
    return pl.pallas_call(
        functools.partial(_s_cmp_kernel, n_pages=n_pages, past_len=past_len),
        out_shape=[jax.ShapeDtypeStruct((nbatch, KV_HEADS, 32, HEAD_DIM), F32),
                   jax.ShapeDtypeStruct((nbatch, KV_HEADS, 8, 2 * n_pages), F32)],
        grid_spec=pltpu.PrefetchScalarGridSpec(
            num_scalar_prefetch=1,
            grid=(nbatch,),
            in_specs=[pl.BlockSpec((1, KV_HEADS, 32, HEAD_DIM), lambda b, pt: (b, 0, 0, 0)),
                      pl.BlockSpec((P, 512), lambda b, pt: (0, 0)),
                      pl.BlockSpec((1, 1, 512), lambda b, pt: (b, 0, 0))],
            out_specs=[pl.BlockSpec((1, KV_HEADS, 32, HEAD_DIM), lambda b, pt: (b, 0, 0, 0)),
                       pl.BlockSpec((1, KV_HEADS, 8, 2 * n_pages), lambda b, pt: (b, 0, 0, 0))],
            scratch_shapes=[pltpu.VMEM((n_pages, 512), F32)]),
        compiler_params=pltpu.CompilerParams(dimension_semantics=("arbitrary",), vmem_limit_bytes=VMEM_LIMIT),
        name="s_cmp",
    )(page_table, q_r, pool, newc)


def _s_select_kernel(imp_ref, idx_ref, *, n_pages, n_pick):
    v = imp_ref[...]
    rows = v.shape[0]
    lane = lax.broadcasted_iota(jnp.int32, (1, 2 * n_pages), 1)
    blk = (2 * (lane & (n_pages - 1)) + (lane >> 7)).astype(F32)
    out_lane = lax.broadcasted_iota(jnp.int32, (1, 128), 1)
    out = jnp.full((rows, 128), float(2 * n_pages), F32)
    for k in range(n_pick):
        m = jnp.max(v, axis=-1, keepdims=True)
        pick = jnp.min(jnp.where(v == m, blk, 1e9), axis=-1, keepdims=True)
        out = jnp.where(out_lane == k, pick, out)
        v = jnp.where(blk == pick, NEG, v)
    idx_ref[...] = out.astype(jnp.int32)


def _s_select_call(imp2d, *, n_pages, n_pick):
    rows = imp2d.shape[0]
    return pl.pallas_call(
        functools.partial(_s_select_kernel, n_pages=n_pages, n_pick=n_pick),
        out_shape=jax.ShapeDtypeStruct((rows, 128), jnp.int32),
        grid=(1,),
        in_specs=[pl.BlockSpec(imp2d.shape, lambda i: (0, 0))],
        out_specs=pl.BlockSpec((rows, 128), lambda i: (0, 0)),
        compiler_params=pltpu.CompilerParams(dimension_semantics=("arbitrary",), vmem_limit_bytes=VMEM_LIMIT),
        name="s_select",
    )(imp2d)


def _s_attn_kernel(pt_ref, idx_ref, q_ref, kvn_ref, win_ref, gz_ref, ocmp_ref, cache_ref, yb_ref, wnext_ref,
                   kbuf, vbuf, sem, *, n_tok, n_gather, past_len):
    b = pl.program_id(0)
    span = n_gather * 128
    slot = b & 1

    def tile_copies(bb, sl, t, h, jj):
        n = idx_ref[((bb * KV_HEADS + h) * n_tok + t) * N_SELECT + jj]
        page = pt_ref[bb, n >> 1]
        kc = pltpu.make_async_copy(cache_ref.at[page, pl.ds(2 * 128 + h * 64, 64), :],
                                   kbuf.at[sl, h, t, :, pl.ds(jj * 128, 128)], sem.at[sl, 0])
        vc = pltpu.make_async_copy(cache_ref.at[page, pl.ds(3 * 128 + h * 64, 64), :],
                                   vbuf.at[sl, h, t, :, pl.ds(jj * 128, 128)], sem.at[sl, 1])
        return kc, vc

    def for_all_tiles(bb, sl, fn):
        for h in range(KV_HEADS):
            for t in range(n_tok):
                for jj in range(n_gather):
                    kc, vc = tile_copies(bb, sl, t, h, jj)
                    fn(kc)
                    fn(vc)

    @pl.when(b == 0)
    def _():
        for_all_tiles(0, 0, lambda cp: cp.start())

    @pl.when(b + 1 < pl.num_programs(0))
    def _():
        for_all_tiles(b + 1, 1 - slot, lambda cp: cp.start())

    rows = 8 * GQA
    t_row = lax.broadcasted_iota(jnp.int32, (rows, 1), 0) & 7
    gz = gz_ref[0]

    lane_n = lax.broadcasted_iota(jnp.int32, (1, 128), 1)
    tok_shift = n_tok.bit_length() - 1
    new_ok = ((lane_n >> tok_shift) == b) & ((lane_n & (n_tok - 1)) <= t_row)
    lane_w = lax.broadcasted_iota(jnp.int32, (1, WINDOW), 1)
    win_ok = lane_w > t_row

    shifted = pltpu.roll(win_ref[0], WINDOW - n_tok, 1)
    moved = pltpu.roll(kvn_ref[4 * 128:, :], (128 - n_tok) - b * n_tok, 1)
    wnext_ref[0, :, :WINDOW - 128] = shifted[:, :WINDOW - 128]
    wnext_ref[0, :, WINDOW - 128:] = jnp.where(lane_n >= 128 - n_tok, moved, shifted[:, WINDOW - 128:])

    def softmax_pv(parts):
        m = None
        for sc, mk, _ in parts:
            mm = jnp.max(jnp.where(mk, sc, NEG), axis=-1, keepdims=True)
            m = mm if m is None else jnp.maximum(m, mm)
        den = jnp.zeros((rows, 1), F32)
        o = jnp.zeros((rows, HEAD_DIM), F32)
        for sc, mk, vts in parts:
            e = jnp.where(mk, jnp.exp(jnp.where(mk, sc, NEG) - m), 0.0)
            den = den + jnp.sum(e, axis=-1, keepdims=True)
            for rmask, vt in vts:
                er = e if rmask is None else jnp.where(rmask, e, 0.0)
                o = o + _dot_nt(er.astype(BF16), vt)
        return o / jnp.maximum(den, 1e-30)

    win_out = []
    for h in range(KV_HEADS):
        qh = q_ref[0, h]
        kw = win_ref[0, h * 64:(h + 1) * 64, :].astype(BF16)
        vw = win_ref[0, 128 + h * 64:128 + (h + 1) * 64, :].astype(BF16)
        kwn = kvn_ref[4 * 128 + h * 64:4 * 128 + (h + 1) * 64, :].astype(BF16)
        vwn = kvn_ref[5 * 128 + h * 64:5 * 128 + (h + 1) * 64, :].astype(BF16)
        win_out.append(softmax_pv([(_dot(qh, kw), win_ok, [(None, vw)]),
                                   (_dot(qh, kwn), new_ok, [(None, vwn)])]))

    for_all_tiles(b, slot, lambda cp: cp.wait())

    lane_s = lax.broadcasted_iota(jnp.int32, (1, span), 1)
    for h in range(KV_HEADS):
        qh = q_ref[0, h]
        sc = jnp.zeros((rows, span), F32)
        half = jnp.zeros((rows, span), jnp.int32)
        for t in range(n_tok):
            st = _dot(qh, kbuf[slot, h, t].astype(BF16))
            hrow = jnp.zeros((1, span), jnp.int32)
            for jj in range(n_gather):
                n = idx_ref[((b * KV_HEADS + h) * n_tok + t) * N_SELECT + jj]
                hrow = jnp.where((lane_s >> 7) == jj, n & 1, hrow)
            sc = jnp.where(t_row == t, st, sc)
            half = jnp.where(t_row == t, hrow, half)
        ok = ((lane_s >> 6) & 1) == half
        ksn = kvn_ref[2 * 128 + h * 64:2 * 128 + (h + 1) * 64, :].astype(BF16)
        vsn = kvn_ref[3 * 128 + h * 64:3 * 128 + (h + 1) * 64, :].astype(BF16)
        vts = [(t_row == t, vbuf[slot, h, t].astype(BF16)) for t in range(n_tok)]
        o_sel = softmax_pv([(sc, ok, vts), (_dot(qh, ksn), new_ok, [(None, vsn)])])
        yb_ref[0, h] = (gz[h, :, 0:1] * ocmp_ref[0, h] + gz[h, :, 1:2] * o_sel + gz[h, :, 2:3] * win_out[h])


def _s_attn_call(page_table, idx_flat, q_r, kvT_new, winT, gz_r, ocmp, cache_pages, *, n_tok, past_len):
    nbatch = page_table.shape[0]
    n_gather = N_SELECT - 1
    span = n_gather * 128
    blk4 = lambda b, pt, ix: (b, 0, 0, 0)
    return pl.pallas_call(
        functools.partial(_s_attn_kernel, n_tok=n_tok, n_gather=n_gather, past_len=past_len),
        out_shape=[jax.ShapeDtypeStruct((nbatch, KV_HEADS, 32, HEAD_DIM), F32),
                   jax.ShapeDtypeStruct((nbatch, 256, WINDOW), F32)],
        grid_spec=pltpu.PrefetchScalarGridSpec(
            num_scalar_prefetch=2,
            grid=(nbatch,),
            in_specs=[pl.BlockSpec((1, KV_HEADS, 32, HEAD_DIM), blk4),
                      pl.BlockSpec(kvT_new.shape, lambda b, pt, ix: (0, 0)),
                      pl.BlockSpec((1, 256, WINDOW), lambda b, pt, ix: (b, 0, 0)),
                      pl.BlockSpec((1, KV_HEADS, 32, 128), blk4),
                      pl.BlockSpec((1, KV_HEADS, 32, HEAD_DIM), blk4),
                      pl.BlockSpec(memory_space=pl.ANY)],
            out_specs=[pl.BlockSpec((1, KV_HEADS, 32, HEAD_DIM), blk4),
                       pl.BlockSpec((1, 256, WINDOW), lambda b, pt, ix: (b, 0, 0))],
            scratch_shapes=[pltpu.VMEM((2, KV_HEADS, n_tok, HEAD_DIM, span), F32),
                            pltpu.VMEM((2, KV_HEADS, n_tok, HEAD_DIM, span), F32),
                            pltpu.SemaphoreType.DMA((2, 2))]),
        compiler_params=pltpu.CompilerParams(dimension_semantics=("arbitrary",), vmem_limit_bytes=VMEM_LIMIT),
        name="s_attn",
    )(page_table, idx_flat, q_r, kvT_new, winT, gz_r, ocmp, cache_pages)


def _post_kernel(*refs, tm, seq_len, has_prev):
    it = iter(refs)
    x_ref, mod_ref, ya_ref, yb_ref = next(it), next(it), next(it), next(it)
    g1_ref, g2_ref, gf_ref = next(it), next(it), next(it)
    wm_ref, wba_ref, wbb_ref, wo_ref, wup_ref, wdn_ref, cw_ref = (next(it) for _ in range(7))
    p1_ref = next(it) if has_prev else None
    p2_ref = next(it) if has_prev else None
    y_ref, up_ref = next(it), next(it)
    carry, h2_s, act_s = next(it), next(it), next(it)

    i = pl.program_id(1)
    x = x_ref[0]
    shift1, scale1, gate1 = mod_ref[0, 0], mod_ref[0, 1], mod_ref[0, 2]
    shift2, scale2, gate2 = mod_ref[0, 3], mod_ref[0, 4], mod_ref[0, 5]

    hb = _rms_mod(x, g1_ref[...], scale1, shift1).astype(BF16)
    gates = jax.nn.sigmoid(_dot(hb, wm_ref[...]))
    mix = (gates[:, :D_MODEL] * _dot(ya_ref[0], wba_ref[...])
           + gates[:, D_MODEL:] * _dot(yb_ref[0], wbb_ref[...]))
    x1 = x + gate1 * _dot(mix.astype(BF16), wo_ref[...])
    h2_s[...] = _rms_mod(x1, g2_ref[...], scale2, shift2).astype(BF16)

    if not has_prev:
        @pl.when(i == 0)
        def _():
            carry[...] = jnp.zeros(carry.shape, F32)

    row = lax.broadcasted_iota(jnp.int32, (tm, 1), 0)
    row8 = lax.broadcasted_iota(jnp.int32, (8, 1), 0)

    def conv_cols(col0):
        cols = slice(col0, col0 + FF_CHUNK)
        up = _dot(h2_s[...], wup_ref[:, cols])
        r1 = pltpu.roll(up, 1, 0)
        r2 = pltpu.roll(up, 2, 0)
        if has_prev:
            s1 = jnp.where((row & (seq_len - 1)) == 0, p1_ref[:, cols], r1)
            s2 = jnp.where((row & (seq_len - 1)) < 2, p2_ref[:, cols], r2)
            up_ref[:, cols] = up
        else:
            prev = carry[:, cols]
            t1 = jnp.where(row8 == 0, pltpu.roll(prev, 1, 0), r1[:8])
            t2 = jnp.where(row8 < 2, pltpu.roll(prev, 2, 0), r2[:8])
            s1 = jnp.concatenate([t1, r1[8:]], axis=0)
            s2 = jnp.concatenate([t2, r2[8:]], axis=0)
            carry[:, cols] = up[tm - 8:]
            up_ref[0, :, cols] = up[tm - 8:]
        cw = cw_ref[:, cols]
        return cw[3:4] + cw[0:1] * s2 + cw[1:2] * s1 + cw[2:3] * up

    for c in range(N_FF_CHUNKS):
        a = conv_cols(c * FF_CHUNK)
        gv = conv_cols(D_FF + c * FF_CHUNK)
        act_s[:, c * FF_CHUNK:(c + 1) * FF_CHUNK] = (jax.nn.gelu(a) * gv).astype(BF16)
    x2 = x1 + gate2 * _dot(act_s[...], wdn_ref[...])
    y_ref[0] = x2 * lax.rsqrt(jnp.mean(x2 * x2, axis=-1, keepdims=True) + EPS) * gf_ref[...]


def _post_call(x, mod, ya, yb, g1, g2, gf, wm, wba, wbb, wo, wup, wdn, cw, prev=None, *, tm, seq_len):
    G, T, _ = x.shape
    R = mod.shape[2]
    nt = T // tm
    has_prev = prev is not None
    single = pl.Buffered(1)
    c2 = lambda b, i: (0, 0)
    c3 = lambda b, i: (0, 0, 0)
    in_specs = [pl.BlockSpec((1, tm, D_MODEL), lambda b, i: (b, i, 0)),
                pl.BlockSpec((1, 6, R, D_MODEL), lambda b, i: (b, 0, 0, 0)),
                pl.BlockSpec((1, tm, A_WIDTH), lambda b, i: (b, i, 0)),
                pl.BlockSpec((1, tm, B_WIDTH), lambda b, i: (b, i, 0)),
                pl.BlockSpec((1, D_MODEL), c2),
                pl.BlockSpec((1, D_MODEL), c2),
                pl.BlockSpec((1, D_MODEL), c2),
                pl.BlockSpec(wm.shape, c2, pipeline_mode=single),
                pl.BlockSpec(wba.shape, c2, pipeline_mode=single),
                pl.BlockSpec(wbb.shape, c2, pipeline_mode=single),
                pl.BlockSpec(wo.shape, c2, pipeline_mode=single),
                pl.BlockSpec(wup.shape, c2, pipeline_mode=single),
                pl.BlockSpec(wdn.shape, c2, pipeline_mode=single),
                pl.BlockSpec(cw.shape, c2, pipeline_mode=single)]
    args = [x, mod, ya, yb, g1, g2, gf, wm, wba, wbb, wo, wup, wdn, cw]
    if has_prev:
        in_specs += [pl.BlockSpec(prev[0].shape, c2), pl.BlockSpec(prev[1].shape, c2)]
        args += list(prev)
        up_shape = jax.ShapeDtypeStruct((T, F2), F32)
        up_spec = pl.BlockSpec((tm, F2), lambda b, i: (i, 0))
    else:
        up_shape = jax.ShapeDtypeStruct((G, 8, F2), F32)
        up_spec = pl.BlockSpec((1, 8, F2), lambda b, i: (b, 0, 0))
    return pl.pallas_call(
        functools.partial(_post_kernel, tm=tm, seq_len=seq_len, has_prev=has_prev),
        out_shape=[jax.ShapeDtypeStruct((G, T, D_MODEL), F32), up_shape],
        grid=(G, nt),
        in_specs=in_specs,
        out_specs=[pl.BlockSpec((1, tm, D_MODEL), lambda b, i: (b, i, 0)), up_spec],
        scratch_shapes=[pltpu.VMEM((8, F2), F32),
                        pltpu.VMEM((tm, D_MODEL), BF16),
                        pltpu.VMEM((tm, D_FF), BF16)],
        compiler_params=pltpu.CompilerParams(dimension_semantics=("arbitrary", "arbitrary"),
                                             vmem_limit_bytes=VMEM_LIMIT),
        name="post",
    )(*args)


def _block_diag2(m):
    z = jnp.zeros_like(m)
    return jnp.concatenate([jnp.concatenate([m, z], axis=-1), jnp.concatenate([z, m], axis=-1)], axis=-2)


def kernel(x_prompt, x_sample, cache_kv, state_kv_win, state_ffn_conv, page_table, c_prompt, c_sample, w_ada, b_ada, g_norm1, w_in, ln_v_g, ln_v_b, w_spatial, b_spatial, cmp_pe, cmp_w1, cmp_b1, cmp_w2, cmp_b2, w_branch_a, w_branch_b, w_out, g_norm2, w_up, w_conv, b_conv, w_down, g_final):
    B, T, _ = x_prompt.shape
    SB, ST, _ = x_sample.shape
    n_pool, page_size = cache_kv.shape[1], cache_kv.shape[2]
    n_pages = page_table.shape[1]
    past_len = n_pages * page_size
    lbuf = state_kv_win.shape[2]
    assert cache_kv.shape[0] == 1 and page_size == 128 and lbuf == WINDOW and SB * ST == 128 and ST <= 8
    assert past_len % BLOCK == 0 and n_pages == 128 and T % 512 == 0 and ST & (ST - 1) == 0 and ST >= 2

    win = w_in[0]
    wa = win[:, :2 * A_WIDTH].astype(BF16)
    o = 2 * A_WIDTH
    wq = win[:, o:o + B_WIDTH].astype(BF16)
    o += B_WIDTH
    wkv = win[:, o:o + KV_COLS]
    wkvT = wkv.T.astype(BF16)
    wkvc = wkv[:, :256].astype(BF16)
    o += KV_COLS
    wg = jnp.pad(win[:, o:o + 3 * B_HEADS], ((0, 0), (0, 128 - 3 * B_HEADS))).astype(BF16)
    wgT = jnp.pad(win[:, o:o + 3 * B_HEADS].T, ((0, 32 - 3 * B_HEADS), (0, 0))).astype(BF16)
    wqT = wq.T
    wrow = jnp.concatenate([wkv[:, :3 * 128], wkv[:, 4 * 128:5 * 128]], axis=1).astype(BF16)
    o += 3 * B_HEADS
    wm = win[:, o:].astype(BF16)
    g1 = g_norm1[0][None]
    g2 = g_norm2[0][None]
    gf = g_final[None]
    lng = ln_v_g[0][None]
    lnb = ln_v_b[0][None]

    ws = w_spatial[0]
    bs = b_spatial[0]
    wsp_p = jnp.tril(ws).astype(BF16)
    bsp_p = jnp.repeat(bs.T, 128, axis=1)
    w4 = jnp.tril(ws[:, :ST, :ST])
    eye_s = jnp.eye(128 // ST, dtype=F32)
    wsp_s = jnp.einsum('ab,gts->gatbs', eye_s, w4).reshape(A_GROUPS, 128, 128).astype(BF16)
    bsp_s = jnp.repeat(jnp.tile(bs[:, :ST].T, (128 // ST, 1)), 128, axis=1)

    w1, pe = cmp_w1[0].astype(BF16), cmp_pe[0]
    w1_rows = _block_diag2(w1)
    pe_rows = jnp.tile(pe, (1, 1, 2))
    w1_pages = _block_diag2(w1.transpose(0, 2, 1, 3)).reshape(2, HEAD_DIM // 2, 256, 128)
    pe_pages = jnp.tile(pe.transpose(0, 2, 1), (1, 1, 2)).reshape(2, HEAD_DIM // 2, 256)
    b1t = jnp.tile(cmp_b1[0], (1, 2))[:, None, :]
    b2t = jnp.tile(cmp_b2[0], (1, 2))[:, None, :]
    w2bd = _block_diag2(cmp_w2[0].astype(BF16))

    wba = w_branch_a[0].astype(BF16)
    wbb = w_branch_b[0].astype(BF16)
    wo = w_out[0].astype(BF16)
    wup = w_up[0].astype(BF16)
    wdn = w_down[0].astype(BF16)
    cw = jnp.concatenate([w_conv[0], b_conv[0][None], jnp.zeros((4, F2), F32)], axis=0)

    mod = _mod_call(jnp.concatenate([c_prompt, c_sample], axis=0), w_ada[0], b_ada)
    mod_p = mod[:B].reshape(B, 6, 1, D_MODEL)
    mod_s = jnp.repeat(mod[B:], ST, axis=0).reshape(SB * ST, 6, D_MODEL).transpose(1, 0, 2)[None]

    ya_p, qT_p, kv4T_p, winT_p, vselT_p, vwinT_p, kvc_p, ksel_p, kwin_p, gzT_p = _inproj_call(
        x_prompt, mod_p, g1, wa, lng, lnb, wsp_p, bsp_p, (wqT, wkvT, wrow, wgT), tm=512, prompt=True)
    xs2 = x_sample.reshape(1, SB * ST, D_MODEL)
    ya_s, q_s, kvT_s, kvc_s, gz_s, vn_s = _inproj_call(
        xs2, mod_s, g1, wa, lng, lnb, wsp_s, bsp_s, (wq, wkvT, wkvc, wg), tm=128, prompt=False)

    nb = T // BLOCK
    cmp_p = _cmp_rows_call(kvc_p.reshape(B * T, 256), pe_rows, w1_rows, b1t, w2bd, b2t, R=128)
    kc_p = cmp_p[0].reshape(B, nb, 128)
    vcT_p = cmp_p[1].reshape(B, nb, 128).transpose(0, 2, 1)
    yb_p = _p_attn_call(qT_p, ksel_p, kwin_p, vselT_p, vwinT_p, kc_p, vcT_p, gzT_p, tq=512, kc=512)

    cache_pages = jnp.transpose(cache_kv[0], (0, 2, 3, 4, 1)).reshape(n_pool, 512, page_size)
    pool = _cmp_pages_call(cache_pages, pe_pages, w1_pages, b1t, w2bd, b2t, Pt=64)
    kv_rows_s = jnp.transpose(kvT_s[0], (1, 0)).reshape(SB, ST, KV_COLS)
    newblk = jnp.pad(jnp.transpose(kv_rows_s[:, :, :256], (0, 2, 1)), ((0, 0), (0, 0), (0, page_size - ST)))
    newc = _cmp_pages_call(newblk, pe_pages, w1_pages, b1t, w2bd, b2t, Pt=SB)

    q_r = jnp.pad(q_s.reshape(SB, ST, KV_HEADS, GQA, HEAD_DIM).transpose(0, 2, 3, 1, 4),
                  ((0, 0), (0, 0), (0, 0), (0, 8 - ST), (0, 0))).reshape(SB, KV_HEADS, 8 * GQA, HEAD_DIM)
    ocmp_s, imp_s = _s_cmp_call(page_table, q_r, pool, newc.reshape(SB, 1, 512), past_len=past_len)
    idx = _s_select_call(imp_s.reshape(SB * KV_HEADS * 8, 2 * n_pages), n_pages=n_pages, n_pick=N_SELECT - 1)
    idx_flat = idx.reshape(SB, KV_HEADS, 8, 128)[:, :, :ST, :N_SELECT].reshape(-1)
    winT = jnp.transpose(state_kv_win[0], (0, 2, 3, 4, 1)).reshape(SB, 256, lbuf)
    gz_r = jnp.pad(gz_s[0, :, :24].reshape(SB, ST, 3, KV_HEADS, GQA).transpose(0, 3, 4, 1, 2),
                   ((0, 0), (0, 0), (0, 0), (0, 8 - ST), (0, 128 - 3))).reshape(SB, KV_HEADS, 8 * GQA, 128)
    yb_r, wnextT = _s_attn_call(page_table, idx_flat, q_r, kvT_s[0], winT, gz_r, ocmp_s, cache_pages,
                        n_tok=ST, past_len=past_len)
    yb_s = yb_r.reshape(SB, KV_HEADS, GQA, 8, HEAD_DIM)[:, :, :, :ST].transpose(0, 3, 1, 2, 4)
    yb_s = yb_s.reshape(1, SB * ST, B_WIDTH).astype(BF16)

    y_p, up_p = _post_call(x_prompt, mod_p, ya_p, yb_p, g1, g2, gf, wm, wba, wbb, wo, wup, wdn, cw,
                           tm=256, seq_len=T)
    st = state_ffn_conv[0]
    zrow = jnp.zeros((SB, 1, F2), F32)
    p1 = jnp.concatenate([st[:, 1:2], jnp.tile(zrow, (1, ST - 1, 1))], axis=1).reshape(SB * ST, F2)
    p2 = jnp.concatenate([st, jnp.tile(zrow, (1, ST - 2, 1))], axis=1).reshape(SB * ST, F2)
    y_s, up_s = _post_call(xs2, mod_s, ya_s, yb_s, g1, g2, gf, wm, wba, wbb, wo, wup, wdn, cw,
                           prev=(p1, p2),
                           tm=SB * ST, seq_len=ST)

    kv_prompt = jnp.transpose(kv4T_p.reshape(B, 4, KV_HEADS, HEAD_DIM, T), (0, 4, 1, 2, 3))[None]
    win_prompt = jnp.transpose(winT_p[:, :, T - WINDOW:].reshape(B, 2, KV_HEADS, HEAD_DIM, WINDOW),
                               (0, 4, 1, 2, 3))[None]
    kv_sample = kv_rows_s[:, :, :512].reshape(SB, ST, 4, KV_HEADS, HEAD_DIM)[None]
    win_sample = jnp.transpose(wnextT.reshape(SB, 2, KV_HEADS, HEAD_DIM, lbuf), (0, 4, 1, 2, 3))[None]
    v_chunk = vn_s.reshape(SB, ST, A_WIDTH)[None]
    conv_prompt = up_p[:, 6:8][None]
    conv_sample = up_s.reshape(SB, ST, F2)[:, ST - 2:][None]
    return (y_p, y_s.reshape(SB, ST, D_MODEL), kv_prompt, kv_sample, win_prompt, win_sample, v_chunk,
            conv_prompt, conv_sample)
```

```python
import functools

import jax
import jax.numpy as jnp
from jax import lax
from jax.experimental import pallas as pl
from jax.experimental.pallas import tpu as pltpu

F32 = jnp.float32
BF16 = jnp.bfloat16

D_MODEL = 1024
A_WIDTH = 512
A_GROUPS = 4
CHUNK = 128
B_HEADS = 8
HEAD_DIM = 64
B_WIDTH = 512
KV_HEADS = 2
GQA = 4
BLOCK = 64
N_SELECT = 16
WINDOW = 512
N_KV_SLOTS = 6
KV_COLS = 768
D_FF = 2816
F2 = 2 * D_FF
CONV_W = 3
EPS = 1e-6
NEG = -1e30
FORCE_SCORE = 1e4

FF_CHUNK = 256
N_FF_CHUNKS = D_FF // FF_CHUNK
ROW_BLOCK = 64
PE_ROWS = 16
ACC_ROWS = HEAD_DIM + 16
LOG2E = 1.4426950408889634
VMEM_LIMIT = 56 * 1024 * 1024


def _dot(a, b):
    return jnp.dot(a, b, preferred_element_type=F32)


def _dot_nt(a, b):
    return lax.dot_general(a, b, (((1,), (1,)), ((), ())), preferred_element_type=F32)


def _rms_mod(x, g, scale, shift):
    y = x * lax.rsqrt(jnp.mean(x * x, axis=-1, keepdims=True) + EPS)
    return (y * g) * (1.0 + scale) + shift


def _mod_kernel(c_ref, w_ref, b_ref, o_ref):
    c = c_ref[...]
    s = c * jax.nn.sigmoid(c)
    o_ref[...] = _dot(s.astype(BF16), w_ref[...].astype(BF16)) + b_ref[...]


def _mod_call(c_all, w_ada, b_ada):
    n = c_all.shape[0]
    tn = 1536
    return pl.pallas_call(
        _mod_kernel,
        out_shape=jax.ShapeDtypeStruct((n, 6 * D_MODEL), F32),
        grid=(6 * D_MODEL // tn,),
        in_specs=[pl.BlockSpec((n, D_MODEL), lambda j: (0, 0)),
                  pl.BlockSpec((D_MODEL, tn), lambda j: (0, j)),
                  pl.BlockSpec((1, tn), lambda j: (0, j))],
        out_specs=pl.BlockSpec((n, tn), lambda j: (0, j)),
        compiler_params=pltpu.CompilerParams(dimension_semantics=("arbitrary",), vmem_limit_bytes=VMEM_LIMIT),
        name="mod",
    )(c_all, w_ada, b_ada)


def _inproj_kernel(x_ref, mod_ref, g1_ref, wa_ref, lng_ref, lnb_ref, wsp_ref, bsp_ref, *refs, tm, prompt):
    x = x_ref[0]
    hb = _rms_mod(x, g1_ref[...], mod_ref[0, 1], mod_ref[0, 0]).astype(BF16)

    ga = jax.nn.gelu(_dot(hb, wa_ref[...]))
    u = ga[:, :A_WIDTH]
    v = ga[:, A_WIDTH:]
    vc = v - jnp.mean(v, axis=-1, keepdims=True)
    vn = vc * lax.rsqrt(jnp.mean(vc * vc, axis=-1, keepdims=True) + EPS) * lng_ref[...] + lnb_ref[...]
    vnb = vn.astype(BF16)

    if prompt:
        (wqT_ref, wkvT_ref, wrow_ref, wgT_ref,
         ya_ref, qT_ref, kv4T_ref, winT_ref, vselT_ref, vwinT_ref, kvc_ref, ksel_ref, kwin_ref, gzT_ref) = refs
    else:
        (wq_ref, wkvT_ref, wkvc_ref, wg_ref, ya_ref, q_ref, kvT_ref, kvc_ref, gz_ref, vn_ref) = refs
        vn_ref[0] = vn

    for c in range(tm // CHUNK):
        rows = slice(c * CHUNK, (c + 1) * CHUNK)
        s = jnp.concatenate(
            [_dot(wsp_ref[g], vnb[rows, g * 128:(g + 1) * 128]) for g in range(A_GROUPS)], axis=1)
        ya_ref[0, rows, :] = (u[rows] * (s + bsp_ref[...])).astype(BF16)

    kvT = _dot_nt(wkvT_ref[...], hb)
    if prompt:
        qT_ref[0] = (_dot_nt(wqT_ref[...], hb) * (HEAD_DIM ** -0.5 * LOG2E)).astype(BF16)
        kv4T_ref[0] = kvT[:4 * 128]
        winT_ref[0] = kvT[4 * 128:]
        vselT_ref[0] = kvT[3 * 128:4 * 128].astype(BF16)
        vwinT_ref[0] = kvT[5 * 128:].astype(BF16)
        row = _dot(hb, wrow_ref[...])
        kvc_ref[0] = row[:, :256]
        ksel_ref[0] = row[:, 256:384].astype(BF16)
        kwin_ref[0] = row[:, 384:].astype(BF16)
        gzT_ref[0] = jax.nn.sigmoid(_dot_nt(wgT_ref[...], hb))
    else:
        q_ref[0] = (_dot(hb, wq_ref[...]) * (HEAD_DIM ** -0.5)).astype(BF16)
        kvT_ref[0] = kvT
        kvc_ref[0] = _dot(hb, wkvc_ref[...])
        gz_ref[0] = jax.nn.sigmoid(_dot(hb, wg_ref[...]))


def _inproj_call(x, mod, g1, wa, lng, lnb, wsp, bsp, proj_w, *, tm, prompt):
    G, T, _ = x.shape
    R = mod.shape[2]
    nt = T // tm
    const2 = lambda b, i: (0, 0)
    const3 = lambda b, i: (0, 0, 0)
    rows = lambda w: pl.BlockSpec((1, tm, w), lambda b, i: (b, i, 0))
    cols = lambda h: pl.BlockSpec((1, h, tm), lambda b, i: (b, 0, i))
    if prompt:
        outs = [((G, T, A_WIDTH), BF16, rows(A_WIDTH)),
                ((G, B_WIDTH, T), BF16, cols(B_WIDTH)),
                ((G, 512, T), F32, cols(512)),
                ((G, 256, T), F32, cols(256)),
                ((G, 128, T), BF16, cols(128)),
                ((G, 128, T), BF16, cols(128)),
                ((G, T, 256), F32, rows(256)),
                ((G, T, 128), BF16, rows(128)),
                ((G, T, 128), BF16, rows(128)),
                ((G, 32, T), F32, cols(32))]
    else:
        outs = [((G, T, A_WIDTH), BF16, rows(A_WIDTH)),
                ((G, T, B_WIDTH), BF16, rows(B_WIDTH)),
                ((G, KV_COLS, T), F32, cols(KV_COLS)),
                ((G, T, 256), F32, rows(256)),
                ((G, T, 128), F32, rows(128)),
                ((G, T, A_WIDTH), F32, rows(A_WIDTH))]
    return pl.pallas_call(
        functools.partial(_inproj_kernel, tm=tm, prompt=prompt),
        out_shape=[jax.ShapeDtypeStruct(s, d) for s, d, _ in outs],
        grid=(G, nt),
        in_specs=[pl.BlockSpec((1, tm, D_MODEL), lambda b, i: (b, i, 0)),
                  pl.BlockSpec((1, 6, R, D_MODEL), lambda b, i: (b, 0, 0, 0)),
                  pl.BlockSpec((1, D_MODEL), const2),
                  pl.BlockSpec(wa.shape, const2),
                  pl.BlockSpec((1, A_WIDTH), const2),
                  pl.BlockSpec((1, A_WIDTH), const2),
                  pl.BlockSpec(wsp.shape, const3),
                  pl.BlockSpec(bsp.shape, const2)] + [pl.BlockSpec(w.shape, const2) for w in proj_w],
        out_specs=[sp for _, _, sp in outs],
        compiler_params=pltpu.CompilerParams(dimension_semantics=("arbitrary", "arbitrary"),
                                             vmem_limit_bytes=VMEM_LIMIT),
        name="inproj",
    )(x, mod, g1, wa, lng, lnb, wsp, bsp, *proj_w)


def _cmp_tail(acc, b1, w2, b2):
    return _dot(jax.nn.gelu(acc + b1).astype(BF16), w2) + b2


def _cmp_rows_kernel(x_ref, pe_ref, w1_ref, b1_ref, w2_ref, b2_ref, o_ref, *, R):
    acc = jnp.zeros((R, 128), F32)
    for j in range(BLOCK):
        xj = x_ref[pl.ds(j, R, stride=BLOCK), :] + pe_ref[0, j:j + 1, :]
        acc = acc + _dot(xj.astype(BF16), w1_ref[0, j])
    o_ref[0] = _cmp_tail(acc, b1_ref[0], w2_ref[0], b2_ref[0])


def _cmp_rows_call(kvc2d, pe, w1, b1, w2, b2, *, R):
    nrows = kvc2d.shape[0] // BLOCK
    return pl.pallas_call(
        functools.partial(_cmp_rows_kernel, R=R),
        out_shape=jax.ShapeDtypeStruct((2, nrows, 128), F32),
        grid=(2, nrows // R),
        in_specs=[pl.BlockSpec((R * BLOCK, 128), lambda s, r: (r, s)),
                  pl.BlockSpec((1, BLOCK, 128), lambda s, r: (s, 0, 0)),
                  pl.BlockSpec((1, BLOCK, 128, 128), lambda s, r: (s, 0, 0, 0)),
                  pl.BlockSpec((1, 1, 128), lambda s, r: (s, 0, 0)),
                  pl.BlockSpec((1, 128, 128), lambda s, r: (s, 0, 0)),
                  pl.BlockSpec((1, 1, 128), lambda s, r: (s, 0, 0))],
        out_specs=pl.BlockSpec((1, R, 128), lambda s, r: (s, r, 0)),
        compiler_params=pltpu.CompilerParams(dimension_semantics=("arbitrary", "arbitrary"),
                                             vmem_limit_bytes=VMEM_LIMIT),
        name="cmp_rows",
    )(kvc2d, pe, w1, b1, w2, b2)


def _compress_pages(load_rows, pe_ref, w1_ref, b1_ref, w2_ref, b2_ref, o_ref, Pt):
    for s in range(2):
        acc = jnp.zeros((2 * Pt + PE_ROWS, 128), F32)
        for dg in range(HEAD_DIM // 8):
            r0 = s * 128 + dg * 8
            xa = jnp.swapaxes(load_rows(r0), 0, 1).astype(BF16)
            xb = jnp.swapaxes(load_rows(HEAD_DIM + r0), 0, 1).astype(BF16)
            for dd in range(0, 8, 2):
                dp = dg * 4 + dd // 2
                xd = jnp.concatenate([jnp.concatenate([xa[dd], xa[dd + 1]], axis=1),
                                      jnp.concatenate([xb[dd], xb[dd + 1]], axis=1),
                                      pe_ref[s, dp]], axis=0)
                acc = acc + _dot(xd, w1_ref[s, dp])
        pre = acc[:2 * Pt] + acc[2 * Pt:2 * Pt + 1]
        r = _cmp_tail(pre, b1_ref[s], w2_ref[s], b2_ref[s])
        o_ref[:, (2 * s) * 128:(2 * s + 1) * 128] = r[:Pt]
        o_ref[:, (2 * s + 1) * 128:(2 * s + 2) * 128] = r[Pt:]


def _cmp_pages_kernel(x_ref, pe_ref, w1_ref, b1_ref, w2_ref, b2_ref, o_ref, *, Pt):
    _compress_pages(lambda r0: x_ref[:, r0:r0 + 8, :], pe_ref, w1_ref, b1_ref, w2_ref, b2_ref, o_ref, Pt)


def _cmp_weight_specs(ws, index):
    return [pl.BlockSpec(w.shape, functools.partial(index, (0,) * w.ndim)) for w in ws]


def _cmp_pages_call(pages, pe, w1, b1, w2, b2, *, Pt):
    P = pages.shape[0]
    ws = (pe, w1, b1, w2, b2)
    return pl.pallas_call(
        functools.partial(_cmp_pages_kernel, Pt=Pt),
        out_shape=jax.ShapeDtypeStruct((P, 512), F32),
        grid=(P // Pt,),
        in_specs=[pl.BlockSpec((Pt, 256, 128), lambda i: (i, 0, 0))] + _cmp_weight_specs(ws, lambda z, i: z),
        out_specs=pl.BlockSpec((Pt, 512), lambda i: (i, 0)),
        compiler_params=pltpu.CompilerParams(dimension_semantics=("arbitrary",), vmem_limit_bytes=VMEM_LIMIT),
        name="cmp_pages",
    )(pages, *ws)


def _cmp_gather_kernel(pt_ref, cache_ref, pe_ref, w1_ref, b1_ref, w2_ref, b2_ref, o_ref, xbuf, sem, *, Pt):
    i = pl.program_id(0)
    slot = i & 1

    def page_copy(step, sl, k):
        return pltpu.make_async_copy(cache_ref.at[pt_ref[step * Pt + k], pl.ds(0, 256), :],
                                     xbuf.at[sl, k], sem.at[sl])

    @pl.when(i == 0)
    def _():
        for k in range(Pt):
            page_copy(0, 0, k).start()

    @pl.when(i + 1 < pl.num_programs(0))
    def _():
        for k in range(Pt):
            page_copy(i + 1, 1 - slot, k).start()

    for k in range(Pt):
        page_copy(i, slot, k).wait()
    _compress_pages(lambda r0: xbuf[slot, :, r0:r0 + 8, :], pe_ref, w1_ref, b1_ref, w2_ref, b2_ref, o_ref, Pt)


def _cmp_gather_call(pt_flat, cache_pages, pe, w1, b1, w2, b2, *, Pt):
    n = pt_flat.shape[0]
    ws = (pe, w1, b1, w2, b2)
    return pl.pallas_call(
        functools.partial(_cmp_gather_kernel, Pt=Pt),
        out_shape=jax.ShapeDtypeStruct((n, 512), F32),
        grid_spec=pltpu.PrefetchScalarGridSpec(
            num_scalar_prefetch=1,
            grid=(n // Pt,),
            in_specs=[pl.BlockSpec(memory_space=pl.ANY)] + _cmp_weight_specs(ws, lambda z, i, pt: z),
            out_specs=pl.BlockSpec((Pt, 512), lambda i, pt: (i, 0)),
            scratch_shapes=[pltpu.VMEM((2, Pt, 256, 128), F32),
                            pltpu.SemaphoreType.DMA((2,))]),
        compiler_params=pltpu.CompilerParams(dimension_semantics=("arbitrary",), vmem_limit_bytes=VMEM_LIMIT),
        name="cmp_gather",
    )(pt_flat, cache_pages, *ws)


def _pair_schedule(T, tq, kc):
    js, cs = [], []
    for j in range(T // tq):
        for c in range(((j + 1) * tq - 1) // kc + 1):
            js.append(j)
            cs.append(c)
    return js, cs


def _p_attn_kernel(jt_ref, ct_ref, qT_ref, ksel_ref, kwin_ref, vselT_ref, vwinT_ref, kc_ref, vcT_ref, gzT_ref,
                   yb_ref, sel_s, ocmp_s, m_s, acc_s, s_scr, p_scr, b_scr, *, tq, kc, T, strip):
    p = pl.program_id(1)
    j = jt_ref[p]
    c = ct_ref[p]
    nb = T // BLOCK
    N = GQA * tq
    q0 = j * tq
    k0 = c * kc
    c_last = ((j + 1) * tq - 1) // kc

    def q_pad(h):
        qT = jnp.concatenate([qT_ref[0, (h * GQA + g) * HEAD_DIM:(h * GQA + g + 1) * HEAD_DIM, :]
                              for g in range(GQA)], axis=1)
        z = jnp.zeros_like(qT)
        return jnp.concatenate([qT if k == h else z for k in range(KV_HEADS)], axis=0)

    @pl.when(c == 0)
    def _():
        m_s[...] = jnp.full(m_s.shape, NEG, F32)
        acc_s[...] = jnp.zeros(acc_s.shape, F32)
        tok_n = q0 + (lax.broadcasted_iota(jnp.int32, (1, N), 1) & (tq - 1))
        tok_1 = q0 + lax.broadcasted_iota(jnp.int32, (1, tq), 1)
        blk = lax.broadcasted_iota(jnp.int32, (nb, 1), 0)
        avail = (blk + 1) * BLOCK <= tok_n + 1
        cur = tok_1 >> 6
        forced = (blk == 0) | (blk == cur) | (blk == cur - 1)
        future = blk > cur
        kcb = kc_ref[0].astype(BF16)
        for h in range(KV_HEADS):
            s = jnp.where(avail, _dot(kcb, q_pad(h)), NEG)
            m = jnp.max(s, axis=0, keepdims=True)
            e = jnp.where(avail, jnp.exp2(s - m), 0.0)
            pr = e / jnp.maximum(jnp.sum(e, axis=0, keepdims=True), 1e-30)
            ocmp_s[h] = _dot(vcT_ref[0, h * HEAD_DIM:(h + 1) * HEAD_DIM, :].astype(BF16), pr.astype(BF16))
            imp = pr[:, 0:tq]
            for g in range(1, GQA):
                imp = imp + pr[:, g * tq:(g + 1) * tq]
            imp = jnp.where(forced, FORCE_SCORE, jnp.where(future, -1.0, imp))
            rows = []
            for i in range(nb):
                vi = imp[i:i + 1, :]
                ahead = (imp > vi) | ((imp == vi) & (blk < i))
                cnt = jnp.sum(ahead.astype(F32), axis=0, keepdims=True)
                rows.append((cnt < float(N_SELECT)).astype(F32))
            sel_s[h] = jnp.concatenate(rows + [jnp.zeros((128 - nb, tq), F32)], axis=0).astype(BF16)

    kpos = k0 + lax.broadcasted_iota(jnp.int32, (kc, tq), 0)
    tok = q0 + lax.broadcasted_iota(jnp.int32, (kc, tq), 1)

    strips = [slice(st * strip, (st + 1) * strip) for st in range(N // strip)]

    row_blocks = [slice(r, r + ROW_BLOCK) for r in range(0, kc, ROW_BLOCK)]

    def scores(h, keys, bh):
        mcur = []
        for st, ls in enumerate(strips):
            g, t0 = divmod(st * strip, tq)
            r0 = (h * GQA + g) * HEAD_DIM
            qT = qT_ref[0, r0:r0 + HEAD_DIM, t0:t0 + strip]
            z = jnp.zeros_like(qT)
            qp = jnp.concatenate([qT if k == h else z for k in range(KV_HEADS)], axis=0)
            s_scr[h, :, ls] = _dot(keys, qp)
        for st, ls in enumerate(strips):
            t0 = (st * strip) % tq
            m = None
            for rb in row_blocks:
                t = s_scr[h, rb, ls] + b_scr[bh, rb, t0:t0 + strip]
                s_scr[h, rb, ls] = t
                m = t if m is None else jnp.maximum(m, t)
            mcur.append(jnp.max(m, axis=0, keepdims=True))
        return jnp.concatenate(mcur, axis=1)

    def probs(idx, h, mcur):
        m_old = m_s[idx]
        m_new = jnp.maximum(m_old, mcur)
        alpha = jnp.exp2(m_old - m_new)
        m_s[idx] = m_new
        for ls in strips:
            mb = m_new[:, ls]
            for rb in row_blocks:
                p_scr[h, rb, ls] = jnp.exp2(s_scr[h, rb, ls] - mb).astype(BF16)
        return alpha

    ones_rows = (lax.broadcasted_iota(jnp.int32, (ACC_ROWS - HEAD_DIM, kc), 0) == 0).astype(BF16)

    def weighted_values(idx, h, vT, alpha):
        vT1 = jnp.concatenate([vT, ones_rows], axis=0)
        for ls in strips:
            acc_s[idx, :, ls] = alpha[:, ls] * acc_s[idx, :, ls] + _dot(vT1, p_scr[h, :, ls])

    def online_softmax(first_idx, keys, vT_ref, bias_slots):
        heads = range(KV_HEADS)
        mcur = [scores(h, keys, bias_slots[h]) for h in heads]
        alpha = [probs(first_idx + h, h, mcur[h]) for h in heads]
        for h in heads:
            weighted_values(first_idx + h, h, vT_ref[0, h * HEAD_DIM:(h + 1) * HEAD_DIM, :], alpha[h])

    blk_of_key = (k0 >> 6) + (lax.broadcasted_iota(jnp.int32, (kc, 128), 0) >> 6)
    expand = (lax.broadcasted_iota(jnp.int32, (kc, 128), 1) == blk_of_key).astype(BF16)
    causal = kpos <= tok
    for h in range(KV_HEADS):
        b_scr[h] = jnp.where((_dot(expand, sel_s[h]) > 0.5) & causal, 0.0, NEG)
    online_softmax(0, ksel_ref[0], vselT_ref, list(range(KV_HEADS)))

    @pl.when((c + 1) * kc - 1 >= q0 - (WINDOW - 1))
    def _():
        diff = tok - kpos
        b_scr[0] = jnp.where((diff >= 0) & (diff < WINDOW), 0.0, NEG)
        online_softmax(KV_HEADS, kwin_ref[0], vwinT_ref, [0] * KV_HEADS)

    @pl.when(c == c_last)
    def _():
        gz = gzT_ref[0]
        outs = []
        for h in range(KV_HEADS):
            a_sel = acc_s[h]
            a_win = acc_s[KV_HEADS + h]
            o_sel = a_sel[:HEAD_DIM] / jnp.maximum(a_sel[HEAD_DIM:HEAD_DIM + 1], 1e-30)
            o_win = a_win[:HEAD_DIM] / jnp.maximum(a_win[HEAD_DIM:HEAD_DIM + 1], 1e-30)
            o_cmp = ocmp_s[h]
            for g in range(GQA):
                col = h * GQA + g
                ls = slice(g * tq, (g + 1) * tq)
                outs.append(gz[col:col + 1] * o_cmp[:, ls] + gz[B_HEADS + col:B_HEADS + col + 1] * o_sel[:, ls]
                            + gz[2 * B_HEADS + col:2 * B_HEADS + col + 1] * o_win[:, ls])
        yb_ref[0] = jnp.concatenate(outs, axis=0).T.astype(BF16)


def _p_attn_call(qT, ksel, kwin, vselT, vwinT, kc_all, vcT, gzT, *, tq, kc):
    B, _, T = qT.shape
    nb = T // BLOCK
    N = GQA * tq
    js, cs = _pair_schedule(T, tq, kc)
    jt = jnp.asarray(js, jnp.int32)
    ct = jnp.asarray(cs, jnp.int32)
    return pl.pallas_call(
        functools.partial(_p_attn_kernel, tq=tq, kc=kc, T=T, strip=256),
        out_shape=jax.ShapeDtypeStruct((B, T, B_WIDTH), BF16),
        grid_spec=pltpu.PrefetchScalarGridSpec(
            num_scalar_prefetch=2,
            grid=(B, len(js)),
            in_specs=[pl.BlockSpec((1, B_WIDTH, tq), lambda b, p, jt, ct: (b, 0, jt[p])),
                      pl.BlockSpec((1, kc, 128), lambda b, p, jt, ct: (b, ct[p], 0)),
                      pl.BlockSpec((1, kc, 128), lambda b, p, jt, ct: (b, ct[p], 0)),
                      pl.BlockSpec((1, 128, kc), lambda b, p, jt, ct: (b, 0, ct[p])),
                      pl.BlockSpec((1, 128, kc), lambda b, p, jt, ct: (b, 0, ct[p])),
                      pl.BlockSpec((1, nb, 128), lambda b, p, jt, ct: (b, 0, 0)),
                      pl.BlockSpec((1, 128, nb), lambda b, p, jt, ct: (b, 0, 0)),
                      pl.BlockSpec((1, 32, tq), lambda b, p, jt, ct: (b, 0, jt[p]))],
            out_specs=pl.BlockSpec((1, tq, B_WIDTH), lambda b, p, jt, ct: (b, jt[p], 0)),
            scratch_shapes=[pltpu.VMEM((KV_HEADS, 128, tq), BF16),
                            pltpu.VMEM((KV_HEADS, HEAD_DIM, N), F32),
                            pltpu.VMEM((2 * KV_HEADS, 1, N), F32),
                            pltpu.VMEM((2 * KV_HEADS, ACC_ROWS, N), F32),
                            pltpu.VMEM((KV_HEADS, kc, N), F32),
                            pltpu.VMEM((KV_HEADS, kc, N), BF16),
                            pltpu.VMEM((KV_HEADS, kc, tq), F32)]),
        compiler_params=pltpu.CompilerParams(dimension_semantics=("arbitrary", "arbitrary"),
                                             vmem_limit_bytes=VMEM_LIMIT),
        name="p_attn",
    )(jt, ct, qT, ksel, kwin, vselT, vwinT, kc_all, vcT, gzT)


def _s_cmp_kernel(q_ref, gath, newc_ref, ocmp_ref, imp_ref, *, n_pages, past_len):
    rows = 8 * GQA
    t_row = lax.broadcasted_iota(jnp.int32, (rows, 1), 0) & 7
    pos = past_len + t_row
    lane = lax.broadcasted_iota(jnp.int32, (1, 2 * n_pages), 1)
    blk = 2 * (lane & (n_pages - 1)) + (lane >> 7)
    new_blk = past_len // BLOCK
    avail = (blk + 1) * BLOCK <= pos + 1
    avail_new = (new_blk + 1) * BLOCK <= pos + 1
    for h in range(KV_HEADS):
        qh = q_ref[0, h]
        kc = [gath[:, (h * 2 + k) * 64:(h * 2 + k + 1) * 64].astype(BF16) for k in range(2)]
        vc = [gath[:, (2 + h) * 128 + k * 64:(2 + h) * 128 + (k + 1) * 64].astype(BF16) for k in range(2)]
        kc_new = newc_ref[0, :, h * 128:h * 128 + 64].astype(BF16).astype(F32)
        vc_new = newc_ref[0, :, (2 + h) * 128:(2 + h) * 128 + 64].astype(BF16).astype(F32)
        s = jnp.where(avail, jnp.concatenate([_dot_nt(qh, kc[0]), _dot_nt(qh, kc[1])], axis=1), NEG)
        s_new = jnp.where(avail_new, jnp.sum(qh.astype(F32) * kc_new, axis=-1, keepdims=True), NEG)
        m = jnp.maximum(jnp.max(s, axis=-1, keepdims=True), s_new)
        e = jnp.where(avail, jnp.exp(s - m), 0.0)
        e_new = jnp.where(avail_new, jnp.exp(s_new - m), 0.0)
        den = jnp.maximum(jnp.sum(e, axis=-1, keepdims=True) + e_new, 1e-30)
        p = e / den
        p_new = e_new / den
        pb = p.astype(BF16)
        ocmp_ref[0, h] = (_dot(pb[:, :n_pages], vc[0]) + _dot(pb[:, n_pages:], vc[1])
                          + p_new.astype(BF16).astype(F32) * vc_new)
        imp = p[0:8] + p[8:16] + p[16:24] + p[24:32]
        cur = (past_len + lax.broadcasted_iota(jnp.int32, (8, 1), 0)) >> 6
        forced = (blk == 0) | (blk == cur) | (blk == cur - 1)
        imp_ref[0, h] = jnp.where(forced, FORCE_SCORE, jnp.where(blk > cur, -1.0, imp))


def _s_cmp_call(q_r, pool_seq, newc, *, n_pages, past_len):
    nbatch = q_r.shape[0]
    return pl.pallas_call(
        functools.partial(_s_cmp_kernel, n_pages=n_pages, past_len=past_len),
        out_shape=[jax.ShapeDtypeStruct((nbatch, KV_HEADS, 32, HEAD_DIM), F32),
                   jax.ShapeDtypeStruct((nbatch, KV_HEADS, 8, 2 * n_pages), F32)],
        grid=(nbatch,),
        in_specs=[pl.BlockSpec((1, KV_HEADS, 32, HEAD_DIM), lambda b: (b, 0, 0, 0)),
                  pl.BlockSpec((n_pages, 512), lambda b: (b, 0)),
                  pl.BlockSpec((1, 1, 512), lambda b: (b, 0, 0))],
        out_specs=[pl.BlockSpec((1, KV_HEADS, 32, HEAD_DIM), lambda b: (b, 0, 0, 0)),
                   pl.BlockSpec((1, KV_HEADS, 8, 2 * n_pages), lambda b: (b, 0, 0, 0))],
        compiler_params=pltpu.CompilerParams(dimension_semantics=("arbitrary",), vmem_limit_bytes=VMEM_LIMIT),
        name="s_cmp",
    )(q_r, pool_seq, newc)


def _s_select_kernel(imp_ref, idx_ref, *, n_pages, n_pick):
    v = imp_ref[...]
    rows = v.shape[0]
    lane = lax.broadcasted_iota(jnp.int32, (1, 2 * n_pages), 1)
    blk = (2 * (lane & (n_pages - 1)) + (lane >> 7)).astype(F32)
    out_lane = lax.broadcasted_iota(jnp.int32, (1, 128), 1)
    out = jnp.full((rows, 128), float(2 * n_pages), F32)
    for k in range(n_pick):
        m = jnp.max(v, axis=-1, keepdims=True)
        pick = jnp.min(jnp.where(v == m, blk, 1e9), axis=-1, keepdims=True)
        out = jnp.where(out_lane == k, pick, out)
        v = jnp.where(blk == pick, NEG, v)
    idx_ref[...] = out.astype(jnp.int32)


def _s_select_call(imp2d, *, n_pages, n_pick):
    rows = imp2d.shape[0]
    return pl.pallas_call(
        functools.partial(_s_select_kernel, n_pages=n_pages, n_pick=n_pick),
        out_shape=jax.ShapeDtypeStruct((rows, 128), jnp.int32),
        grid=(1,),
        in_specs=[pl.BlockSpec(imp2d.shape, lambda i: (0, 0))],
        out_specs=pl.BlockSpec((rows, 128), lambda i: (0, 0)),
        compiler_params=pltpu.CompilerParams(dimension_semantics=("arbitrary",), vmem_limit_bytes=VMEM_LIMIT),
        name="s_select",
    )(imp2d)


def _s_attn_kernel(pt_ref, idx_ref, q_ref, kvn_ref, win_ref, gz_ref, ocmp_ref, cache_ref, yb_ref, wnext_ref,
                   kbuf, vbuf, sem, *, n_tok, n_gather, past_len):
    b = pl.program_id(0)
    span = n_gather * 128
    slot = b & 1

    def tile_copies(bb, sl, t, h, jj):
        n = idx_ref[((bb * KV_HEADS + h) * n_tok + t) * N_SELECT + jj]
        page = pt_ref[bb, n >> 1]
        kc = pltpu.make_async_copy(cache_ref.at[page, pl.ds(2 * 128 + h * 64, 64), :],
                                   kbuf.at[sl, h, t, :, pl.ds(jj * 128, 128)], sem.at[sl, 0])
        vc = pltpu.make_async_copy(cache_ref.at[page, pl.ds(3 * 128 + h * 64, 64), :],
                                   vbuf.at[sl, h, t, :, pl.ds(jj * 128, 128)], sem.at[sl, 1])
        return kc, vc

    def for_all_tiles(bb, sl, fn):
        for h in range(KV_HEADS):
            for t in range(n_tok):
                for jj in range(n_gather):
                    kc, vc = tile_copies(bb, sl, t, h, jj)
                    fn(kc)
                    fn(vc)

    @pl.when(b == 0)
    def _():
        for_all_tiles(0, 0, lambda cp: cp.start())

    @pl.when(b + 1 < pl.num_programs(0))
    def _():
        for_all_tiles(b + 1, 1 - slot, lambda cp: cp.start())

    rows = 8 * GQA
    t_row = lax.broadcasted_iota(jnp.int32, (rows, 1), 0) & 7
    gz = gz_ref[0]

    lane_n = lax.broadcasted_iota(jnp.int32, (1, 128), 1)
    tok_shift = n_tok.bit_length() - 1
    new_ok = ((lane_n >> tok_shift) == b) & ((lane_n & (n_tok - 1)) <= t_row)
    lane_w = lax.broadcasted_iota(jnp.int32, (1, WINDOW), 1)
    win_ok = lane_w > t_row

    shifted = pltpu.roll(win_ref[0], WINDOW - n_tok, 1)
    moved = pltpu.roll(kvn_ref[4 * 128:, :], (128 - n_tok) - b * n_tok, 1)
    wnext_ref[0, :, :WINDOW - 128] = shifted[:, :WINDOW - 128]
    wnext_ref[0, :, WINDOW - 128:] = jnp.where(lane_n >= 128 - n_tok, moved, shifted[:, WINDOW - 128:])

    def softmax_pv(parts):
        m = None
        for sc, mk, _ in parts:
            mm = jnp.max(jnp.where(mk, sc, NEG), axis=-1, keepdims=True)
            m = mm if m is None else jnp.maximum(m, mm)
        den = jnp.zeros((rows, 1), F32)
        o = jnp.zeros((rows, HEAD_DIM), F32)
        for sc, mk, vts in parts:
            e = jnp.where(mk, jnp.exp(jnp.where(mk, sc, NEG) - m), 0.0)
            den = den + jnp.sum(e, axis=-1, keepdims=True)
            for rmask, vt in vts:
                er = e if rmask is None else jnp.where(rmask, e, 0.0)
                o = o + _dot_nt(er.astype(BF16), vt)
        return o / jnp.maximum(den, 1e-30)

    win_out = []
    for h in range(KV_HEADS):
        qh = q_ref[0, h]
        kw = win_ref[0, h * 64:(h + 1) * 64, :].astype(BF16)
        vw = win_ref[0, 128 + h * 64:128 + (h + 1) * 64, :].astype(BF16)
        kwn = kvn_ref[4 * 128 + h * 64:4 * 128 + (h + 1) * 64, :].astype(BF16)
        vwn = kvn_ref[5 * 128 + h * 64:5 * 128 + (h + 1) * 64, :].astype(BF16)
        win_out.append(softmax_pv([(_dot(qh, kw), win_ok, [(None, vw)]),
                                   (_dot(qh, kwn), new_ok, [(None, vwn)])]))

    for_all_tiles(b, slot, lambda cp: cp.wait())

    lane_s = lax.broadcasted_iota(jnp.int32, (1, span), 1)
    for h in range(KV_HEADS):
        qh = q_ref[0, h]
        sc = jnp.zeros((rows, span), F32)
        half = jnp.zeros((rows, span), jnp.int32)
        for t in range(n_tok):
            st = _dot(qh, kbuf[slot, h, t].astype(BF16))
            hrow = jnp.zeros((1, span), jnp.int32)
            for jj in range(n_gather):
                n = idx_ref[((b * KV_HEADS + h) * n_tok + t) * N_SELECT + jj]
                hrow = jnp.where((lane_s >> 7) == jj, n & 1, hrow)
            sc = jnp.where(t_row == t, st, sc)
            half = jnp.where(t_row == t, hrow, half)
        ok = ((lane_s >> 6) & 1) == half
        ksn = kvn_ref[2 * 128 + h * 64:2 * 128 + (h + 1) * 64, :].astype(BF16)
        vsn = kvn_ref[3 * 128 + h * 64:3 * 128 + (h + 1) * 64, :].astype(BF16)
        vts = [(t_row == t, vbuf[slot, h, t].astype(BF16)) for t in range(n_tok)]
        o_sel = softmax_pv([(sc, ok, vts), (_dot(qh, ksn), new_ok, [(None, vsn)])])
        yb_ref[0, h] = (gz[h, :, 0:1] * ocmp_ref[0, h] + gz[h, :, 1:2] * o_sel + gz[h, :, 2:3] * win_out[h])


def _s_attn_call(page_table, idx_flat, q_r, kvT_new, winT, gz_r, ocmp, cache_pages, *, n_tok, past_len):
    nbatch = page_table.shape[0]
    n_gather = N_SELECT - 1
    span = n_gather * 128
    blk4 = lambda b, pt, ix: (b, 0, 0, 0)
    return pl.pallas_call(
        functools.partial(_s_attn_kernel, n_tok=n_tok, n_gather=n_gather, past_len=past_len),
        out_shape=[jax.ShapeDtypeStruct((nbatch, KV_HEADS, 32, HEAD_DIM), F32),
                   jax.ShapeDtypeStruct((nbatch, 256, WINDOW), F32)],
        grid_spec=pltpu.PrefetchScalarGridSpec(
            num_scalar_prefetch=2,
            grid=(nbatch,),
            in_specs=[pl.BlockSpec((1, KV_HEADS, 32, HEAD_DIM), blk4),
                      pl.BlockSpec(kvT_new.shape, lambda b, pt, ix: (0, 0)),
                      pl.BlockSpec((1, 256, WINDOW), lambda b, pt, ix: (b, 0, 0)),
                      pl.BlockSpec((1, KV_HEADS, 32, 128), blk4),
                      pl.BlockSpec((1, KV_HEADS, 32, HEAD_DIM), blk4),
                      pl.BlockSpec(memory_space=pl.ANY)],
            out_specs=[pl.BlockSpec((1, KV_HEADS, 32, HEAD_DIM), blk4),
                       pl.BlockSpec((1, 256, WINDOW), lambda b, pt, ix: (b, 0, 0))],
            scratch_shapes=[pltpu.VMEM((2, KV_HEADS, n_tok, HEAD_DIM, span), F32),
                            pltpu.VMEM((2, KV_HEADS, n_tok, HEAD_DIM, span), F32),
                            pltpu.SemaphoreType.DMA((2, 2))]),
        compiler_params=pltpu.CompilerParams(dimension_semantics=("arbitrary",), vmem_limit_bytes=VMEM_LIMIT),
        name="s_attn",
    )(page_table, idx_flat, q_r, kvT_new, winT, gz_r, ocmp, cache_pages)


def _post_kernel(*refs, tm, seq_len, has_prev):
    it = iter(refs)
    x_ref, mod_ref, ya_ref, yb_ref = next(it), next(it), next(it), next(it)
    g1_ref, g2_ref, gf_ref = next(it), next(it), next(it)
    wm_ref, wba_ref, wbb_ref, wo_ref, wup_ref, wdn_ref, cw_ref = (next(it) for _ in range(7))
    p1_ref = next(it) if has_prev else None
    p2_ref = next(it) if has_prev else None
    y_ref, up_ref = next(it), next(it)
    carry, h2_s, act_s = next(it), next(it), next(it)

    i = pl.program_id(1)
    x = x_ref[0]
    shift1, scale1, gate1 = mod_ref[0, 0], mod_ref[0, 1], mod_ref[0, 2]
    shift2, scale2, gate2 = mod_ref[0, 3], mod_ref[0, 4], mod_ref[0, 5]

    hb = _rms_mod(x, g1_ref[...], scale1, shift1).astype(BF16)
    gates = jax.nn.sigmoid(_dot(hb, wm_ref[...]))
    mix = (gates[:, :D_MODEL] * _dot(ya_ref[0], wba_ref[...])
           + gates[:, D_MODEL:] * _dot(yb_ref[0], wbb_ref[...]))
    x1 = x + gate1 * _dot(mix.astype(BF16), wo_ref[...])
    h2_s[...] = _rms_mod(x1, g2_ref[...], scale2, shift2).astype(BF16)

    if not has_prev:
        @pl.when(i == 0)
        def _():
            carry[...] = jnp.zeros(carry.shape, F32)

    row = lax.broadcasted_iota(jnp.int32, (tm, 1), 0)
    row8 = lax.broadcasted_iota(jnp.int32, (8, 1), 0)

    def conv_cols(col0):
        cols = slice(col0, col0 + FF_CHUNK)
        up = _dot(h2_s[...], wup_ref[:, cols])
        r1 = pltpu.roll(up, 1, 0)
        r2 = pltpu.roll(up, 2, 0)
        if has_prev:
            s1 = jnp.where((row & (seq_len - 1)) == 0, p1_ref[:, cols], r1)
            s2 = jnp.where((row & (seq_len - 1)) < 2, p2_ref[:, cols], r2)
            up_ref[:, cols] = up
        else:
            prev = carry[:, cols]
            t1 = jnp.where(row8 == 0, pltpu.roll(prev, 1, 0), r1[:8])
            t2 = jnp.where(row8 < 2, pltpu.roll(prev, 2, 0), r2[:8])
            s1 = jnp.concatenate([t1, r1[8:]], axis=0)
            s2 = jnp.concatenate([t2, r2[8:]], axis=0)
            carry[:, cols] = up[tm - 8:]
            up_ref[0, :, cols] = up[tm - 8:]
        cw = cw_ref[:, cols]
        return cw[3:4] + cw[0:1] * s2 + cw[1:2] * s1 + cw[2:3] * up

    for c in range(N_FF_CHUNKS):
        a = conv_cols(c * FF_CHUNK)
        gv = conv_cols(D_FF + c * FF_CHUNK)
        act_s[:, c * FF_CHUNK:(c + 1) * FF_CHUNK] = (jax.nn.gelu(a) * gv).astype(BF16)
    x2 = x1 + gate2 * _dot(act_s[...], wdn_ref[...])
    y_ref[0] = x2 * lax.rsqrt(jnp.mean(x2 * x2, axis=-1, keepdims=True) + EPS) * gf_ref[...]


def _post_call(x, mod, ya, yb, g1, g2, gf, wm, wba, wbb, wo, wup, wdn, cw, prev=None, *, tm, seq_len):
    G, T, _ = x.shape
    R = mod.shape[2]
    nt = T // tm
    has_prev = prev is not None
    single = pl.Buffered(1)
    c2 = lambda b, i: (0, 0)
    c3 = lambda b, i: (0, 0, 0)
    in_specs = [pl.BlockSpec((1, tm, D_MODEL), lambda b, i: (b, i, 0)),
                pl.BlockSpec((1, 6, R, D_MODEL), lambda b, i: (b, 0, 0, 0)),
                pl.BlockSpec((1, tm, A_WIDTH), lambda b, i: (b, i, 0)),
                pl.BlockSpec((1, tm, B_WIDTH), lambda b, i: (b, i, 0)),
                pl.BlockSpec((1, D_MODEL), c2),
                pl.BlockSpec((1, D_MODEL), c2),
                pl.BlockSpec((1, D_MODEL), c2),
                pl.BlockSpec(wm.shape, c2, pipeline_mode=single),
                pl.BlockSpec(wba.shape, c2, pipeline_mode=single),
                pl.BlockSpec(wbb.shape, c2, pipeline_mode=single),
                pl.BlockSpec(wo.shape, c2, pipeline_mode=single),
                pl.BlockSpec(wup.shape, c2, pipeline_mode=single),
                pl.BlockSpec(wdn.shape, c2, pipeline_mode=single),
                pl.BlockSpec(cw.shape, c2, pipeline_mode=single)]
    args = [x, mod, ya, yb, g1, g2, gf, wm, wba, wbb, wo, wup, wdn, cw]
    if has_prev:
        in_specs += [pl.BlockSpec(prev[0].shape, c2), pl.BlockSpec(prev[1].shape, c2)]
        args += list(prev)
        up_shape = jax.ShapeDtypeStruct((T, F2), F32)
        up_spec = pl.BlockSpec((tm, F2), lambda b, i: (i, 0))
    else:
        up_shape = jax.ShapeDtypeStruct((G, 8, F2), F32)
        up_spec = pl.BlockSpec((1, 8, F2), lambda b, i: (b, 0, 0))
    return pl.pallas_call(
        functools.partial(_post_kernel, tm=tm, seq_len=seq_len, has_prev=has_prev),
        out_shape=[jax.ShapeDtypeStruct((G, T, D_MODEL), F32), up_shape],
        grid=(G, nt),
        in_specs=in_specs,
        out_specs=[pl.BlockSpec((1, tm, D_MODEL), lambda b, i: (b, i, 0)), up_spec],
        scratch_shapes=[pltpu.VMEM((8, F2), F32),
                        pltpu.VMEM((tm, D_MODEL), BF16),
                        pltpu.VMEM((tm, D_FF), BF16)],
        compiler_params=pltpu.CompilerParams(dimension_semantics=("arbitrary", "arbitrary"),
                                             vmem_limit_bytes=VMEM_LIMIT),
        name="post",
    )(*args)


def _block_diag2(m):
    z = jnp.zeros_like(m)
    return jnp.concatenate([jnp.concatenate([m, z], axis=-1), jnp.concatenate([z, m], axis=-1)], axis=-2)


def kernel(x_prompt, x_sample, cache_kv, state_kv_win, state_ffn_conv, page_table, c_prompt, c_sample, w_ada, b_ada, g_norm1, w_in, ln_v_g, ln_v_b, w_spatial, b_spatial, cmp_pe, cmp_w1, cmp_b1, cmp_w2, cmp_b2, w_branch_a, w_branch_b, w_out, g_norm2, w_up, w_conv, b_conv, w_down, g_final):
    B, T, _ = x_prompt.shape
    SB, ST, _ = x_sample.shape
    n_pool, page_size = cache_kv.shape[1], cache_kv.shape[2]
    n_pages = page_table.shape[1]
    past_len = n_pages * page_size
    lbuf = state_kv_win.shape[2]
    assert cache_kv.shape[0] == 1 and page_size == 128 and lbuf == WINDOW and SB * ST == 128 and ST <= 8
    assert past_len % BLOCK == 0 and n_pages == 128 and T % 512 == 0 and ST & (ST - 1) == 0 and ST >= 2

    win = w_in[0]
    wa = win[:, :2 * A_WIDTH].astype(BF16)
    o = 2 * A_WIDTH
    wq = win[:, o:o + B_WIDTH].astype(BF16)
    o += B_WIDTH
    wkv = win[:, o:o + KV_COLS]
    wkvT = wkv.T.astype(BF16)
    wkvc = wkv[:, :256].astype(BF16)
    o += KV_COLS
    wg = jnp.pad(win[:, o:o + 3 * B_HEADS], ((0, 0), (0, 128 - 3 * B_HEADS))).astype(BF16)
    wgT = jnp.pad(win[:, o:o + 3 * B_HEADS].T, ((0, 32 - 3 * B_HEADS), (0, 0))).astype(BF16)
    wqT = wq.T
    wrow = jnp.concatenate([wkv[:, :3 * 128], wkv[:, 4 * 128:5 * 128]], axis=1).astype(BF16)
    o += 3 * B_HEADS
    wm = win[:, o:].astype(BF16)
    g1 = g_norm1[0][None]
    g2 = g_norm2[0][None]
    gf = g_final[None]
    lng = ln_v_g[0][None]
    lnb = ln_v_b[0][None]

    ws = w_spatial[0]
    bs = b_spatial[0]
    wsp_p = jnp.tril(ws).astype(BF16)
    bsp_p = jnp.repeat(bs.T, 128, axis=1)
    w4 = jnp.tril(ws[:, :ST, :ST])
    eye_s = jnp.eye(128 // ST, dtype=F32)
    wsp_s = jnp.einsum('ab,gts->gatbs', eye_s, w4).reshape(A_GROUPS, 128, 128).astype(BF16)
    bsp_s = jnp.repeat(jnp.tile(bs[:, :ST].T, (128 // ST, 1)), 128, axis=1)

    w1, pe = cmp_w1[0].astype(BF16), cmp_pe[0]
    w1_rows = _block_diag2(w1)
    pe_rows = jnp.tile(pe, (1, 1, 2))
    w1_pages = _block_diag2(w1.transpose(0, 2, 1, 3)).reshape(2, HEAD_DIM // 2, 256, 128)
    pe_pages = jnp.tile(pe.transpose(0, 2, 1), (1, 1, 2)).reshape(2, HEAD_DIM // 2, 1, 256)
    pe_pages = jnp.pad(pe_pages, ((0, 0), (0, 0), (0, PE_ROWS - 1), (0, 0))).astype(BF16)
    b1t = jnp.tile(cmp_b1[0], (1, 2))[:, None, :]
    b2t = jnp.tile(cmp_b2[0], (1, 2))[:, None, :]
    w2bd = _block_diag2(cmp_w2[0].astype(BF16))

    wba = w_branch_a[0].astype(BF16)
    wbb = w_branch_b[0].astype(BF16)
    wo = w_out[0].astype(BF16)
    wup = w_up[0].astype(BF16)
    wdn = w_down[0].astype(BF16)
    cw = jnp.concatenate([w_conv[0], b_conv[0][None], jnp.zeros((4, F2), F32)], axis=0)

    mod = _mod_call(jnp.concatenate([c_prompt, c_sample], axis=0), w_ada[0], b_ada)
    mod_p = mod[:B].reshape(B, 6, 1, D_MODEL)
    mod_s = jnp.repeat(mod[B:], ST, axis=0).reshape(SB * ST, 6, D_MODEL).transpose(1, 0, 2)[None]

    ya_p, qT_p, kv4T_p, winT_p, vselT_p, vwinT_p, kvc_p, ksel_p, kwin_p, gzT_p = _inproj_call(
        x_prompt, mod_p, g1, wa, lng, lnb, wsp_p, bsp_p, (wqT, wkvT, wrow, wgT), tm=512, prompt=True)
    xs2 = x_sample.reshape(1, SB * ST, D_MODEL)
    ya_s, q_s, kvT_s, kvc_s, gz_s, vn_s = _inproj_call(
        xs2, mod_s, g1, wa, lng, lnb, wsp_s, bsp_s, (wq, wkvT, wkvc, wg), tm=128, prompt=False)

    nb = T // BLOCK
    cmp_p = _cmp_rows_call(kvc_p.reshape(B * T, 256), pe_rows, w1_rows, b1t, w2bd, b2t, R=128)
    kc_p = cmp_p[0].reshape(B, nb, 128)
    vcT_p = cmp_p[1].reshape(B, nb, 128).transpose(0, 2, 1)
    yb_p = _p_attn_call(qT_p, ksel_p, kwin_p, vselT_p, vwinT_p, kc_p, vcT_p, gzT_p, tq=512, kc=512)

    cache_pages = jnp.transpose(cache_kv[0], (0, 2, 3, 4, 1)).reshape(n_pool, 512, page_size)
    pool_seq = _cmp_gather_call(page_table.reshape(-1), cache_pages, pe_pages, w1_pages, b1t, w2bd, b2t,
                                Pt=64)
    kv_rows_s = jnp.transpose(kvT_s[0], (1, 0)).reshape(SB, ST, KV_COLS)
    newblk = jnp.pad(jnp.transpose(kv_rows_s[:, :, :256], (0, 2, 1)), ((0, 0), (0, 0), (0, page_size - ST)))
    newc = _cmp_pages_call(newblk, pe_pages, w1_pages, b1t, w2bd, b2t, Pt=SB)

    q_r = jnp.pad(q_s.reshape(SB, ST, KV_HEADS, GQA, HEAD_DIM).transpose(0, 2, 3, 1, 4),
                  ((0, 0), (0, 0), (0, 0), (0, 8 - ST), (0, 0))).reshape(SB, KV_HEADS, 8 * GQA, HEAD_DIM)
    ocmp_s, imp_s = _s_cmp_call(q_r, pool_seq, newc.reshape(SB, 1, 512), n_pages=n_pages, past_len=past_len)
    idx = _s_select_call(imp_s.reshape(SB * KV_HEADS * 8, 2 * n_pages), n_pages=n_pages, n_pick=N_SELECT - 1)
    idx_flat = idx.reshape(SB, KV_HEADS, 8, 128)[:, :, :ST, :N_SELECT].reshape(-1)
    winT = jnp.transpose(state_kv_win[0], (0, 2, 3, 4, 1)).reshape(SB, 256, lbuf)
    gz_r = jnp.pad(gz_s[0, :, :24].reshape(SB, ST, 3, KV_HEADS, GQA).transpose(0, 3, 4, 1, 2),
                   ((0, 0), (0, 0), (0, 0), (0, 8 - ST), (0, 128 - 3))).reshape(SB, KV_HEADS, 8 * GQA, 128)
    yb_r, wnextT = _s_attn_call(page_table, idx_flat, q_r, kvT_s[0], winT, gz_r, ocmp_s, cache_pages,
                        n_tok=ST, past_len=past_len)
    yb_s = yb_r.reshape(SB, KV_HEADS, GQA, 8, HEAD_DIM)[:, :, :, :ST].transpose(0, 3, 1, 2, 4)
    yb_s = yb_s.reshape(1, SB * ST, B_WIDTH).astype(BF16)

    y_p, up_p = _post_call(x_prompt, mod_p, ya_p, yb_p, g1, g2, gf, wm, wba, wbb, wo, wup, wdn, cw,
                           tm=512, seq_len=T)
    st = state_ffn_conv[0]
    zrow = jnp.zeros((SB, 1, F2), F32)
    p1 = jnp.concatenate([st[:, 1:2], jnp.tile(zrow, (1, ST - 1, 1))], axis=1).reshape(SB * ST, F2)
    p2 = jnp.concatenate([st, jnp.tile(zrow, (1, ST - 2, 1))], axis=1).reshape(SB * ST, F2)
    y_s, up_s = _post_call(xs2, mod_s, ya_s, yb_s, g1, g2, gf, wm, wba, wbb, wo, wup, wdn, cw,
                           prev=(p1, p2),
                           tm=SB * ST, seq_len=ST)

    kv_prompt = jnp.transpose(kv4T_p.reshape(B, 4, KV_HEADS, HEAD_DIM, T), (0, 4, 1, 2, 3))[None]
    win_prompt = jnp.transpose(winT_p[:, :, T - WINDOW:].reshape(B, 2, KV_HEADS, HEAD_DIM, WINDOW),
                               (0, 4, 1, 2, 3))[None]
    kv_sample = kv_rows_s[:, :, :512].reshape(SB, ST, 4, KV_HEADS, HEAD_DIM)[None]
    win_sample = jnp.transpose(wnextT.reshape(SB, 2, KV_HEADS, HEAD_DIM, lbuf), (0, 4, 1, 2, 3))[None]
    v_chunk = vn_s.reshape(SB, ST, A_WIDTH)[None]
    conv_prompt = up_p[:, 6:8][None]
    conv_sample = up_s.reshape(SB, ST, F2)[:, ST - 2:][None]
    return (y_p, y_s.reshape(SB, ST, D_MODEL), kv_prompt, kv_sample, win_prompt, win_sample, v_chunk,
            conv_prompt, conv_sample)
```

```python
import functools

import jax
import jax.numpy as jnp
from jax import lax
from jax.experimental import pallas as pl
from jax.experimental.pallas import tpu as pltpu

F32 = jnp.float32
BF16 = jnp.bfloat16

D_MODEL = 1024
A_WIDTH = 512
A_GROUPS = 4
CHUNK = 128
B_HEADS = 8
HEAD_DIM = 64
B_WIDTH = 512
KV_HEADS = 2
GQA = 4
BLOCK = 64
N_SELECT = 16
WINDOW = 512
N_KV_SLOTS = 6
KV_COLS = 768
D_FF = 2816
F2 = 2 * D_FF
CONV_W = 3
EPS = 1e-6
NEG = -1e30
FORCE_SCORE = 1e4

FF_CHUNK = 256
N_FF_CHUNKS = D_FF // FF_CHUNK
ROW_BLOCK = 64
PE_ROWS = 16
ACC_ROWS = HEAD_DIM + 16
LOG2E = 1.4426950408889634
VMEM_LIMIT = 56 * 1024 * 1024


def _dot(a, b):
    return jnp.dot(a, b, preferred_element_type=F32)


def _dot_nt(a, b):
    return lax.dot_general(a, b, (((1,), (1,)), ((), ())), preferred_element_type=F32)


def _rms_mod(x, g, scale, shift):
    y = x * lax.rsqrt(jnp.mean(x * x, axis=-1, keepdims=True) + EPS)
    return (y * g) * (1.0 + scale) + shift


def _mod_kernel(c_ref, w_ref, b_ref, o_ref):
    c = c_ref[...]
    s = c * jax.nn.sigmoid(c)
    o_ref[...] = _dot(s.astype(BF16), w_ref[...].astype(BF16)) + b_ref[...]


def _mod_call(c_all, w_ada, b_ada):
    n = c_all.shape[0]
    tn = 1536
    return pl.pallas_call(
        _mod_kernel,
        out_shape=jax.ShapeDtypeStruct((n, 6 * D_MODEL), F32),
        grid=(6 * D_MODEL // tn,),
        in_specs=[pl.BlockSpec((n, D_MODEL), lambda j: (0, 0)),
                  pl.BlockSpec((D_MODEL, tn), lambda j: (0, j)),
                  pl.BlockSpec((1, tn), lambda j: (0, j))],
        out_specs=pl.BlockSpec((n, tn), lambda j: (0, j)),
        compiler_params=pltpu.CompilerParams(dimension_semantics=("arbitrary",), vmem_limit_bytes=VMEM_LIMIT),
        name="mod",
    )(c_all, w_ada, b_ada)


def _inproj_kernel(x_ref, mod_ref, g1_ref, wa_ref, lng_ref, lnb_ref, wsp_ref, bsp_ref, *refs, tm, prompt):
    x = x_ref[0]
    hb = _rms_mod(x, g1_ref[...], mod_ref[0, 1], mod_ref[0, 0]).astype(BF16)

    ga = jax.nn.gelu(_dot(hb, wa_ref[...]))
    u = ga[:, :A_WIDTH]
    v = ga[:, A_WIDTH:]
    vc = v - jnp.mean(v, axis=-1, keepdims=True)
    vn = vc * lax.rsqrt(jnp.mean(vc * vc, axis=-1, keepdims=True) + EPS) * lng_ref[...] + lnb_ref[...]
    vnb = vn.astype(BF16)

    if prompt:
        (wqT_ref, wkvT_ref, wrow_ref, wgT_ref,
         ya_ref, qT_ref, kv4T_ref, winT_ref, vselT_ref, vwinT_ref, kvc_ref, ksel_ref, kwin_ref, gzT_ref) = refs
    else:
        (wq_ref, wkvT_ref, wkvc_ref, wg_ref, ya_ref, q_ref, kvT_ref, kvc_ref, gz_ref, vn_ref) = refs
        vn_ref[0] = vn

    for c in range(tm // CHUNK):
        rows = slice(c * CHUNK, (c + 1) * CHUNK)
        s = jnp.concatenate(
            [_dot(wsp_ref[g], vnb[rows, g * 128:(g + 1) * 128]) for g in range(A_GROUPS)], axis=1)
        ya_ref[0, rows, :] = (u[rows] * (s + bsp_ref[...])).astype(BF16)

    kvT = _dot_nt(wkvT_ref[...], hb)
    if prompt:
        qT_ref[0] = (_dot_nt(wqT_ref[...], hb) * (HEAD_DIM ** -0.5 * LOG2E)).astype(BF16)
        kv4T_ref[0] = kvT[:4 * 128]
        winT_ref[0] = kvT[4 * 128:]
        vselT_ref[0] = kvT[3 * 128:4 * 128].astype(BF16)
        vwinT_ref[0] = kvT[5 * 128:].astype(BF16)
        row = _dot(hb, wrow_ref[...])
        kvc_ref[0] = row[:, :256]
        ksel_ref[0] = row[:, 256:384].astype(BF16)
        kwin_ref[0] = row[:, 384:].astype(BF16)
        gzT_ref[0] = jax.nn.sigmoid(_dot_nt(wgT_ref[...], hb))
    else:
        q_ref[0] = (_dot(hb, wq_ref[...]) * (HEAD_DIM ** -0.5)).astype(BF16)
        kvT_ref[0] = kvT
        kvc_ref[0] = _dot(hb, wkvc_ref[...])
        gz_ref[0] = jax.nn.sigmoid(_dot(hb, wg_ref[...]))


def _inproj_call(x, mod, g1, wa, lng, lnb, wsp, bsp, proj_w, *, tm, prompt):
    G, T, _ = x.shape
    R = mod.shape[2]
    nt = T // tm
    const2 = lambda b, i: (0, 0)
    const3 = lambda b, i: (0, 0, 0)
    rows = lambda w: pl.BlockSpec((1, tm, w), lambda b, i: (b, i, 0))
    cols = lambda h: pl.BlockSpec((1, h, tm), lambda b, i: (b, 0, i))
    if prompt:
        outs = [((G, T, A_WIDTH), BF16, rows(A_WIDTH)),
                ((G, B_WIDTH, T), BF16, cols(B_WIDTH)),
                ((G, 512, T), F32, cols(512)),
                ((G, 256, T), F32, cols(256)),
                ((G, 128, T), BF16, cols(128)),
                ((G, 128, T), BF16, cols(128)),
                ((G, T, 256), F32, rows(256)),
                ((G, T, 128), BF16, rows(128)),
                ((G, T, 128), BF16, rows(128)),
                ((G, 32, T), F32, cols(32))]
    else:
        outs = [((G, T, A_WIDTH), BF16, rows(A_WIDTH)),
                ((G, T, B_WIDTH), BF16, rows(B_WIDTH)),
                ((G, KV_COLS, T), F32, cols(KV_COLS)),
                ((G, T, 256), F32, rows(256)),
                ((G, T, 128), F32, rows(128)),
                ((G, T, A_WIDTH), F32, rows(A_WIDTH))]
    return pl.pallas_call(
        functools.partial(_inproj_kernel, tm=tm, prompt=prompt),
        out_shape=[jax.ShapeDtypeStruct(s, d) for s, d, _ in outs],
        grid=(G, nt),
        in_specs=[pl.BlockSpec((1, tm, D_MODEL), lambda b, i: (b, i, 0)),
                  pl.BlockSpec((1, 6, R, D_MODEL), lambda b, i: (b, 0, 0, 0)),
                  pl.BlockSpec((1, D_MODEL), const2),
                  pl.BlockSpec(wa.shape, const2),
                  pl.BlockSpec((1, A_WIDTH), const2),
                  pl.BlockSpec((1, A_WIDTH), const2),
                  pl.BlockSpec(wsp.shape, const3),
                  pl.BlockSpec(bsp.shape, const2)] + [pl.BlockSpec(w.shape, const2) for w in proj_w],
        out_specs=[sp for _, _, sp in outs],
        compiler_params=pltpu.CompilerParams(dimension_semantics=("arbitrary", "arbitrary"),
                                             vmem_limit_bytes=VMEM_LIMIT),
        name="inproj",
    )(x, mod, g1, wa, lng, lnb, wsp, bsp, *proj_w)


def _cmp_tail(acc, b1, w2, b2):
    return _dot(jax.nn.gelu(acc + b1).astype(BF16), w2) + b2


def _cmp_rows_kernel(x_ref, pe_ref, w1_ref, b1_ref, w2_ref, b2_ref, o_ref, *, R):
    acc = jnp.zeros((R, 128), F32)
    for j in range(BLOCK):
        xj = x_ref[pl.ds(j, R, stride=BLOCK), :] + pe_ref[0, j:j + 1, :]
        acc = acc + _dot(xj.astype(BF16), w1_ref[0, j])
    o_ref[0] = _cmp_tail(acc, b1_ref[0], w2_ref[0], b2_ref[0])


def _cmp_rows_call(kvc2d, pe, w1, b1, w2, b2, *, R):
    nrows = kvc2d.shape[0] // BLOCK
    return pl.pallas_call(
        functools.partial(_cmp_rows_kernel, R=R),
        out_shape=jax.ShapeDtypeStruct((2, nrows, 128), F32),
        grid=(2, nrows // R),
        in_specs=[pl.BlockSpec((R * BLOCK, 128), lambda s, r: (r, s)),
                  pl.BlockSpec((1, BLOCK, 128), lambda s, r: (s, 0, 0)),
                  pl.BlockSpec((1, BLOCK, 128, 128), lambda s, r: (s, 0, 0, 0)),
                  pl.BlockSpec((1, 1, 128), lambda s, r: (s, 0, 0)),
                  pl.BlockSpec((1, 128, 128), lambda s, r: (s, 0, 0)),
                  pl.BlockSpec((1, 1, 128), lambda s, r: (s, 0, 0))],
        out_specs=pl.BlockSpec((1, R, 128), lambda s, r: (s, r, 0)),
        compiler_params=pltpu.CompilerParams(dimension_semantics=("arbitrary", "arbitrary"),
                                             vmem_limit_bytes=VMEM_LIMIT),
        name="cmp_rows",
    )(kvc2d, pe, w1, b1, w2, b2)


def _compress_pages(load_rows, pe_ref, w1_ref, b1_ref, w2_ref, b2_ref, o_ref, Pt):
    for s in range(2):
        acc = jnp.zeros((2 * Pt + PE_ROWS, 128), F32)
        for dg in range(HEAD_DIM // 8):
            r0 = s * 128 + dg * 8
            xa = jnp.swapaxes(load_rows(r0), 0, 1).astype(BF16)
            xb = jnp.swapaxes(load_rows(HEAD_DIM + r0), 0, 1).astype(BF16)
            for dd in range(0, 8, 2):
                dp = dg * 4 + dd // 2
                xd = jnp.concatenate([jnp.concatenate([xa[dd], xa[dd + 1]], axis=1),
                                      jnp.concatenate([xb[dd], xb[dd + 1]], axis=1),
                                      pe_ref[s, dp]], axis=0)
                acc = acc + _dot(xd, w1_ref[s, dp])
        pre = acc[:2 * Pt] + acc[2 * Pt:2 * Pt + 1]
        r = _cmp_tail(pre, b1_ref[s], w2_ref[s], b2_ref[s])
        o_ref[:, (2 * s) * 128:(2 * s + 1) * 128] = r[:Pt]
        o_ref[:, (2 * s + 1) * 128:(2 * s + 2) * 128] = r[Pt:]


def _cmp_pages_kernel(x_ref, pe_ref, w1_ref, b1_ref, w2_ref, b2_ref, o_ref, *, Pt):
    _compress_pages(lambda r0: x_ref[:, r0:r0 + 8, :], pe_ref, w1_ref, b1_ref, w2_ref, b2_ref, o_ref, Pt)


def _cmp_weight_specs(ws, index):
    return [pl.BlockSpec(w.shape, functools.partial(index, (0,) * w.ndim)) for w in ws]


def _cmp_pages_call(pages, pe, w1, b1, w2, b2, *, Pt):
    P = pages.shape[0]
    ws = (pe, w1, b1, w2, b2)
    return pl.pallas_call(
        functools.partial(_cmp_pages_kernel, Pt=Pt),
        out_shape=jax.ShapeDtypeStruct((P, 512), F32),
        grid=(P // Pt,),
        in_specs=[pl.BlockSpec((Pt, 256, 128), lambda i: (i, 0, 0))] + _cmp_weight_specs(ws, lambda z, i: z),
        out_specs=pl.BlockSpec((Pt, 512), lambda i: (i, 0)),
        compiler_params=pltpu.CompilerParams(dimension_semantics=("arbitrary",), vmem_limit_bytes=VMEM_LIMIT),
        name="cmp_pages",
    )(pages, *ws)


def _cmp_gather_kernel(pt_ref, cache_ref, pe_ref, w1_ref, b1_ref, w2_ref, b2_ref, o_ref, xbuf, sem, *, Pt):
    i = pl.program_id(0)
    slot = i & 1

    def page_copy(step, sl, k):
        return pltpu.make_async_copy(cache_ref.at[pt_ref[step * Pt + k], pl.ds(0, 256), :],
                                     xbuf.at[sl, k], sem.at[sl])

    @pl.when(i == 0)
    def _():
        for k in range(Pt):
            page_copy(0, 0, k).start()

    @pl.when(i + 1 < pl.num_programs(0))
    def _():
        for k in range(Pt):
            page_copy(i + 1, 1 - slot, k).start()

    for k in range(Pt):
        page_copy(i, slot, k).wait()
    _compress_pages(lambda r0: xbuf[slot, :, r0:r0 + 8, :], pe_ref, w1_ref, b1_ref, w2_ref, b2_ref, o_ref, Pt)


def _cmp_gather_call(pt_flat, cache_pages, pe, w1, b1, w2, b2, *, Pt):
    n = pt_flat.shape[0]
    ws = (pe, w1, b1, w2, b2)
    return pl.pallas_call(
        functools.partial(_cmp_gather_kernel, Pt=Pt),
        out_shape=jax.ShapeDtypeStruct((n, 512), F32),
        grid_spec=pltpu.PrefetchScalarGridSpec(
            num_scalar_prefetch=1,
            grid=(n // Pt,),
            in_specs=[pl.BlockSpec(memory_space=pl.ANY)] + _cmp_weight_specs(ws, lambda z, i, pt: z),
            out_specs=pl.BlockSpec((Pt, 512), lambda i, pt: (i, 0)),
            scratch_shapes=[pltpu.VMEM((2, Pt, 256, 128), F32),
                            pltpu.SemaphoreType.DMA((2,))]),
        compiler_params=pltpu.CompilerParams(dimension_semantics=("arbitrary",), vmem_limit_bytes=VMEM_LIMIT),
        name="cmp_gather",
    )(pt_flat, cache_pages, *ws)


def _pair_schedule(T, tq, kc):
    js, cs = [], []
    for j in range(T // tq):
        for c in range(((j + 1) * tq - 1) // kc + 1):
            js.append(j)
            cs.append(c)
    return js, cs


def _p_attn_kernel(jt_ref, ct_ref, qT_ref, ksel_ref, kwin_ref, vselT_ref, vwinT_ref, kc_ref, vcT_ref, gzT_ref,
                   yb_ref, sel_s, ocmp_s, m_s, acc_s, s_scr, p_scr, b_scr, *, tq, kc, T, strip):
    p = pl.program_id(1)
    j = jt_ref[p]
    c = ct_ref[p]
    nb = T // BLOCK
    N = GQA * tq
    q0 = j * tq
    k0 = c * kc
    c_last = ((j + 1) * tq - 1) // kc

    def q_pad(h):
        qT = jnp.concatenate([qT_ref[0, (h * GQA + g) * HEAD_DIM:(h * GQA + g + 1) * HEAD_DIM, :]
                              for g in range(GQA)], axis=1)
        z = jnp.zeros_like(qT)
        return jnp.concatenate([qT if k == h else z for k in range(KV_HEADS)], axis=0)

    @pl.when(c == 0)
    def _():
        m_s[...] = jnp.full(m_s.shape, NEG, F32)
        acc_s[...] = jnp.zeros(acc_s.shape, F32)
        tok_n = q0 + (lax.broadcasted_iota(jnp.int32, (1, N), 1) & (tq - 1))
        tok_1 = q0 + lax.broadcasted_iota(jnp.int32, (1, tq), 1)
        blk = lax.broadcasted_iota(jnp.int32, (nb, 1), 0)
        avail = (blk + 1) * BLOCK <= tok_n + 1
        cur = tok_1 >> 6
        forced = (blk == 0) | (blk == cur) | (blk == cur - 1)
        future = blk > cur
        kcb = kc_ref[0].astype(BF16)
        for h in range(KV_HEADS):
            s = jnp.where(avail, _dot(kcb, q_pad(h)), NEG)
            m = jnp.max(s, axis=0, keepdims=True)
            e = jnp.where(avail, jnp.exp2(s - m), 0.0)
            pr = e / jnp.maximum(jnp.sum(e, axis=0, keepdims=True), 1e-30)
            ocmp_s[h] = _dot(vcT_ref[0, h * HEAD_DIM:(h + 1) * HEAD_DIM, :].astype(BF16), pr.astype(BF16))
            imp = pr[:, 0:tq]
            for g in range(1, GQA):
                imp = imp + pr[:, g * tq:(g + 1) * tq]
            imp = jnp.where(forced, FORCE_SCORE, jnp.where(future, -1.0, imp))
            rows = []
            for i in range(nb):
                vi = imp[i:i + 1, :]
                ahead = (imp > vi) | ((imp == vi) & (blk < i))
                cnt = jnp.sum(ahead.astype(F32), axis=0, keepdims=True)
                rows.append((cnt < float(N_SELECT)).astype(F32))
            sel_s[h] = jnp.concatenate(rows + [jnp.zeros((128 - nb, tq), F32)], axis=0).astype(BF16)

    kpos = k0 + lax.broadcasted_iota(jnp.int32, (kc, tq), 0)
    tok = q0 + lax.broadcasted_iota(jnp.int32, (kc, tq), 1)

    strips = [slice(st * strip, (st + 1) * strip) for st in range(N // strip)]

    row_blocks = [slice(r, r + ROW_BLOCK) for r in range(0, kc, ROW_BLOCK)]

    ones_rows = (lax.broadcasted_iota(jnp.int32, (ACC_ROWS - HEAD_DIM, kc), 0) == 0).astype(BF16)

    def online_softmax(first_idx, keys, vT_ref, bias_slots):
        vT1 = [jnp.concatenate([vT_ref[0, h * HEAD_DIM:(h + 1) * HEAD_DIM, :], ones_rows], axis=0)
               for h in range(KV_HEADS)]
        streams = [(h, st) for st in range(len(strips)) for h in range(KV_HEADS)]
        alpha = {}

        def scores(h, st):
            g, t0 = divmod(st * strip, tq)
            r0 = (h * GQA + g) * HEAD_DIM
            qT = qT_ref[0, r0:r0 + HEAD_DIM, t0:t0 + strip]
            z = jnp.zeros_like(qT)
            qp = jnp.concatenate([qT if k == h else z for k in range(KV_HEADS)], axis=0)
            s_scr[h, :, strips[st]] = _dot(keys, qp)

        def probs(h, st):
            ls = strips[st]
            t0 = (st * strip) % tq
            idx = first_idx + h
            m = None
            for rb in row_blocks:
                t = s_scr[h, rb, ls] + b_scr[bias_slots[h], rb, t0:t0 + strip]
                s_scr[h, rb, ls] = t
                m = t if m is None else jnp.maximum(m, t)
            m_old = m_s[idx, :, ls]
            m_new = jnp.maximum(m_old, jnp.max(m, axis=0, keepdims=True))
            alpha[(h, st)] = jnp.exp2(m_old - m_new)
            m_s[idx, :, ls] = m_new
            for rb in row_blocks:
                p_scr[h, rb, ls] = jnp.exp2(s_scr[h, rb, ls] - m_new).astype(BF16)

        def weighted_values(h, st):
            ls = strips[st]
            idx = first_idx + h
            acc_s[idx, :, ls] = alpha[(h, st)] * acc_s[idx, :, ls] + _dot(vT1[h], p_scr[h, :, ls])

        stages = (scores, probs, weighted_values)
        for k in range(len(streams) + len(stages) - 1):
            for lag, stage in enumerate(stages):
                if 0 <= k - lag < len(streams):
                    stage(*streams[k - lag])

    blk_of_key = (k0 >> 6) + (lax.broadcasted_iota(jnp.int32, (kc, 128), 0) >> 6)
    expand = (lax.broadcasted_iota(jnp.int32, (kc, 128), 1) == blk_of_key).astype(BF16)
    causal = kpos <= tok
    for h in range(KV_HEADS):
        b_scr[h] = jnp.where((_dot(expand, sel_s[h]) > 0.5) & causal, 0.0, NEG)
    online_softmax(0, ksel_ref[0], vselT_ref, list(range(KV_HEADS)))

    @pl.when((c + 1) * kc - 1 >= q0 - (WINDOW - 1))
    def _():
        diff = tok - kpos
        b_scr[0] = jnp.where((diff >= 0) & (diff < WINDOW), 0.0, NEG)
        online_softmax(KV_HEADS, kwin_ref[0], vwinT_ref, [0] * KV_HEADS)

    @pl.when(c == c_last)
    def _():
        gz = gzT_ref[0]
        outs = []
        for h in range(KV_HEADS):
            a_sel = acc_s[h]
            a_win = acc_s[KV_HEADS + h]
            o_sel = a_sel[:HEAD_DIM] / jnp.maximum(a_sel[HEAD_DIM:HEAD_DIM + 1], 1e-30)
            o_win = a_win[:HEAD_DIM] / jnp.maximum(a_win[HEAD_DIM:HEAD_DIM + 1], 1e-30)
            o_cmp = ocmp_s[h]
            for g in range(GQA):
                col = h * GQA + g
                ls = slice(g * tq, (g + 1) * tq)
                outs.append(gz[col:col + 1] * o_cmp[:, ls] + gz[B_HEADS + col:B_HEADS + col + 1] * o_sel[:, ls]
                            + gz[2 * B_HEADS + col:2 * B_HEADS + col + 1] * o_win[:, ls])
        yb_ref[0] = jnp.concatenate(outs, axis=0).T.astype(BF16)


def _p_attn_call(qT, ksel, kwin, vselT, vwinT, kc_all, vcT, gzT, *, tq, kc):
    B, _, T = qT.shape
    nb = T // BLOCK
    N = GQA * tq
    js, cs = _pair_schedule(T, tq, kc)
    jt = jnp.asarray(js, jnp.int32)
    ct = jnp.asarray(cs, jnp.int32)
    return pl.pallas_call(
        functools.partial(_p_attn_kernel, tq=tq, kc=kc, T=T, strip=256),
        out_shape=jax.ShapeDtypeStruct((B, T, B_WIDTH), BF16),
        grid_spec=pltpu.PrefetchScalarGridSpec(
            num_scalar_prefetch=2,
            grid=(B, len(js)),
            in_specs=[pl.BlockSpec((1, B_WIDTH, tq), lambda b, p, jt, ct: (b, 0, jt[p])),
                      pl.BlockSpec((1, kc, 128), lambda b, p, jt, ct: (b, ct[p], 0)),
                      pl.BlockSpec((1, kc, 128), lambda b, p, jt, ct: (b, ct[p], 0)),
                      pl.BlockSpec((1, 128, kc), lambda b, p, jt, ct: (b, 0, ct[p])),
                      pl.BlockSpec((1, 128, kc), lambda b, p, jt, ct: (b, 0, ct[p])),
                      pl.BlockSpec((1, nb, 128), lambda b, p, jt, ct: (b, 0, 0)),
                      pl.BlockSpec((1, 128, nb), lambda b, p, jt, ct: (b, 0, 0)),
                      pl.BlockSpec((1, 32, tq), lambda b, p, jt, ct: (b, 0, jt[p]))],
            out_specs=pl.BlockSpec((1, tq, B_WIDTH), lambda b, p, jt, ct: (b, jt[p], 0)),
            scratch_shapes=[pltpu.VMEM((KV_HEADS, 128, tq), BF16),
                            pltpu.VMEM((KV_HEADS, HEAD_DIM, N), F32),
                            pltpu.VMEM((2 * KV_HEADS, 1, N), F32),
                            pltpu.VMEM((2 * KV_HEADS, ACC_ROWS, N), F32),
                            pltpu.VMEM((KV_HEADS, kc, N), F32),
                            pltpu.VMEM((KV_HEADS, kc, N), BF16),
                            pltpu.VMEM((KV_HEADS, kc, tq), F32)]),
        compiler_params=pltpu.CompilerParams(dimension_semantics=("arbitrary", "arbitrary"),
                                             vmem_limit_bytes=VMEM_LIMIT),
        name="p_attn",
    )(jt, ct, qT, ksel, kwin, vselT, vwinT, kc_all, vcT, gzT)


def _s_cmp_kernel(q_ref, gath, newc_ref, ocmp_ref, imp_ref, *, n_pages, past_len):
    rows = 8 * GQA
    t_row = lax.broadcasted_iota(jnp.int32, (rows, 1), 0) & 7
    pos = past_len + t_row
    lane = lax.broadcasted_iota(jnp.int32, (1, 2 * n_pages), 1)
    blk = 2 * (lane & (n_pages - 1)) + (lane >> 7)
    new_blk = past_len // BLOCK
    avail = (blk + 1) * BLOCK <= pos + 1
    avail_new = (new_blk + 1) * BLOCK <= pos + 1
    for h in range(KV_HEADS):
        qh = q_ref[0, h]
        kc = [gath[:, (h * 2 + k) * 64:(h * 2 + k + 1) * 64].astype(BF16) for k in range(2)]
        vc = [gath[:, (2 + h) * 128 + k * 64:(2 + h) * 128 + (k + 1) * 64].astype(BF16) for k in range(2)]
        kc_new = newc_ref[0, :, h * 128:h * 128 + 64].astype(BF16).astype(F32)
        vc_new = newc_ref[0, :, (2 + h) * 128:(2 + h) * 128 + 64].astype(BF16).astype(F32)
        s = jnp.where(avail, jnp.concatenate([_dot_nt(qh, kc[0]), _dot_nt(qh, kc[1])], axis=1), NEG)
        s_new = jnp.where(avail_new, jnp.sum(qh.astype(F32) * kc_new, axis=-1, keepdims=True), NEG)
        m = jnp.maximum(jnp.max(s, axis=-1, keepdims=True), s_new)
        e = jnp.where(avail, jnp.exp(s - m), 0.0)
        e_new = jnp.where(avail_new, jnp.exp(s_new - m), 0.0)
        den = jnp.maximum(jnp.sum(e, axis=-1, keepdims=True) + e_new, 1e-30)
        p = e / den
        p_new = e_new / den
        pb = p.astype(BF16)
        ocmp_ref[0, h] = (_dot(pb[:, :n_pages], vc[0]) + _dot(pb[:, n_pages:], vc[1])
                          + p_new.astype(BF16).astype(F32) * vc_new)
        imp = p[0:8] + p[8:16] + p[16:24] + p[24:32]
        cur = (past_len + lax.broadcasted_iota(jnp.int32, (8, 1), 0)) >> 6
        forced = (blk == 0) | (blk == cur) | (blk == cur - 1)
        imp_ref[0, h] = jnp.where(forced, FORCE_SCORE, jnp.where(blk > cur, -1.0, imp))


def _s_cmp_call(q_r, pool_seq, newc, *, n_pages, past_len):
    nbatch = q_r.shape[0]
    return pl.pallas_call(
        functools.partial(_s_cmp_kernel, n_pages=n_pages, past_len=past_len),
        out_shape=[jax.ShapeDtypeStruct((nbatch, KV_HEADS, 32, HEAD_DIM), F32),
                   jax.ShapeDtypeStruct((nbatch, KV_HEADS, 8, 2 * n_pages), F32)],
        grid=(nbatch,),
        in_specs=[pl.BlockSpec((1, KV_HEADS, 32, HEAD_DIM), lambda b: (b, 0, 0, 0)),
                  pl.BlockSpec((n_pages, 512), lambda b: (b, 0)),
                  pl.BlockSpec((1, 1, 512), lambda b: (b, 0, 0))],
        out_specs=[pl.BlockSpec((1, KV_HEADS, 32, HEAD_DIM), lambda b: (b, 0, 0, 0)),
                   pl.BlockSpec((1, KV_HEADS, 8, 2 * n_pages), lambda b: (b, 0, 0, 0))],
        compiler_params=pltpu.CompilerParams(dimension_semantics=("arbitrary",), vmem_limit_bytes=VMEM_LIMIT),
        name="s_cmp",
    )(q_r, pool_seq, newc)


def _s_select_kernel(imp_ref, idx_ref, *, n_pages, n_pick):
    v = imp_ref[...]
    rows = v.shape[0]
    lane = lax.broadcasted_iota(jnp.int32, (1, 2 * n_pages), 1)
    blk = (2 * (lane & (n_pages - 1)) + (lane >> 7)).astype(F32)
    out_lane = lax.broadcasted_iota(jnp.int32, (1, 128), 1)
    out = jnp.full((rows, 128), float(2 * n_pages), F32)
    for k in range(n_pick):
        m = jnp.max(v, axis=-1, keepdims=True)
        pick = jnp.min(jnp.where(v == m, blk, 1e9), axis=-1, keepdims=True)
        out = jnp.where(out_lane == k, pick, out)
        v = jnp.where(blk == pick, NEG, v)
    idx_ref[...] = out.astype(jnp.int32)


def _s_select_call(imp2d, *, n_pages, n_pick):
    rows = imp2d.shape[0]
    return pl.pallas_call(
        functools.partial(_s_select_kernel, n_pages=n_pages, n_pick=n_pick),
        out_shape=jax.ShapeDtypeStruct((rows, 128), jnp.int32),
        grid=(1,),
        in_specs=[pl.BlockSpec(imp2d.shape, lambda i: (0, 0))],
        out_specs=pl.BlockSpec((rows, 128), lambda i: (0, 0)),
        compiler_params=pltpu.CompilerParams(dimension_semantics=("arbitrary",), vmem_limit_bytes=VMEM_LIMIT),
        name="s_select",
    )(imp2d)


def _s_attn_kernel(pt_ref, idx_ref, q_ref, kvn_ref, win_ref, gz_ref, ocmp_ref, cache_ref, yb_ref, wnext_ref,
                   kvbuf, sem, *, n_tok, n_gather, past_len):
    b = pl.program_id(0)
    span = n_gather * 128
    slot = b & 1

    def tile_copies(bb, sl, t, h, jj):
        n = idx_ref[((bb * KV_HEADS + h) * n_tok + t) * N_SELECT + jj]
        page = pt_ref[bb, n >> 1]
        return [pltpu.make_async_copy(cache_ref.at[page, 2 + kv, h],
                                      kvbuf.at[sl, h, t, kv, :, pl.ds(jj * 128, 128)], sem.at[sl])
                for kv in range(2)]

    def for_all_tiles(bb, sl, fn):
        for h in range(KV_HEADS):
            for t in range(n_tok):
                for jj in range(n_gather):
                    for cp in tile_copies(bb, sl, t, h, jj):
                        fn(cp)

    @pl.when(b == 0)
    def _():
        for_all_tiles(0, 0, lambda cp: cp.start())

    @pl.when(b + 1 < pl.num_programs(0))
    def _():
        for_all_tiles(b + 1, 1 - slot, lambda cp: cp.start())

    rows = 8 * GQA
    t_row = lax.broadcasted_iota(jnp.int32, (rows, 1), 0) & 7
    gz = gz_ref[0]

    lane_n = lax.broadcasted_iota(jnp.int32, (1, 128), 1)
    tok_shift = n_tok.bit_length() - 1
    new_ok = ((lane_n >> tok_shift) == b) & ((lane_n & (n_tok - 1)) <= t_row)
    lane_w = lax.broadcasted_iota(jnp.int32, (1, WINDOW), 1)
    win_ok = lane_w > t_row

    shifted = pltpu.roll(win_ref[0], WINDOW - n_tok, 1)
    moved = pltpu.roll(kvn_ref[4 * 128:, :], (128 - n_tok) - b * n_tok, 1)
    wnext_ref[0, :, :WINDOW - 128] = shifted[:, :WINDOW - 128]
    wnext_ref[0, :, WINDOW - 128:] = jnp.where(lane_n >= 128 - n_tok, moved, shifted[:, WINDOW - 128:])

    def softmax_pv(parts):
        m = None
        for sc, mk, _ in parts:
            mm = jnp.max(jnp.where(mk, sc, NEG), axis=-1, keepdims=True)
            m = mm if m is None else jnp.maximum(m, mm)
        den = jnp.zeros((rows, 1), F32)
        o = jnp.zeros((rows, HEAD_DIM), F32)
        for sc, mk, vts in parts:
            e = jnp.where(mk, jnp.exp(jnp.where(mk, sc, NEG) - m), 0.0)
            den = den + jnp.sum(e, axis=-1, keepdims=True)
            for rmask, vt in vts:
                er = e if rmask is None else jnp.where(rmask, e, 0.0)
                o = o + _dot_nt(er.astype(BF16), vt)
        return o / jnp.maximum(den, 1e-30)

    win_out = []
    for h in range(KV_HEADS):
        qh = q_ref[0, h]
        kw = win_ref[0, h * 64:(h + 1) * 64, :].astype(BF16)
        vw = win_ref[0, 128 + h * 64:128 + (h + 1) * 64, :].astype(BF16)
        kwn = kvn_ref[4 * 128 + h * 64:4 * 128 + (h + 1) * 64, :].astype(BF16)
        vwn = kvn_ref[5 * 128 + h * 64:5 * 128 + (h + 1) * 64, :].astype(BF16)
        win_out.append(softmax_pv([(_dot(qh, kw), win_ok, [(None, vw)]),
                                   (_dot(qh, kwn), new_ok, [(None, vwn)])]))

    for_all_tiles(b, slot, lambda cp: cp.wait())

    lane_s = lax.broadcasted_iota(jnp.int32, (1, span), 1)
    for h in range(KV_HEADS):
        qh = q_ref[0, h]
        sc = jnp.zeros((rows, span), F32)
        half = jnp.zeros((rows, span), jnp.int32)
        for t in range(n_tok):
            st = _dot(qh, kvbuf[slot, h, t, 0].astype(BF16))
            hrow = jnp.zeros((1, span), jnp.int32)
            for jj in range(n_gather):
                n = idx_ref[((b * KV_HEADS + h) * n_tok + t) * N_SELECT + jj]
                hrow = jnp.where((lane_s >> 7) == jj, n & 1, hrow)
            sc = jnp.where(t_row == t, st, sc)
            half = jnp.where(t_row == t, hrow, half)
        ok = ((lane_s >> 6) & 1) == half
        ksn = kvn_ref[2 * 128 + h * 64:2 * 128 + (h + 1) * 64, :].astype(BF16)
        vsn = kvn_ref[3 * 128 + h * 64:3 * 128 + (h + 1) * 64, :].astype(BF16)
        vts = [(t_row == t, kvbuf[slot, h, t, 1].astype(BF16)) for t in range(n_tok)]
        o_sel = softmax_pv([(sc, ok, vts), (_dot(qh, ksn), new_ok, [(None, vsn)])])
        yb_ref[0, h] = (gz[h, :, 0:1] * ocmp_ref[0, h] + gz[h, :, 1:2] * o_sel + gz[h, :, 2:3] * win_out[h])


def _s_attn_call(page_table, idx_flat, q_r, kvT_new, winT, gz_r, ocmp, cache_pages, *, n_tok, past_len):
    nbatch = page_table.shape[0]
    n_gather = N_SELECT - 1
    span = n_gather * 128
    blk4 = lambda b, pt, ix: (b, 0, 0, 0)
    return pl.pallas_call(
        functools.partial(_s_attn_kernel, n_tok=n_tok, n_gather=n_gather, past_len=past_len),
        out_shape=[jax.ShapeDtypeStruct((nbatch, KV_HEADS, 32, HEAD_DIM), F32),
                   jax.ShapeDtypeStruct((nbatch, 256, WINDOW), F32)],
        grid_spec=pltpu.PrefetchScalarGridSpec(
            num_scalar_prefetch=2,
            grid=(nbatch,),
            in_specs=[pl.BlockSpec((1, KV_HEADS, 32, HEAD_DIM), blk4),
                      pl.BlockSpec(kvT_new.shape, lambda b, pt, ix: (0, 0)),
                      pl.BlockSpec((1, 256, WINDOW), lambda b, pt, ix: (b, 0, 0)),
                      pl.BlockSpec((1, KV_HEADS, 32, 128), blk4),
                      pl.BlockSpec((1, KV_HEADS, 32, HEAD_DIM), blk4),
                      pl.BlockSpec(memory_space=pl.ANY)],
            out_specs=[pl.BlockSpec((1, KV_HEADS, 32, HEAD_DIM), blk4),
                       pl.BlockSpec((1, 256, WINDOW), lambda b, pt, ix: (b, 0, 0))],
            scratch_shapes=[pltpu.VMEM((2, KV_HEADS, n_tok, 2, HEAD_DIM, span), F32),
                            pltpu.SemaphoreType.DMA((2,))]),
        compiler_params=pltpu.CompilerParams(dimension_semantics=("arbitrary",), vmem_limit_bytes=VMEM_LIMIT),
        name="s_attn",
    )(page_table, idx_flat, q_r, kvT_new, winT, gz_r, ocmp, cache_pages)


def _post_kernel(*refs, tm, seq_len, has_prev):
    it = iter(refs)
    x_ref, mod_ref, ya_ref, yb_ref = next(it), next(it), next(it), next(it)
    g1_ref, g2_ref, gf_ref = next(it), next(it), next(it)
    wm_ref, wba_ref, wbb_ref, wo_ref, wup_ref, wdn_ref, cw_ref = (next(it) for _ in range(7))
    p1_ref = next(it) if has_prev else None
    p2_ref = next(it) if has_prev else None
    y_ref, up_ref = next(it), next(it)
    carry, h2_s, act_s = next(it), next(it), next(it)

    i = pl.program_id(1)
    x = x_ref[0]
    shift1, scale1, gate1 = mod_ref[0, 0], mod_ref[0, 1], mod_ref[0, 2]
    shift2, scale2, gate2 = mod_ref[0, 3], mod_ref[0, 4], mod_ref[0, 5]

    hb = _rms_mod(x, g1_ref[...], scale1, shift1).astype(BF16)
    gates = jax.nn.sigmoid(_dot(hb, wm_ref[...]))
    mix = (gates[:, :D_MODEL] * _dot(ya_ref[0], wba_ref[...])
           + gates[:, D_MODEL:] * _dot(yb_ref[0], wbb_ref[...]))
    x1 = x + gate1 * _dot(mix.astype(BF16), wo_ref[...])
    h2_s[...] = _rms_mod(x1, g2_ref[...], scale2, shift2).astype(BF16)

    if not has_prev:
        @pl.when(i == 0)
        def _():
            carry[...] = jnp.zeros(carry.shape, F32)

    row = lax.broadcasted_iota(jnp.int32, (tm, 1), 0)
    row8 = lax.broadcasted_iota(jnp.int32, (8, 1), 0)

    def conv_cols(col0):
        cols = slice(col0, col0 + FF_CHUNK)
        up = _dot(h2_s[...], wup_ref[:, cols])
        r1 = pltpu.roll(up, 1, 0)
        r2 = pltpu.roll(up, 2, 0)
        if has_prev:
            s1 = jnp.where((row & (seq_len - 1)) == 0, p1_ref[:, cols], r1)
            s2 = jnp.where((row & (seq_len - 1)) < 2, p2_ref[:, cols], r2)
            up_ref[:, cols] = up
        else:
            prev = carry[:, cols]
            t1 = jnp.where(row8 == 0, pltpu.roll(prev, 1, 0), r1[:8])
            t2 = jnp.where(row8 < 2, pltpu.roll(prev, 2, 0), r2[:8])
            s1 = jnp.concatenate([t1, r1[8:]], axis=0)
            s2 = jnp.concatenate([t2, r2[8:]], axis=0)
            carry[:, cols] = up[tm - 8:]
            up_ref[0, :, cols] = up[tm - 8:]
        cw = cw_ref[:, cols]
        return cw[3:4] + cw[0:1] * s2 + cw[1:2] * s1 + cw[2:3] * up

    for c in range(N_FF_CHUNKS):
        a = conv_cols(c * FF_CHUNK)
        gv = conv_cols(D_FF + c * FF_CHUNK)
        act_s[:, c * FF_CHUNK:(c + 1) * FF_CHUNK] = (jax.nn.gelu(a) * gv).astype(BF16)
    x2 = x1 + gate2 * _dot(act_s[...], wdn_ref[...])
    y_ref[0] = x2 * lax.rsqrt(jnp.mean(x2 * x2, axis=-1, keepdims=True) + EPS) * gf_ref[...]


def _post_call(x, mod, ya, yb, g1, g2, gf, wm, wba, wbb, wo, wup, wdn, cw, prev=None, *, tm, seq_len):
    G, T, _ = x.shape
    R = mod.shape[2]
    nt = T // tm
    has_prev = prev is not None
    single = pl.Buffered(1)
    c2 = lambda b, i: (0, 0)
    c3 = lambda b, i: (0, 0, 0)
    in_specs = [pl.BlockSpec((1, tm, D_MODEL), lambda b, i: (b, i, 0)),
                pl.BlockSpec((1, 6, R, D_MODEL), lambda b, i: (b, 0, 0, 0)),
                pl.BlockSpec((1, tm, A_WIDTH), lambda b, i: (b, i, 0)),
                pl.BlockSpec((1, tm, B_WIDTH), lambda b, i: (b, i, 0)),
                pl.BlockSpec((1, D_MODEL), c2),
                pl.BlockSpec((1, D_MODEL), c2),
                pl.BlockSpec((1, D_MODEL), c2),
                pl.BlockSpec(wm.shape, c2, pipeline_mode=single),
                pl.BlockSpec(wba.shape, c2, pipeline_mode=single),
                pl.BlockSpec(wbb.shape, c2, pipeline_mode=single),
                pl.BlockSpec(wo.shape, c2, pipeline_mode=single),
                pl.BlockSpec(wup.shape, c2, pipeline_mode=single),
                pl.BlockSpec(wdn.shape, c2, pipeline_mode=single),
                pl.BlockSpec(cw.shape, c2, pipeline_mode=single)]
    args = [x, mod, ya, yb, g1, g2, gf, wm, wba, wbb, wo, wup, wdn, cw]
    if has_prev:
        in_specs += [pl.BlockSpec(prev[0].shape, c2), pl.BlockSpec(prev[1].shape, c2)]
        args += list(prev)
        up_shape = jax.ShapeDtypeStruct((T, F2), F32)
        up_spec = pl.BlockSpec((tm, F2), lambda b, i: (i, 0))
    else:
        up_shape = jax.ShapeDtypeStruct((G, 8, F2), F32)
        up_spec = pl.BlockSpec((1, 8, F2), lambda b, i: (b, 0, 0))
    return pl.pallas_call(
        functools.partial(_post_kernel, tm=tm, seq_len=seq_len, has_prev=has_prev),
        out_shape=[jax.ShapeDtypeStruct((G, T, D_MODEL), F32), up_shape],
        grid=(G, nt),
        in_specs=in_specs,
        out_specs=[pl.BlockSpec((1, tm, D_MODEL), lambda b, i: (b, i, 0)), up_spec],
        scratch_shapes=[pltpu.VMEM((8, F2), F32),
                        pltpu.VMEM((tm, D_MODEL), BF16),
                        pltpu.VMEM((tm, D_FF), BF16)],
        compiler_params=pltpu.CompilerParams(dimension_semantics=("arbitrary", "arbitrary"),
                                             vmem_limit_bytes=VMEM_LIMIT),
        name="post",
    )(*args)


def _block_diag2(m):
    z = jnp.zeros_like(m)
    return jnp.concatenate([jnp.concatenate([m, z], axis=-1), jnp.concatenate([z, m], axis=-1)], axis=-2)


def kernel(x_prompt, x_sample, cache_kv, state_kv_win, state_ffn_conv, page_table, c_prompt, c_sample, w_ada, b_ada, g_norm1, w_in, ln_v_g, ln_v_b, w_spatial, b_spatial, cmp_pe, cmp_w1, cmp_b1, cmp_w2, cmp_b2, w_branch_a, w_branch_b, w_out, g_norm2, w_up, w_conv, b_conv, w_down, g_final):
    B, T, _ = x_prompt.shape
    SB, ST, _ = x_sample.shape
    n_pool, page_size = cache_kv.shape[1], cache_kv.shape[2]
    n_pages = page_table.shape[1]
    past_len = n_pages * page_size
    lbuf = state_kv_win.shape[2]
    assert cache_kv.shape[0] == 1 and page_size == 128 and lbuf == WINDOW and SB * ST == 128 and ST <= 8
    assert past_len % BLOCK == 0 and n_pages == 128 and T % 512 == 0 and ST & (ST - 1) == 0 and ST >= 2

    win = w_in[0]
    wa = win[:, :2 * A_WIDTH].astype(BF16)
    o = 2 * A_WIDTH
    wq = win[:, o:o + B_WIDTH].astype(BF16)
    o += B_WIDTH
    wkv = win[:, o:o + KV_COLS]
    wkvT = wkv.T.astype(BF16)
    wkvc = wkv[:, :256].astype(BF16)
    o += KV_COLS
    wg = jnp.pad(win[:, o:o + 3 * B_HEADS], ((0, 0), (0, 128 - 3 * B_HEADS))).astype(BF16)
    wgT = jnp.pad(win[:, o:o + 3 * B_HEADS].T, ((0, 32 - 3 * B_HEADS), (0, 0))).astype(BF16)
    wqT = wq.T
    wrow = jnp.concatenate([wkv[:, :3 * 128], wkv[:, 4 * 128:5 * 128]], axis=1).astype(BF16)
    o += 3 * B_HEADS
    wm = win[:, o:].astype(BF16)
    g1 = g_norm1[0][None]
    g2 = g_norm2[0][None]
    gf = g_final[None]
    lng = ln_v_g[0][None]
    lnb = ln_v_b[0][None]

    ws = w_spatial[0]
    bs = b_spatial[0]
    wsp_p = jnp.tril(ws).astype(BF16)
    bsp_p = jnp.repeat(bs.T, 128, axis=1)
    w4 = jnp.tril(ws[:, :ST, :ST])
    eye_s = jnp.eye(128 // ST, dtype=F32)
    wsp_s = jnp.einsum('ab,gts->gatbs', eye_s, w4).reshape(A_GROUPS, 128, 128).astype(BF16)
    bsp_s = jnp.repeat(jnp.tile(bs[:, :ST].T, (128 // ST, 1)), 128, axis=1)

    w1, pe = cmp_w1[0].astype(BF16), cmp_pe[0]
    w1_rows = _block_diag2(w1)
    pe_rows = jnp.tile(pe, (1, 1, 2))
    w1_pages = _block_diag2(w1.transpose(0, 2, 1, 3)).reshape(2, HEAD_DIM // 2, 256, 128)
    pe_pages = jnp.tile(pe.transpose(0, 2, 1), (1, 1, 2)).reshape(2, HEAD_DIM // 2, 1, 256)
    pe_pages = jnp.pad(pe_pages, ((0, 0), (0, 0), (0, PE_ROWS - 1), (0, 0))).astype(BF16)
    b1t = jnp.tile(cmp_b1[0], (1, 2))[:, None, :]
    b2t = jnp.tile(cmp_b2[0], (1, 2))[:, None, :]
    w2bd = _block_diag2(cmp_w2[0].astype(BF16))

    wba = w_branch_a[0].astype(BF16)
    wbb = w_branch_b[0].astype(BF16)
    wo = w_out[0].astype(BF16)
    wup = w_up[0].astype(BF16)
    wdn = w_down[0].astype(BF16)
    cw = jnp.concatenate([w_conv[0], b_conv[0][None], jnp.zeros((4, F2), F32)], axis=0)

    mod = _mod_call(jnp.concatenate([c_prompt, c_sample], axis=0), w_ada[0], b_ada)
    mod_p = mod[:B].reshape(B, 6, 1, D_MODEL)
    mod_s = jnp.repeat(mod[B:], ST, axis=0).reshape(SB * ST, 6, D_MODEL).transpose(1, 0, 2)[None]

    ya_p, qT_p, kv4T_p, winT_p, vselT_p, vwinT_p, kvc_p, ksel_p, kwin_p, gzT_p = _inproj_call(
        x_prompt, mod_p, g1, wa, lng, lnb, wsp_p, bsp_p, (wqT, wkvT, wrow, wgT), tm=512, prompt=True)
    xs2 = x_sample.reshape(1, SB * ST, D_MODEL)
    ya_s, q_s, kvT_s, kvc_s, gz_s, vn_s = _inproj_call(
        xs2, mod_s, g1, wa, lng, lnb, wsp_s, bsp_s, (wq, wkvT, wkvc, wg), tm=128, prompt=False)

    nb = T // BLOCK
    cmp_p = _cmp_rows_call(kvc_p.reshape(B * T, 256), pe_rows, w1_rows, b1t, w2bd, b2t, R=128)
    kc_p = cmp_p[0].reshape(B, nb, 128)
    vcT_p = cmp_p[1].reshape(B, nb, 128).transpose(0, 2, 1)
    yb_p = _p_attn_call(qT_p, ksel_p, kwin_p, vselT_p, vwinT_p, kc_p, vcT_p, gzT_p, tq=512, kc=512)

    cache_pages = jnp.transpose(cache_kv[0], (0, 2, 3, 4, 1)).reshape(n_pool, 512, page_size)
    pool_seq = _cmp_gather_call(page_table.reshape(-1), cache_pages, pe_pages, w1_pages, b1t, w2bd, b2t,
                                Pt=64)
    kv_rows_s = jnp.transpose(kvT_s[0], (1, 0)).reshape(SB, ST, KV_COLS)
    newblk = jnp.pad(jnp.transpose(kv_rows_s[:, :, :256], (0, 2, 1)), ((0, 0), (0, 0), (0, page_size - ST)))
    newc = _cmp_pages_call(newblk, pe_pages, w1_pages, b1t, w2bd, b2t, Pt=SB)

    q_r = jnp.pad(q_s.reshape(SB, ST, KV_HEADS, GQA, HEAD_DIM).transpose(0, 2, 3, 1, 4),
                  ((0, 0), (0, 0), (0, 0), (0, 8 - ST), (0, 0))).reshape(SB, KV_HEADS, 8 * GQA, HEAD_DIM)
    ocmp_s, imp_s = _s_cmp_call(q_r, pool_seq, newc.reshape(SB, 1, 512), n_pages=n_pages, past_len=past_len)
    idx = _s_select_call(imp_s.reshape(SB * KV_HEADS * 8, 2 * n_pages), n_pages=n_pages, n_pick=N_SELECT - 1)
    idx_flat = idx.reshape(SB, KV_HEADS, 8, 128)[:, :, :ST, :N_SELECT].reshape(-1)
    winT = jnp.transpose(state_kv_win[0], (0, 2, 3, 4, 1)).reshape(SB, 256, lbuf)
    gz_r = jnp.pad(gz_s[0, :, :24].reshape(SB, ST, 3, KV_HEADS, GQA).transpose(0, 3, 4, 1, 2),
                   ((0, 0), (0, 0), (0, 0), (0, 8 - ST), (0, 128 - 3))).reshape(SB, KV_HEADS, 8 * GQA, 128)
    cache_tiles = cache_pages.reshape(n_pool, 4, KV_HEADS, HEAD_DIM, page_size)
    yb_r, wnextT = _s_attn_call(page_table, idx_flat, q_r, kvT_s[0], winT, gz_r, ocmp_s, cache_tiles,
                        n_tok=ST, past_len=past_len)
    yb_s = yb_r.reshape(SB, KV_HEADS, GQA, 8, HEAD_DIM)[:, :, :, :ST].transpose(0, 3, 1, 2, 4)
    yb_s = yb_s.reshape(1, SB * ST, B_WIDTH).astype(BF16)

    y_p, up_p = _post_call(x_prompt, mod_p, ya_p, yb_p, g1, g2, gf, wm, wba, wbb, wo, wup, wdn, cw,
                           tm=512, seq_len=T)
    st = state_ffn_conv[0]
    zrow = jnp.zeros((SB, 1, F2), F32)
    p1 = jnp.concatenate([st[:, 1:2], jnp.tile(zrow, (1, ST - 1, 1))], axis=1).reshape(SB * ST, F2)
    p2 = jnp.concatenate([st, jnp.tile(zrow, (1, ST - 2, 1))], axis=1).reshape(SB * ST, F2)
    y_s, up_s = _post_call(xs2, mod_s, ya_s, yb_s, g1, g2, gf, wm, wba, wbb, wo, wup, wdn, cw,
                           prev=(p1, p2),
                           tm=SB * ST, seq_len=ST)

    kv_prompt = jnp.transpose(kv4T_p.reshape(B, 4, KV_HEADS, HEAD_DIM, T), (0, 4, 1, 2, 3))[None]
    win_prompt = jnp.transpose(winT_p[:, :, T - WINDOW:].reshape(B, 2, KV_HEADS, HEAD_DIM, WINDOW),
                               (0, 4, 1, 2, 3))[None]
    kv_sample = kv_rows_s[:, :, :512].reshape(SB, ST, 4, KV_HEADS, HEAD_DIM)[None]
    win_sample = jnp.transpose(wnextT.reshape(SB, 2, KV_HEADS, HEAD_DIM, lbuf), (0, 4, 1, 2, 3))[None]
    v_chunk = vn_s.reshape(SB, ST, A_WIDTH)[None]
    conv_prompt = up_p[:, 6:8][None]
    conv_sample = up_s.reshape(SB, ST, F2)[:, ST - 2:][None]
    return (y_p, y_s.reshape(SB, ST, D_MODEL), kv_prompt, kv_sample, win_prompt, win_sample, v_chunk,
            conv_prompt, conv_sample)
```

```python
import functools

import jax
import jax.numpy as jnp
from jax import lax
from jax.experimental import pallas as pl
from jax.experimental.pallas import tpu as pltpu

F32 = jnp.float32
BF16 = jnp.bfloat16

D_MODEL = 1024
A_WIDTH = 512
A_GROUPS = 4
CHUNK = 128
B_HEADS = 8
HEAD_DIM = 64
B_WIDTH = 512
KV_HEADS = 2
GQA = 4
BLOCK = 64
N_SELECT = 16
WINDOW = 512
N_KV_SLOTS = 6
KV_COLS = 768
D_FF = 2816
F2 = 2 * D_FF
CONV_W = 3
EPS = 1e-6
NEG = -1e30
FORCE_SCORE = 1e4

FF_CHUNK = 256
N_FF_CHUNKS = D_FF // FF_CHUNK
ROW_BLOCK = 64
PE_ROWS = 16
ACC_ROWS = HEAD_DIM + 16
LOG2E = 1.4426950408889634
VMEM_LIMIT = 56 * 1024 * 1024


def _dot(a, b):
    return jnp.dot(a, b, preferred_element_type=F32)


def _dot_nt(a, b):
    return lax.dot_general(a, b, (((1,), (1,)), ((), ())), preferred_element_type=F32)


def _rms_mod(x, g, scale, shift):
    y = x * lax.rsqrt(jnp.mean(x * x, axis=-1, keepdims=True) + EPS)
    return (y * g) * (1.0 + scale) + shift


def _mod_kernel(c_ref, w_ref, b_ref, o_ref):
    c = c_ref[...]
    s = c * jax.nn.sigmoid(c)
    o_ref[...] = _dot(s.astype(BF16), w_ref[...].astype(BF16)) + b_ref[...]


def _mod_call(c_all, w_ada, b_ada):
    n = c_all.shape[0]
    tn = 1536
    return pl.pallas_call(
        _mod_kernel,
        out_shape=jax.ShapeDtypeStruct((n, 6 * D_MODEL), F32),
        grid=(6 * D_MODEL // tn,),
        in_specs=[pl.BlockSpec((n, D_MODEL), lambda j: (0, 0)),
                  pl.BlockSpec((D_MODEL, tn), lambda j: (0, j)),
                  pl.BlockSpec((1, tn), lambda j: (0, j))],
        out_specs=pl.BlockSpec((n, tn), lambda j: (0, j)),
        compiler_params=pltpu.CompilerParams(dimension_semantics=("arbitrary",), vmem_limit_bytes=VMEM_LIMIT),
        name="mod",
    )(c_all, w_ada, b_ada)


def _inproj_kernel(x_ref, mod_ref, g1_ref, wa_ref, lng_ref, lnb_ref, wsp_ref, bsp_ref, *refs, tm, prompt):
    x = x_ref[0]
    hb = _rms_mod(x, g1_ref[...], mod_ref[0, 1], mod_ref[0, 0]).astype(BF16)

    ga = jax.nn.gelu(_dot(hb, wa_ref[...]))
    u = ga[:, :A_WIDTH]
    v = ga[:, A_WIDTH:]
    vc = v - jnp.mean(v, axis=-1, keepdims=True)
    vn = vc * lax.rsqrt(jnp.mean(vc * vc, axis=-1, keepdims=True) + EPS) * lng_ref[...] + lnb_ref[...]
    vnb = vn.astype(BF16)

    if prompt:
        (wqT_ref, wkvT_ref, wrow_ref, wgT_ref,
         ya_ref, qT_ref, kv4T_ref, winT_ref, vselT_ref, vwinT_ref, kvc_ref, ksel_ref, kwin_ref, gzT_ref) = refs
    else:
        (wq_ref, wkvT_ref, wkvc_ref, wg_ref, ya_ref, q_ref, kvT_ref, kvc_ref, gz_ref, vn_ref) = refs
        vn_ref[0] = vn

    for c in range(tm // CHUNK):
        rows = slice(c * CHUNK, (c + 1) * CHUNK)
        s = jnp.concatenate(
            [_dot(wsp_ref[g], vnb[rows, g * 128:(g + 1) * 128]) for g in range(A_GROUPS)], axis=1)
        ya_ref[0, rows, :] = (u[rows] * (s + bsp_ref[...])).astype(BF16)

    kvT = _dot_nt(wkvT_ref[...], hb)
    if prompt:
        qT_ref[0] = (_dot_nt(wqT_ref[...], hb) * (HEAD_DIM ** -0.5 * LOG2E)).astype(BF16)
        kv4T_ref[0] = kvT[:4 * 128]
        winT_ref[0] = kvT[4 * 128:]
        vselT_ref[0] = kvT[3 * 128:4 * 128].astype(BF16)
        vwinT_ref[0] = kvT[5 * 128:].astype(BF16)
        row = _dot(hb, wrow_ref[...])
        kvc_ref[0] = row[:, :256]
        ksel_ref[0] = row[:, 256:384].astype(BF16)
        kwin_ref[0] = row[:, 384:].astype(BF16)
        gzT_ref[0] = jax.nn.sigmoid(_dot_nt(wgT_ref[...], hb))
    else:
        q_ref[0] = (_dot(hb, wq_ref[...]) * (HEAD_DIM ** -0.5)).astype(BF16)
        kvT_ref[0] = kvT
        kvc_ref[0] = _dot(hb, wkvc_ref[...])
        gz_ref[0] = jax.nn.sigmoid(_dot(hb, wg_ref[...]))


def _inproj_call(x, mod, g1, wa, lng, lnb, wsp, bsp, proj_w, *, tm, prompt):
    G, T, _ = x.shape
    R = mod.shape[2]
    nt = T // tm
    const2 = lambda b, i: (0, 0)
    const3 = lambda b, i: (0, 0, 0)
    rows = lambda w: pl.BlockSpec((1, tm, w), lambda b, i: (b, i, 0))
    cols = lambda h: pl.BlockSpec((1, h, tm), lambda b, i: (b, 0, i))
    if prompt:
        outs = [((G, T, A_WIDTH), BF16, rows(A_WIDTH)),
                ((G, B_WIDTH, T), BF16, cols(B_WIDTH)),
                ((G, 512, T), F32, cols(512)),
                ((G, 256, T), F32, cols(256)),
                ((G, 128, T), BF16, cols(128)),
                ((G, 128, T), BF16, cols(128)),
                ((G, T, 256), F32, rows(256)),
                ((G, T, 128), BF16, rows(128)),
                ((G, T, 128), BF16, rows(128)),
                ((G, 32, T), F32, cols(32))]
    else:
        outs = [((G, T, A_WIDTH), BF16, rows(A_WIDTH)),
                ((G, T, B_WIDTH), BF16, rows(B_WIDTH)),
                ((G, KV_COLS, T), F32, cols(KV_COLS)),
                ((G, T, 256), F32, rows(256)),
                ((G, T, 128), F32, rows(128)),
                ((G, T, A_WIDTH), F32, rows(A_WIDTH))]
    return pl.pallas_call(
        functools.partial(_inproj_kernel, tm=tm, prompt=prompt),
        out_shape=[jax.ShapeDtypeStruct(s, d) for s, d, _ in outs],
        grid=(G, nt),
        in_specs=[pl.BlockSpec((1, tm, D_MODEL), lambda b, i: (b, i, 0)),
                  pl.BlockSpec((1, 6, R, D_MODEL), lambda b, i: (b, 0, 0, 0)),
                  pl.BlockSpec((1, D_MODEL), const2),
                  pl.BlockSpec(wa.shape, const2),
                  pl.BlockSpec((1, A_WIDTH), const2),
                  pl.BlockSpec((1, A_WIDTH), const2),
                  pl.BlockSpec(wsp.shape, const3),
                  pl.BlockSpec(bsp.shape, const2)] + [pl.BlockSpec(w.shape, const2) for w in proj_w],
        out_specs=[sp for _, _, sp in outs],
        compiler_params=pltpu.CompilerParams(dimension_semantics=("arbitrary", "arbitrary"),
                                             vmem_limit_bytes=VMEM_LIMIT),
        name="inproj",
    )(x, mod, g1, wa, lng, lnb, wsp, bsp, *proj_w)


def _cmp_tail(acc, b1, w2, b2):
    return _dot(jax.nn.gelu(acc + b1).astype(BF16), w2) + b2


def _cmp_rows_kernel(x_ref, pe_ref, w1_ref, b1_ref, w2_ref, b2_ref, o_ref, *, R):
    acc = jnp.zeros((R, 128), F32)
    for j in range(BLOCK):
        xj = x_ref[pl.ds(j, R, stride=BLOCK), :] + pe_ref[0, j:j + 1, :]
        acc = acc + _dot(xj.astype(BF16), w1_ref[0, j])
    o_ref[0] = _cmp_tail(acc, b1_ref[0], w2_ref[0], b2_ref[0])


def _cmp_rows_call(kvc2d, pe, w1, b1, w2, b2, *, R):
    nrows = kvc2d.shape[0] // BLOCK
    return pl.pallas_call(
        functools.partial(_cmp_rows_kernel, R=R),
        out_shape=jax.ShapeDtypeStruct((2, nrows, 128), F32),
        grid=(2, nrows // R),
        in_specs=[pl.BlockSpec((R * BLOCK, 128), lambda s, r: (r, s)),
                  pl.BlockSpec((1, BLOCK, 128), lambda s, r: (s, 0, 0)),
                  pl.BlockSpec((1, BLOCK, 128, 128), lambda s, r: (s, 0, 0, 0)),
                  pl.BlockSpec((1, 1, 128), lambda s, r: (s, 0, 0)),
                  pl.BlockSpec((1, 128, 128), lambda s, r: (s, 0, 0)),
                  pl.BlockSpec((1, 1, 128), lambda s, r: (s, 0, 0))],
        out_specs=pl.BlockSpec((1, R, 128), lambda s, r: (s, r, 0)),
        compiler_params=pltpu.CompilerParams(dimension_semantics=("arbitrary", "arbitrary"),
                                             vmem_limit_bytes=VMEM_LIMIT),
        name="cmp_rows",
    )(kvc2d, pe, w1, b1, w2, b2)


def _compress_pages(load_rows, pe_ref, w1_ref, b1_ref, w2_ref, b2_ref, o_ref, Pt):
    for s in range(2):
        acc = jnp.zeros((2 * Pt + PE_ROWS, 128), F32)
        for dg in range(HEAD_DIM // 8):
            r0 = s * 128 + dg * 8
            xa = jnp.swapaxes(load_rows(r0), 0, 1).astype(BF16)
            xb = jnp.swapaxes(load_rows(HEAD_DIM + r0), 0, 1).astype(BF16)
            for dd in range(0, 8, 2):
                dp = dg * 4 + dd // 2
                xd = jnp.concatenate([jnp.concatenate([xa[dd], xa[dd + 1]], axis=1),
                                      jnp.concatenate([xb[dd], xb[dd + 1]], axis=1),
                                      pe_ref[s, dp]], axis=0)
                acc = acc + _dot(xd, w1_ref[s, dp])
        pre = acc[:2 * Pt] + acc[2 * Pt:2 * Pt + 1]
        r = _cmp_tail(pre, b1_ref[s], w2_ref[s], b2_ref[s])
        o_ref[:, (2 * s) * 128:(2 * s + 1) * 128] = r[:Pt]
        o_ref[:, (2 * s + 1) * 128:(2 * s + 2) * 128] = r[Pt:]


def _cmp_pages_kernel(x_ref, pe_ref, w1_ref, b1_ref, w2_ref, b2_ref, o_ref, *, Pt):
    _compress_pages(lambda r0: x_ref[:, r0:r0 + 8, :], pe_ref, w1_ref, b1_ref, w2_ref, b2_ref, o_ref, Pt)


def _cmp_weight_specs(ws, index):
    return [pl.BlockSpec(w.shape, functools.partial(index, (0,) * w.ndim)) for w in ws]


def _cmp_pages_call(pages, pe, w1, b1, w2, b2, *, Pt):
    P = pages.shape[0]
    ws = (pe, w1, b1, w2, b2)
    return pl.pallas_call(
        functools.partial(_cmp_pages_kernel, Pt=Pt),
        out_shape=jax.ShapeDtypeStruct((P, 512), F32),
        grid=(P // Pt,),
        in_specs=[pl.BlockSpec((Pt, 256, 128), lambda i: (i, 0, 0))] + _cmp_weight_specs(ws, lambda z, i: z),
        out_specs=pl.BlockSpec((Pt, 512), lambda i: (i, 0)),
        compiler_params=pltpu.CompilerParams(dimension_semantics=("arbitrary",), vmem_limit_bytes=VMEM_LIMIT),
        name="cmp_pages",
    )(pages, *ws)


def _cmp_gather_kernel(pt_ref, cache_ref, pe_ref, w1_ref, b1_ref, w2_ref, b2_ref, o_ref, xbuf, sem, *, Pt):
    i = pl.program_id(0)
    slot = i & 1

    def page_copy(step, sl, k):
        return pltpu.make_async_copy(cache_ref.at[pt_ref[step * Pt + k], pl.ds(0, 256), :],
                                     xbuf.at[sl, k], sem.at[sl])

    @pl.when(i == 0)
    def _():
        for k in range(Pt):
            page_copy(0, 0, k).start()

    @pl.when(i + 1 < pl.num_programs(0))
    def _():
        for k in range(Pt):
            page_copy(i + 1, 1 - slot, k).start()

    for k in range(Pt):
        page_copy(i, slot, k).wait()
    _compress_pages(lambda r0: xbuf[slot, :, r0:r0 + 8, :], pe_ref, w1_ref, b1_ref, w2_ref, b2_ref, o_ref, Pt)


def _cmp_gather_call(pt_flat, cache_pages, pe, w1, b1, w2, b2, *, Pt):
    n = pt_flat.shape[0]
    ws = (pe, w1, b1, w2, b2)
    return pl.pallas_call(
        functools.partial(_cmp_gather_kernel, Pt=Pt),
        out_shape=jax.ShapeDtypeStruct((n, 512), F32),
        grid_spec=pltpu.PrefetchScalarGridSpec(
            num_scalar_prefetch=1,
            grid=(n // Pt,),
            in_specs=[pl.BlockSpec(memory_space=pl.ANY)] + _cmp_weight_specs(ws, lambda z, i, pt: z),
            out_specs=pl.BlockSpec((Pt, 512), lambda i, pt: (i, 0)),
            scratch_shapes=[pltpu.VMEM((2, Pt, 256, 128), F32),
                            pltpu.SemaphoreType.DMA((2,))]),
        compiler_params=pltpu.CompilerParams(dimension_semantics=("arbitrary",), vmem_limit_bytes=VMEM_LIMIT),
        name="cmp_gather",
    )(pt_flat, cache_pages, *ws)


def _pair_schedule(T, tq, kc):
    js, cs = [], []
    for j in range(T // tq):
        for c in range(((j + 1) * tq - 1) // kc + 1):
            js.append(j)
            cs.append(c)
    return js, cs


def _p_attn_kernel(jt_ref, ct_ref, qT_ref, ksel_ref, vselT_ref, kwin_ref, kwinp_ref, vwinT_ref, vwinTp_ref,
                   kc_ref, vcT_ref, gzT_ref, yb_ref,
                   sel_s, ocmp_s, m_s, acc_s, s_scr, s2_scr, p_scr, p2_scr, b_scr, *, tq, kc, T, strip):
    p = pl.program_id(1)
    j = jt_ref[p]
    c = ct_ref[p]
    nb = T // BLOCK
    N = GQA * tq
    q0 = j * tq
    k0 = c * kc
    c_last = ((j + 1) * tq - 1) // kc

    def q_pad(h):
        qT = jnp.concatenate([qT_ref[0, (h * GQA + g) * HEAD_DIM:(h * GQA + g + 1) * HEAD_DIM, :]
                              for g in range(GQA)], axis=1)
        z = jnp.zeros_like(qT)
        return jnp.concatenate([qT if k == h else z for k in range(KV_HEADS)], axis=0)

    @pl.when(c == 0)
    def _():
        m_s[...] = jnp.full(m_s.shape, NEG, F32)
        acc_s[...] = jnp.zeros(acc_s.shape, F32)
        tok_n = q0 + (lax.broadcasted_iota(jnp.int32, (1, N), 1) & (tq - 1))
        tok_1 = q0 + lax.broadcasted_iota(jnp.int32, (1, tq), 1)
        blk = lax.broadcasted_iota(jnp.int32, (nb, 1), 0)
        avail = (blk + 1) * BLOCK <= tok_n + 1
        cur = tok_1 >> 6
        forced = (blk == 0) | (blk == cur) | (blk == cur - 1)
        future = blk > cur
        kcb = kc_ref[0].astype(BF16)
        for h in range(KV_HEADS):
            s = jnp.where(avail, _dot(kcb, q_pad(h)), NEG)
            m = jnp.max(s, axis=0, keepdims=True)
            e = jnp.where(avail, jnp.exp2(s - m), 0.0)
            pr = e / jnp.maximum(jnp.sum(e, axis=0, keepdims=True), 1e-30)
            ocmp_s[h] = _dot(vcT_ref[0, h * HEAD_DIM:(h + 1) * HEAD_DIM, :].astype(BF16), pr.astype(BF16))
            imp = pr[:, 0:tq]
            for g in range(1, GQA):
                imp = imp + pr[:, g * tq:(g + 1) * tq]
            imp = jnp.where(forced, FORCE_SCORE, jnp.where(future, -1.0, imp))
            rows = []
            for i in range(nb):
                vi = imp[i:i + 1, :]
                ahead = (imp > vi) | ((imp == vi) & (blk < i))
                cnt = jnp.sum(ahead.astype(F32), axis=0, keepdims=True)
                rows.append((cnt < float(N_SELECT)).astype(F32))
            sel_s[h] = jnp.concatenate(rows + [jnp.zeros((128 - nb, tq), F32)], axis=0).astype(BF16)

    kpos = k0 + lax.broadcasted_iota(jnp.int32, (kc, tq), 0)
    tok = q0 + lax.broadcasted_iota(jnp.int32, (kc, tq), 1)

    strips = [slice(st * strip, (st + 1) * strip) for st in range(N // strip)]

    row_blocks = [slice(r, r + ROW_BLOCK) for r in range(0, kc, ROW_BLOCK)]

    ones_rows = (lax.broadcasted_iota(jnp.int32, (ACC_ROWS - HEAD_DIM, kc), 0) == 0).astype(BF16)

    streams = [(h, st) for st in range(len(strips)) for h in range(KV_HEADS)]

    def q_strip(h, st):
        g, t0 = divmod(st * strip, tq)
        r0 = (h * GQA + g) * HEAD_DIM
        qT = qT_ref[0, r0:r0 + HEAD_DIM, t0:t0 + strip]
        z = jnp.zeros_like(qT)
        return jnp.concatenate([qT if k == h else z for k in range(KV_HEADS)], axis=0)

    def with_ones(vT_ref, h):
        return jnp.concatenate([vT_ref[0, h * HEAD_DIM:(h + 1) * HEAD_DIM, :], ones_rows], axis=0)

    def run_skewed(stages):
        for k in range(len(streams) + len(stages) - 1):
            for lag, stage in enumerate(stages):
                if 0 <= k - lag < len(streams):
                    stage(*streams[k - lag])

    def selected_step(diagonal):
        keys = ksel_ref[0]
        vT1 = [with_ones(vselT_ref, h) for h in range(KV_HEADS)]
        alpha = {}

        def n_keys(st):
            return min(kc, (st * strip) % tq + strip) if diagonal else kc

        def scores(h, st):
            nk = n_keys(st)
            s_scr[h, :nk, strips[st]] = _dot(keys[:nk], q_strip(h, st))

        def probs(h, st):
            ls = strips[st]
            t0 = (st * strip) % tq
            rbs = row_blocks[:n_keys(st) // ROW_BLOCK]
            m = None
            for rb in rbs:
                t = s_scr[h, rb, ls] + b_scr[h, rb, t0:t0 + strip]
                s_scr[h, rb, ls] = t
                m = t if m is None else jnp.maximum(m, t)
            m_old = m_s[h, :, ls]
            m_new = jnp.maximum(m_old, jnp.max(m, axis=0, keepdims=True))
            alpha[(h, st)] = jnp.exp2(m_old - m_new)
            m_s[h, :, ls] = m_new
            for rb in rbs:
                p_scr[h, rb, ls] = jnp.exp2(s_scr[h, rb, ls] - m_new).astype(BF16)

        def weighted_values(h, st):
            ls = strips[st]
            nk = n_keys(st)
            acc_s[h, :, ls] = alpha[(h, st)] * acc_s[h, :, ls] + _dot(vT1[h][:, :nk], p_scr[h, :nk, ls])

        run_skewed((scores, probs, weighted_values))

    def window_step():
        no_prev = jnp.where(j == 0, NEG, 0.0)
        vT1 = [with_ones(vwinT_ref, h) for h in range(KV_HEADS)]
        vT1p = [with_ones(vwinTp_ref, h) for h in range(KV_HEADS)]

        def scores(h, st):
            qp = q_strip(h, st)
            s_scr[h, :, strips[st]] = _dot(kwin_ref[0], qp)
            s2_scr[h, :, strips[st]] = _dot(kwinp_ref[0], qp)

        def probs(h, st):
            ls = strips[st]
            t0 = (st * strip) % tq
            m = None
            for rb in row_blocks:
                cur = b_scr[KV_HEADS, rb, t0:t0 + strip] > 0.5
                t = jnp.where(cur, s_scr[h, rb, ls], s2_scr[h, rb, ls] + no_prev)
                s_scr[h, rb, ls] = t
                m = t if m is None else jnp.maximum(m, t)
            m_new = jnp.max(m, axis=0, keepdims=True)
            for rb in row_blocks:
                pe = jnp.exp2(s_scr[h, rb, ls] - m_new)
                pc = pe * b_scr[KV_HEADS, rb, t0:t0 + strip]
                p_scr[h, rb, ls] = pc.astype(BF16)
                p2_scr[h, rb, ls] = (pe - pc).astype(BF16)

        def weighted_values(h, st):
            ls = strips[st]
            acc_s[KV_HEADS + h, :, ls] = _dot(vT1[h], p_scr[h, :, ls]) + _dot(vT1p[h], p2_scr[h, :, ls])

        run_skewed((scores, probs, weighted_values))

    blk_of_key = (k0 >> 6) + (lax.broadcasted_iota(jnp.int32, (kc, 128), 0) >> 6)
    expand = (lax.broadcasted_iota(jnp.int32, (kc, 128), 1) == blk_of_key).astype(BF16)
    causal = kpos <= tok
    for h in range(KV_HEADS):
        b_scr[h] = jnp.where((_dot(expand, sel_s[h]) > 0.5) & causal, 0.0, NEG)

    @pl.when(c < c_last)
    def _():
        selected_step(diagonal=False)

    @pl.when(c == c_last)
    def _():
        selected_step(diagonal=True)
        b_scr[KV_HEADS] = causal.astype(F32)
        window_step()

    @pl.when(c == c_last)
    def _():
        gz = gzT_ref[0]
        outs = []
        for h in range(KV_HEADS):
            a_sel = acc_s[h]
            a_win = acc_s[KV_HEADS + h]
            o_sel = a_sel[:HEAD_DIM] / jnp.maximum(a_sel[HEAD_DIM:HEAD_DIM + 1], 1e-30)
            o_win = a_win[:HEAD_DIM] / jnp.maximum(a_win[HEAD_DIM:HEAD_DIM + 1], 1e-30)
            o_cmp = ocmp_s[h]
            for g in range(GQA):
                col = h * GQA + g
                ls = slice(g * tq, (g + 1) * tq)
                outs.append(gz[col:col + 1] * o_cmp[:, ls] + gz[B_HEADS + col:B_HEADS + col + 1] * o_sel[:, ls]
                            + gz[2 * B_HEADS + col:2 * B_HEADS + col + 1] * o_win[:, ls])
        yb_ref[0] = jnp.concatenate(outs, axis=0).T.astype(BF16)


def _p_attn_call(qT, ksel, kwin, vselT, vwinT, kc_all, vcT, gzT, *, tq, kc):
    B, _, T = qT.shape
    nb = T // BLOCK
    N = GQA * tq
    assert tq == kc == WINDOW
    js, cs = _pair_schedule(T, tq, kc)
    jt = jnp.asarray(js, jnp.int32)
    ct = jnp.asarray(cs, jnp.int32)
    prev = lambda jt, p: jnp.maximum(jt[p] - 1, 0)
    return pl.pallas_call(
        functools.partial(_p_attn_kernel, tq=tq, kc=kc, T=T, strip=256),
        out_shape=jax.ShapeDtypeStruct((B, T, B_WIDTH), BF16),
        grid_spec=pltpu.PrefetchScalarGridSpec(
            num_scalar_prefetch=2,
            grid=(B, len(js)),
            in_specs=[pl.BlockSpec((1, B_WIDTH, tq), lambda b, p, jt, ct: (b, 0, jt[p])),
                      pl.BlockSpec((1, kc, 128), lambda b, p, jt, ct: (b, ct[p], 0)),
                      pl.BlockSpec((1, 128, kc), lambda b, p, jt, ct: (b, 0, ct[p])),
                      pl.BlockSpec((1, kc, 128), lambda b, p, jt, ct: (b, jt[p], 0)),
                      pl.BlockSpec((1, kc, 128), lambda b, p, jt, ct: (b, prev(jt, p), 0)),
                      pl.BlockSpec((1, 128, kc), lambda b, p, jt, ct: (b, 0, jt[p])),
                      pl.BlockSpec((1, 128, kc), lambda b, p, jt, ct: (b, 0, prev(jt, p))),
                      pl.BlockSpec((1, nb, 128), lambda b, p, jt, ct: (b, 0, 0)),
                      pl.BlockSpec((1, 128, nb), lambda b, p, jt, ct: (b, 0, 0)),
                      pl.BlockSpec((1, 32, tq), lambda b, p, jt, ct: (b, 0, jt[p]))],
            out_specs=pl.BlockSpec((1, tq, B_WIDTH), lambda b, p, jt, ct: (b, jt[p], 0)),
            scratch_shapes=[pltpu.VMEM((KV_HEADS, 128, tq), BF16),
                            pltpu.VMEM((KV_HEADS, HEAD_DIM, N), F32),
                            pltpu.VMEM((2 * KV_HEADS, 1, N), F32),
                            pltpu.VMEM((2 * KV_HEADS, ACC_ROWS, N), F32),
                            pltpu.VMEM((KV_HEADS, kc, N), F32),
                            pltpu.VMEM((KV_HEADS, kc, N), F32),
                            pltpu.VMEM((KV_HEADS, kc, N), BF16),
                            pltpu.VMEM((KV_HEADS, kc, N), BF16),
                            pltpu.VMEM((KV_HEADS + 1, kc, tq), F32)]),
        compiler_params=pltpu.CompilerParams(dimension_semantics=("arbitrary", "arbitrary"),
                                             vmem_limit_bytes=VMEM_LIMIT),
        name="p_attn",
    )(jt, ct, qT, ksel, vselT, kwin, kwin, vwinT, vwinT, kc_all, vcT, gzT)


def _s_cmp_kernel(q_ref, gath, newc_ref, ocmp_ref, imp_ref, *, n_pages, past_len):
    rows = 8 * GQA
    t_row = lax.broadcasted_iota(jnp.int32, (rows, 1), 0) & 7
    pos = past_len + t_row
    lane = lax.broadcasted_iota(jnp.int32, (1, 2 * n_pages), 1)
    blk = 2 * (lane & (n_pages - 1)) + (lane >> 7)
    new_blk = past_len // BLOCK
    avail = (blk + 1) * BLOCK <= pos + 1
    avail_new = (new_blk + 1) * BLOCK <= pos + 1
    for h in range(KV_HEADS):
        qh = q_ref[0, h]
        kc = [gath[:, (h * 2 + k) * 64:(h * 2 + k + 1) * 64].astype(BF16) for k in range(2)]
        vc = [gath[:, (2 + h) * 128 + k * 64:(2 + h) * 128 + (k + 1) * 64].astype(BF16) for k in range(2)]
        kc_new = newc_ref[0, :, h * 128:h * 128 + 64].astype(BF16).astype(F32)
        vc_new = newc_ref[0, :, (2 + h) * 128:(2 + h) * 128 + 64].astype(BF16).astype(F32)
        s = jnp.where(avail, jnp.concatenate([_dot_nt(qh, kc[0]), _dot_nt(qh, kc[1])], axis=1), NEG)
        s_new = jnp.where(avail_new, jnp.sum(qh.astype(F32) * kc_new, axis=-1, keepdims=True), NEG)
        m = jnp.maximum(jnp.max(s, axis=-1, keepdims=True), s_new)
        e = jnp.where(avail, jnp.exp(s - m), 0.0)
        e_new = jnp.where(avail_new, jnp.exp(s_new - m), 0.0)
        den = jnp.maximum(jnp.sum(e, axis=-1, keepdims=True) + e_new, 1e-30)
        p = e / den
        p_new = e_new / den
        pb = p.astype(BF16)
        ocmp_ref[0, h] = (_dot(pb[:, :n_pages], vc[0]) + _dot(pb[:, n_pages:], vc[1])
                          + p_new.astype(BF16).astype(F32) * vc_new)
        imp = p[0:8] + p[8:16] + p[16:24] + p[24:32]
        cur = (past_len + lax.broadcasted_iota(jnp.int32, (8, 1), 0)) >> 6
        forced = (blk == 0) | (blk == cur) | (blk == cur - 1)
        imp_ref[0, h] = jnp.where(forced, FORCE_SCORE, jnp.where(blk > cur, -1.0, imp))


def _s_cmp_call(q_r, pool_seq, newc, *, n_pages, past_len):
    nbatch = q_r.shape[0]
    return pl.pallas_call(
        functools.partial(_s_cmp_kernel, n_pages=n_pages, past_len=past_len),
        out_shape=[jax.ShapeDtypeStruct((nbatch, KV_HEADS, 32, HEAD_DIM), F32),
                   jax.ShapeDtypeStruct((nbatch, KV_HEADS, 8, 2 * n_pages), F32)],
        grid=(nbatch,),
        in_specs=[pl.BlockSpec((1, KV_HEADS, 32, HEAD_DIM), lambda b: (b, 0, 0, 0)),
                  pl.BlockSpec((n_pages, 512), lambda b: (b, 0)),
                  pl.BlockSpec((1, 1, 512), lambda b: (b, 0, 0))],
        out_specs=[pl.BlockSpec((1, KV_HEADS, 32, HEAD_DIM), lambda b: (b, 0, 0, 0)),
                   pl.BlockSpec((1, KV_HEADS, 8, 2 * n_pages), lambda b: (b, 0, 0, 0))],
        compiler_params=pltpu.CompilerParams(dimension_semantics=("arbitrary",), vmem_limit_bytes=VMEM_LIMIT),
        name="s_cmp",
    )(q_r, pool_seq, newc)


def _s_select_kernel(imp_ref, idx_ref, *, n_pages, n_pick):
    v = imp_ref[...]
    rows = v.shape[0]
    lane = lax.broadcasted_iota(jnp.int32, (1, 2 * n_pages), 1)
    blk = (2 * (lane & (n_pages - 1)) + (lane >> 7)).astype(F32)
    out_lane = lax.broadcasted_iota(jnp.int32, (1, 128), 1)
    out = jnp.full((rows, 128), float(2 * n_pages), F32)
    for k in range(n_pick):
        m = jnp.max(v, axis=-1, keepdims=True)
        pick = jnp.min(jnp.where(v == m, blk, 1e9), axis=-1, keepdims=True)
        out = jnp.where(out_lane == k, pick, out)
        v = jnp.where(blk == pick, NEG, v)
    idx_ref[...] = out.astype(jnp.int32)


def _s_select_call(imp2d, *, n_pages, n_pick):
    rows = imp2d.shape[0]
    return pl.pallas_call(
        functools.partial(_s_select_kernel, n_pages=n_pages, n_pick=n_pick),
        out_shape=jax.ShapeDtypeStruct((rows, 128), jnp.int32),
        grid=(1,),
        in_specs=[pl.BlockSpec(imp2d.shape, lambda i: (0, 0))],
        out_specs=pl.BlockSpec((rows, 128), lambda i: (0, 0)),
        compiler_params=pltpu.CompilerParams(dimension_semantics=("arbitrary",), vmem_limit_bytes=VMEM_LIMIT),
        name="s_select",
    )(imp2d)


def _s_attn_kernel(pt_ref, idx_ref, q_ref, kvn_ref, win_ref, gz_ref, ocmp_ref, cache_ref, yb_ref, wnext_ref,
                   kvbuf, sem, *, n_tok, n_gather, past_len):
    b = pl.program_id(0)
    span = n_gather * 128
    slot = b & 1

    def tile_copies(bb, sl, t, h, jj):
        n = idx_ref[((bb * KV_HEADS + h) * n_tok + t) * N_SELECT + jj]
        page = pt_ref[bb, n >> 1]
        return [pltpu.make_async_copy(cache_ref.at[page, 2 + kv, h],
                                      kvbuf.at[sl, h, t, kv, :, pl.ds(jj * 128, 128)], sem.at[sl])
                for kv in range(2)]

    def for_all_tiles(bb, sl, fn):
        for h in range(KV_HEADS):
            for t in range(n_tok):
                for jj in range(n_gather):
                    for cp in tile_copies(bb, sl, t, h, jj):
                        fn(cp)

    @pl.when(b == 0)
    def _():
        for_all_tiles(0, 0, lambda cp: cp.start())

    @pl.when(b + 1 < pl.num_programs(0))
    def _():
        for_all_tiles(b + 1, 1 - slot, lambda cp: cp.start())

    rows = 8 * GQA
    t_row = lax.broadcasted_iota(jnp.int32, (rows, 1), 0) & 7
    gz = gz_ref[0]

    lane_n = lax.broadcasted_iota(jnp.int32, (1, 128), 1)
    tok_shift = n_tok.bit_length() - 1
    new_ok = ((lane_n >> tok_shift) == b) & ((lane_n & (n_tok - 1)) <= t_row)
    lane_w = lax.broadcasted_iota(jnp.int32, (1, WINDOW), 1)
    win_ok = lane_w > t_row

    shifted = pltpu.roll(win_ref[0], WINDOW - n_tok, 1)
    moved = pltpu.roll(kvn_ref[4 * 128:, :], (128 - n_tok) - b * n_tok, 1)
    wnext_ref[0, :, :WINDOW - 128] = shifted[:, :WINDOW - 128]
    wnext_ref[0, :, WINDOW - 128:] = jnp.where(lane_n >= 128 - n_tok, moved, shifted[:, WINDOW - 128:])

    def softmax_pv(parts):
        m = None
        for sc, mk, _ in parts:
            mm = jnp.max(jnp.where(mk, sc, NEG), axis=-1, keepdims=True)
            m = mm if m is None else jnp.maximum(m, mm)
        den = jnp.zeros((rows, 1), F32)
        o = jnp.zeros((rows, HEAD_DIM), F32)
        for sc, mk, vts in parts:
            e = jnp.where(mk, jnp.exp(jnp.where(mk, sc, NEG) - m), 0.0)
            den = den + jnp.sum(e, axis=-1, keepdims=True)
            for rmask, vt in vts:
                er = e if rmask is None else jnp.where(rmask, e, 0.0)
                o = o + _dot_nt(er.astype(BF16), vt)
        return o / jnp.maximum(den, 1e-30)

    win_out = []
    for h in range(KV_HEADS):
        qh = q_ref[0, h]
        kw = win_ref[0, h * 64:(h + 1) * 64, :].astype(BF16)
        vw = win_ref[0, 128 + h * 64:128 + (h + 1) * 64, :].astype(BF16)
        kwn = kvn_ref[4 * 128 + h * 64:4 * 128 + (h + 1) * 64, :].astype(BF16)
        vwn = kvn_ref[5 * 128 + h * 64:5 * 128 + (h + 1) * 64, :].astype(BF16)
        win_out.append(softmax_pv([(_dot(qh, kw), win_ok, [(None, vw)]),
                                   (_dot(qh, kwn), new_ok, [(None, vwn)])]))

    for_all_tiles(b, slot, lambda cp: cp.wait())

    lane_s = lax.broadcasted_iota(jnp.int32, (1, span), 1)
    for h in range(KV_HEADS):
        qh = q_ref[0, h]
        sc = jnp.zeros((rows, span), F32)
        half = jnp.zeros((rows, span), jnp.int32)
        for t in range(n_tok):
            st = _dot(qh, kvbuf[slot, h, t, 0].astype(BF16))
            hrow = jnp.zeros((1, span), jnp.int32)
            for jj in range(n_gather):
                n = idx_ref[((b * KV_HEADS + h) * n_tok + t) * N_SELECT + jj]
                hrow = jnp.where((lane_s >> 7) == jj, n & 1, hrow)
            sc = jnp.where(t_row == t, st, sc)
            half = jnp.where(t_row == t, hrow, half)
        ok = ((lane_s >> 6) & 1) == half
        ksn = kvn_ref[2 * 128 + h * 64:2 * 128 + (h + 1) * 64, :].astype(BF16)
        vsn = kvn_ref[3 * 128 + h * 64:3 * 128 + (h + 1) * 64, :].astype(BF16)
        vts = [(t_row == t, kvbuf[slot, h, t, 1].astype(BF16)) for t in range(n_tok)]
        o_sel = softmax_pv([(sc, ok, vts), (_dot(qh, ksn), new_ok, [(None, vsn)])])
        yb_ref[0, h] = (gz[h, :, 0:1] * ocmp_ref[0, h] + gz[h, :, 1:2] * o_sel + gz[h, :, 2:3] * win_out[h])


def _s_attn_call(page_table, idx_flat, q_r, kvT_new, winT, gz_r, ocmp, cache_pages, *, n_tok, past_len):
    nbatch = page_table.shape[0]
    n_gather = N_SELECT - 1
    span = n_gather * 128
    blk4 = lambda b, pt, ix: (b, 0, 0, 0)
    return pl.pallas_call(
        functools.partial(_s_attn_kernel, n_tok=n_tok, n_gather=n_gather, past_len=past_len),
        out_shape=[jax.ShapeDtypeStruct((nbatch, KV_HEADS, 32, HEAD_DIM), F32),
                   jax.ShapeDtypeStruct((nbatch, 256, WINDOW), F32)],
        grid_spec=pltpu.PrefetchScalarGridSpec(
            num_scalar_prefetch=2,
            grid=(nbatch,),
            in_specs=[pl.BlockSpec((1, KV_HEADS, 32, HEAD_DIM), blk4),
                      pl.BlockSpec(kvT_new.shape, lambda b, pt, ix: (0, 0)),
                      pl.BlockSpec((1, 256, WINDOW), lambda b, pt, ix: (b, 0, 0)),
                      pl.BlockSpec((1, KV_HEADS, 32, 128), blk4),
                      pl.BlockSpec((1, KV_HEADS, 32, HEAD_DIM), blk4),
                      pl.BlockSpec(memory_space=pl.ANY)],
            out_specs=[pl.BlockSpec((1, KV_HEADS, 32, HEAD_DIM), blk4),
                       pl.BlockSpec((1, 256, WINDOW), lambda b, pt, ix: (b, 0, 0))],
            scratch_shapes=[pltpu.VMEM((2, KV_HEADS, n_tok, 2, HEAD_DIM, span), F32),
                            pltpu.SemaphoreType.DMA((2,))]),
        compiler_params=pltpu.CompilerParams(dimension_semantics=("arbitrary",), vmem_limit_bytes=VMEM_LIMIT),
        name="s_attn",
    )(page_table, idx_flat, q_r, kvT_new, winT, gz_r, ocmp, cache_pages)


def _post_kernel(*refs, tm, seq_len, has_prev):
    it = iter(refs)
    x_ref, mod_ref, ya_ref, yb_ref = next(it), next(it), next(it), next(it)
    g1_ref, g2_ref, gf_ref = next(it), next(it), next(it)
    wm_ref, wba_ref, wbb_ref, wo_ref, wup_ref, wdn_ref, cw_ref = (next(it) for _ in range(7))
    p1_ref = next(it) if has_prev else None
    p2_ref = next(it) if has_prev else None
    y_ref, up_ref = next(it), next(it)
    carry, h2_s, act_s = next(it), next(it), next(it)

    i = pl.program_id(1)
    x = x_ref[0]
    shift1, scale1, gate1 = mod_ref[0, 0], mod_ref[0, 1], mod_ref[0, 2]
    shift2, scale2, gate2 = mod_ref[0, 3], mod_ref[0, 4], mod_ref[0, 5]

    hb = _rms_mod(x, g1_ref[...], scale1, shift1).astype(BF16)
    gates = jax.nn.sigmoid(_dot(hb, wm_ref[...]))
    mix = (gates[:, :D_MODEL] * _dot(ya_ref[0], wba_ref[...])
           + gates[:, D_MODEL:] * _dot(yb_ref[0], wbb_ref[...]))
    x1 = x + gate1 * _dot(mix.astype(BF16), wo_ref[...])
    h2_s[...] = _rms_mod(x1, g2_ref[...], scale2, shift2).astype(BF16)

    if not has_prev:
        @pl.when(i == 0)
        def _():
            carry[...] = jnp.zeros(carry.shape, F32)

    row = lax.broadcasted_iota(jnp.int32, (tm, 1), 0)
    row8 = lax.broadcasted_iota(jnp.int32, (8, 1), 0)

    def conv_cols(col0):
        cols = slice(col0, col0 + FF_CHUNK)
        up = _dot(h2_s[...], wup_ref[:, cols])
        r1 = pltpu.roll(up, 1, 0)
        r2 = pltpu.roll(up, 2, 0)
        if has_prev:
            s1 = jnp.where((row & (seq_len - 1)) == 0, p1_ref[:, cols], r1)
            s2 = jnp.where((row & (seq_len - 1)) < 2, p2_ref[:, cols], r2)
            up_ref[:, cols] = up
        else:
            prev = carry[:, cols]
            t1 = jnp.where(row8 == 0, pltpu.roll(prev, 1, 0), r1[:8])
            t2 = jnp.where(row8 < 2, pltpu.roll(prev, 2, 0), r2[:8])
            s1 = jnp.concatenate([t1, r1[8:]], axis=0)
            s2 = jnp.concatenate([t2, r2[8:]], axis=0)
            carry[:, cols] = up[tm - 8:]
            up_ref[0, :, cols] = up[tm - 8:]
        cw = cw_ref[:, cols]
        return cw[3:4] + cw[0:1] * s2 + cw[1:2] * s1 + cw[2:3] * up

    for c in range(N_FF_CHUNKS):
        a = conv_cols(c * FF_CHUNK)
        gv = conv_cols(D_FF + c * FF_CHUNK)
        act_s[:, c * FF_CHUNK:(c + 1) * FF_CHUNK] = (jax.nn.gelu(a) * gv).astype(BF16)
    x2 = x1 + gate2 * _dot(act_s[...], wdn_ref[...])
    y_ref[0] = x2 * lax.rsqrt(jnp.mean(x2 * x2, axis=-1, keepdims=True) + EPS) * gf_ref[...]


def _post_call(x, mod, ya, yb, g1, g2, gf, wm, wba, wbb, wo, wup, wdn, cw, prev=None, *, tm, seq_len):
    G, T, _ = x.shape
    R = mod.shape[2]
    nt = T // tm
    has_prev = prev is not None
    single = pl.Buffered(1)
    c2 = lambda b, i: (0, 0)
    c3 = lambda b, i: (0, 0, 0)
    in_specs = [pl.BlockSpec((1, tm, D_MODEL), lambda b, i: (b, i, 0)),
                pl.BlockSpec((1, 6, R, D_MODEL), lambda b, i: (b, 0, 0, 0)),
                pl.BlockSpec((1, tm, A_WIDTH), lambda b, i: (b, i, 0)),
                pl.BlockSpec((1, tm, B_WIDTH), lambda b, i: (b, i, 0)),
                pl.BlockSpec((1, D_MODEL), c2),
                pl.BlockSpec((1, D_MODEL), c2),
                pl.BlockSpec((1, D_MODEL), c2),
                pl.BlockSpec(wm.shape, c2, pipeline_mode=single),
                pl.BlockSpec(wba.shape, c2, pipeline_mode=single),
                pl.BlockSpec(wbb.shape, c2, pipeline_mode=single),
                pl.BlockSpec(wo.shape, c2, pipeline_mode=single),
                pl.BlockSpec(wup.shape, c2, pipeline_mode=single),
                pl.BlockSpec(wdn.shape, c2, pipeline_mode=single),
                pl.BlockSpec(cw.shape, c2, pipeline_mode=single)]
    args = [x, mod, ya, yb, g1, g2, gf, wm, wba, wbb, wo, wup, wdn, cw]
    if has_prev:
        in_specs += [pl.BlockSpec(prev[0].shape, c2), pl.BlockSpec(prev[1].shape, c2)]
        args += list(prev)
        up_shape = jax.ShapeDtypeStruct((T, F2), F32)
        up_spec = pl.BlockSpec((tm, F2), lambda b, i: (i, 0))
    else:
        up_shape = jax.ShapeDtypeStruct((G, 8, F2), F32)
        up_spec = pl.BlockSpec((1, 8, F2), lambda b, i: (b, 0, 0))
    return pl.pallas_call(
        functools.partial(_post_kernel, tm=tm, seq_len=seq_len, has_prev=has_prev),
        out_shape=[jax.ShapeDtypeStruct((G, T, D_MODEL), F32), up_shape],
        grid=(G, nt),
        in_specs=in_specs,
        out_specs=[pl.BlockSpec((1, tm, D_MODEL), lambda b, i: (b, i, 0)), up_spec],
        scratch_shapes=[pltpu.VMEM((8, F2), F32),
                        pltpu.VMEM((tm, D_MODEL), BF16),
                        pltpu.VMEM((tm, D_FF), BF16)],
        compiler_params=pltpu.CompilerParams(dimension_semantics=("arbitrary", "arbitrary"),
                                             vmem_limit_bytes=VMEM_LIMIT),
        name="post",
    )(*args)


def _block_diag2(m):
    z = jnp.zeros_like(m)
    return jnp.concatenate([jnp.concatenate([m, z], axis=-1), jnp.concatenate([z, m], axis=-1)], axis=-2)


def kernel(x_prompt, x_sample, cache_kv, state_kv_win, state_ffn_conv, page_table, c_prompt, c_sample, w_ada, b_ada, g_norm1, w_in, ln_v_g, ln_v_b, w_spatial, b_spatial, cmp_pe, cmp_w1, cmp_b1, cmp_w2, cmp_b2, w_branch_a, w_branch_b, w_out, g_norm2, w_up, w_conv, b_conv, w_down, g_final):
    B, T, _ = x_prompt.shape
    SB, ST, _ = x_sample.shape
    n_pool, page_size = cache_kv.shape[1], cache_kv.shape[2]
    n_pages = page_table.shape[1]
    past_len = n_pages * page_size
    lbuf = state_kv_win.shape[2]
    assert cache_kv.shape[0] == 1 and page_size == 128 and lbuf == WINDOW and SB * ST == 128 and ST <= 8
    assert past_len % BLOCK == 0 and n_pages == 128 and T % 512 == 0 and ST & (ST - 1) == 0 and ST >= 2

    win = w_in[0]
    wa = win[:, :2 * A_WIDTH].astype(BF16)
    o = 2 * A_WIDTH
    wq = win[:, o:o + B_WIDTH].astype(BF16)
    o += B_WIDTH
    wkv = win[:, o:o + KV_COLS]
    wkvT = wkv.T.astype(BF16)
    wkvc = wkv[:, :256].astype(BF16)
    o += KV_COLS
    wg = jnp.pad(win[:, o:o + 3 * B_HEADS], ((0, 0), (0, 128 - 3 * B_HEADS))).astype(BF16)
    wgT = jnp.pad(win[:, o:o + 3 * B_HEADS].T, ((0, 32 - 3 * B_HEADS), (0, 0))).astype(BF16)
    wqT = wq.T
    wrow = jnp.concatenate([wkv[:, :3 * 128], wkv[:, 4 * 128:5 * 128]], axis=1).astype(BF16)
    o += 3 * B_HEADS
    wm = win[:, o:].astype(BF16)
    g1 = g_norm1[0][None]
    g2 = g_norm2[0][None]
    gf = g_final[None]
    lng = ln_v_g[0][None]
    lnb = ln_v_b[0][None]

    ws = w_spatial[0]
    bs = b_spatial[0]
    wsp_p = jnp.tril(ws).astype(BF16)
    bsp_p = jnp.repeat(bs.T, 128, axis=1)
    w4 = jnp.tril(ws[:, :ST, :ST])
    seq_of = jnp.arange(128, dtype=jnp.int32) // ST
    same_seq = seq_of[:, None] == seq_of[None, :]
    wsp_s = jnp.where(same_seq, jnp.tile(w4, (1, 128 // ST, 128 // ST)), 0.0).astype(BF16)
    bsp_s = jnp.repeat(jnp.tile(bs[:, :ST].T, (128 // ST, 1)), 128, axis=1)

    w1, pe = cmp_w1[0].astype(BF16), cmp_pe[0]
    w1_rows = _block_diag2(w1)
    pe_rows = jnp.tile(pe, (1, 1, 2))
    w1_pages = _block_diag2(w1.transpose(0, 2, 1, 3)).reshape(2, HEAD_DIM // 2, 256, 128)
    pe_pages = jnp.tile(pe.transpose(0, 2, 1), (1, 1, 2)).reshape(2, HEAD_DIM // 2, 1, 256)
    pe_pages = jnp.pad(pe_pages, ((0, 0), (0, 0), (0, PE_ROWS - 1), (0, 0))).astype(BF16)
    b1t = jnp.tile(cmp_b1[0], (1, 2))[:, None, :]
    b2t = jnp.tile(cmp_b2[0], (1, 2))[:, None, :]
    w2bd = _block_diag2(cmp_w2[0].astype(BF16))

    wba = w_branch_a[0].astype(BF16)
    wbb = w_branch_b[0].astype(BF16)
    wo = w_out[0].astype(BF16)
    wup = w_up[0].astype(BF16)
    wdn = w_down[0].astype(BF16)
    cw = jnp.concatenate([w_conv[0], b_conv[0][None], jnp.zeros((4, F2), F32)], axis=0)

    mod = _mod_call(jnp.concatenate([c_prompt, c_sample], axis=0), w_ada[0], b_ada)
    mod_p = mod[:B].reshape(B, 6, 1, D_MODEL)
    mod_s = jnp.repeat(mod[B:], ST, axis=0).reshape(SB * ST, 6, D_MODEL).transpose(1, 0, 2)[None]

    ya_p, qT_p, kv4T_p, winT_p, vselT_p, vwinT_p, kvc_p, ksel_p, kwin_p, gzT_p = _inproj_call(
        x_prompt, mod_p, g1, wa, lng, lnb, wsp_p, bsp_p, (wqT, wkvT, wrow, wgT), tm=512, prompt=True)
    xs2 = x_sample.reshape(1, SB * ST, D_MODEL)
    ya_s, q_s, kvT_s, kvc_s, gz_s, vn_s = _inproj_call(
        xs2, mod_s, g1, wa, lng, lnb, wsp_s, bsp_s, (wq, wkvT, wkvc, wg), tm=128, prompt=False)

    nb = T // BLOCK
    cmp_p = _cmp_rows_call(kvc_p.reshape(B * T, 256), pe_rows, w1_rows, b1t, w2bd, b2t, R=128)
    kc_p = cmp_p[0].reshape(B, nb, 128)
    vcT_p = cmp_p[1].reshape(B, nb, 128).transpose(0, 2, 1)
    yb_p = _p_attn_call(qT_p, ksel_p, kwin_p, vselT_p, vwinT_p, kc_p, vcT_p, gzT_p, tq=512, kc=512)

    cache_pages = jnp.transpose(cache_kv[0], (0, 2, 3, 4, 1)).reshape(n_pool, 512, page_size)
    pool_seq = _cmp_gather_call(page_table.reshape(-1), cache_pages, pe_pages, w1_pages, b1t, w2bd, b2t,
                                Pt=64)
    kv_rows_s = jnp.transpose(kvT_s[0], (1, 0)).reshape(SB, ST, KV_COLS)
    newblk = jnp.pad(jnp.transpose(kv_rows_s[:, :, :256], (0, 2, 1)), ((0, 0), (0, 0), (0, page_size - ST)))
    newc = _cmp_pages_call(newblk, pe_pages, w1_pages, b1t, w2bd, b2t, Pt=SB)

    q_r = jnp.pad(q_s.reshape(SB, ST, KV_HEADS, GQA, HEAD_DIM).transpose(0, 2, 3, 1, 4),
                  ((0, 0), (0, 0), (0, 0), (0, 8 - ST), (0, 0))).reshape(SB, KV_HEADS, 8 * GQA, HEAD_DIM)
    ocmp_s, imp_s = _s_cmp_call(q_r, pool_seq, newc.reshape(SB, 1, 512), n_pages=n_pages, past_len=past_len)
    idx = _s_select_call(imp_s.reshape(SB * KV_HEADS * 8, 2 * n_pages), n_pages=n_pages, n_pick=N_SELECT - 1)
    idx_flat = idx.reshape(SB, KV_HEADS, 8, 128)[:, :, :ST, :N_SELECT].reshape(-1)
    winT = jnp.transpose(state_kv_win[0], (0, 2, 3, 4, 1)).reshape(SB, 256, lbuf)
    gz_r = jnp.pad(gz_s[0, :, :24].reshape(SB, ST, 3, KV_HEADS, GQA).transpose(0, 3, 4, 1, 2),
                   ((0, 0), (0, 0), (0, 0), (0, 8 - ST), (0, 128 - 3))).reshape(SB, KV_HEADS, 8 * GQA, 128)
    cache_tiles = cache_pages.reshape(n_pool, 4, KV_HEADS, HEAD_DIM, page_size)
    yb_r, wnextT = _s_attn_call(page_table, idx_flat, q_r, kvT_s[0], winT, gz_r, ocmp_s, cache_tiles,
                        n_tok=ST, past_len=past_len)
    yb_s = yb_r.reshape(SB, KV_HEADS, GQA, 8, HEAD_DIM)[:, :, :, :ST].transpose(0, 3, 1, 2, 4)
    yb_s = yb_s.reshape(1, SB * ST, B_WIDTH).astype(BF16)

    y_p, up_p = _post_call(x_prompt, mod_p, ya_p, yb_p, g1, g2, gf, wm, wba, wbb, wo, wup, wdn, cw,
                           tm=512, seq_len=T)
    st = state_ffn_conv[0]
    zrow = jnp.zeros((SB, 1, F2), F32)
    p1 = jnp.concatenate([st[:, 1:2], jnp.tile(zrow, (1, ST - 1, 1))], axis=1).reshape(SB * ST, F2)
    p2 = jnp.concatenate([st, jnp.tile(zrow, (1, ST - 2, 1))], axis=1).reshape(SB * ST, F2)
    y_s, up_s = _post_call(xs2, mod_s, ya_s, yb_s, g1, g2, gf, wm, wba, wbb, wo, wup, wdn, cw,
                           prev=(p1, p2),
                           tm=SB * ST, seq_len=ST)

    kv_prompt = jnp.transpose(kv4T_p.reshape(B, 4, KV_HEADS, HEAD_DIM, T), (0, 4, 1, 2, 3))[None]
    win_prompt = jnp.transpose(winT_p[:, :, T - WINDOW:].reshape(B, 2, KV_HEADS, HEAD_DIM, WINDOW),
                               (0, 4, 1, 2, 3))[None]
    kv_sample = kv_rows_s[:, :, :512].reshape(SB, ST, 4, KV_HEADS, HEAD_DIM)[None]
    win_sample = jnp.transpose(wnextT.reshape(SB, 2, KV_HEADS, HEAD_DIM, lbuf), (0, 4, 1, 2, 3))[None]
    v_chunk = vn_s.reshape(SB, ST, A_WIDTH)[None]
    conv_prompt = up_p[:, 6:8][None]
    conv_sample = up_s.reshape(SB, ST, F2)[:, ST - 2:][None]
    return (y_p, y_s.reshape(SB, ST, D_MODEL), kv_prompt, kv_sample, win_prompt, win_sample, v_chunk,
            conv_prompt, conv_sample)
```

```python
import functools

import jax
import jax.numpy as jnp
from jax import lax
from jax.experimental import pallas as pl
from jax.experimental.pallas import tpu as pltpu

F32 = jnp.float32
BF16 = jnp.bfloat16

D_MODEL = 1024
A_WIDTH = 512
A_GROUPS = 4
CHUNK = 128
B_HEADS = 8
HEAD_DIM = 64
B_WIDTH = 512
KV_HEADS = 2
GQA = 4
BLOCK = 64
N_SELECT = 16
WINDOW = 512
N_KV_SLOTS = 6
KV_COLS = 768
D_FF = 2816
F2 = 2 * D_FF
CONV_W = 3
EPS = 1e-6
NEG = -1e30
FORCE_SCORE = 1e4

FF_CHUNK = 256
N_FF_CHUNKS = D_FF // FF_CHUNK
ROW_BLOCK = 64
PE_ROWS = 16
SKEW = 2
ACC_ROWS = HEAD_DIM + 16
LOG2E = 1.4426950408889634
VMEM_LIMIT = 56 * 1024 * 1024


def _dot(a, b):
    return jnp.dot(a, b, preferred_element_type=F32)


def _dot_nt(a, b):
    return lax.dot_general(a, b, (((1,), (1,)), ((), ())), preferred_element_type=F32)


def _rms_mod(x, g, scale, shift):
    y = x * lax.rsqrt(jnp.mean(x * x, axis=-1, keepdims=True) + EPS)
    return (y * g) * (1.0 + scale) + shift


def _mod_kernel(c_ref, w_ref, b_ref, o_ref):
    c = c_ref[...]
    s = c * jax.nn.sigmoid(c)
    o_ref[...] = _dot(s.astype(BF16), w_ref[...].astype(BF16)) + b_ref[...]


def _mod_call(c_all, w_ada, b_ada):
    n = c_all.shape[0]
    tn = 1536
    return pl.pallas_call(
        _mod_kernel,
        out_shape=jax.ShapeDtypeStruct((n, 6 * D_MODEL), F32),
        grid=(6 * D_MODEL // tn,),
        in_specs=[pl.BlockSpec((n, D_MODEL), lambda j: (0, 0)),
                  pl.BlockSpec((D_MODEL, tn), lambda j: (0, j)),
                  pl.BlockSpec((1, tn), lambda j: (0, j))],
        out_specs=pl.BlockSpec((n, tn), lambda j: (0, j)),
        compiler_params=pltpu.CompilerParams(dimension_semantics=("arbitrary",), vmem_limit_bytes=VMEM_LIMIT),
        name="mod",
    )(c_all, w_ada, b_ada)


def _inproj_kernel(x_ref, mod_ref, g1_ref, wa_ref, lng_ref, lnb_ref, wsp_ref, bsp_ref, *refs, tm, prompt):
    x = x_ref[0]
    hb = _rms_mod(x, g1_ref[...], mod_ref[0, 1], mod_ref[0, 0]).astype(BF16)

    ga = jax.nn.gelu(_dot(hb, wa_ref[...]))
    u = ga[:, :A_WIDTH]
    v = ga[:, A_WIDTH:]
    vc = v - jnp.mean(v, axis=-1, keepdims=True)
    vn = vc * lax.rsqrt(jnp.mean(vc * vc, axis=-1, keepdims=True) + EPS) * lng_ref[...] + lnb_ref[...]
    vnb = vn.astype(BF16)

    if prompt:
        (wqT_ref, wkvT_ref, wrow_ref, wgT_ref,
         ya_ref, qT_ref, kv4T_ref, winT_ref, vselT_ref, vwinT_ref, kvc_ref, ksel_ref, kwin_ref, gzT_ref) = refs
    else:
        (wq_ref, wkvT_ref, wkvc_ref, wg_ref, ya_ref, q_ref, kvT_ref, kvc_ref, gz_ref, vn_ref) = refs
        vn_ref[0] = vn

    for c in range(tm // CHUNK):
        rows = slice(c * CHUNK, (c + 1) * CHUNK)
        s = jnp.concatenate(
            [_dot(wsp_ref[g], vnb[rows, g * 128:(g + 1) * 128]) for g in range(A_GROUPS)], axis=1)
        ya_ref[0, rows, :] = (u[rows] * (s + bsp_ref[...])).astype(BF16)

    kvT = _dot_nt(wkvT_ref[...], hb)
    if prompt:
        qT_ref[0] = (_dot_nt(wqT_ref[...], hb) * (HEAD_DIM ** -0.5 * LOG2E)).astype(BF16)
        kv4T_ref[0] = kvT[:4 * 128]
        winT_ref[0] = kvT[4 * 128:]
        vselT_ref[0] = kvT[3 * 128:4 * 128].astype(BF16)
        vwinT_ref[0] = kvT[5 * 128:].astype(BF16)
        row = _dot(hb, wrow_ref[...])
        kvc_ref[0] = row[:, :256]
        ksel_ref[0] = row[:, 256:384].astype(BF16)
        kwin_ref[0] = row[:, 384:].astype(BF16)
        gzT_ref[0] = jax.nn.sigmoid(_dot_nt(wgT_ref[...], hb))
    else:
        q_ref[0] = (_dot(hb, wq_ref[...]) * (HEAD_DIM ** -0.5)).astype(BF16)
        kvT_ref[0] = kvT
        kvc_ref[0] = _dot(hb, wkvc_ref[...])
        gz_ref[0] = jax.nn.sigmoid(_dot(hb, wg_ref[...]))


def _inproj_call(x, mod, g1, wa, lng, lnb, wsp, bsp, proj_w, *, tm, prompt):
    G, T, _ = x.shape
    R = mod.shape[2]
    nt = T // tm
    const2 = lambda b, i: (0, 0)
    const3 = lambda b, i: (0, 0, 0)
    rows = lambda w: pl.BlockSpec((1, tm, w), lambda b, i: (b, i, 0))
    cols = lambda h: pl.BlockSpec((1, h, tm), lambda b, i: (b, 0, i))
    if prompt:
        outs = [((G, T, A_WIDTH), BF16, rows(A_WIDTH)),
                ((G, B_WIDTH, T), BF16, cols(B_WIDTH)),
                ((G, 512, T), F32, cols(512)),
                ((G, 256, T), F32, cols(256)),
                ((G, 128, T), BF16, cols(128)),
                ((G, 128, T), BF16, cols(128)),
                ((G, T, 256), F32, rows(256)),
                ((G, T, 128), BF16, rows(128)),
                ((G, T, 128), BF16, rows(128)),
                ((G, 32, T), F32, cols(32))]
    else:
        outs = [((G, T, A_WIDTH), BF16, rows(A_WIDTH)),
                ((G, T, B_WIDTH), BF16, rows(B_WIDTH)),
                ((G, KV_COLS, T), F32, cols(KV_COLS)),
                ((G, T, 256), F32, rows(256)),
                ((G, T, 128), F32, rows(128)),
                ((G, T, A_WIDTH), F32, rows(A_WIDTH))]
    return pl.pallas_call(
        functools.partial(_inproj_kernel, tm=tm, prompt=prompt),
        out_shape=[jax.ShapeDtypeStruct(s, d) for s, d, _ in outs],
        grid=(G, nt),
        in_specs=[pl.BlockSpec((1, tm, D_MODEL), lambda b, i: (b, i, 0)),
                  pl.BlockSpec((1, 6, R, D_MODEL), lambda b, i: (b, 0, 0, 0)),
                  pl.BlockSpec((1, D_MODEL), const2),
                  pl.BlockSpec(wa.shape, const2),
                  pl.BlockSpec((1, A_WIDTH), const2),
                  pl.BlockSpec((1, A_WIDTH), const2),
                  pl.BlockSpec(wsp.shape, const3),
                  pl.BlockSpec(bsp.shape, const2)] + [pl.BlockSpec(w.shape, const2) for w in proj_w],
        out_specs=[sp for _, _, sp in outs],
        compiler_params=pltpu.CompilerParams(dimension_semantics=("arbitrary", "arbitrary"),
                                             vmem_limit_bytes=VMEM_LIMIT),
        name="inproj",
    )(x, mod, g1, wa, lng, lnb, wsp, bsp, *proj_w)


def _cmp_tail(acc, b1, w2, b2):
    return _dot(jax.nn.gelu(acc + b1).astype(BF16), w2) + b2


def _cmp_rows_kernel(x_ref, pe_ref, w1_ref, b1_ref, w2_ref, b2_ref, o_ref, *, R):
    acc = jnp.zeros((R, 128), F32)
    for j in range(BLOCK):
        xj = x_ref[pl.ds(j, R, stride=BLOCK), :] + pe_ref[0, j:j + 1, :]
        acc = acc + _dot(xj.astype(BF16), w1_ref[0, j])
    o_ref[0] = _cmp_tail(acc, b1_ref[0], w2_ref[0], b2_ref[0])


def _cmp_rows_call(kvc2d, pe, w1, b1, w2, b2, *, R):
    nrows = kvc2d.shape[0] // BLOCK
    return pl.pallas_call(
        functools.partial(_cmp_rows_kernel, R=R),
        out_shape=jax.ShapeDtypeStruct((2, nrows, 128), F32),
        grid=(2, nrows // R),
        in_specs=[pl.BlockSpec((R * BLOCK, 128), lambda s, r: (r, s)),
                  pl.BlockSpec((1, BLOCK, 128), lambda s, r: (s, 0, 0)),
                  pl.BlockSpec((1, BLOCK, 128, 128), lambda s, r: (s, 0, 0, 0)),
                  pl.BlockSpec((1, 1, 128), lambda s, r: (s, 0, 0)),
                  pl.BlockSpec((1, 128, 128), lambda s, r: (s, 0, 0)),
                  pl.BlockSpec((1, 1, 128), lambda s, r: (s, 0, 0))],
        out_specs=pl.BlockSpec((1, R, 128), lambda s, r: (s, r, 0)),
        compiler_params=pltpu.CompilerParams(dimension_semantics=("arbitrary", "arbitrary"),
                                             vmem_limit_bytes=VMEM_LIMIT),
        name="cmp_rows",
    )(kvc2d, pe, w1, b1, w2, b2)


def _compress_pages(load_rows, pe_ref, w1_ref, b1_ref, w2_ref, b2_ref, o_ref, Pt):
    for s in range(2):
        acc = jnp.zeros((2 * Pt + PE_ROWS, 128), F32)
        for dg in range(HEAD_DIM // 8):
            r0 = s * 128 + dg * 8
            xa = jnp.swapaxes(load_rows(r0), 0, 1).astype(BF16)
            xb = jnp.swapaxes(load_rows(HEAD_DIM + r0), 0, 1).astype(BF16)
            for dd in range(0, 8, 2):
                dp = dg * 4 + dd // 2
                xd = jnp.concatenate([jnp.concatenate([xa[dd], xa[dd + 1]], axis=1),
                                      jnp.concatenate([xb[dd], xb[dd + 1]], axis=1),
                                      pe_ref[s, dp]], axis=0)
                acc = acc + _dot(xd, w1_ref[s, dp])
        pre = acc[:2 * Pt] + acc[2 * Pt:2 * Pt + 1]
        r = _cmp_tail(pre, b1_ref[s], w2_ref[s], b2_ref[s])
        o_ref[:, (2 * s) * 128:(2 * s + 1) * 128] = r[:Pt]
        o_ref[:, (2 * s + 1) * 128:(2 * s + 2) * 128] = r[Pt:]


def _cmp_pages_kernel(x_ref, pe_ref, w1_ref, b1_ref, w2_ref, b2_ref, o_ref, *, Pt):
    _compress_pages(lambda r0: x_ref[:, r0:r0 + 8, :], pe_ref, w1_ref, b1_ref, w2_ref, b2_ref, o_ref, Pt)


def _cmp_weight_specs(ws, index):
    return [pl.BlockSpec(w.shape, functools.partial(index, (0,) * w.ndim)) for w in ws]


def _cmp_pages_call(pages, pe, w1, b1, w2, b2, *, Pt):
    P = pages.shape[0]
    ws = (pe, w1, b1, w2, b2)
    return pl.pallas_call(
        functools.partial(_cmp_pages_kernel, Pt=Pt),
        out_shape=jax.ShapeDtypeStruct((P, 512), F32),
        grid=(P // Pt,),
        in_specs=[pl.BlockSpec((Pt, 256, 128), lambda i: (i, 0, 0))] + _cmp_weight_specs(ws, lambda z, i: z),
        out_specs=pl.BlockSpec((Pt, 512), lambda i: (i, 0)),
        compiler_params=pltpu.CompilerParams(dimension_semantics=("arbitrary",), vmem_limit_bytes=VMEM_LIMIT),
        name="cmp_pages",
    )(pages, *ws)


def _cmp_gather_kernel(pt_ref, cache_ref, pe_ref, w1_ref, b1_ref, w2_ref, b2_ref, o_ref, xbuf, sem, *, Pt):
    i = pl.program_id(0)
    slot = i & 1

    def page_copy(step, sl, k):
        return pltpu.make_async_copy(cache_ref.at[pt_ref[step * Pt + k], pl.ds(0, 256), :],
                                     xbuf.at[sl, k], sem.at[sl])

    @pl.when(i == 0)
    def _():
        for k in range(Pt):
            page_copy(0, 0, k).start()

    @pl.when(i + 1 < pl.num_programs(0))
    def _():
        for k in range(Pt):
            page_copy(i + 1, 1 - slot, k).start()

    for k in range(Pt):
        page_copy(i, slot, k).wait()
    _compress_pages(lambda r0: xbuf[slot, :, r0:r0 + 8, :], pe_ref, w1_ref, b1_ref, w2_ref, b2_ref, o_ref, Pt)


def _cmp_gather_call(pt_flat, cache_pages, pe, w1, b1, w2, b2, *, Pt):
    n = pt_flat.shape[0]
    ws = (pe, w1, b1, w2, b2)
    return pl.pallas_call(
        functools.partial(_cmp_gather_kernel, Pt=Pt),
        out_shape=jax.ShapeDtypeStruct((n, 512), F32),
        grid_spec=pltpu.PrefetchScalarGridSpec(
            num_scalar_prefetch=1,
            grid=(n // Pt,),
            in_specs=[pl.BlockSpec(memory_space=pl.ANY)] + _cmp_weight_specs(ws, lambda z, i, pt: z),
            out_specs=pl.BlockSpec((Pt, 512), lambda i, pt: (i, 0)),
            scratch_shapes=[pltpu.VMEM((2, Pt, 256, 128), F32),
                            pltpu.SemaphoreType.DMA((2,))]),
        compiler_params=pltpu.CompilerParams(dimension_semantics=("arbitrary",), vmem_limit_bytes=VMEM_LIMIT),
        name="cmp_gather",
    )(pt_flat, cache_pages, *ws)


def _pair_schedule(T, tq, kc):
    js, cs = [], []
    for j in range(T // tq):
        for c in range(((j + 1) * tq - 1) // kc + 1):
            js.append(j)
            cs.append(c)
    return js, cs


def _p_attn_kernel(jt_ref, ct_ref, qT_ref, ksel_ref, vselT_ref, kwin_ref, kwinp_ref, vwinT_ref, vwinTp_ref,
                   kc_ref, vcT_ref, gzT_ref, yb_ref,
                   sel_s, neg_s, ocmp_s, m_s, acc_s, s_scr, s2_scr, p_scr, p2_scr, b_scr, k2_s, *, tq, kc, T, strip):
    p = pl.program_id(1)
    j = jt_ref[p]
    c = ct_ref[p]
    nb = T // BLOCK
    N = GQA * tq
    q0 = j * tq
    k0 = c * kc
    c_last = ((j + 1) * tq - 1) // kc

    def q_pad(h):
        qT = jnp.concatenate([qT_ref[0, (h * GQA + g) * HEAD_DIM:(h * GQA + g + 1) * HEAD_DIM, :]
                              for g in range(GQA)], axis=1)
        z = jnp.zeros_like(qT)
        return jnp.concatenate([qT if k == h else z for k in range(KV_HEADS)], axis=0)

    @pl.when(c == 0)
    def _():
        m_s[...] = jnp.full(m_s.shape, NEG, F32)
        acc_s[...] = jnp.zeros(acc_s.shape, F32)
        tok_n = q0 + (lax.broadcasted_iota(jnp.int32, (1, N), 1) & (tq - 1))
        tok_1 = q0 + lax.broadcasted_iota(jnp.int32, (1, tq), 1)
        blk = lax.broadcasted_iota(jnp.int32, (nb, 1), 0)
        avail = (blk + 1) * BLOCK <= tok_n + 1
        cur = tok_1 >> 6
        forced = (blk == 0) | (blk == cur) | (blk == cur - 1)
        future = blk > cur
        kcb = kc_ref[0].astype(BF16)
        for h in range(KV_HEADS):
            s = jnp.where(avail, _dot(kcb, q_pad(h)), NEG)
            m = jnp.max(s, axis=0, keepdims=True)
            e = jnp.where(avail, jnp.exp2(s - m), 0.0)
            pr = e / jnp.maximum(jnp.sum(e, axis=0, keepdims=True), 1e-30)
            ocmp_s[h] = _dot(vcT_ref[0, h * HEAD_DIM:(h + 1) * HEAD_DIM, :].astype(BF16), pr.astype(BF16))
            imp = pr[:, 0:tq]
            for g in range(1, GQA):
                imp = imp + pr[:, g * tq:(g + 1) * tq]
            imp = jnp.where(forced, FORCE_SCORE, jnp.where(future, -1.0, imp))
            rows = []
            for i in range(nb):
                vi = imp[i:i + 1, :]
                ahead = (imp > vi) | ((imp == vi) & (blk < i))
                cnt = jnp.sum(ahead.astype(F32), axis=0, keepdims=True)
                rows.append((cnt < float(N_SELECT)).astype(F32))
            sel = jnp.concatenate(rows + [jnp.zeros((128 - nb, tq), F32)], axis=0)
            sel_s[h] = sel.astype(BF16)
            neg_s[h] = ((sel - 1.0) * -NEG).astype(BF16)

    kpos = k0 + lax.broadcasted_iota(jnp.int32, (kc, tq), 0)
    tok = q0 + lax.broadcasted_iota(jnp.int32, (kc, tq), 1)

    strips = [slice(st * strip, (st + 1) * strip) for st in range(N // strip)]

    row_blocks = [slice(r, r + ROW_BLOCK) for r in range(0, kc, ROW_BLOCK)]

    ones_rows = (lax.broadcasted_iota(jnp.int32, (ACC_ROWS - HEAD_DIM, kc), 0) == 0).astype(BF16)

    streams = [(h, st) for st in range(len(strips)) for h in range(KV_HEADS)]

    def q_strip(h, st):
        g, t0 = divmod(st * strip, tq)
        r0 = (h * GQA + g) * HEAD_DIM
        qT = qT_ref[0, r0:r0 + HEAD_DIM, t0:t0 + strip]
        z = jnp.zeros_like(qT)
        return jnp.concatenate([qT if k == h else z for k in range(KV_HEADS)], axis=0)

    def with_ones(vT_ref, h):
        return jnp.concatenate([vT_ref[0, h * HEAD_DIM:(h + 1) * HEAD_DIM, :], ones_rows], axis=0)

    def run_skewed(stages):
        for k in range(len(streams) + SKEW * (len(stages) - 1)):
            for lag, stage in enumerate(stages):
                if 0 <= k - SKEW * lag < len(streams):
                    stage(*streams[k - SKEW * lag])

    def selected_step(diagonal):
        vT1 = [with_ones(vselT_ref, h) for h in range(KV_HEADS)]
        alpha = {}
        if diagonal:
            keys = ksel_ref[0]
            for h in range(KV_HEADS):
                b_scr[h] = jnp.where((_dot(expand, sel_s[h]) > 0.5) & causal, 0.0, NEG)
        else:
            k2_s[:, :128] = ksel_ref[0]
            k2_s[:, 128:] = expand
            keys = k2_s

        def n_keys(st):
            return min(kc, (st * strip) % tq + strip) if diagonal else kc

        def scores(h, st):
            nk = n_keys(st)
            qp = q_strip(h, st)
            if not diagonal:
                t0 = (st * strip) % tq
                qp = jnp.concatenate([qp, neg_s[h, :, t0:t0 + strip]], axis=0)
            s_scr[h, :nk, strips[st]] = _dot(keys[:nk, :], qp)

        def probs(h, st):
            ls = strips[st]
            t0 = (st * strip) % tq
            rbs = row_blocks[:n_keys(st) // ROW_BLOCK]
            m = None
            for rb in rbs:
                t = s_scr[h, rb, ls]
                if diagonal:
                    t = t + b_scr[h, rb, t0:t0 + strip]
                    s_scr[h, rb, ls] = t
                m = t if m is None else jnp.maximum(m, t)
            m_old = m_s[h, :, ls]
            m_new = jnp.maximum(m_old, jnp.max(m, axis=0, keepdims=True))
            alpha[(h, st)] = jnp.exp2(m_old - m_new)
            m_s[h, :, ls] = m_new
            for rb in rbs:
                p_scr[h, rb, ls] = jnp.exp2(s_scr[h, rb, ls] - m_new).astype(BF16)

        def weighted_values(h, st):
            ls = strips[st]
            nk = n_keys(st)
            acc_s[h, :, ls] = alpha[(h, st)] * acc_s[h, :, ls] + _dot(vT1[h][:, :nk], p_scr[h, :nk, ls])

        run_skewed((scores, probs, weighted_values))

    def window_step():
        no_prev = jnp.where(j == 0, NEG, 0.0)
        vT1 = [with_ones(vwinT_ref, h) for h in range(KV_HEADS)]
        vT1p = [with_ones(vwinTp_ref, h) for h in range(KV_HEADS)]

        def scores(h, st):
            qp = q_strip(h, st)
            s_scr[h, :, strips[st]] = _dot(kwin_ref[0], qp)
            s2_scr[h, :, strips[st]] = _dot(kwinp_ref[0], qp)

        def probs(h, st):
            ls = strips[st]
            t0 = (st * strip) % tq
            m = None
            for rb in row_blocks:
                cur = b_scr[KV_HEADS, rb, t0:t0 + strip] > 0.5
                t = jnp.where(cur, s_scr[h, rb, ls], s2_scr[h, rb, ls] + no_prev)
                s_scr[h, rb, ls] = t
                m = t if m is None else jnp.maximum(m, t)
            m_new = jnp.max(m, axis=0, keepdims=True)
            for rb in row_blocks:
                pe = jnp.exp2(s_scr[h, rb, ls] - m_new)
                pc = pe * b_scr[KV_HEADS, rb, t0:t0 + strip]
                p_scr[h, rb, ls] = pc.astype(BF16)
                p2_scr[h, rb, ls] = (pe - pc).astype(BF16)

        def weighted_values(h, st):
            ls = strips[st]
            acc_s[KV_HEADS + h, :, ls] = _dot(vT1[h], p_scr[h, :, ls]) + _dot(vT1p[h], p2_scr[h, :, ls])

        run_skewed((scores, probs, weighted_values))

    blk_of_key = (k0 >> 6) + (lax.broadcasted_iota(jnp.int32, (kc, 128), 0) >> 6)
    expand = (lax.broadcasted_iota(jnp.int32, (kc, 128), 1) == blk_of_key).astype(BF16)
    causal = kpos <= tok

    @pl.when(c < c_last)
    def _():
        selected_step(diagonal=False)

    @pl.when(c == c_last)
    def _():
        selected_step(diagonal=True)
        b_scr[KV_HEADS] = causal.astype(F32)
        window_step()

    @pl.when(c == c_last)
    def _():
        gz = gzT_ref[0]
        outs = []
        for h in range(KV_HEADS):
            a_sel = acc_s[h]
            a_win = acc_s[KV_HEADS + h]
            o_sel = a_sel[:HEAD_DIM] / jnp.maximum(a_sel[HEAD_DIM:HEAD_DIM + 1], 1e-30)
            o_win = a_win[:HEAD_DIM] / jnp.maximum(a_win[HEAD_DIM:HEAD_DIM + 1], 1e-30)
            o_cmp = ocmp_s[h]
            for g in range(GQA):
                col = h * GQA + g
                ls = slice(g * tq, (g + 1) * tq)
                outs.append(gz[col:col + 1] * o_cmp[:, ls] + gz[B_HEADS + col:B_HEADS + col + 1] * o_sel[:, ls]
                            + gz[2 * B_HEADS + col:2 * B_HEADS + col + 1] * o_win[:, ls])
        yb_ref[0] = jnp.concatenate(outs, axis=0).T.astype(BF16)


def _p_attn_call(qT, ksel, kwin, vselT, vwinT, kc_all, vcT, gzT, *, tq, kc):
    B, _, T = qT.shape
    nb = T // BLOCK
    N = GQA * tq
    assert tq == kc == WINDOW
    js, cs = _pair_schedule(T, tq, kc)
    jt = jnp.asarray(js, jnp.int32)
    ct = jnp.asarray(cs, jnp.int32)
    prev = lambda jt, p: jnp.maximum(jt[p] - 1, 0)
    return pl.pallas_call(
        functools.partial(_p_attn_kernel, tq=tq, kc=kc, T=T, strip=256),
        out_shape=jax.ShapeDtypeStruct((B, T, B_WIDTH), BF16),
        grid_spec=pltpu.PrefetchScalarGridSpec(
            num_scalar_prefetch=2,
            grid=(B, len(js)),
            in_specs=[pl.BlockSpec((1, B_WIDTH, tq), lambda b, p, jt, ct: (b, 0, jt[p])),
                      pl.BlockSpec((1, kc, 128), lambda b, p, jt, ct: (b, ct[p], 0)),
                      pl.BlockSpec((1, 128, kc), lambda b, p, jt, ct: (b, 0, ct[p])),
                      pl.BlockSpec((1, kc, 128), lambda b, p, jt, ct: (b, jt[p], 0)),
                      pl.BlockSpec((1, kc, 128), lambda b, p, jt, ct: (b, prev(jt, p), 0)),
                      pl.BlockSpec((1, 128, kc), lambda b, p, jt, ct: (b, 0, jt[p])),
                      pl.BlockSpec((1, 128, kc), lambda b, p, jt, ct: (b, 0, prev(jt, p))),
                      pl.BlockSpec((1, nb, 128), lambda b, p, jt, ct: (b, 0, 0)),
                      pl.BlockSpec((1, 128, nb), lambda b, p, jt, ct: (b, 0, 0)),
                      pl.BlockSpec((1, 32, tq), lambda b, p, jt, ct: (b, 0, jt[p]))],
            out_specs=pl.BlockSpec((1, tq, B_WIDTH), lambda b, p, jt, ct: (b, jt[p], 0)),
            scratch_shapes=[pltpu.VMEM((KV_HEADS, 128, tq), BF16),
                            pltpu.VMEM((KV_HEADS, 128, tq), BF16),
                            pltpu.VMEM((KV_HEADS, HEAD_DIM, N), F32),
                            pltpu.VMEM((2 * KV_HEADS, 1, N), F32),
                            pltpu.VMEM((2 * KV_HEADS, ACC_ROWS, N), F32),
                            pltpu.VMEM((KV_HEADS, kc, N), F32),
                            pltpu.VMEM((KV_HEADS, kc, N), F32),
                            pltpu.VMEM((KV_HEADS, kc, N), BF16),
                            pltpu.VMEM((KV_HEADS, kc, N), BF16),
                            pltpu.VMEM((KV_HEADS + 1, kc, tq), F32),
                            pltpu.VMEM((kc, 256), BF16)]),
        compiler_params=pltpu.CompilerParams(dimension_semantics=("arbitrary", "arbitrary"),
                                             vmem_limit_bytes=VMEM_LIMIT),
        name="p_attn",
    )(jt, ct, qT, ksel, vselT, kwin, kwin, vwinT, vwinT, kc_all, vcT, gzT)


def _s_cmp_kernel(q_ref, gath, newc_ref, ocmp_ref, imp_ref, *, n_pages, past_len):
    rows = 8 * GQA
    t_row = lax.broadcasted_iota(jnp.int32, (rows, 1), 0) & 7
    pos = past_len + t_row
    lane = lax.broadcasted_iota(jnp.int32, (1, 2 * n_pages), 1)
    blk = 2 * (lane & (n_pages - 1)) + (lane >> 7)
    new_blk = past_len // BLOCK
    avail = (blk + 1) * BLOCK <= pos + 1
    avail_new = (new_blk + 1) * BLOCK <= pos + 1
    for h in range(KV_HEADS):
        qh = q_ref[0, h]
        kc = [gath[:, (h * 2 + k) * 64:(h * 2 + k + 1) * 64].astype(BF16) for k in range(2)]
        vc = [gath[:, (2 + h) * 128 + k * 64:(2 + h) * 128 + (k + 1) * 64].astype(BF16) for k in range(2)]
        kc_new = newc_ref[0, :, h * 128:h * 128 + 64].astype(BF16).astype(F32)
        vc_new = newc_ref[0, :, (2 + h) * 128:(2 + h) * 128 + 64].astype(BF16).astype(F32)
        s = jnp.where(avail, jnp.concatenate([_dot_nt(qh, kc[0]), _dot_nt(qh, kc[1])], axis=1), NEG)
        s_new = jnp.where(avail_new, jnp.sum(qh.astype(F32) * kc_new, axis=-1, keepdims=True), NEG)
        m = jnp.maximum(jnp.max(s, axis=-1, keepdims=True), s_new)
        e = jnp.where(avail, jnp.exp(s - m), 0.0)
        e_new = jnp.where(avail_new, jnp.exp(s_new - m), 0.0)
        den = jnp.maximum(jnp.sum(e, axis=-1, keepdims=True) + e_new, 1e-30)
        p = e / den
        p_new = e_new / den
        pb = p.astype(BF16)
        ocmp_ref[0, h] = (_dot(pb[:, :n_pages], vc[0]) + _dot(pb[:, n_pages:], vc[1])
                          + p_new.astype(BF16).astype(F32) * vc_new)
        imp = p[0:8] + p[8:16] + p[16:24] + p[24:32]
        cur = (past_len + lax.broadcasted_iota(jnp.int32, (8, 1), 0)) >> 6
        forced = (blk == 0) | (blk == cur) | (blk == cur - 1)
        imp_ref[0, h] = jnp.where(forced, FORCE_SCORE, jnp.where(blk > cur, -1.0, imp))


def _s_cmp_call(q_r, pool_seq, newc, *, n_pages, past_len):
    nbatch = q_r.shape[0]
    return pl.pallas_call(
        functools.partial(_s_cmp_kernel, n_pages=n_pages, past_len=past_len),
        out_shape=[jax.ShapeDtypeStruct((nbatch, KV_HEADS, 32, HEAD_DIM), F32),
                   jax.ShapeDtypeStruct((nbatch, KV_HEADS, 8, 2 * n_pages), F32)],
        grid=(nbatch,),
        in_specs=[pl.BlockSpec((1, KV_HEADS, 32, HEAD_DIM), lambda b: (b, 0, 0, 0)),
                  pl.BlockSpec((n_pages, 512), lambda b: (b, 0)),
                  pl.BlockSpec((1, 1, 512), lambda b: (b, 0, 0))],
        out_specs=[pl.BlockSpec((1, KV_HEADS, 32, HEAD_DIM), lambda b: (b, 0, 0, 0)),
                   pl.BlockSpec((1, KV_HEADS, 8, 2 * n_pages), lambda b: (b, 0, 0, 0))],
        compiler_params=pltpu.CompilerParams(dimension_semantics=("arbitrary",), vmem_limit_bytes=VMEM_LIMIT),
        name="s_cmp",
    )(q_r, pool_seq, newc)


def _s_select_kernel(imp_ref, idx_ref, *, n_pages, n_pick):
    v = imp_ref[...]
    rows = v.shape[0]
    lane = lax.broadcasted_iota(jnp.int32, (1, 2 * n_pages), 1)
    blk = (2 * (lane & (n_pages - 1)) + (lane >> 7)).astype(F32)
    out_lane = lax.broadcasted_iota(jnp.int32, (1, 128), 1)
    out = jnp.full((rows, 128), float(2 * n_pages), F32)
    for k in range(n_pick):
        m = jnp.max(v, axis=-1, keepdims=True)
        pick = jnp.min(jnp.where(v == m, blk, 1e9), axis=-1, keepdims=True)
        out = jnp.where(out_lane == k, pick, out)
        v = jnp.where(blk == pick, NEG, v)
    idx_ref[...] = out.astype(jnp.int32)


def _s_select_call(imp2d, *, n_pages, n_pick):
    rows = imp2d.shape[0]
    return pl.pallas_call(
        functools.partial(_s_select_kernel, n_pages=n_pages, n_pick=n_pick),
        out_shape=jax.ShapeDtypeStruct((rows, 128), jnp.int32),
        grid=(1,),
        in_specs=[pl.BlockSpec(imp2d.shape, lambda i: (0, 0))],
        out_specs=pl.BlockSpec((rows, 128), lambda i: (0, 0)),
        compiler_params=pltpu.CompilerParams(dimension_semantics=("arbitrary",), vmem_limit_bytes=VMEM_LIMIT),
        name="s_select",
    )(imp2d)


def _s_attn_kernel(pt_ref, idx_ref, q_ref, kvn_ref, win_ref, gz_ref, ocmp_ref, cache_ref, yb_ref, wnext_ref,
                   kvbuf, sem, *, n_tok, n_gather, past_len):
    b = pl.program_id(0)
    span = n_gather * 128
    slot = b & 1

    def tile_copies(bb, sl, t, h, jj):
        n = idx_ref[((bb * KV_HEADS + h) * n_tok + t) * N_SELECT + jj]
        page = pt_ref[bb, n >> 1]
        return [pltpu.make_async_copy(cache_ref.at[page, 2 + kv, h],
                                      kvbuf.at[sl, h, t, kv, :, pl.ds(jj * 128, 128)], sem.at[sl])
                for kv in range(2)]

    def for_all_tiles(bb, sl, fn):
        for h in range(KV_HEADS):
            for t in range(n_tok):
                for jj in range(n_gather):
                    for cp in tile_copies(bb, sl, t, h, jj):
                        fn(cp)

    @pl.when(b == 0)
    def _():
        for_all_tiles(0, 0, lambda cp: cp.start())

    @pl.when(b + 1 < pl.num_programs(0))
    def _():
        for_all_tiles(b + 1, 1 - slot, lambda cp: cp.start())

    rows = 8 * GQA
    t_row = lax.broadcasted_iota(jnp.int32, (rows, 1), 0) & 7
    gz = gz_ref[0]

    lane_n = lax.broadcasted_iota(jnp.int32, (1, 128), 1)
    tok_shift = n_tok.bit_length() - 1
    new_ok = ((lane_n >> tok_shift) == b) & ((lane_n & (n_tok - 1)) <= t_row)
    lane_w = lax.broadcasted_iota(jnp.int32, (1, WINDOW), 1)
    win_ok = lane_w > t_row

    shifted = pltpu.roll(win_ref[0], WINDOW - n_tok, 1)
    moved = pltpu.roll(kvn_ref[4 * 128:, :], (128 - n_tok) - b * n_tok, 1)
    wnext_ref[0, :, :WINDOW - 128] = shifted[:, :WINDOW - 128]
    wnext_ref[0, :, WINDOW - 128:] = jnp.where(lane_n >= 128 - n_tok, moved, shifted[:, WINDOW - 128:])

    def softmax_pv(parts):
        m = None
        for sc, mk, _ in parts:
            mm = jnp.max(jnp.where(mk, sc, NEG), axis=-1, keepdims=True)
            m = mm if m is None else jnp.maximum(m, mm)
        den = jnp.zeros((rows, 1), F32)
        o = jnp.zeros((rows, HEAD_DIM), F32)
        for sc, mk, vts in parts:
            e = jnp.where(mk, jnp.exp(jnp.where(mk, sc, NEG) - m), 0.0)
            den = den + jnp.sum(e, axis=-1, keepdims=True)
            for rmask, vt in vts:
                er = e if rmask is None else jnp.where(rmask, e, 0.0)
                o = o + _dot_nt(er.astype(BF16), vt)
        return o / jnp.maximum(den, 1e-30)

    win_out = []
    for h in range(KV_HEADS):
        qh = q_ref[0, h]
        kw = win_ref[0, h * 64:(h + 1) * 64, :].astype(BF16)
        vw = win_ref[0, 128 + h * 64:128 + (h + 1) * 64, :].astype(BF16)
        kwn = kvn_ref[4 * 128 + h * 64:4 * 128 + (h + 1) * 64, :].astype(BF16)
        vwn = kvn_ref[5 * 128 + h * 64:5 * 128 + (h + 1) * 64, :].astype(BF16)
        win_out.append(softmax_pv([(_dot(qh, kw), win_ok, [(None, vw)]),
                                   (_dot(qh, kwn), new_ok, [(None, vwn)])]))

    for_all_tiles(b, slot, lambda cp: cp.wait())

    lane_s = lax.broadcasted_iota(jnp.int32, (1, span), 1)
    for h in range(KV_HEADS):
        qh = q_ref[0, h]
        sc = jnp.zeros((rows, span), F32)
        half = jnp.zeros((rows, span), jnp.int32)
        for t in range(n_tok):
            st = _dot(qh, kvbuf[slot, h, t, 0].astype(BF16))
            hrow = jnp.zeros((1, span), jnp.int32)
            for jj in range(n_gather):
                n = idx_ref[((b * KV_HEADS + h) * n_tok + t) * N_SELECT + jj]
                hrow = jnp.where((lane_s >> 7) == jj, n & 1, hrow)
            sc = jnp.where(t_row == t, st, sc)
            half = jnp.where(t_row == t, hrow, half)
        ok = ((lane_s >> 6) & 1) == half
        ksn = kvn_ref[2 * 128 + h * 64:2 * 128 + (h + 1) * 64, :].astype(BF16)
        vsn = kvn_ref[3 * 128 + h * 64:3 * 128 + (h + 1) * 64, :].astype(BF16)
        vts = [(t_row == t, kvbuf[slot, h, t, 1].astype(BF16)) for t in range(n_tok)]
        o_sel = softmax_pv([(sc, ok, vts), (_dot(qh, ksn), new_ok, [(None, vsn)])])
        yb_ref[0, h] = (gz[h, :, 0:1] * ocmp_ref[0, h] + gz[h, :, 1:2] * o_sel + gz[h, :, 2:3] * win_out[h])


def _s_attn_call(page_table, idx_flat, q_r, kvT_new, winT, gz_r, ocmp, cache_pages, *, n_tok, past_len):
    nbatch = page_table.shape[0]
    n_gather = N_SELECT - 1
    span = n_gather * 128
    blk4 = lambda b, pt, ix: (b, 0, 0, 0)
    return pl.pallas_call(
        functools.partial(_s_attn_kernel, n_tok=n_tok, n_gather=n_gather, past_len=past_len),
        out_shape=[jax.ShapeDtypeStruct((nbatch, KV_HEADS, 32, HEAD_DIM), F32),
                   jax.ShapeDtypeStruct((nbatch, 256, WINDOW), F32)],
        grid_spec=pltpu.PrefetchScalarGridSpec(
            num_scalar_prefetch=2,
            grid=(nbatch,),
            in_specs=[pl.BlockSpec((1, KV_HEADS, 32, HEAD_DIM), blk4),
                      pl.BlockSpec(kvT_new.shape, lambda b, pt, ix: (0, 0)),
                      pl.BlockSpec((1, 256, WINDOW), lambda b, pt, ix: (b, 0, 0)),
                      pl.BlockSpec((1, KV_HEADS, 32, 128), blk4),
                      pl.BlockSpec((1, KV_HEADS, 32, HEAD_DIM), blk4),
                      pl.BlockSpec(memory_space=pl.ANY)],
            out_specs=[pl.BlockSpec((1, KV_HEADS, 32, HEAD_DIM), blk4),
                       pl.BlockSpec((1, 256, WINDOW), lambda b, pt, ix: (b, 0, 0))],
            scratch_shapes=[pltpu.VMEM((2, KV_HEADS, n_tok, 2, HEAD_DIM, span), F32),
                            pltpu.SemaphoreType.DMA((2,))]),
        compiler_params=pltpu.CompilerParams(dimension_semantics=("arbitrary",), vmem_limit_bytes=VMEM_LIMIT),
        name="s_attn",
    )(page_table, idx_flat, q_r, kvT_new, winT, gz_r, ocmp, cache_pages)


def _post_kernel(*refs, tm, seq_len, has_prev):
    it = iter(refs)
    x_ref, mod_ref, ya_ref, yb_ref = next(it), next(it), next(it), next(it)
    g1_ref, g2_ref, gf_ref = next(it), next(it), next(it)
    wm_ref, wba_ref, wbb_ref, wo_ref, wup_ref, wdn_ref, cw_ref = (next(it) for _ in range(7))
    p1_ref = next(it) if has_prev else None
    p2_ref = next(it) if has_prev else None
    y_ref, up_ref = next(it), next(it)
    carry, h2_s, act_s = next(it), next(it), next(it)

    i = pl.program_id(1)
    x = x_ref[0]
    shift1, scale1, gate1 = mod_ref[0, 0], mod_ref[0, 1], mod_ref[0, 2]
    shift2, scale2, gate2 = mod_ref[0, 3], mod_ref[0, 4], mod_ref[0, 5]

    hb = _rms_mod(x, g1_ref[...], scale1, shift1).astype(BF16)
    gates = jax.nn.sigmoid(_dot(hb, wm_ref[...]))
    mix = (gates[:, :D_MODEL] * _dot(ya_ref[0], wba_ref[...])
           + gates[:, D_MODEL:] * _dot(yb_ref[0], wbb_ref[...]))
    x1 = x + gate1 * _dot(mix.astype(BF16), wo_ref[...])
    h2_s[...] = _rms_mod(x1, g2_ref[...], scale2, shift2).astype(BF16)

    if not has_prev:
        @pl.when(i == 0)
        def _():
            carry[...] = jnp.zeros(carry.shape, F32)

    row = lax.broadcasted_iota(jnp.int32, (tm, 1), 0)
    row8 = lax.broadcasted_iota(jnp.int32, (8, 1), 0)

    def conv_cols(col0):
        cols = slice(col0, col0 + FF_CHUNK)
        up = _dot(h2_s[...], wup_ref[:, cols])
        r1 = pltpu.roll(up, 1, 0)
        r2 = pltpu.roll(up, 2, 0)
        if has_prev:
            s1 = jnp.where((row & (seq_len - 1)) == 0, p1_ref[:, cols], r1)
            s2 = jnp.where((row & (seq_len - 1)) < 2, p2_ref[:, cols], r2)
            up_ref[:, cols] = up
        else:
            prev = carry[:, cols]
            t1 = jnp.where(row8 == 0, pltpu.roll(prev, 1, 0), r1[:8])
            t2 = jnp.where(row8 < 2, pltpu.roll(prev, 2, 0), r2[:8])
            s1 = jnp.concatenate([t1, r1[8:]], axis=0)
            s2 = jnp.concatenate([t2, r2[8:]], axis=0)
            carry[:, cols] = up[tm - 8:]
            up_ref[0, :, cols] = up[tm - 8:]
        cw = cw_ref[:, cols]
        return cw[3:4] + cw[0:1] * s2 + cw[1:2] * s1 + cw[2:3] * up

    for c in range(N_FF_CHUNKS):
        a = conv_cols(c * FF_CHUNK)
        gv = conv_cols(D_FF + c * FF_CHUNK)
        act_s[:, c * FF_CHUNK:(c + 1) * FF_CHUNK] = (jax.nn.gelu(a) * gv).astype(BF16)
    x2 = x1 + gate2 * _dot(act_s[...], wdn_ref[...])
    y_ref[0] = x2 * lax.rsqrt(jnp.mean(x2 * x2, axis=-1, keepdims=True) + EPS) * gf_ref[...]


def _post_call(x, mod, ya, yb, g1, g2, gf, wm, wba, wbb, wo, wup, wdn, cw, prev=None, *, tm, seq_len):
    G, T, _ = x.shape
    R = mod.shape[2]
    nt = T // tm
    has_prev = prev is not None
    single = pl.Buffered(1)
    c2 = lambda b, i: (0, 0)
    c3 = lambda b, i: (0, 0, 0)
    in_specs = [pl.BlockSpec((1, tm, D_MODEL), lambda b, i: (b, i, 0)),
                pl.BlockSpec((1, 6, R, D_MODEL), lambda b, i: (b, 0, 0, 0)),
                pl.BlockSpec((1, tm, A_WIDTH), lambda b, i: (b, i, 0)),
                pl.BlockSpec((1, tm, B_WIDTH), lambda b, i: (b, i, 0)),
                pl.BlockSpec((1, D_MODEL), c2),
                pl.BlockSpec((1, D_MODEL), c2),
                pl.BlockSpec((1, D_MODEL), c2),
                pl.BlockSpec(wm.shape, c2, pipeline_mode=single),
                pl.BlockSpec(wba.shape, c2, pipeline_mode=single),
                pl.BlockSpec(wbb.shape, c2, pipeline_mode=single),
                pl.BlockSpec(wo.shape, c2, pipeline_mode=single),
                pl.BlockSpec(wup.shape, c2, pipeline_mode=single),
                pl.BlockSpec(wdn.shape, c2, pipeline_mode=single),
                pl.BlockSpec(cw.shape, c2, pipeline_mode=single)]
    args = [x, mod, ya, yb, g1, g2, gf, wm, wba, wbb, wo, wup, wdn, cw]
    if has_prev:
        in_specs += [pl.BlockSpec(prev[0].shape, c2), pl.BlockSpec(prev[1].shape, c2)]
        args += list(prev)
        up_shape = jax.ShapeDtypeStruct((T, F2), F32)
        up_spec = pl.BlockSpec((tm, F2), lambda b, i: (i, 0))
    else:
        up_shape = jax.ShapeDtypeStruct((G, 8, F2), F32)
        up_spec = pl.BlockSpec((1, 8, F2), lambda b, i: (b, 0, 0))
    return pl.pallas_call(
        functools.partial(_post_kernel, tm=tm, seq_len=seq_len, has_prev=has_prev),
        out_shape=[jax.ShapeDtypeStruct((G, T, D_MODEL), F32), up_shape],
        grid=(G, nt),
        in_specs=in_specs,
        out_specs=[pl.BlockSpec((1, tm, D_MODEL), lambda b, i: (b, i, 0)), up_spec],
        scratch_shapes=[pltpu.VMEM((8, F2), F32),
                        pltpu.VMEM((tm, D_MODEL), BF16),
                        pltpu.VMEM((tm, D_FF), BF16)],
        compiler_params=pltpu.CompilerParams(dimension_semantics=("arbitrary", "arbitrary"),
                                             vmem_limit_bytes=VMEM_LIMIT),
        name="post",
    )(*args)


def _block_diag2(m):
    z = jnp.zeros_like(m)
    return jnp.concatenate([jnp.concatenate([m, z], axis=-1), jnp.concatenate([z, m], axis=-1)], axis=-2)


def kernel(x_prompt, x_sample, cache_kv, state_kv_win, state_ffn_conv, page_table, c_prompt, c_sample, w_ada, b_ada, g_norm1, w_in, ln_v_g, ln_v_b, w_spatial, b_spatial, cmp_pe, cmp_w1, cmp_b1, cmp_w2, cmp_b2, w_branch_a, w_branch_b, w_out, g_norm2, w_up, w_conv, b_conv, w_down, g_final):
    B, T, _ = x_prompt.shape
    SB, ST, _ = x_sample.shape
    n_pool, page_size = cache_kv.shape[1], cache_kv.shape[2]
    n_pages = page_table.shape[1]
    past_len = n_pages * page_size
    lbuf = state_kv_win.shape[2]
    assert cache_kv.shape[0] == 1 and page_size == 128 and lbuf == WINDOW and SB * ST == 128 and ST <= 8
    assert past_len % BLOCK == 0 and n_pages == 128 and T % 512 == 0 and ST & (ST - 1) == 0 and ST >= 2

    win = w_in[0]
    wa = win[:, :2 * A_WIDTH].astype(BF16)
    o = 2 * A_WIDTH
    wq = win[:, o:o + B_WIDTH].astype(BF16)
    o += B_WIDTH
    wkv = win[:, o:o + KV_COLS]
    wkvT = wkv.T.astype(BF16)
    wkvc = wkv[:, :256].astype(BF16)
    o += KV_COLS
    wg = jnp.pad(win[:, o:o + 3 * B_HEADS], ((0, 0), (0, 128 - 3 * B_HEADS))).astype(BF16)
    wgT = jnp.pad(win[:, o:o + 3 * B_HEADS].T, ((0, 32 - 3 * B_HEADS), (0, 0))).astype(BF16)
    wqT = wq.T
    wrow = jnp.concatenate([wkv[:, :3 * 128], wkv[:, 4 * 128:5 * 128]], axis=1).astype(BF16)
    o += 3 * B_HEADS
    wm = win[:, o:].astype(BF16)
    g1 = g_norm1[0][None]
    g2 = g_norm2[0][None]
    gf = g_final[None]
    lng = ln_v_g[0][None]
    lnb = ln_v_b[0][None]

    ws = w_spatial[0]
    bs = b_spatial[0]
    wsp_p = jnp.tril(ws).astype(BF16)
    bsp_p = jnp.repeat(bs.T, 128, axis=1)
    w4 = jnp.tril(ws[:, :ST, :ST])
    seq_of = jnp.arange(128, dtype=jnp.int32) // ST
    same_seq = seq_of[:, None] == seq_of[None, :]
    wsp_s = jnp.where(same_seq, jnp.tile(w4, (1, 128 // ST, 128 // ST)), 0.0).astype(BF16)
    bsp_s = jnp.repeat(jnp.tile(bs[:, :ST].T, (128 // ST, 1)), 128, axis=1)

    w1, pe = cmp_w1[0].astype(BF16), cmp_pe[0]
    w1_rows = _block_diag2(w1)
    pe_rows = jnp.tile(pe, (1, 1, 2))
    w1_pages = _block_diag2(w1.transpose(0, 2, 1, 3)).reshape(2, HEAD_DIM // 2, 256, 128)
    pe_pages = jnp.tile(pe.transpose(0, 2, 1), (1, 1, 2)).reshape(2, HEAD_DIM // 2, 1, 256)
    pe_pages = jnp.pad(pe_pages, ((0, 0), (0, 0), (0, PE_ROWS - 1), (0, 0))).astype(BF16)
    b1t = jnp.tile(cmp_b1[0], (1, 2))[:, None, :]
    b2t = jnp.tile(cmp_b2[0], (1, 2))[:, None, :]
    w2bd = _block_diag2(cmp_w2[0].astype(BF16))

    wba = w_branch_a[0].astype(BF16)
    wbb = w_branch_b[0].astype(BF16)
    wo = w_out[0].astype(BF16)
    wup = w_up[0].astype(BF16)
    wdn = w_down[0].astype(BF16)
    cw = jnp.concatenate([w_conv[0], b_conv[0][None], jnp.zeros((4, F2), F32)], axis=0)

    mod = _mod_call(jnp.concatenate([c_prompt, c_sample], axis=0), w_ada[0], b_ada)
    mod_p = mod[:B].reshape(B, 6, 1, D_MODEL)
    mod_s = jnp.repeat(mod[B:], ST, axis=0).reshape(SB * ST, 6, D_MODEL).transpose(1, 0, 2)[None]

    ya_p, qT_p, kv4T_p, winT_p, vselT_p, vwinT_p, kvc_p, ksel_p, kwin_p, gzT_p = _inproj_call(
        x_prompt, mod_p, g1, wa, lng, lnb, wsp_p, bsp_p, (wqT, wkvT, wrow, wgT), tm=512, prompt=True)
    xs2 = x_sample.reshape(1, SB * ST, D_MODEL)
    ya_s, q_s, kvT_s, kvc_s, gz_s, vn_s = _inproj_call(
        xs2, mod_s, g1, wa, lng, lnb, wsp_s, bsp_s, (wq, wkvT, wkvc, wg), tm=128, prompt=False)

    nb = T // BLOCK
    cmp_p = _cmp_rows_call(kvc_p.reshape(B * T, 256), pe_rows, w1_rows, b1t, w2bd, b2t, R=128)
    kc_p = cmp_p[0].reshape(B, nb, 128)
    vcT_p = cmp_p[1].reshape(B, nb, 128).transpose(0, 2, 1)
    yb_p = _p_attn_call(qT_p, ksel_p, kwin_p, vselT_p, vwinT_p, kc_p, vcT_p, gzT_p, tq=512, kc=512)

    cache_pages = jnp.transpose(cache_kv[0], (0, 2, 3, 4, 1)).reshape(n_pool, 512, page_size)
    pool_seq = _cmp_gather_call(page_table.reshape(-1), cache_pages, pe_pages, w1_pages, b1t, w2bd, b2t,
                                Pt=64)
    kv_rows_s = jnp.transpose(kvT_s[0], (1, 0)).reshape(SB, ST, KV_COLS)
    newblk = jnp.pad(jnp.transpose(kv_rows_s[:, :, :256], (0, 2, 1)), ((0, 0), (0, 0), (0, page_size - ST)))
    newc = _cmp_pages_call(newblk, pe_pages, w1_pages, b1t, w2bd, b2t, Pt=SB)

    q_r = jnp.pad(q_s.reshape(SB, ST, KV_HEADS, GQA, HEAD_DIM).transpose(0, 2, 3, 1, 4),
                  ((0, 0), (0, 0), (0, 0), (0, 8 - ST), (0, 0))).reshape(SB, KV_HEADS, 8 * GQA, HEAD_DIM)
    ocmp_s, imp_s = _s_cmp_call(q_r, pool_seq, newc.reshape(SB, 1, 512), n_pages=n_pages, past_len=past_len)
    idx = _s_select_call(imp_s.reshape(SB * KV_HEADS * 8, 2 * n_pages), n_pages=n_pages, n_pick=N_SELECT - 1)
    idx_flat = idx.reshape(SB, KV_HEADS, 8, 128)[:, :, :ST, :N_SELECT].reshape(-1)
    winT = jnp.transpose(state_kv_win[0], (0, 2, 3, 4, 1)).reshape(SB, 256, lbuf)
    gz_r = jnp.pad(gz_s[0, :, :24].reshape(SB, ST, 3, KV_HEADS, GQA).transpose(0, 3, 4, 1, 2),
                   ((0, 0), (0, 0), (0, 0), (0, 8 - ST), (0, 128 - 3))).reshape(SB, KV_HEADS, 8 * GQA, 128)
    cache_tiles = cache_pages.reshape(n_pool, 4, KV_HEADS, HEAD_DIM, page_size)
    yb_r, wnextT = _s_attn_call(page_table, idx_flat, q_r, kvT_s[0], winT, gz_r, ocmp_s, cache_tiles,
                        n_tok=ST, past_len=past_len)
    yb_s = yb_r.reshape(SB, KV_HEADS, GQA, 8, HEAD_DIM)[:, :, :, :ST].transpose(0, 3, 1, 2, 4)
    yb_s = yb_s.reshape(1, SB * ST, B_WIDTH).astype(BF16)

    y_p, up_p = _post_call(x_prompt, mod_p, ya_p, yb_p, g1, g2, gf, wm, wba, wbb, wo, wup, wdn, cw,
                           tm=512, seq_len=T)
    st = state_ffn_conv[0]
    zrow = jnp.zeros((SB, 1, F2), F32)
    p1 = jnp.concatenate([st[:, 1:2], jnp.tile(zrow, (1, ST - 1, 1))], axis=1).reshape(SB * ST, F2)
    p2 = jnp.concatenate([st, jnp.tile(zrow, (1, ST - 2, 1))], axis=1).reshape(SB * ST, F2)
    y_s, up_s = _post_call(xs2, mod_s, ya_s, yb_s, g1, g2, gf, wm, wba, wbb, wo, wup, wdn, cw,
                           prev=(p1, p2),
                           tm=SB * ST, seq_len=ST)

    kv_prompt = jnp.transpose(kv4T_p.reshape(B, 4, KV_HEADS, HEAD_DIM, T), (0, 4, 1, 2, 3))[None]
    win_prompt = jnp.transpose(winT_p[:, :, T - WINDOW:].reshape(B, 2, KV_HEADS, HEAD_DIM, WINDOW),
                               (0, 4, 1, 2, 3))[None]
    kv_sample = kv_rows_s[:, :, :512].reshape(SB, ST, 4, KV_HEADS, HEAD_DIM)[None]
    win_sample = jnp.transpose(wnextT.reshape(SB, 2, KV_HEADS, HEAD_DIM, lbuf), (0, 4, 1, 2, 3))[None]
    v_chunk = vn_s.reshape(SB, ST, A_WIDTH)[None]
    conv_prompt = up_p[:, 6:8][None]
    conv_sample = up_s.reshape(SB, ST, F2)[:, ST - 2:][None]
    return (y_p, y_s.reshape(SB, ST, D_MODEL), kv_prompt, kv_sample, win_prompt, win_sample, v_chunk,
            conv_prompt, conv_sample)
```

```python
import functools

import jax
import jax.numpy as jnp
from jax import lax
from jax.experimental import pallas as pl
from jax.experimental.pallas import tpu as pltpu

F32 = jnp.float32
BF16 = jnp.bfloat16

D_MODEL = 1024
A_WIDTH = 512
A_GROUPS = 4
CHUNK = 128
B_HEADS = 8
HEAD_DIM = 64
B_WIDTH = 512
KV_HEADS = 2
GQA = 4
BLOCK = 64
N_SELECT = 16
WINDOW = 512
N_KV_SLOTS = 6
KV_COLS = 768
D_FF = 2816
F2 = 2 * D_FF
CONV_W = 3
EPS = 1e-6
NEG = -1e30
FORCE_SCORE = 1e4

FF_CHUNK = 256
N_FF_CHUNKS = D_FF // FF_CHUNK
ROW_BLOCK = 64
PE_ROWS = 16
SKEW = 2
ACC_ROWS = HEAD_DIM + 16
LOG2E = 1.4426950408889634
VMEM_LIMIT = 56 * 1024 * 1024


def _dot(a, b):
    return jnp.dot(a, b, preferred_element_type=F32)


def _dot_nt(a, b):
    return lax.dot_general(a, b, (((1,), (1,)), ((), ())), preferred_element_type=F32)


def _rms_mod(x, g, scale, shift):
    y = x * lax.rsqrt(jnp.mean(x * x, axis=-1, keepdims=True) + EPS)
    return (y * g) * (1.0 + scale) + shift


def _mod_kernel(c_ref, w_ref, b_ref, o_ref):
    c = c_ref[...]
    s = c * jax.nn.sigmoid(c)
    o_ref[...] = _dot(s.astype(BF16), w_ref[...].astype(BF16)) + b_ref[...]


def _mod_call(c_all, w_ada, b_ada):
    n = c_all.shape[0]
    tn = 1536
    return pl.pallas_call(
        _mod_kernel,
        out_shape=jax.ShapeDtypeStruct((n, 6 * D_MODEL), F32),
        grid=(6 * D_MODEL // tn,),
        in_specs=[pl.BlockSpec((n, D_MODEL), lambda j: (0, 0)),
                  pl.BlockSpec((D_MODEL, tn), lambda j: (0, j)),
                  pl.BlockSpec((1, tn), lambda j: (0, j))],
        out_specs=pl.BlockSpec((n, tn), lambda j: (0, j)),
        compiler_params=pltpu.CompilerParams(dimension_semantics=("arbitrary",), vmem_limit_bytes=VMEM_LIMIT),
        name="mod",
    )(c_all, w_ada, b_ada)


def _inproj_kernel(x_ref, mod_ref, g1_ref, wa_ref, lng_ref, lnb_ref, wsp_ref, bsp_ref, *refs, tm, prompt):
    x = x_ref[0]
    hb = _rms_mod(x, g1_ref[...], mod_ref[0, 1], mod_ref[0, 0]).astype(BF16)

    ga = jax.nn.gelu(_dot(hb, wa_ref[...]))
    u = ga[:, :A_WIDTH]
    v = ga[:, A_WIDTH:]
    vc = v - jnp.mean(v, axis=-1, keepdims=True)
    vn = vc * lax.rsqrt(jnp.mean(vc * vc, axis=-1, keepdims=True) + EPS) * lng_ref[...] + lnb_ref[...]
    vnb = vn.astype(BF16)

    if prompt:
        (wqT_ref, wkvT_ref, wrow_ref, wgT_ref,
         ya_ref, qT_ref, kv4T_ref, winT_ref, vselT_ref, vwinT_ref, kvc_ref, ksel_ref, kwin_ref, gzT_ref) = refs
    else:
        (wq_ref, wkvT_ref, wkvc_ref, wg_ref, ya_ref, q_ref, kvT_ref, kvc_ref, gz_ref, vn_ref) = refs
        vn_ref[0] = vn

    for c in range(tm // CHUNK):
        rows = slice(c * CHUNK, (c + 1) * CHUNK)
        s = jnp.concatenate(
            [_dot(wsp_ref[g], vnb[rows, g * 128:(g + 1) * 128]) for g in range(A_GROUPS)], axis=1)
        ya_ref[0, rows, :] = (u[rows] * (s + bsp_ref[...])).astype(BF16)

    kvT = _dot_nt(wkvT_ref[...], hb)
    if prompt:
        qT_ref[0] = (_dot_nt(wqT_ref[...], hb) * (HEAD_DIM ** -0.5 * LOG2E)).astype(BF16)
        kv4T_ref[0] = kvT[:4 * 128]
        winT_ref[0] = kvT[4 * 128:]
        vselT_ref[0] = kvT[3 * 128:4 * 128].astype(BF16)
        vwinT_ref[0] = kvT[5 * 128:].astype(BF16)
        row = _dot(hb, wrow_ref[...])
        kvc_ref[0] = row[:, :256]
        ksel_ref[0] = row[:, 256:384].astype(BF16)
        kwin_ref[0] = row[:, 384:].astype(BF16)
        gzT_ref[0] = jax.nn.sigmoid(_dot_nt(wgT_ref[...], hb))
    else:
        q_ref[0] = (_dot(hb, wq_ref[...]) * (HEAD_DIM ** -0.5)).astype(BF16)
        kvT_ref[0] = kvT
        kvc_ref[0] = _dot(hb, wkvc_ref[...])
        gz_ref[0] = jax.nn.sigmoid(_dot(hb, wg_ref[...]))


def _inproj_call(x, mod, g1, wa, lng, lnb, wsp, bsp, proj_w, *, tm, prompt):
    G, T, _ = x.shape
    R = mod.shape[2]
    nt = T // tm
    const2 = lambda b, i: (0, 0)
    const3 = lambda b, i: (0, 0, 0)
    rows = lambda w: pl.BlockSpec((1, tm, w), lambda b, i: (b, i, 0))
    cols = lambda h: pl.BlockSpec((1, h, tm), lambda b, i: (b, 0, i))
    if prompt:
        outs = [((G, T, A_WIDTH), BF16, rows(A_WIDTH)),
                ((G, B_WIDTH, T), BF16, cols(B_WIDTH)),
                ((G, 512, T), F32, cols(512)),
                ((G, 256, T), F32, cols(256)),
                ((G, 128, T), BF16, cols(128)),
                ((G, 128, T), BF16, cols(128)),
                ((G, T, 256), F32, rows(256)),
                ((G, T, 128), BF16, rows(128)),
                ((G, T, 128), BF16, rows(128)),
                ((G, 32, T), F32, cols(32))]
    else:
        outs = [((G, T, A_WIDTH), BF16, rows(A_WIDTH)),
                ((G, T, B_WIDTH), BF16, rows(B_WIDTH)),
                ((G, KV_COLS, T), F32, cols(KV_COLS)),
                ((G, T, 256), F32, rows(256)),
                ((G, T, 128), F32, rows(128)),
                ((G, T, A_WIDTH), F32, rows(A_WIDTH))]
    return pl.pallas_call(
        functools.partial(_inproj_kernel, tm=tm, prompt=prompt),
        out_shape=[jax.ShapeDtypeStruct(s, d) for s, d, _ in outs],
        grid=(G, nt),
        in_specs=[pl.BlockSpec((1, tm, D_MODEL), lambda b, i: (b, i, 0)),
                  pl.BlockSpec((1, 6, R, D_MODEL), lambda b, i: (b, 0, 0, 0)),
                  pl.BlockSpec((1, D_MODEL), const2),
                  pl.BlockSpec(wa.shape, const2),
                  pl.BlockSpec((1, A_WIDTH), const2),
                  pl.BlockSpec((1, A_WIDTH), const2),
                  pl.BlockSpec(wsp.shape, const3),
                  pl.BlockSpec(bsp.shape, const2)] + [pl.BlockSpec(w.shape, const2) for w in proj_w],
        out_specs=[sp for _, _, sp in outs],
        compiler_params=pltpu.CompilerParams(dimension_semantics=("arbitrary", "arbitrary"),
                                             vmem_limit_bytes=VMEM_LIMIT),
        name="inproj",
    )(x, mod, g1, wa, lng, lnb, wsp, bsp, *proj_w)


def _cmp_tail(acc, b1, w2, b2):
    return _dot(jax.nn.gelu(acc + b1).astype(BF16), w2) + b2


def _cmp_rows_kernel(x_ref, pe_ref, w1_ref, b1_ref, w2_ref, b2_ref, o_ref, *, R):
    acc = jnp.zeros((R, 128), F32)
    for j in range(BLOCK):
        xj = x_ref[pl.ds(j, R, stride=BLOCK), :] + pe_ref[0, j:j + 1, :]
        acc = acc + _dot(xj.astype(BF16), w1_ref[0, j])
    o_ref[0] = _cmp_tail(acc, b1_ref[0], w2_ref[0], b2_ref[0])


def _cmp_rows_call(kvc2d, pe, w1, b1, w2, b2, *, R):
    nrows = kvc2d.shape[0] // BLOCK
    return pl.pallas_call(
        functools.partial(_cmp_rows_kernel, R=R),
        out_shape=jax.ShapeDtypeStruct((2, nrows, 128), F32),
        grid=(2, nrows // R),
        in_specs=[pl.BlockSpec((R * BLOCK, 128), lambda s, r: (r, s)),
                  pl.BlockSpec((1, BLOCK, 128), lambda s, r: (s, 0, 0)),
                  pl.BlockSpec((1, BLOCK, 128, 128), lambda s, r: (s, 0, 0, 0)),
                  pl.BlockSpec((1, 1, 128), lambda s, r: (s, 0, 0)),
                  pl.BlockSpec((1, 128, 128), lambda s, r: (s, 0, 0)),
                  pl.BlockSpec((1, 1, 128), lambda s, r: (s, 0, 0))],
        out_specs=pl.BlockSpec((1, R, 128), lambda s, r: (s, r, 0)),
        compiler_params=pltpu.CompilerParams(dimension_semantics=("arbitrary", "arbitrary"),
                                             vmem_limit_bytes=VMEM_LIMIT),
        name="cmp_rows",
    )(kvc2d, pe, w1, b1, w2, b2)


def _compress_pages(load_rows, pe_ref, w1_ref, b1_ref, w2_ref, b2_ref, o_ref, Pt):
    for s in range(2):
        acc = jnp.zeros((2 * Pt + PE_ROWS, 128), F32)
        for dg in range(HEAD_DIM // 8):
            r0 = s * 128 + dg * 8
            xa = jnp.swapaxes(load_rows(r0), 0, 1).astype(BF16)
            xb = jnp.swapaxes(load_rows(HEAD_DIM + r0), 0, 1).astype(BF16)
            for dd in range(0, 8, 2):
                dp = dg * 4 + dd // 2
                xd = jnp.concatenate([jnp.concatenate([xa[dd], xa[dd + 1]], axis=1),
                                      jnp.concatenate([xb[dd], xb[dd + 1]], axis=1),
                                      pe_ref[s, dp]], axis=0)
                acc = acc + _dot(xd, w1_ref[s, dp])
        pre = acc[:2 * Pt] + acc[2 * Pt:2 * Pt + 1]
        r = _cmp_tail(pre, b1_ref[s], w2_ref[s], b2_ref[s])
        o_ref[:, (2 * s) * 128:(2 * s + 1) * 128] = r[:Pt]
        o_ref[:, (2 * s + 1) * 128:(2 * s + 2) * 128] = r[Pt:]


def _cmp_pages_kernel(x_ref, pe_ref, w1_ref, b1_ref, w2_ref, b2_ref, o_ref, *, Pt):
    _compress_pages(lambda r0: x_ref[:, r0:r0 + 8, :], pe_ref, w1_ref, b1_ref, w2_ref, b2_ref, o_ref, Pt)


def _cmp_weight_specs(ws, index):
    return [pl.BlockSpec(w.shape, functools.partial(index, (0,) * w.ndim)) for w in ws]


def _cmp_pages_call(pages, pe, w1, b1, w2, b2, *, Pt):
    P = pages.shape[0]
    ws = (pe, w1, b1, w2, b2)
    return pl.pallas_call(
        functools.partial(_cmp_pages_kernel, Pt=Pt),
        out_shape=jax.ShapeDtypeStruct((P, 512), F32),
        grid=(P // Pt,),
        in_specs=[pl.BlockSpec((Pt, 256, 128), lambda i: (i, 0, 0))] + _cmp_weight_specs(ws, lambda z, i: z),
        out_specs=pl.BlockSpec((Pt, 512), lambda i: (i, 0)),
        compiler_params=pltpu.CompilerParams(dimension_semantics=("arbitrary",), vmem_limit_bytes=VMEM_LIMIT),
        name="cmp_pages",
    )(pages, *ws)


def _cmp_gather_kernel(pt_ref, cache_ref, pe_ref, w1_ref, b1_ref, w2_ref, b2_ref, o_ref, xbuf, sem, *, Pt):
    i = pl.program_id(0)
    slot = i & 1

    def page_copy(step, sl, k):
        return pltpu.make_async_copy(cache_ref.at[pt_ref[step * Pt + k], pl.ds(0, 256), :],
                                     xbuf.at[sl, k], sem.at[sl])

    @pl.when(i == 0)
    def _():
        for k in range(Pt):
            page_copy(0, 0, k).start()

    @pl.when(i + 1 < pl.num_programs(0))
    def _():
        for k in range(Pt):
            page_copy(i + 1, 1 - slot, k).start()

    for k in range(Pt):
        page_copy(i, slot, k).wait()
    _compress_pages(lambda r0: xbuf[slot, :, r0:r0 + 8, :], pe_ref, w1_ref, b1_ref, w2_ref, b2_ref, o_ref, Pt)


def _cmp_gather_call(pt_flat, cache_pages, pe, w1, b1, w2, b2, *, Pt):
    n = pt_flat.shape[0]
    ws = (pe, w1, b1, w2, b2)
    return pl.pallas_call(
        functools.partial(_cmp_gather_kernel, Pt=Pt),
        out_shape=jax.ShapeDtypeStruct((n, 512), F32),
        grid_spec=pltpu.PrefetchScalarGridSpec(
            num_scalar_prefetch=1,
            grid=(n // Pt,),
            in_specs=[pl.BlockSpec(memory_space=pl.ANY)] + _cmp_weight_specs(ws, lambda z, i, pt: z),
            out_specs=pl.BlockSpec((Pt, 512), lambda i, pt: (i, 0)),
            scratch_shapes=[pltpu.VMEM((2, Pt, 256, 128), F32),
                            pltpu.SemaphoreType.DMA((2,))]),
        compiler_params=pltpu.CompilerParams(dimension_semantics=("arbitrary",), vmem_limit_bytes=VMEM_LIMIT),
        name="cmp_gather",
    )(pt_flat, cache_pages, *ws)


def _pair_schedule(T, tq, kc):
    js, cs = [], []
    for j in range(T // tq):
        for c in range(((j + 1) * tq - 1) // kc + 1):
            js.append(j)
            cs.append(c)
    return js, cs


def _p_attn_kernel(jt_ref, ct_ref, qT_ref, ksel_ref, vselT_ref, kwin_ref, kwinp_ref, vwinT_ref, vwinTp_ref,
                   kc_ref, vcT_ref, gzT_ref, yb_ref,
                   sel_s, neg_s, ocmp_s, m_s, acc_s, s_scr, s2_scr, p_scr, p2_scr, b_scr, k2_s, *, tq, kc, T, strip):
    p = pl.program_id(1)
    j = jt_ref[p]
    c = ct_ref[p]
    nb = T // BLOCK
    N = GQA * tq
    q0 = j * tq
    k0 = c * kc
    c_last = ((j + 1) * tq - 1) // kc

    def q_pad(h):
        qT = jnp.concatenate([qT_ref[0, (h * GQA + g) * HEAD_DIM:(h * GQA + g + 1) * HEAD_DIM, :]
                              for g in range(GQA)], axis=1)
        z = jnp.zeros_like(qT)
        return jnp.concatenate([qT if k == h else z for k in range(KV_HEADS)], axis=0)

    @pl.when(c == 0)
    def _():
        m_s[...] = jnp.full(m_s.shape, NEG, F32)
        acc_s[...] = jnp.zeros(acc_s.shape, F32)
        tok_n = q0 + (lax.broadcasted_iota(jnp.int32, (1, N), 1) & (tq - 1))
        tok_1 = q0 + lax.broadcasted_iota(jnp.int32, (1, tq), 1)
        blk = lax.broadcasted_iota(jnp.int32, (nb, 1), 0)
        avail = (blk + 1) * BLOCK <= tok_n + 1
        cur = tok_1 >> 6
        forced = (blk == 0) | (blk == cur) | (blk == cur - 1)
        future = blk > cur
        kcb = kc_ref[0].astype(BF16)
        for h in range(KV_HEADS):
            s = jnp.where(avail, _dot(kcb, q_pad(h)), NEG)
            m = jnp.max(s, axis=0, keepdims=True)
            e = jnp.where(avail, jnp.exp2(s - m), 0.0)
            pr = e / jnp.maximum(jnp.sum(e, axis=0, keepdims=True), 1e-30)
            ocmp_s[h] = _dot(vcT_ref[0, h * HEAD_DIM:(h + 1) * HEAD_DIM, :].astype(BF16), pr.astype(BF16))
            imp = pr[:, 0:tq]
            for g in range(1, GQA):
                imp = imp + pr[:, g * tq:(g + 1) * tq]
            imp = jnp.where(forced, FORCE_SCORE, jnp.where(future, -1.0, imp))
            groups = [imp[r:r + 8] for r in range(0, nb, 8)]
            ranks = [jnp.zeros((8, tq), F32) for _ in groups]
            for n in range(nb):
                vn = imp[n:n + 1, :]
                for k, grp in enumerate(groups):
                    if 8 * k > n:
                        ahead = vn >= grp
                    elif 8 * k + 7 <= n:
                        ahead = vn > grp
                    else:
                        ahead = (vn > grp) | ((vn == grp) & (blk[8 * k:8 * k + 8] > n))
                    ranks[k] = ranks[k] + ahead.astype(F32)
            sel = jnp.concatenate([(r < float(N_SELECT)).astype(F32) for r in ranks]
                                  + [jnp.zeros((128 - nb, tq), F32)], axis=0)
            sel_s[h] = sel.astype(BF16)
            neg_s[h] = ((sel - 1.0) * -NEG).astype(BF16)

    kpos = k0 + lax.broadcasted_iota(jnp.int32, (kc, tq), 0)
    tok = q0 + lax.broadcasted_iota(jnp.int32, (kc, tq), 1)

    strips = [slice(st * strip, (st + 1) * strip) for st in range(N // strip)]

    row_blocks = [slice(r, r + ROW_BLOCK) for r in range(0, kc, ROW_BLOCK)]

    ones_rows = (lax.broadcasted_iota(jnp.int32, (ACC_ROWS - HEAD_DIM, kc), 0) == 0).astype(BF16)

    streams = [(h, st) for st in range(len(strips)) for h in range(KV_HEADS)]

    def q_strip(h, st):
        g, t0 = divmod(st * strip, tq)
        r0 = (h * GQA + g) * HEAD_DIM
        qT = qT_ref[0, r0:r0 + HEAD_DIM, t0:t0 + strip]
        z = jnp.zeros_like(qT)
        return jnp.concatenate([qT if k == h else z for k in range(KV_HEADS)], axis=0)

    def with_ones(vT_ref, h):
        return jnp.concatenate([vT_ref[0, h * HEAD_DIM:(h + 1) * HEAD_DIM, :], ones_rows], axis=0)

    def run_skewed(stages):
        for k in range(len(streams) + SKEW * (len(stages) - 1)):
            for lag, stage in enumerate(stages):
                if 0 <= k - SKEW * lag < len(streams):
                    stage(*streams[k - SKEW * lag])

    def selected_step(diagonal):
        vT1 = [with_ones(vselT_ref, h) for h in range(KV_HEADS)]
        alpha = {}
        if diagonal:
            keys = ksel_ref[0]
            for h in range(KV_HEADS):
                b_scr[h] = jnp.where((_dot(expand, sel_s[h]) > 0.5) & causal, 0.0, NEG)
        else:
            k2_s[:, :128] = ksel_ref[0]
            k2_s[:, 128:] = expand
            keys = k2_s

        def n_keys(st):
            return min(kc, (st * strip) % tq + strip) if diagonal else kc

        def scores(h, st):
            nk = n_keys(st)
            qp = q_strip(h, st)
            if not diagonal:
                t0 = (st * strip) % tq
                qp = jnp.concatenate([qp, neg_s[h, :, t0:t0 + strip]], axis=0)
            s_scr[h, :nk, strips[st]] = _dot(keys[:nk, :], qp)

        def probs(h, st):
            ls = strips[st]
            t0 = (st * strip) % tq
            rbs = row_blocks[:n_keys(st) // ROW_BLOCK]
            m = None
            for rb in rbs:
                t = s_scr[h, rb, ls]
                if diagonal:
                    t = t + b_scr[h, rb, t0:t0 + strip]
                    s_scr[h, rb, ls] = t
                m = t if m is None else jnp.maximum(m, t)
            m_old = m_s[h, :, ls]
            m_new = jnp.maximum(m_old, jnp.max(m, axis=0, keepdims=True))
            alpha[(h, st)] = jnp.exp2(m_old - m_new)
            m_s[h, :, ls] = m_new
            for rb in rbs:
                p_scr[h, rb, ls] = jnp.exp2(s_scr[h, rb, ls] - m_new).astype(BF16)

        def weighted_values(h, st):
            ls = strips[st]
            nk = n_keys(st)
            acc_s[h, :, ls] = alpha[(h, st)] * acc_s[h, :, ls] + _dot(vT1[h][:, :nk], p_scr[h, :nk, ls])

        run_skewed((scores, probs, weighted_values))

    def window_step():
        no_prev = jnp.where(j == 0, NEG, 0.0)
        vT1 = [with_ones(vwinT_ref, h) for h in range(KV_HEADS)]
        vT1p = [with_ones(vwinTp_ref, h) for h in range(KV_HEADS)]

        def scores(h, st):
            qp = q_strip(h, st)
            s_scr[h, :, strips[st]] = _dot(kwin_ref[0], qp)
            s2_scr[h, :, strips[st]] = _dot(kwinp_ref[0], qp)

        def probs(h, st):
            ls = strips[st]
            t0 = (st * strip) % tq
            m = None
            for rb in row_blocks:
                cur = b_scr[KV_HEADS, rb, t0:t0 + strip] > 0.5
                t = jnp.where(cur, s_scr[h, rb, ls], s2_scr[h, rb, ls] + no_prev)
                s_scr[h, rb, ls] = t
                m = t if m is None else jnp.maximum(m, t)
            m_new = jnp.max(m, axis=0, keepdims=True)
            for rb in row_blocks:
                pe = jnp.exp2(s_scr[h, rb, ls] - m_new)
                pc = pe * b_scr[KV_HEADS, rb, t0:t0 + strip]
                p_scr[h, rb, ls] = pc.astype(BF16)
                p2_scr[h, rb, ls] = (pe - pc).astype(BF16)

        def weighted_values(h, st):
            ls = strips[st]
            acc_s[KV_HEADS + h, :, ls] = _dot(vT1[h], p_scr[h, :, ls]) + _dot(vT1p[h], p2_scr[h, :, ls])

        run_skewed((scores, probs, weighted_values))

    blk_of_key = (k0 >> 6) + (lax.broadcasted_iota(jnp.int32, (kc, 128), 0) >> 6)
    expand = (lax.broadcasted_iota(jnp.int32, (kc, 128), 1) == blk_of_key).astype(BF16)
    causal = kpos <= tok

    @pl.when(c < c_last)
    def _():
        selected_step(diagonal=False)

    @pl.when(c == c_last)
    def _():
        selected_step(diagonal=True)
        b_scr[KV_HEADS] = causal.astype(F32)
        window_step()

    @pl.when(c == c_last)
    def _():
        gz = gzT_ref[0]
        outs = []
        for h in range(KV_HEADS):
            a_sel = acc_s[h]
            a_win = acc_s[KV_HEADS + h]
            o_sel = a_sel[:HEAD_DIM] / jnp.maximum(a_sel[HEAD_DIM:HEAD_DIM + 1], 1e-30)
            o_win = a_win[:HEAD_DIM] / jnp.maximum(a_win[HEAD_DIM:HEAD_DIM + 1], 1e-30)
            o_cmp = ocmp_s[h]
            for g in range(GQA):
                col = h * GQA + g
                ls = slice(g * tq, (g + 1) * tq)
                outs.append(gz[col:col + 1] * o_cmp[:, ls] + gz[B_HEADS + col:B_HEADS + col + 1] * o_sel[:, ls]
                            + gz[2 * B_HEADS + col:2 * B_HEADS + col + 1] * o_win[:, ls])
        yb_ref[0] = jnp.concatenate(outs, axis=0).T.astype(BF16)


def _p_attn_call(qT, ksel, kwin, vselT, vwinT, kc_all, vcT, gzT, *, tq, kc):
    B, _, T = qT.shape
    nb = T // BLOCK
    N = GQA * tq
    assert tq == kc == WINDOW
    js, cs = _pair_schedule(T, tq, kc)
    jt = jnp.asarray(js, jnp.int32)
    ct = jnp.asarray(cs, jnp.int32)
    prev = lambda jt, p: jnp.maximum(jt[p] - 1, 0)
    return pl.pallas_call(
        functools.partial(_p_attn_kernel, tq=tq, kc=kc, T=T, strip=256),
        out_shape=jax.ShapeDtypeStruct((B, T, B_WIDTH), BF16),
        grid_spec=pltpu.PrefetchScalarGridSpec(
            num_scalar_prefetch=2,
            grid=(B, len(js)),
            in_specs=[pl.BlockSpec((1, B_WIDTH, tq), lambda b, p, jt, ct: (b, 0, jt[p])),
                      pl.BlockSpec((1, kc, 128), lambda b, p, jt, ct: (b, ct[p], 0)),
                      pl.BlockSpec((1, 128, kc), lambda b, p, jt, ct: (b, 0, ct[p])),
                      pl.BlockSpec((1, kc, 128), lambda b, p, jt, ct: (b, jt[p], 0)),
                      pl.BlockSpec((1, kc, 128), lambda b, p, jt, ct: (b, prev(jt, p), 0)),
                      pl.BlockSpec((1, 128, kc), lambda b, p, jt, ct: (b, 0, jt[p])),
                      pl.BlockSpec((1, 128, kc), lambda b, p, jt, ct: (b, 0, prev(jt, p))),
                      pl.BlockSpec((1, nb, 128), lambda b, p, jt, ct: (b, 0, 0)),
                      pl.BlockSpec((1, 128, nb), lambda b, p, jt, ct: (b, 0, 0)),
                      pl.BlockSpec((1, 32, tq), lambda b, p, jt, ct: (b, 0, jt[p]))],
            out_specs=pl.BlockSpec((1, tq, B_WIDTH), lambda b, p, jt, ct: (b, jt[p], 0)),
            scratch_shapes=[pltpu.VMEM((KV_HEADS, 128, tq), BF16),
                            pltpu.VMEM((KV_HEADS, 128, tq), BF16),
                            pltpu.VMEM((KV_HEADS, HEAD_DIM, N), F32),
                            pltpu.VMEM((2 * KV_HEADS, 1, N), F32),
                            pltpu.VMEM((2 * KV_HEADS, ACC_ROWS, N), F32),
                            pltpu.VMEM((KV_HEADS, kc, N), F32),
                            pltpu.VMEM((KV_HEADS, kc, N), F32),
                            pltpu.VMEM((KV_HEADS, kc, N), BF16),
                            pltpu.VMEM((KV_HEADS, kc, N), BF16),
                            pltpu.VMEM((KV_HEADS + 1, kc, tq), F32),
                            pltpu.VMEM((kc, 256), BF16)]),
        compiler_params=pltpu.CompilerParams(dimension_semantics=("arbitrary", "arbitrary"),
                                             vmem_limit_bytes=VMEM_LIMIT),
        name="p_attn",
    )(jt, ct, qT, ksel, vselT, kwin, kwin, vwinT, vwinT, kc_all, vcT, gzT)


def _s_cmp_kernel(q_ref, gath, newc_ref, ocmp_ref, imp_ref, *, n_pages, past_len):
    rows = 8 * GQA
    t_row = lax.broadcasted_iota(jnp.int32, (rows, 1), 0) & 7
    pos = past_len + t_row
    lane = lax.broadcasted_iota(jnp.int32, (1, 2 * n_pages), 1)
    blk = 2 * (lane & (n_pages - 1)) + (lane >> 7)
    new_blk = past_len // BLOCK
    avail = (blk + 1) * BLOCK <= pos + 1
    avail_new = (new_blk + 1) * BLOCK <= pos + 1
    for h in range(KV_HEADS):
        qh = q_ref[0, h]
        kc = [gath[:, (h * 2 + k) * 64:(h * 2 + k + 1) * 64].astype(BF16) for k in range(2)]
        vc = [gath[:, (2 + h) * 128 + k * 64:(2 + h) * 128 + (k + 1) * 64].astype(BF16) for k in range(2)]
        kc_new = newc_ref[0, :, h * 128:h * 128 + 64].astype(BF16).astype(F32)
        vc_new = newc_ref[0, :, (2 + h) * 128:(2 + h) * 128 + 64].astype(BF16).astype(F32)
        s = jnp.where(avail, jnp.concatenate([_dot_nt(qh, kc[0]), _dot_nt(qh, kc[1])], axis=1), NEG)
        s_new = jnp.where(avail_new, jnp.sum(qh.astype(F32) * kc_new, axis=-1, keepdims=True), NEG)
        m = jnp.maximum(jnp.max(s, axis=-1, keepdims=True), s_new)
        e = jnp.where(avail, jnp.exp(s - m), 0.0)
        e_new = jnp.where(avail_new, jnp.exp(s_new - m), 0.0)
        den = jnp.maximum(jnp.sum(e, axis=-1, keepdims=True) + e_new, 1e-30)
        p = e / den
        p_new = e_new / den
        pb = p.astype(BF16)
        ocmp_ref[0, h] = (_dot(pb[:, :n_pages], vc[0]) + _dot(pb[:, n_pages:], vc[1])
                          + p_new.astype(BF16).astype(F32) * vc_new)
        imp = p[0:8] + p[8:16] + p[16:24] + p[24:32]
        cur = (past_len + lax.broadcasted_iota(jnp.int32, (8, 1), 0)) >> 6
        forced = (blk == 0) | (blk == cur) | (blk == cur - 1)
        imp_ref[0, h] = jnp.where(forced, FORCE_SCORE, jnp.where(blk > cur, -1.0, imp))


def _s_cmp_call(q_r, pool_seq, newc, *, n_pages, past_len):
    nbatch = q_r.shape[0]
    return pl.pallas_call(
        functools.partial(_s_cmp_kernel, n_pages=n_pages, past_len=past_len),
        out_shape=[jax.ShapeDtypeStruct((nbatch, KV_HEADS, 32, HEAD_DIM), F32),
                   jax.ShapeDtypeStruct((nbatch, KV_HEADS, 8, 2 * n_pages), F32)],
        grid=(nbatch,),
        in_specs=[pl.BlockSpec((1, KV_HEADS, 32, HEAD_DIM), lambda b: (b, 0, 0, 0)),
                  pl.BlockSpec((n_pages, 512), lambda b: (b, 0)),
                  pl.BlockSpec((1, 1, 512), lambda b: (b, 0, 0))],
        out_specs=[pl.BlockSpec((1, KV_HEADS, 32, HEAD_DIM), lambda b: (b, 0, 0, 0)),
                   pl.BlockSpec((1, KV_HEADS, 8, 2 * n_pages), lambda b: (b, 0, 0, 0))],
        compiler_params=pltpu.CompilerParams(dimension_semantics=("arbitrary",), vmem_limit_bytes=VMEM_LIMIT),
        name="s_cmp",
    )(q_r, pool_seq, newc)


def _s_select_kernel(imp_ref, idx_ref, *, n_pages, n_pick):
    v = imp_ref[...]
    rows = v.shape[0]
    lane = lax.broadcasted_iota(jnp.int32, (1, 2 * n_pages), 1)
    blk = (2 * (lane & (n_pages - 1)) + (lane >> 7)).astype(F32)
    out_lane = lax.broadcasted_iota(jnp.int32, (1, 128), 1)
    out = jnp.full((rows, 128), float(2 * n_pages), F32)
    for k in range(n_pick):
        m = jnp.max(v, axis=-1, keepdims=True)
        pick = jnp.min(jnp.where(v == m, blk, 1e9), axis=-1, keepdims=True)
        out = jnp.where(out_lane == k, pick, out)
        v = jnp.where(blk == pick, NEG, v)
    idx_ref[...] = out.astype(jnp.int32)


def _s_select_call(imp2d, *, n_pages, n_pick):
    rows = imp2d.shape[0]
    return pl.pallas_call(
        functools.partial(_s_select_kernel, n_pages=n_pages, n_pick=n_pick),
        out_shape=jax.ShapeDtypeStruct((rows, 128), jnp.int32),
        grid=(1,),
        in_specs=[pl.BlockSpec(imp2d.shape, lambda i: (0, 0))],
        out_specs=pl.BlockSpec((rows, 128), lambda i: (0, 0)),
        compiler_params=pltpu.CompilerParams(dimension_semantics=("arbitrary",), vmem_limit_bytes=VMEM_LIMIT),
        name="s_select",
    )(imp2d)


def _s_attn_kernel(pt_ref, idx_ref, q_ref, kvn_ref, win_ref, gz_ref, ocmp_ref, cache_ref, yb_ref, wnext_ref,
                   kvbuf, sem, *, n_tok, n_gather, past_len):
    b = pl.program_id(0)
    span = n_gather * 128
    slot = b & 1

    def tile_copies(bb, sl, t, h, jj):
        n = idx_ref[((bb * KV_HEADS + h) * n_tok + t) * N_SELECT + jj]
        page = pt_ref[bb, n >> 1]
        return [pltpu.make_async_copy(cache_ref.at[page, 2 + kv, h],
                                      kvbuf.at[sl, h, t, kv, :, pl.ds(jj * 128, 128)], sem.at[sl])
                for kv in range(2)]

    def for_all_tiles(bb, sl, fn):
        for h in range(KV_HEADS):
            for t in range(n_tok):
                for jj in range(n_gather):
                    for cp in tile_copies(bb, sl, t, h, jj):
                        fn(cp)

    @pl.when(b == 0)
    def _():
        for_all_tiles(0, 0, lambda cp: cp.start())

    @pl.when(b + 1 < pl.num_programs(0))
    def _():
        for_all_tiles(b + 1, 1 - slot, lambda cp: cp.start())

    rows = 8 * GQA
    t_row = lax.broadcasted_iota(jnp.int32, (rows, 1), 0) & 7
    gz = gz_ref[0]

    lane_n = lax.broadcasted_iota(jnp.int32, (1, 128), 1)
    tok_shift = n_tok.bit_length() - 1
    new_ok = ((lane_n >> tok_shift) == b) & ((lane_n & (n_tok - 1)) <= t_row)
    lane_w = lax.broadcasted_iota(jnp.int32, (1, WINDOW), 1)
    win_ok = lane_w > t_row

    shifted = pltpu.roll(win_ref[0], WINDOW - n_tok, 1)
    moved = pltpu.roll(kvn_ref[4 * 128:, :], (128 - n_tok) - b * n_tok, 1)
    wnext_ref[0, :, :WINDOW - 128] = shifted[:, :WINDOW - 128]
    wnext_ref[0, :, WINDOW - 128:] = jnp.where(lane_n >= 128 - n_tok, moved, shifted[:, WINDOW - 128:])

    def softmax_pv(parts):
        m = None
        for sc, mk, _ in parts:
            mm = jnp.max(jnp.where(mk, sc, NEG), axis=-1, keepdims=True)
            m = mm if m is None else jnp.maximum(m, mm)
        den = jnp.zeros((rows, 1), F32)
        o = jnp.zeros((rows, HEAD_DIM), F32)
        for sc, mk, vts in parts:
            e = jnp.where(mk, jnp.exp(jnp.where(mk, sc, NEG) - m), 0.0)
            den = den + jnp.sum(e, axis=-1, keepdims=True)
            for rmask, vt in vts:
                er = e if rmask is None else jnp.where(rmask, e, 0.0)
                o = o + _dot_nt(er.astype(BF16), vt)
        return o / jnp.maximum(den, 1e-30)

    win_out = []
    for h in range(KV_HEADS):
        qh = q_ref[0, h]
        kw = win_ref[0, h * 64:(h + 1) * 64, :].astype(BF16)
        vw = win_ref[0, 128 + h * 64:128 + (h + 1) * 64, :].astype(BF16)
        kwn = kvn_ref[4 * 128 + h * 64:4 * 128 + (h + 1) * 64, :].astype(BF16)
        vwn = kvn_ref[5 * 128 + h * 64:5 * 128 + (h + 1) * 64, :].astype(BF16)
        win_out.append(softmax_pv([(_dot(qh, kw), win_ok, [(None, vw)]),
                                   (_dot(qh, kwn), new_ok, [(None, vwn)])]))

    for_all_tiles(b, slot, lambda cp: cp.wait())

    lane_s = lax.broadcasted_iota(jnp.int32, (1, span), 1)
    for h in range(KV_HEADS):
        qh = q_ref[0, h]
        sc = jnp.zeros((rows, span), F32)
        half = jnp.zeros((rows, span), jnp.int32)
        for t in range(n_tok):
            st = _dot(qh, kvbuf[slot, h, t, 0].astype(BF16))
            hrow = jnp.zeros((1, span), jnp.int32)
            for jj in range(n_gather):
                n = idx_ref[((b * KV_HEADS + h) * n_tok + t) * N_SELECT + jj]
                hrow = jnp.where((lane_s >> 7) == jj, n & 1, hrow)
            sc = jnp.where(t_row == t, st, sc)
            half = jnp.where(t_row == t, hrow, half)
        ok = ((lane_s >> 6) & 1) == half
        ksn = kvn_ref[2 * 128 + h * 64:2 * 128 + (h + 1) * 64, :].astype(BF16)
        vsn = kvn_ref[3 * 128 + h * 64:3 * 128 + (h + 1) * 64, :].astype(BF16)
        vts = [(t_row == t, kvbuf[slot, h, t, 1].astype(BF16)) for t in range(n_tok)]
        o_sel = softmax_pv([(sc, ok, vts), (_dot(qh, ksn), new_ok, [(None, vsn)])])
        yb_ref[0, h] = (gz[h, :, 0:1] * ocmp_ref[0, h] + gz[h, :, 1:2] * o_sel + gz[h, :, 2:3] * win_out[h])


def _s_attn_call(page_table, idx_flat, q_r, kvT_new, winT, gz_r, ocmp, cache_pages, *, n_tok, past_len):
    nbatch = page_table.shape[0]
    n_gather = N_SELECT - 1
    span = n_gather * 128
    blk4 = lambda b, pt, ix: (b, 0, 0, 0)
    return pl.pallas_call(
        functools.partial(_s_attn_kernel, n_tok=n_tok, n_gather=n_gather, past_len=past_len),
        out_shape=[jax.ShapeDtypeStruct((nbatch, KV_HEADS, 32, HEAD_DIM), F32),
                   jax.ShapeDtypeStruct((nbatch, 256, WINDOW), F32)],
        grid_spec=pltpu.PrefetchScalarGridSpec(
            num_scalar_prefetch=2,
            grid=(nbatch,),
            in_specs=[pl.BlockSpec((1, KV_HEADS, 32, HEAD_DIM), blk4),
                      pl.BlockSpec(kvT_new.shape, lambda b, pt, ix: (0, 0)),
                      pl.BlockSpec((1, 256, WINDOW), lambda b, pt, ix: (b, 0, 0)),
                      pl.BlockSpec((1, KV_HEADS, 32, 128), blk4),
                      pl.BlockSpec((1, KV_HEADS, 32, HEAD_DIM), blk4),
                      pl.BlockSpec(memory_space=pl.ANY)],
            out_specs=[pl.BlockSpec((1, KV_HEADS, 32, HEAD_DIM), blk4),
                       pl.BlockSpec((1, 256, WINDOW), lambda b, pt, ix: (b, 0, 0))],
            scratch_shapes=[pltpu.VMEM((2, KV_HEADS, n_tok, 2, HEAD_DIM, span), F32),
                            pltpu.SemaphoreType.DMA((2,))]),
        compiler_params=pltpu.CompilerParams(dimension_semantics=("arbitrary",), vmem_limit_bytes=VMEM_LIMIT),
        name="s_attn",
    )(page_table, idx_flat, q_r, kvT_new, winT, gz_r, ocmp, cache_pages)


def _post_kernel(*refs, tm, seq_len, has_prev):
    it = iter(refs)
    x_ref, mod_ref, ya_ref, yb_ref = next(it), next(it), next(it), next(it)
    g1_ref, g2_ref, gf_ref = next(it), next(it), next(it)
    wm_ref, wba_ref, wbb_ref, wo_ref, wup_ref, wdn_ref, cw_ref = (next(it) for _ in range(7))
    p1_ref = next(it) if has_prev else None
    p2_ref = next(it) if has_prev else None
    y_ref, up_ref = next(it), next(it)
    carry, h2_s, act_s = next(it), next(it), next(it)

    i = pl.program_id(1)
    x = x_ref[0]
    shift1, scale1, gate1 = mod_ref[0, 0], mod_ref[0, 1], mod_ref[0, 2]
    shift2, scale2, gate2 = mod_ref[0, 3], mod_ref[0, 4], mod_ref[0, 5]

    hb = _rms_mod(x, g1_ref[...], scale1, shift1).astype(BF16)
    gates = jax.nn.sigmoid(_dot(hb, wm_ref[...]))
    mix = (gates[:, :D_MODEL] * _dot(ya_ref[0], wba_ref[...])
           + gates[:, D_MODEL:] * _dot(yb_ref[0], wbb_ref[...]))
    x1 = x + gate1 * _dot(mix.astype(BF16), wo_ref[...])
    h2_s[...] = _rms_mod(x1, g2_ref[...], scale2, shift2).astype(BF16)

    if not has_prev:
        @pl.when(i == 0)
        def _():
            carry[...] = jnp.zeros(carry.shape, F32)

    row = lax.broadcasted_iota(jnp.int32, (tm, 1), 0)
    row8 = lax.broadcasted_iota(jnp.int32, (8, 1), 0)

    def conv_cols(col0):
        cols = slice(col0, col0 + FF_CHUNK)
        up = _dot(h2_s[...], wup_ref[:, cols])
        r1 = pltpu.roll(up, 1, 0)
        r2 = pltpu.roll(up, 2, 0)
        if has_prev:
            s1 = jnp.where((row & (seq_len - 1)) == 0, p1_ref[:, cols], r1)
            s2 = jnp.where((row & (seq_len - 1)) < 2, p2_ref[:, cols], r2)
            up_ref[:, cols] = up
        else:
            prev = carry[:, cols]
            t1 = jnp.where(row8 == 0, pltpu.roll(prev, 1, 0), r1[:8])
            t2 = jnp.where(row8 < 2, pltpu.roll(prev, 2, 0), r2[:8])
            s1 = jnp.concatenate([t1, r1[8:]], axis=0)
            s2 = jnp.concatenate([t2, r2[8:]], axis=0)
            carry[:, cols] = up[tm - 8:]
            up_ref[0, :, cols] = up[tm - 8:]
        cw = cw_ref[:, cols]
        return cw[3:4] + cw[0:1] * s2 + cw[1:2] * s1 + cw[2:3] * up

    for c in range(N_FF_CHUNKS):
        a = conv_cols(c * FF_CHUNK)
        gv = conv_cols(D_FF + c * FF_CHUNK)
        act_s[:, c * FF_CHUNK:(c + 1) * FF_CHUNK] = (jax.nn.gelu(a) * gv).astype(BF16)
    x2 = x1 + gate2 * _dot(act_s[...], wdn_ref[...])
    y_ref[0] = x2 * lax.rsqrt(jnp.mean(x2 * x2, axis=-1, keepdims=True) + EPS) * gf_ref[...]


def _post_call(x, mod, ya, yb, g1, g2, gf, wm, wba, wbb, wo, wup, wdn, cw, prev=None, *, tm, seq_len):
    G, T, _ = x.shape
    R = mod.shape[2]
    nt = T // tm
    has_prev = prev is not None
    single = pl.Buffered(1)
    c2 = lambda b, i: (0, 0)
    c3 = lambda b, i: (0, 0, 0)
    in_specs = [pl.BlockSpec((1, tm, D_MODEL), lambda b, i: (b, i, 0)),
                pl.BlockSpec((1, 6, R, D_MODEL), lambda b, i: (b, 0, 0, 0)),
                pl.BlockSpec((1, tm, A_WIDTH), lambda b, i: (b, i, 0)),
                pl.BlockSpec((1, tm, B_WIDTH), lambda b, i: (b, i, 0)),
                pl.BlockSpec((1, D_MODEL), c2),
                pl.BlockSpec((1, D_MODEL), c2),
                pl.BlockSpec((1, D_MODEL), c2),
                pl.BlockSpec(wm.shape, c2, pipeline_mode=single),
                pl.BlockSpec(wba.shape, c2, pipeline_mode=single),
                pl.BlockSpec(wbb.shape, c2, pipeline_mode=single),
                pl.BlockSpec(wo.shape, c2, pipeline_mode=single),
                pl.BlockSpec(wup.shape, c2, pipeline_mode=single),
                pl.BlockSpec(wdn.shape, c2, pipeline_mode=single),
                pl.BlockSpec(cw.shape, c2, pipeline_mode=single)]
    args = [x, mod, ya, yb, g1, g2, gf, wm, wba, wbb, wo, wup, wdn, cw]
    if has_prev:
        in_specs += [pl.BlockSpec(prev[0].shape, c2), pl.BlockSpec(prev[1].shape, c2)]
        args += list(prev)
        up_shape = jax.ShapeDtypeStruct((T, F2), F32)
        up_spec = pl.BlockSpec((tm, F2), lambda b, i: (i, 0))
    else:
        up_shape = jax.ShapeDtypeStruct((G, 8, F2), F32)
        up_spec = pl.BlockSpec((1, 8, F2), lambda b, i: (b, 0, 0))
    return pl.pallas_call(
        functools.partial(_post_kernel, tm=tm, seq_len=seq_len, has_prev=has_prev),
        out_shape=[jax.ShapeDtypeStruct((G, T, D_MODEL), F32), up_shape],
        grid=(G, nt),
        in_specs=in_specs,
        out_specs=[pl.BlockSpec((1, tm, D_MODEL), lambda b, i: (b, i, 0)), up_spec],
        scratch_shapes=[pltpu.VMEM((8, F2), F32),
                        pltpu.VMEM((tm, D_MODEL), BF16),
                        pltpu.VMEM((tm, D_FF), BF16)],
        compiler_params=pltpu.CompilerParams(dimension_semantics=("arbitrary", "arbitrary"),
                                             vmem_limit_bytes=VMEM_LIMIT),
        name="post",
    )(*args)


def _block_diag2(m):
    z = jnp.zeros_like(m)
    return jnp.concatenate([jnp.concatenate([m, z], axis=-1), jnp.concatenate([z, m], axis=-1)], axis=-2)


def kernel(x_prompt, x_sample, cache_kv, state_kv_win, state_ffn_conv, page_table, c_prompt, c_sample, w_ada, b_ada, g_norm1, w_in, ln_v_g, ln_v_b, w_spatial, b_spatial, cmp_pe, cmp_w1, cmp_b1, cmp_w2, cmp_b2, w_branch_a, w_branch_b, w_out, g_norm2, w_up, w_conv, b_conv, w_down, g_final):
    B, T, _ = x_prompt.shape
    SB, ST, _ = x_sample.shape
    n_pool, page_size = cache_kv.shape[1], cache_kv.shape[2]
    n_pages = page_table.shape[1]
    past_len = n_pages * page_size
    lbuf = state_kv_win.shape[2]
    assert cache_kv.shape[0] == 1 and page_size == 128 and lbuf == WINDOW and SB * ST == 128 and ST <= 8
    assert past_len % BLOCK == 0 and n_pages == 128 and T % 512 == 0 and ST & (ST - 1) == 0 and ST >= 2

    win = w_in[0]
    wa = win[:, :2 * A_WIDTH].astype(BF16)
    o = 2 * A_WIDTH
    wq = win[:, o:o + B_WIDTH].astype(BF16)
    o += B_WIDTH
    wkv = win[:, o:o + KV_COLS]
    wkvT = wkv.T.astype(BF16)
    wkvc = wkv[:, :256].astype(BF16)
    o += KV_COLS
    wg = jnp.pad(win[:, o:o + 3 * B_HEADS], ((0, 0), (0, 128 - 3 * B_HEADS))).astype(BF16)
    wgT = jnp.pad(win[:, o:o + 3 * B_HEADS].T, ((0, 32 - 3 * B_HEADS), (0, 0))).astype(BF16)
    wqT = wq.T
    wrow = jnp.concatenate([wkv[:, :3 * 128], wkv[:, 4 * 128:5 * 128]], axis=1).astype(BF16)
    o += 3 * B_HEADS
    wm = win[:, o:].astype(BF16)
    g1 = g_norm1[0][None]
    g2 = g_norm2[0][None]
    gf = g_final[None]
    lng = ln_v_g[0][None]
    lnb = ln_v_b[0][None]

    ws = w_spatial[0]
    bs = b_spatial[0]
    wsp_p = jnp.tril(ws).astype(BF16)
    bsp_p = jnp.repeat(bs.T, 128, axis=1)
    w4 = jnp.tril(ws[:, :ST, :ST])
    seq_of = jnp.arange(128, dtype=jnp.int32) // ST
    same_seq = seq_of[:, None] == seq_of[None, :]
    wsp_s = jnp.where(same_seq, jnp.tile(w4, (1, 128 // ST, 128 // ST)), 0.0).astype(BF16)
    bsp_s = jnp.repeat(jnp.tile(bs[:, :ST].T, (128 // ST, 1)), 128, axis=1)

    w1, pe = cmp_w1[0].astype(BF16), cmp_pe[0]
    w1_rows = _block_diag2(w1)
    pe_rows = jnp.tile(pe, (1, 1, 2))
    w1_pages = _block_diag2(w1.transpose(0, 2, 1, 3)).reshape(2, HEAD_DIM // 2, 256, 128)
    pe_pages = jnp.tile(pe.transpose(0, 2, 1), (1, 1, 2)).reshape(2, HEAD_DIM // 2, 1, 256)
    pe_pages = jnp.pad(pe_pages, ((0, 0), (0, 0), (0, PE_ROWS - 1), (0, 0))).astype(BF16)
    b1t = jnp.tile(cmp_b1[0], (1, 2))[:, None, :]
    b2t = jnp.tile(cmp_b2[0], (1, 2))[:, None, :]
    w2bd = _block_diag2(cmp_w2[0].astype(BF16))

    wba = w_branch_a[0].astype(BF16)
    wbb = w_branch_b[0].astype(BF16)
    wo = w_out[0].astype(BF16)
    wup = w_up[0].astype(BF16)
    wdn = w_down[0].astype(BF16)
    cw = jnp.concatenate([w_conv[0], b_conv[0][None], jnp.zeros((4, F2), F32)], axis=0)

    mod = _mod_call(jnp.concatenate([c_prompt, c_sample], axis=0), w_ada[0], b_ada)
    mod_p = mod[:B].reshape(B, 6, 1, D_MODEL)
    mod_s = jnp.repeat(mod[B:], ST, axis=0).reshape(SB * ST, 6, D_MODEL).transpose(1, 0, 2)[None]

    ya_p, qT_p, kv4T_p, winT_p, vselT_p, vwinT_p, kvc_p, ksel_p, kwin_p, gzT_p = _inproj_call(
        x_prompt, mod_p, g1, wa, lng, lnb, wsp_p, bsp_p, (wqT, wkvT, wrow, wgT), tm=512, prompt=True)
    xs2 = x_sample.reshape(1, SB * ST, D_MODEL)
    ya_s, q_s, kvT_s, kvc_s, gz_s, vn_s = _inproj_call(
        xs2, mod_s, g1, wa, lng, lnb, wsp_s, bsp_s, (wq, wkvT, wkvc, wg), tm=128, prompt=False)

    nb = T // BLOCK
    cmp_p = _cmp_rows_call(kvc_p.reshape(B * T, 256), pe_rows, w1_rows, b1t, w2bd, b2t, R=128)
    kc_p = cmp_p[0].reshape(B, nb, 128)
    vcT_p = cmp_p[1].reshape(B, nb, 128).transpose(0, 2, 1)
    yb_p = _p_attn_call(qT_p, ksel_p, kwin_p, vselT_p, vwinT_p, kc_p, vcT_p, gzT_p, tq=512, kc=512)

    cache_pages = jnp.transpose(cache_kv[0], (0, 2, 3, 4, 1)).reshape(n_pool, 512, page_size)
    pool_seq = _cmp_gather_call(page_table.reshape(-1), cache_pages, pe_pages, w1_pages, b1t, w2bd, b2t,
                                Pt=64)
    kv_rows_s = jnp.transpose(kvT_s[0], (1, 0)).reshape(SB, ST, KV_COLS)
    newblk = jnp.pad(jnp.transpose(kv_rows_s[:, :, :256], (0, 2, 1)), ((0, 0), (0, 0), (0, page_size - ST)))
    newc = _cmp_pages_call(newblk, pe_pages, w1_pages, b1t, w2bd, b2t, Pt=SB)

    q_r = jnp.pad(q_s.reshape(SB, ST, KV_HEADS, GQA, HEAD_DIM).transpose(0, 2, 3, 1, 4),
                  ((0, 0), (0, 0), (0, 0), (0, 8 - ST), (0, 0))).reshape(SB, KV_HEADS, 8 * GQA, HEAD_DIM)
    ocmp_s, imp_s = _s_cmp_call(q_r, pool_seq, newc.reshape(SB, 1, 512), n_pages=n_pages, past_len=past_len)
    idx = _s_select_call(imp_s.reshape(SB * KV_HEADS * 8, 2 * n_pages), n_pages=n_pages, n_pick=N_SELECT - 1)
    idx_flat = idx.reshape(SB, KV_HEADS, 8, 128)[:, :, :ST, :N_SELECT].reshape(-1)
    winT = jnp.transpose(state_kv_win[0], (0, 2, 3, 4, 1)).reshape(SB, 256, lbuf)
    gz_r = jnp.pad(gz_s[0, :, :24].reshape(SB, ST, 3, KV_HEADS, GQA).transpose(0, 3, 4, 1, 2),
                   ((0, 0), (0, 0), (0, 0), (0, 8 - ST), (0, 128 - 3))).reshape(SB, KV_HEADS, 8 * GQA, 128)
    cache_tiles = cache_pages.reshape(n_pool, 4, KV_HEADS, HEAD_DIM, page_size)
    yb_r, wnextT = _s_attn_call(page_table, idx_flat, q_r, kvT_s[0], winT, gz_r, ocmp_s, cache_tiles,
                        n_tok=ST, past_len=past_len)
    yb_s = yb_r.reshape(SB, KV_HEADS, GQA, 8, HEAD_DIM)[:, :, :, :ST].transpose(0, 3, 1, 2, 4)
    yb_s = yb_s.reshape(1, SB * ST, B_WIDTH).astype(BF16)

    y_p, up_p = _post_call(x_prompt, mod_p, ya_p, yb_p, g1, g2, gf, wm, wba, wbb, wo, wup, wdn, cw,
                           tm=512, seq_len=T)
    st = state_ffn_conv[0]
    zrow = jnp.zeros((SB, 1, F2), F32)
    p1 = jnp.concatenate([st[:, 1:2], jnp.tile(zrow, (1, ST - 1, 1))], axis=1).reshape(SB * ST, F2)
    p2 = jnp.concatenate([st, jnp.tile(zrow, (1, ST - 2, 1))], axis=1).reshape(SB * ST, F2)
    y_s, up_s = _post_call(xs2, mod_s, ya_s, yb_s, g1, g2, gf, wm, wba, wbb, wo, wup, wdn, cw,
                           prev=(p1, p2),
                           tm=SB * ST, seq_len=ST)

    kv_prompt = jnp.transpose(kv4T_p.reshape(B, 4, KV_HEADS, HEAD_DIM, T), (0, 4, 1, 2, 3))[None]
    win_prompt = jnp.transpose(winT_p[:, :, T - WINDOW:].reshape(B, 2, KV_HEADS, HEAD_DIM, WINDOW),
                               (0, 4, 1, 2, 3))[None]
    kv_sample = kv_rows_s[:, :, :512].reshape(SB, ST, 4, KV_HEADS, HEAD_DIM)[None]
    win_sample = jnp.transpose(wnextT.reshape(SB, 2, KV_HEADS, HEAD_DIM, lbuf), (0, 4, 1, 2, 3))[None]
    v_chunk = vn_s.reshape(SB, ST, A_WIDTH)[None]
    conv_prompt = up_p[:, 6:8][None]
    conv_sample = up_s.reshape(SB, ST, F2)[:, ST - 2:][None]
    return (y_p, y_s.reshape(SB, ST, D_MODEL), kv_prompt, kv_sample, win_prompt, win_sample, v_chunk,
            conv_prompt, conv_sample)
```

```python
import functools

import jax
import jax.numpy as jnp
from jax import lax
from jax.experimental import pallas as pl
from jax.experimental.pallas import tpu as pltpu

F32 = jnp.float32
BF16 = jnp.bfloat16

D_MODEL = 1024
A_WIDTH = 512
A_GROUPS = 4
CHUNK = 128
B_HEADS = 8
HEAD_DIM = 64
B_WIDTH = 512
KV_HEADS = 2
GQA = 4
BLOCK = 64
N_SELECT = 16
WINDOW = 512
N_KV_SLOTS = 6
KV_COLS = 768
D_FF = 2816
F2 = 2 * D_FF
CONV_W = 3
EPS = 1e-6
NEG = -1e30
FORCE_SCORE = 1e4

FF_CHUNK = 256
N_FF_CHUNKS = D_FF // FF_CHUNK
ROW_BLOCK = 64
PE_ROWS = 16
SKEW = 2
ACC_ROWS = HEAD_DIM + 16
LOG2E = 1.4426950408889634
VMEM_LIMIT = 56 * 1024 * 1024


def _dot(a, b):
    return jnp.dot(a, b, preferred_element_type=F32)


def _dot_nt(a, b):
    return lax.dot_general(a, b, (((1,), (1,)), ((), ())), preferred_element_type=F32)


def _rms_mod(x, g, scale, shift):
    y = x * lax.rsqrt(jnp.mean(x * x, axis=-1, keepdims=True) + EPS)
    return (y * g) * (1.0 + scale) + shift


def _mod_kernel(c_ref, w_ref, b_ref, o_ref):
    c = c_ref[...]
    s = c * jax.nn.sigmoid(c)
    o_ref[...] = _dot(s.astype(BF16), w_ref[...].astype(BF16)) + b_ref[...]


def _mod_call(c_all, w_ada, b_ada):
    n = c_all.shape[0]
    tn = 1536
    return pl.pallas_call(
        _mod_kernel,
        out_shape=jax.ShapeDtypeStruct((n, 6 * D_MODEL), F32),
        grid=(6 * D_MODEL // tn,),
        in_specs=[pl.BlockSpec((n, D_MODEL), lambda j: (0, 0)),
                  pl.BlockSpec((D_MODEL, tn), lambda j: (0, j)),
                  pl.BlockSpec((1, tn), lambda j: (0, j))],
        out_specs=pl.BlockSpec((n, tn), lambda j: (0, j)),
        compiler_params=pltpu.CompilerParams(dimension_semantics=("arbitrary",), vmem_limit_bytes=VMEM_LIMIT),
        name="mod",
    )(c_all, w_ada, b_ada)


def _inproj_kernel(x_ref, mod_ref, g1_ref, wa_ref, lng_ref, lnb_ref, wsp_ref, bsp_ref, *refs, tm, prompt):
    x = x_ref[0]
    hb = _rms_mod(x, g1_ref[...], mod_ref[0, 1], mod_ref[0, 0]).astype(BF16)

    ga = jax.nn.gelu(_dot(hb, wa_ref[...]))
    u = ga[:, :A_WIDTH]
    v = ga[:, A_WIDTH:]
    vc = v - jnp.mean(v, axis=-1, keepdims=True)
    vn = vc * lax.rsqrt(jnp.mean(vc * vc, axis=-1, keepdims=True) + EPS) * lng_ref[...] + lnb_ref[...]
    vnb = vn.astype(BF16)

    if prompt:
        (wqT_ref, wkvT_ref, wgT_ref,
         ya_ref, qT_ref, kv4T_ref, winT_ref, vselT_ref, vwinT_ref, kvc_ref, ksel_ref, kwin_ref, gzT_ref) = refs
    else:
        (wq_ref, wkvT_ref, wkvc_ref, wg_ref, ya_ref, q_ref, kvT_ref, kvc_ref, gz_ref, vn_ref) = refs
        vn_ref[0] = vn

    for c in range(tm // CHUNK):
        rows = slice(c * CHUNK, (c + 1) * CHUNK)
        s = jnp.concatenate(
            [_dot(wsp_ref[g], vnb[rows, g * 128:(g + 1) * 128]) for g in range(A_GROUPS)], axis=1)
        ya_ref[0, rows, :] = (u[rows] * (s + bsp_ref[...])).astype(BF16)

    kvT = _dot_nt(wkvT_ref[...], hb)
    if prompt:
        qT_ref[0] = (_dot_nt(wqT_ref[...], hb) * (HEAD_DIM ** -0.5 * LOG2E)).astype(BF16)
        kv4T_ref[0] = kvT[:4 * 128]
        winT_ref[0] = kvT[4 * 128:]
        vselT_ref[0] = kvT[3 * 128:4 * 128].astype(BF16)
        vwinT_ref[0] = kvT[5 * 128:].astype(BF16)
        kvc_ref[0] = kvT[:2 * 128].T
        ksel_ref[0] = kvT[2 * 128:3 * 128].T.astype(BF16)
        kwin_ref[0] = kvT[4 * 128:5 * 128].T.astype(BF16)
        gzT_ref[0] = jax.nn.sigmoid(_dot_nt(wgT_ref[...], hb))
    else:
        q_ref[0] = (_dot(hb, wq_ref[...]) * (HEAD_DIM ** -0.5)).astype(BF16)
        kvT_ref[0] = kvT
        kvc_ref[0] = _dot(hb, wkvc_ref[...])
        gz_ref[0] = jax.nn.sigmoid(_dot(hb, wg_ref[...]))


def _inproj_call(x, mod, g1, wa, lng, lnb, wsp, bsp, proj_w, *, tm, prompt):
    G, T, _ = x.shape
    R = mod.shape[2]
    nt = T // tm
    const2 = lambda b, i: (0, 0)
    const3 = lambda b, i: (0, 0, 0)
    rows = lambda w: pl.BlockSpec((1, tm, w), lambda b, i: (b, i, 0))
    cols = lambda h: pl.BlockSpec((1, h, tm), lambda b, i: (b, 0, i))
    if prompt:
        outs = [((G, T, A_WIDTH), BF16, rows(A_WIDTH)),
                ((G, B_WIDTH, T), BF16, cols(B_WIDTH)),
                ((G, 512, T), F32, cols(512)),
                ((G, 256, T), F32, cols(256)),
                ((G, 128, T), BF16, cols(128)),
                ((G, 128, T), BF16, cols(128)),
                ((G, T, 256), F32, rows(256)),
                ((G, T, 128), BF16, rows(128)),
                ((G, T, 128), BF16, rows(128)),
                ((G, 32, T), F32, cols(32))]
    else:
        outs = [((G, T, A_WIDTH), BF16, rows(A_WIDTH)),
                ((G, T, B_WIDTH), BF16, rows(B_WIDTH)),
                ((G, KV_COLS, T), F32, cols(KV_COLS)),
                ((G, T, 256), F32, rows(256)),
                ((G, T, 128), F32, rows(128)),
                ((G, T, A_WIDTH), F32, rows(A_WIDTH))]
    return pl.pallas_call(
        functools.partial(_inproj_kernel, tm=tm, prompt=prompt),
        out_shape=[jax.ShapeDtypeStruct(s, d) for s, d, _ in outs],
        grid=(G, nt),
        in_specs=[pl.BlockSpec((1, tm, D_MODEL), lambda b, i: (b, i, 0)),
                  pl.BlockSpec((1, 6, R, D_MODEL), lambda b, i: (b, 0, 0, 0)),
                  pl.BlockSpec((1, D_MODEL), const2),
                  pl.BlockSpec(wa.shape, const2),
                  pl.BlockSpec((1, A_WIDTH), const2),
                  pl.BlockSpec((1, A_WIDTH), const2),
                  pl.BlockSpec(wsp.shape, const3),
                  pl.BlockSpec(bsp.shape, const2)] + [pl.BlockSpec(w.shape, const2) for w in proj_w],
        out_specs=[sp for _, _, sp in outs],
        compiler_params=pltpu.CompilerParams(dimension_semantics=("arbitrary", "arbitrary"),
                                             vmem_limit_bytes=VMEM_LIMIT),
        name="inproj",
    )(x, mod, g1, wa, lng, lnb, wsp, bsp, *proj_w)


def _cmp_tail(acc, b1, w2, b2):
    return _dot(jax.nn.gelu(acc + b1).astype(BF16), w2) + b2


def _cmp_rows_kernel(x_ref, pe_ref, w1_ref, b1_ref, w2_ref, b2_ref, o_ref, *, R):
    acc = jnp.zeros((R, 128), F32)
    for j in range(BLOCK):
        xj = x_ref[pl.ds(j, R, stride=BLOCK), :] + pe_ref[0, j:j + 1, :]
        acc = acc + _dot(xj.astype(BF16), w1_ref[0, j])
    o_ref[0] = _cmp_tail(acc, b1_ref[0], w2_ref[0], b2_ref[0])


def _cmp_rows_call(kvc2d, pe, w1, b1, w2, b2, *, R):
    nrows = kvc2d.shape[0] // BLOCK
    return pl.pallas_call(
        functools.partial(_cmp_rows_kernel, R=R),
        out_shape=jax.ShapeDtypeStruct((2, nrows, 128), F32),
        grid=(2, nrows // R),
        in_specs=[pl.BlockSpec((R * BLOCK, 128), lambda s, r: (r, s)),
                  pl.BlockSpec((1, BLOCK, 128), lambda s, r: (s, 0, 0)),
                  pl.BlockSpec((1, BLOCK, 128, 128), lambda s, r: (s, 0, 0, 0)),
                  pl.BlockSpec((1, 1, 128), lambda s, r: (s, 0, 0)),
                  pl.BlockSpec((1, 128, 128), lambda s, r: (s, 0, 0)),
                  pl.BlockSpec((1, 1, 128), lambda s, r: (s, 0, 0))],
        out_specs=pl.BlockSpec((1, R, 128), lambda s, r: (s, r, 0)),
        compiler_params=pltpu.CompilerParams(dimension_semantics=("arbitrary", "arbitrary"),
                                             vmem_limit_bytes=VMEM_LIMIT),
        name="cmp_rows",
    )(kvc2d, pe, w1, b1, w2, b2)


def _compress_pages(load_rows, pe_ref, w1_ref, b1_ref, w2_ref, b2_ref, o_ref, Pt):
    for s in range(2):
        acc = jnp.zeros((2 * Pt + PE_ROWS, 128), F32)
        for dg in range(HEAD_DIM // 8):
            r0 = s * 128 + dg * 8
            xa = jnp.swapaxes(load_rows(r0), 0, 1).astype(BF16)
            xb = jnp.swapaxes(load_rows(HEAD_DIM + r0), 0, 1).astype(BF16)
            for dd in range(0, 8, 2):
                dp = dg * 4 + dd // 2
                xd = jnp.concatenate([jnp.concatenate([xa[dd], xa[dd + 1]], axis=1),
                                      jnp.concatenate([xb[dd], xb[dd + 1]], axis=1),
                                      pe_ref[s, dp]], axis=0)
                acc = acc + _dot(xd, w1_ref[s, dp])
        pre = acc[:2 * Pt] + acc[2 * Pt:2 * Pt + 1]
        r = _cmp_tail(pre, b1_ref[s], w2_ref[s], b2_ref[s])
        o_ref[:, (2 * s) * 128:(2 * s + 1) * 128] = r[:Pt]
        o_ref[:, (2 * s + 1) * 128:(2 * s + 2) * 128] = r[Pt:]


def _cmp_pages_kernel(x_ref, pe_ref, w1_ref, b1_ref, w2_ref, b2_ref, o_ref, *, Pt):
    _compress_pages(lambda r0: x_ref[:, r0:r0 + 8, :], pe_ref, w1_ref, b1_ref, w2_ref, b2_ref, o_ref, Pt)


def _cmp_weight_specs(ws, index):
    return [pl.BlockSpec(w.shape, functools.partial(index, (0,) * w.ndim)) for w in ws]


def _cmp_pages_call(pages, pe, w1, b1, w2, b2, *, Pt):
    P = pages.shape[0]
    ws = (pe, w1, b1, w2, b2)
    return pl.pallas_call(
        functools.partial(_cmp_pages_kernel, Pt=Pt),
        out_shape=jax.ShapeDtypeStruct((P, 512), F32),
        grid=(P // Pt,),
        in_specs=[pl.BlockSpec((Pt, 256, 128), lambda i: (i, 0, 0))] + _cmp_weight_specs(ws, lambda z, i: z),
        out_specs=pl.BlockSpec((Pt, 512), lambda i: (i, 0)),
        compiler_params=pltpu.CompilerParams(dimension_semantics=("arbitrary",), vmem_limit_bytes=VMEM_LIMIT),
        name="cmp_pages",
    )(pages, *ws)


def _cmp_gather_kernel(pt_ref, cache_ref, pe_ref, w1_ref, b1_ref, w2_ref, b2_ref, o_ref, xbuf, sem, *, Pt):
    i = pl.program_id(0)
    slot = i & 1

    def page_copy(step, sl, k):
        return pltpu.make_async_copy(cache_ref.at[pt_ref[step * Pt + k], pl.ds(0, 256), :],
                                     xbuf.at[sl, k], sem.at[sl])

    @pl.when(i == 0)
    def _():
        for k in range(Pt):
            page_copy(0, 0, k).start()

    @pl.when(i + 1 < pl.num_programs(0))
    def _():
        for k in range(Pt):
            page_copy(i + 1, 1 - slot, k).start()

    for k in range(Pt):
        page_copy(i, slot, k).wait()
    _compress_pages(lambda r0: xbuf[slot, :, r0:r0 + 8, :], pe_ref, w1_ref, b1_ref, w2_ref, b2_ref, o_ref, Pt)


def _cmp_gather_call(pt_flat, cache_pages, pe, w1, b1, w2, b2, *, Pt):
    n = pt_flat.shape[0]
    ws = (pe, w1, b1, w2, b2)
    return pl.pallas_call(
        functools.partial(_cmp_gather_kernel, Pt=Pt),
        out_shape=jax.ShapeDtypeStruct((n, 512), F32),
        grid_spec=pltpu.PrefetchScalarGridSpec(
            num_scalar_prefetch=1,
            grid=(n // Pt,),
            in_specs=[pl.BlockSpec(memory_space=pl.ANY)] + _cmp_weight_specs(ws, lambda z, i, pt: z),
            out_specs=pl.BlockSpec((Pt, 512), lambda i, pt: (i, 0)),
            scratch_shapes=[pltpu.VMEM((2, Pt, 256, 128), F32),
                            pltpu.SemaphoreType.DMA((2,))]),
        compiler_params=pltpu.CompilerParams(dimension_semantics=("arbitrary",), vmem_limit_bytes=VMEM_LIMIT),
        name="cmp_gather",
    )(pt_flat, cache_pages, *ws)


def _pair_schedule(T, tq, kc):
    js, cs = [], []
    for j in range(T // tq):
        for c in range(((j + 1) * tq - 1) // kc + 1):
            js.append(j)
            cs.append(c)
    return js, cs


def _p_attn_kernel(jt_ref, ct_ref, qT_ref, ksel_ref, vselT_ref, kwin_ref, kwinp_ref, vwinT_ref, vwinTp_ref,
                   kc_ref, vcT_ref, gzT_ref, yb_ref,
                   sel_s, neg_s, ocmp_s, m_s, acc_s, s_scr, s2_scr, p_scr, p2_scr, b_scr, k2_s, *, tq, kc, T, strip):
    p = pl.program_id(1)
    j = jt_ref[p]
    c = ct_ref[p]
    nb = T // BLOCK
    N = GQA * tq
    q0 = j * tq
    k0 = c * kc
    c_last = ((j + 1) * tq - 1) // kc

    def q_pad(h):
        qT = jnp.concatenate([qT_ref[0, (h * GQA + g) * HEAD_DIM:(h * GQA + g + 1) * HEAD_DIM, :]
                              for g in range(GQA)], axis=1)
        z = jnp.zeros_like(qT)
        return jnp.concatenate([qT if k == h else z for k in range(KV_HEADS)], axis=0)

    @pl.when(c == 0)
    def _():
        m_s[...] = jnp.full(m_s.shape, NEG, F32)
        acc_s[...] = jnp.zeros(acc_s.shape, F32)
        tok_n = q0 + (lax.broadcasted_iota(jnp.int32, (1, N), 1) & (tq - 1))
        tok_1 = q0 + lax.broadcasted_iota(jnp.int32, (1, tq), 1)
        blk = lax.broadcasted_iota(jnp.int32, (nb, 1), 0)
        avail = (blk + 1) * BLOCK <= tok_n + 1
        cur = tok_1 >> 6
        forced = (blk == 0) | (blk == cur) | (blk == cur - 1)
        future = blk > cur
        kcb = kc_ref[0].astype(BF16)
        for h in range(KV_HEADS):
            s = jnp.where(avail, _dot(kcb, q_pad(h)), NEG)
            m = jnp.max(s, axis=0, keepdims=True)
            e = jnp.where(avail, jnp.exp2(s - m), 0.0)
            pr = e / jnp.maximum(jnp.sum(e, axis=0, keepdims=True), 1e-30)
            ocmp_s[h] = _dot(vcT_ref[0, h * HEAD_DIM:(h + 1) * HEAD_DIM, :].astype(BF16), pr.astype(BF16))
            imp = pr[:, 0:tq]
            for g in range(1, GQA):
                imp = imp + pr[:, g * tq:(g + 1) * tq]
            imp = jnp.where(forced, FORCE_SCORE, jnp.where(future, -1.0, imp))
            groups = [imp[r:r + 8] for r in range(0, nb, 8)]
            ranks = [jnp.zeros((8, tq), F32) for _ in groups]
            for n in range(nb):
                vn = imp[n:n + 1, :]
                for k, grp in enumerate(groups):
                    if 8 * k > n:
                        ahead = vn >= grp
                    elif 8 * k + 7 <= n:
                        ahead = vn > grp
                    else:
                        ahead = (vn > grp) | ((vn == grp) & (blk[8 * k:8 * k + 8] > n))
                    ranks[k] = ranks[k] + ahead.astype(F32)
            sel = jnp.concatenate([(r < float(N_SELECT)).astype(F32) for r in ranks]
                                  + [jnp.zeros((128 - nb, tq), F32)], axis=0)
            sel_s[h] = sel.astype(BF16)
            neg_s[h] = ((sel - 1.0) * -NEG).astype(BF16)

    kpos = k0 + lax.broadcasted_iota(jnp.int32, (kc, tq), 0)
    tok = q0 + lax.broadcasted_iota(jnp.int32, (kc, tq), 1)

    strips = [slice(st * strip, (st + 1) * strip) for st in range(N // strip)]

    row_blocks = [slice(r, r + ROW_BLOCK) for r in range(0, kc, ROW_BLOCK)]

    ones_rows = (lax.broadcasted_iota(jnp.int32, (ACC_ROWS - HEAD_DIM, kc), 0) == 0).astype(BF16)

    streams = [(h, st) for st in range(len(strips)) for h in range(KV_HEADS)]

    def q_strip(h, st):
        g, t0 = divmod(st * strip, tq)
        r0 = (h * GQA + g) * HEAD_DIM
        qT = qT_ref[0, r0:r0 + HEAD_DIM, t0:t0 + strip]
        z = jnp.zeros_like(qT)
        return jnp.concatenate([qT if k == h else z for k in range(KV_HEADS)], axis=0)

    def with_ones(vT_ref, h):
        return jnp.concatenate([vT_ref[0, h * HEAD_DIM:(h + 1) * HEAD_DIM, :], ones_rows], axis=0)

    def run_skewed(stages):
        for k in range(len(streams) + SKEW * (len(stages) - 1)):
            for lag, stage in enumerate(stages):
                if 0 <= k - SKEW * lag < len(streams):
                    stage(*streams[k - SKEW * lag])

    def selected_step(diagonal):
        vT1 = [with_ones(vselT_ref, h) for h in range(KV_HEADS)]
        alpha = {}
        if diagonal:
            keys = ksel_ref[0]
            for h in range(KV_HEADS):
                b_scr[h] = jnp.where((_dot(expand, sel_s[h]) > 0.5) & causal, 0.0, NEG)
        else:
            k2_s[:, :128] = ksel_ref[0]
            k2_s[:, 128:] = expand
            keys = k2_s

        def n_keys(st):
            return min(kc, (st * strip) % tq + strip) if diagonal else kc

        def scores(h, st):
            nk = n_keys(st)
            qp = q_strip(h, st)
            if not diagonal:
                t0 = (st * strip) % tq
                qp = jnp.concatenate([qp, neg_s[h, :, t0:t0 + strip]], axis=0)
            s_scr[h, :nk, strips[st]] = _dot(keys[:nk, :], qp)

        def probs(h, st):
            ls = strips[st]
            t0 = (st * strip) % tq
            rbs = row_blocks[:n_keys(st) // ROW_BLOCK]
            m = None
            for rb in rbs:
                t = s_scr[h, rb, ls]
                if diagonal:
                    t = t + b_scr[h, rb, t0:t0 + strip]
                    s_scr[h, rb, ls] = t
                m = t if m is None else jnp.maximum(m, t)
            m_old = m_s[h, :, ls]
            m_new = jnp.maximum(m_old, jnp.max(m, axis=0, keepdims=True))
            alpha[(h, st)] = jnp.exp2(m_old - m_new)
            m_s[h, :, ls] = m_new
            for rb in rbs:
                p_scr[h, rb, ls] = jnp.exp2(s_scr[h, rb, ls] - m_new).astype(BF16)

        def weighted_values(h, st):
            ls = strips[st]
            nk = n_keys(st)
            acc_s[h, :, ls] = alpha[(h, st)] * acc_s[h, :, ls] + _dot(vT1[h][:, :nk], p_scr[h, :nk, ls])

        run_skewed((scores, probs, weighted_values))

    def window_step():
        no_prev = jnp.where(j == 0, NEG, 0.0)
        vT1 = [with_ones(vwinT_ref, h) for h in range(KV_HEADS)]
        vT1p = [with_ones(vwinTp_ref, h) for h in range(KV_HEADS)]

        def scores(h, st):
            qp = q_strip(h, st)
            s_scr[h, :, strips[st]] = _dot(kwin_ref[0], qp)
            s2_scr[h, :, strips[st]] = _dot(kwinp_ref[0], qp)

        def probs(h, st):
            ls = strips[st]
            t0 = (st * strip) % tq
            m = None
            for rb in row_blocks:
                cur = b_scr[KV_HEADS, rb, t0:t0 + strip] > 0.5
                t = jnp.where(cur, s_scr[h, rb, ls], s2_scr[h, rb, ls] + no_prev)
                s_scr[h, rb, ls] = t
                m = t if m is None else jnp.maximum(m, t)
            m_new = jnp.max(m, axis=0, keepdims=True)
            for rb in row_blocks:
                pe = jnp.exp2(s_scr[h, rb, ls] - m_new)
                pc = pe * b_scr[KV_HEADS, rb, t0:t0 + strip]
                p_scr[h, rb, ls] = pc.astype(BF16)
                p2_scr[h, rb, ls] = (pe - pc).astype(BF16)

        def weighted_values(h, st):
            ls = strips[st]
            acc_s[KV_HEADS + h, :, ls] = _dot(vT1[h], p_scr[h, :, ls]) + _dot(vT1p[h], p2_scr[h, :, ls])

        run_skewed((scores, probs, weighted_values))

    blk_of_key = (k0 >> 6) + (lax.broadcasted_iota(jnp.int32, (kc, 128), 0) >> 6)
    expand = (lax.broadcasted_iota(jnp.int32, (kc, 128), 1) == blk_of_key).astype(BF16)
    causal = kpos <= tok

    @pl.when(c < c_last)
    def _():
        selected_step(diagonal=False)

    @pl.when(c == c_last)
    def _():
        selected_step(diagonal=True)
        b_scr[KV_HEADS] = causal.astype(F32)
        window_step()

    @pl.when(c == c_last)
    def _():
        gz = gzT_ref[0]
        outs = []
        for h in range(KV_HEADS):
            a_sel = acc_s[h]
            a_win = acc_s[KV_HEADS + h]
            o_sel = a_sel[:HEAD_DIM] / jnp.maximum(a_sel[HEAD_DIM:HEAD_DIM + 1], 1e-30)
            o_win = a_win[:HEAD_DIM] / jnp.maximum(a_win[HEAD_DIM:HEAD_DIM + 1], 1e-30)
            o_cmp = ocmp_s[h]
            for g in range(GQA):
                col = h * GQA + g
                ls = slice(g * tq, (g + 1) * tq)
                outs.append(gz[col:col + 1] * o_cmp[:, ls] + gz[B_HEADS + col:B_HEADS + col + 1] * o_sel[:, ls]
                            + gz[2 * B_HEADS + col:2 * B_HEADS + col + 1] * o_win[:, ls])
        yb_ref[0] = jnp.concatenate(outs, axis=0).T.astype(BF16)


def _p_attn_call(qT, ksel, kwin, vselT, vwinT, kc_all, vcT, gzT, *, tq, kc):
    B, _, T = qT.shape
    nb = T // BLOCK
    N = GQA * tq
    assert tq == kc == WINDOW
    js, cs = _pair_schedule(T, tq, kc)
    jt = jnp.asarray(js, jnp.int32)
    ct = jnp.asarray(cs, jnp.int32)
    prev = lambda jt, p: jnp.maximum(jt[p] - 1, 0)
    return pl.pallas_call(
        functools.partial(_p_attn_kernel, tq=tq, kc=kc, T=T, strip=256),
        out_shape=jax.ShapeDtypeStruct((B, T, B_WIDTH), BF16),
        grid_spec=pltpu.PrefetchScalarGridSpec(
            num_scalar_prefetch=2,
            grid=(B, len(js)),
            in_specs=[pl.BlockSpec((1, B_WIDTH, tq), lambda b, p, jt, ct: (b, 0, jt[p])),
                      pl.BlockSpec((1, kc, 128), lambda b, p, jt, ct: (b, ct[p], 0)),
                      pl.BlockSpec((1, 128, kc), lambda b, p, jt, ct: (b, 0, ct[p])),
                      pl.BlockSpec((1, kc, 128), lambda b, p, jt, ct: (b, jt[p], 0)),
                      pl.BlockSpec((1, kc, 128), lambda b, p, jt, ct: (b, prev(jt, p), 0)),
                      pl.BlockSpec((1, 128, kc), lambda b, p, jt, ct: (b, 0, jt[p])),
                      pl.BlockSpec((1, 128, kc), lambda b, p, jt, ct: (b, 0, prev(jt, p))),
                      pl.BlockSpec((1, nb, 128), lambda b, p, jt, ct: (b, 0, 0)),
                      pl.BlockSpec((1, 128, nb), lambda b, p, jt, ct: (b, 0, 0)),
                      pl.BlockSpec((1, 32, tq), lambda b, p, jt, ct: (b, 0, jt[p]))],
            out_specs=pl.BlockSpec((1, tq, B_WIDTH), lambda b, p, jt, ct: (b, jt[p], 0)),
            scratch_shapes=[pltpu.VMEM((KV_HEADS, 128, tq), BF16),
                            pltpu.VMEM((KV_HEADS, 128, tq), BF16),
                            pltpu.VMEM((KV_HEADS, HEAD_DIM, N), F32),
                            pltpu.VMEM((2 * KV_HEADS, 1, N), F32),
                            pltpu.VMEM((2 * KV_HEADS, ACC_ROWS, N), F32),
                            pltpu.VMEM((KV_HEADS, kc, N), F32),
                            pltpu.VMEM((KV_HEADS, kc, N), F32),
                            pltpu.VMEM((KV_HEADS, kc, N), BF16),
                            pltpu.VMEM((KV_HEADS, kc, N), BF16),
                            pltpu.VMEM((KV_HEADS + 1, kc, tq), F32),
                            pltpu.VMEM((kc, 256), BF16)]),
        compiler_params=pltpu.CompilerParams(dimension_semantics=("arbitrary", "arbitrary"),
                                             vmem_limit_bytes=VMEM_LIMIT),
        name="p_attn",
    )(jt, ct, qT, ksel, vselT, kwin, kwin, vwinT, vwinT, kc_all, vcT, gzT)


def _s_cmp_kernel(q_ref, gath, newc_ref, ocmp_ref, imp_ref, *, n_pages, past_len):
    rows = 8 * GQA
    t_row = lax.broadcasted_iota(jnp.int32, (rows, 1), 0) & 7
    pos = past_len + t_row
    lane = lax.broadcasted_iota(jnp.int32, (1, 2 * n_pages), 1)
    blk = 2 * (lane & (n_pages - 1)) + (lane >> 7)
    new_blk = past_len // BLOCK
    avail = (blk + 1) * BLOCK <= pos + 1
    avail_new = (new_blk + 1) * BLOCK <= pos + 1
    for h in range(KV_HEADS):
        qh = q_ref[0, h]
        kc = [gath[:, (h * 2 + k) * 64:(h * 2 + k + 1) * 64].astype(BF16) for k in range(2)]
        vc = [gath[:, (2 + h) * 128 + k * 64:(2 + h) * 128 + (k + 1) * 64].astype(BF16) for k in range(2)]
        kc_new = newc_ref[0, :, h * 128:h * 128 + 64].astype(BF16).astype(F32)
        vc_new = newc_ref[0, :, (2 + h) * 128:(2 + h) * 128 + 64].astype(BF16).astype(F32)
        s = jnp.where(avail, jnp.concatenate([_dot_nt(qh, kc[0]), _dot_nt(qh, kc[1])], axis=1), NEG)
        s_new = jnp.where(avail_new, jnp.sum(qh.astype(F32) * kc_new, axis=-1, keepdims=True), NEG)
        m = jnp.maximum(jnp.max(s, axis=-1, keepdims=True), s_new)
        e = jnp.where(avail, jnp.exp(s - m), 0.0)
        e_new = jnp.where(avail_new, jnp.exp(s_new - m), 0.0)
        den = jnp.maximum(jnp.sum(e, axis=-1, keepdims=True) + e_new, 1e-30)
        p = e / den
        p_new = e_new / den
        pb = p.astype(BF16)
        ocmp_ref[0, h] = (_dot(pb[:, :n_pages], vc[0]) + _dot(pb[:, n_pages:], vc[1])
                          + p_new.astype(BF16).astype(F32) * vc_new)
        imp = p[0:8] + p[8:16] + p[16:24] + p[24:32]
        cur = (past_len + lax.broadcasted_iota(jnp.int32, (8, 1), 0)) >> 6
        forced = (blk == 0) | (blk == cur) | (blk == cur - 1)
        imp_ref[0, h] = jnp.where(forced, FORCE_SCORE, jnp.where(blk > cur, -1.0, imp))


def _s_cmp_call(q_r, pool_seq, newc, *, n_pages, past_len):
    nbatch = q_r.shape[0]
    return pl.pallas_call(
        functools.partial(_s_cmp_kernel, n_pages=n_pages, past_len=past_len),
        out_shape=[jax.ShapeDtypeStruct((nbatch, KV_HEADS, 32, HEAD_DIM), F32),
                   jax.ShapeDtypeStruct((nbatch, KV_HEADS, 8, 2 * n_pages), F32)],
        grid=(nbatch,),
        in_specs=[pl.BlockSpec((1, KV_HEADS, 32, HEAD_DIM), lambda b: (b, 0, 0, 0)),
                  pl.BlockSpec((n_pages, 512), lambda b: (b, 0)),
                  pl.BlockSpec((1, 1, 512), lambda b: (b, 0, 0))],
        out_specs=[pl.BlockSpec((1, KV_HEADS, 32, HEAD_DIM), lambda b: (b, 0, 0, 0)),
                   pl.BlockSpec((1, KV_HEADS, 8, 2 * n_pages), lambda b: (b, 0, 0, 0))],
        compiler_params=pltpu.CompilerParams(dimension_semantics=("arbitrary",), vmem_limit_bytes=VMEM_LIMIT),
        name="s_cmp",
    )(q_r, pool_seq, newc)


def _s_select_kernel(imp_ref, idx_ref, *, n_pages, n_pick):
    v = imp_ref[...]
    rows = v.shape[0]
    lane = lax.broadcasted_iota(jnp.int32, (1, 2 * n_pages), 1)
    blk = (2 * (lane & (n_pages - 1)) + (lane >> 7)).astype(F32)
    out_lane = lax.broadcasted_iota(jnp.int32, (1, 128), 1)
    out = jnp.full((rows, 128), float(2 * n_pages), F32)
    for k in range(n_pick):
        m = jnp.max(v, axis=-1, keepdims=True)
        pick = jnp.min(jnp.where(v == m, blk, 1e9), axis=-1, keepdims=True)
        out = jnp.where(out_lane == k, pick, out)
        v = jnp.where(blk == pick, NEG, v)
    idx_ref[...] = out.astype(jnp.int32)


def _s_select_call(imp2d, *, n_pages, n_pick):
    rows = imp2d.shape[0]
    return pl.pallas_call(
        functools.partial(_s_select_kernel, n_pages=n_pages, n_pick=n_pick),
        out_shape=jax.ShapeDtypeStruct((rows, 128), jnp.int32),
        grid=(1,),
        in_specs=[pl.BlockSpec(imp2d.shape, lambda i: (0, 0))],
        out_specs=pl.BlockSpec((rows, 128), lambda i: (0, 0)),
        compiler_params=pltpu.CompilerParams(dimension_semantics=("arbitrary",), vmem_limit_bytes=VMEM_LIMIT),
        name="s_select",
    )(imp2d)


def _s_attn_kernel(pt_ref, idx_ref, q_ref, kvn_ref, win_ref, gz_ref, ocmp_ref, cache_ref, yb_ref, wnext_ref,
                   kvbuf, sem, *, n_tok, n_gather, past_len):
    b = pl.program_id(0)
    span = n_gather * 128
    slot = b & 1

    def tile_copies(bb, sl, t, h, jj):
        n = idx_ref[((bb * KV_HEADS + h) * n_tok + t) * N_SELECT + jj]
        page = pt_ref[bb, n >> 1]
        return [pltpu.make_async_copy(cache_ref.at[page, 2 + kv, h],
                                      kvbuf.at[sl, h, t, kv, :, pl.ds(jj * 128, 128)], sem.at[sl])
                for kv in range(2)]

    def for_all_tiles(bb, sl, fn):
        for h in range(KV_HEADS):
            for t in range(n_tok):
                for jj in range(n_gather):
                    for cp in tile_copies(bb, sl, t, h, jj):
                        fn(cp)

    @pl.when(b == 0)
    def _():
        for_all_tiles(0, 0, lambda cp: cp.start())

    @pl.when(b + 1 < pl.num_programs(0))
    def _():
        for_all_tiles(b + 1, 1 - slot, lambda cp: cp.start())

    rows = 8 * GQA
    t_row = lax.broadcasted_iota(jnp.int32, (rows, 1), 0) & 7
    gz = gz_ref[0]

    lane_n = lax.broadcasted_iota(jnp.int32, (1, 128), 1)
    tok_shift = n_tok.bit_length() - 1
    new_ok = ((lane_n >> tok_shift) == b) & ((lane_n & (n_tok - 1)) <= t_row)
    lane_w = lax.broadcasted_iota(jnp.int32, (1, WINDOW), 1)
    win_ok = lane_w > t_row

    shifted = pltpu.roll(win_ref[0], WINDOW - n_tok, 1)
    moved = pltpu.roll(kvn_ref[4 * 128:, :], (128 - n_tok) - b * n_tok, 1)
    wnext_ref[0, :, :WINDOW - 128] = shifted[:, :WINDOW - 128]
    wnext_ref[0, :, WINDOW - 128:] = jnp.where(lane_n >= 128 - n_tok, moved, shifted[:, WINDOW - 128:])

    def softmax_pv(parts):
        m = None
        for sc, mk, _ in parts:
            mm = jnp.max(jnp.where(mk, sc, NEG), axis=-1, keepdims=True)
            m = mm if m is None else jnp.maximum(m, mm)
        den = jnp.zeros((rows, 1), F32)
        o = jnp.zeros((rows, HEAD_DIM), F32)
        for sc, mk, vts in parts:
            e = jnp.where(mk, jnp.exp(jnp.where(mk, sc, NEG) - m), 0.0)
            den = den + jnp.sum(e, axis=-1, keepdims=True)
            for rmask, vt in vts:
                er = e if rmask is None else jnp.where(rmask, e, 0.0)
                o = o + _dot_nt(er.astype(BF16), vt)
        return o / jnp.maximum(den, 1e-30)

    win_out = []
    for h in range(KV_HEADS):
        qh = q_ref[0, h]
        kw = win_ref[0, h * 64:(h + 1) * 64, :].astype(BF16)
        vw = win_ref[0, 128 + h * 64:128 + (h + 1) * 64, :].astype(BF16)
        kwn = kvn_ref[4 * 128 + h * 64:4 * 128 + (h + 1) * 64, :].astype(BF16)
        vwn = kvn_ref[5 * 128 + h * 64:5 * 128 + (h + 1) * 64, :].astype(BF16)
        win_out.append(softmax_pv([(_dot(qh, kw), win_ok, [(None, vw)]),
                                   (_dot(qh, kwn), new_ok, [(None, vwn)])]))

    for_all_tiles(b, slot, lambda cp: cp.wait())

    lane_s = lax.broadcasted_iota(jnp.int32, (1, span), 1)
    for h in range(KV_HEADS):
        qh = q_ref[0, h]
        sc = jnp.zeros((rows, span), F32)
        half = jnp.zeros((rows, span), jnp.int32)
        for t in range(n_tok):
            st = _dot(qh, kvbuf[slot, h, t, 0].astype(BF16))
            hrow = jnp.zeros((1, span), jnp.int32)
            for jj in range(n_gather):
                n = idx_ref[((b * KV_HEADS + h) * n_tok + t) * N_SELECT + jj]
                hrow = jnp.where((lane_s >> 7) == jj, n & 1, hrow)
            sc = jnp.where(t_row == t, st, sc)
            half = jnp.where(t_row == t, hrow, half)
        ok = ((lane_s >> 6) & 1) == half
        ksn = kvn_ref[2 * 128 + h * 64:2 * 128 + (h + 1) * 64, :].astype(BF16)
        vsn = kvn_ref[3 * 128 + h * 64:3 * 128 + (h + 1) * 64, :].astype(BF16)
        vts = [(t_row == t, kvbuf[slot, h, t, 1].astype(BF16)) for t in range(n_tok)]
        o_sel = softmax_pv([(sc, ok, vts), (_dot(qh, ksn), new_ok, [(None, vsn)])])
        yb_ref[0, h] = (gz[h, :, 0:1] * ocmp_ref[0, h] + gz[h, :, 1:2] * o_sel + gz[h, :, 2:3] * win_out[h])


def _s_attn_call(page_table, idx_flat, q_r, kvT_new, winT, gz_r, ocmp, cache_pages, *, n_tok, past_len):
    nbatch = page_table.shape[0]
    n_gather = N_SELECT - 1
    span = n_gather * 128
    blk4 = lambda b, pt, ix: (b, 0, 0, 0)
    return pl.pallas_call(
        functools.partial(_s_attn_kernel, n_tok=n_tok, n_gather=n_gather, past_len=past_len),
        out_shape=[jax.ShapeDtypeStruct((nbatch, KV_HEADS, 32, HEAD_DIM), F32),
                   jax.ShapeDtypeStruct((nbatch, 256, WINDOW), F32)],
        grid_spec=pltpu.PrefetchScalarGridSpec(
            num_scalar_prefetch=2,
            grid=(nbatch,),
            in_specs=[pl.BlockSpec((1, KV_HEADS, 32, HEAD_DIM), blk4),
                      pl.BlockSpec(kvT_new.shape, lambda b, pt, ix: (0, 0)),
                      pl.BlockSpec((1, 256, WINDOW), lambda b, pt, ix: (b, 0, 0)),
                      pl.BlockSpec((1, KV_HEADS, 32, 128), blk4),
                      pl.BlockSpec((1, KV_HEADS, 32, HEAD_DIM), blk4),
                      pl.BlockSpec(memory_space=pl.ANY)],
            out_specs=[pl.BlockSpec((1, KV_HEADS, 32, HEAD_DIM), blk4),
                       pl.BlockSpec((1, 256, WINDOW), lambda b, pt, ix: (b, 0, 0))],
            scratch_shapes=[pltpu.VMEM((2, KV_HEADS, n_tok, 2, HEAD_DIM, span), F32),
                            pltpu.SemaphoreType.DMA((2,))]),
        compiler_params=pltpu.CompilerParams(dimension_semantics=("arbitrary",), vmem_limit_bytes=VMEM_LIMIT),
        name="s_attn",
    )(page_table, idx_flat, q_r, kvT_new, winT, gz_r, ocmp, cache_pages)


def _post_kernel(*refs, tm, seq_len, has_prev):
    it = iter(refs)
    x_ref, mod_ref, ya_ref, yb_ref = next(it), next(it), next(it), next(it)
    g1_ref, g2_ref, gf_ref = next(it), next(it), next(it)
    wm_ref, wba_ref, wbb_ref, wo_ref, wup_ref, wdn_ref, cw_ref = (next(it) for _ in range(7))
    p1_ref = next(it) if has_prev else None
    p2_ref = next(it) if has_prev else None
    y_ref, up_ref = next(it), next(it)
    carry, h2_s, act_s = next(it), next(it), next(it)

    i = pl.program_id(1)
    x = x_ref[0]
    shift1, scale1, gate1 = mod_ref[0, 0], mod_ref[0, 1], mod_ref[0, 2]
    shift2, scale2, gate2 = mod_ref[0, 3], mod_ref[0, 4], mod_ref[0, 5]

    hb = _rms_mod(x, g1_ref[...], scale1, shift1).astype(BF16)
    gates = jax.nn.sigmoid(_dot(hb, wm_ref[...]))
    mix = (gates[:, :D_MODEL] * _dot(ya_ref[0], wba_ref[...])
           + gates[:, D_MODEL:] * _dot(yb_ref[0], wbb_ref[...]))
    x1 = x + gate1 * _dot(mix.astype(BF16), wo_ref[...])
    h2_s[...] = _rms_mod(x1, g2_ref[...], scale2, shift2).astype(BF16)

    if not has_prev:
        @pl.when(i == 0)
        def _():
            carry[...] = jnp.zeros(carry.shape, F32)

    row = lax.broadcasted_iota(jnp.int32, (tm, 1), 0)
    row8 = lax.broadcasted_iota(jnp.int32, (8, 1), 0)

    def conv_cols(col0):
        cols = slice(col0, col0 + FF_CHUNK)
        up = _dot(h2_s[...], wup_ref[:, cols])
        r1 = pltpu.roll(up, 1, 0)
        r2 = pltpu.roll(up, 2, 0)
        if has_prev:
            s1 = jnp.where((row & (seq_len - 1)) == 0, p1_ref[:, cols], r1)
            s2 = jnp.where((row & (seq_len - 1)) < 2, p2_ref[:, cols], r2)
            up_ref[:, cols] = up
        else:
            prev = carry[:, cols]
            t1 = jnp.where(row8 == 0, pltpu.roll(prev, 1, 0), r1[:8])
            t2 = jnp.where(row8 < 2, pltpu.roll(prev, 2, 0), r2[:8])
            s1 = jnp.concatenate([t1, r1[8:]], axis=0)
            s2 = jnp.concatenate([t2, r2[8:]], axis=0)
            carry[:, cols] = up[tm - 8:]
            up_ref[0, :, cols] = up[tm - 8:]
        cw = cw_ref[:, cols]
        return cw[3:4] + cw[0:1] * s2 + cw[1:2] * s1 + cw[2:3] * up

    for c in range(N_FF_CHUNKS):
        a = conv_cols(c * FF_CHUNK)
        gv = conv_cols(D_FF + c * FF_CHUNK)
        act_s[:, c * FF_CHUNK:(c + 1) * FF_CHUNK] = (jax.nn.gelu(a) * gv).astype(BF16)
    x2 = x1 + gate2 * _dot(act_s[...], wdn_ref[...])
    y_ref[0] = x2 * lax.rsqrt(jnp.mean(x2 * x2, axis=-1, keepdims=True) + EPS) * gf_ref[...]


def _post_call(x, mod, ya, yb, g1, g2, gf, wm, wba, wbb, wo, wup, wdn, cw, prev=None, *, tm, seq_len):
    G, T, _ = x.shape
    R = mod.shape[2]
    nt = T // tm
    has_prev = prev is not None
    single = pl.Buffered(1)
    c2 = lambda b, i: (0, 0)
    c3 = lambda b, i: (0, 0, 0)
    in_specs = [pl.BlockSpec((1, tm, D_MODEL), lambda b, i: (b, i, 0)),
                pl.BlockSpec((1, 6, R, D_MODEL), lambda b, i: (b, 0, 0, 0)),
                pl.BlockSpec((1, tm, A_WIDTH), lambda b, i: (b, i, 0)),
                pl.BlockSpec((1, tm, B_WIDTH), lambda b, i: (b, i, 0)),
                pl.BlockSpec((1, D_MODEL), c2),
                pl.BlockSpec((1, D_MODEL), c2),
                pl.BlockSpec((1, D_MODEL), c2),
                pl.BlockSpec(wm.shape, c2, pipeline_mode=single),
                pl.BlockSpec(wba.shape, c2, pipeline_mode=single),
                pl.BlockSpec(wbb.shape, c2, pipeline_mode=single),
                pl.BlockSpec(wo.shape, c2, pipeline_mode=single),
                pl.BlockSpec(wup.shape, c2, pipeline_mode=single),
                pl.BlockSpec(wdn.shape, c2, pipeline_mode=single),
                pl.BlockSpec(cw.shape, c2, pipeline_mode=single)]
    args = [x, mod, ya, yb, g1, g2, gf, wm, wba, wbb, wo, wup, wdn, cw]
    if has_prev:
        in_specs += [pl.BlockSpec(prev[0].shape, c2), pl.BlockSpec(prev[1].shape, c2)]
        args += list(prev)
        up_shape = jax.ShapeDtypeStruct((T, F2), F32)
        up_spec = pl.BlockSpec((tm, F2), lambda b, i: (i, 0))
    else:
        up_shape = jax.ShapeDtypeStruct((G, 8, F2), F32)
        up_spec = pl.BlockSpec((1, 8, F2), lambda b, i: (b, 0, 0))
    return pl.pallas_call(
        functools.partial(_post_kernel, tm=tm, seq_len=seq_len, has_prev=has_prev),
        out_shape=[jax.ShapeDtypeStruct((G, T, D_MODEL), F32), up_shape],
        grid=(G, nt),
        in_specs=in_specs,
        out_specs=[pl.BlockSpec((1, tm, D_MODEL), lambda b, i: (b, i, 0)), up_spec],
        scratch_shapes=[pltpu.VMEM((8, F2), F32),
                        pltpu.VMEM((tm, D_MODEL), BF16),
                        pltpu.VMEM((tm, D_FF), BF16)],
        compiler_params=pltpu.CompilerParams(dimension_semantics=("arbitrary", "arbitrary"),
                                             vmem_limit_bytes=VMEM_LIMIT),
        name="post",
    )(*args)


def _block_diag2(m):
    z = jnp.zeros_like(m)
    return jnp.concatenate([jnp.concatenate([m, z], axis=-1), jnp.concatenate([z, m], axis=-1)], axis=-2)


def kernel(x_prompt, x_sample, cache_kv, state_kv_win, state_ffn_conv, page_table, c_prompt, c_sample, w_ada, b_ada, g_norm1, w_in, ln_v_g, ln_v_b, w_spatial, b_spatial, cmp_pe, cmp_w1, cmp_b1, cmp_w2, cmp_b2, w_branch_a, w_branch_b, w_out, g_norm2, w_up, w_conv, b_conv, w_down, g_final):
    B, T, _ = x_prompt.shape
    SB, ST, _ = x_sample.shape
    n_pool, page_size = cache_kv.shape[1], cache_kv.shape[2]
    n_pages = page_table.shape[1]
    past_len = n_pages * page_size
    lbuf = state_kv_win.shape[2]
    assert cache_kv.shape[0] == 1 and page_size == 128 and lbuf == WINDOW and SB * ST == 128 and ST <= 8
    assert past_len % BLOCK == 0 and n_pages == 128 and T % 512 == 0 and ST & (ST - 1) == 0 and ST >= 2

    win = w_in[0]
    wa = win[:, :2 * A_WIDTH].astype(BF16)
    o = 2 * A_WIDTH
    wq = win[:, o:o + B_WIDTH].astype(BF16)
    o += B_WIDTH
    wkv = win[:, o:o + KV_COLS]
    wkvT = wkv.T.astype(BF16)
    wkvc = wkv[:, :256].astype(BF16)
    o += KV_COLS
    wg = jnp.pad(win[:, o:o + 3 * B_HEADS], ((0, 0), (0, 128 - 3 * B_HEADS))).astype(BF16)
    wgT = jnp.pad(win[:, o:o + 3 * B_HEADS].T, ((0, 32 - 3 * B_HEADS), (0, 0))).astype(BF16)
    wqT = wq.T
    o += 3 * B_HEADS
    wm = win[:, o:].astype(BF16)
    g1 = g_norm1[0][None]
    g2 = g_norm2[0][None]
    gf = g_final[None]
    lng = ln_v_g[0][None]
    lnb = ln_v_b[0][None]

    ws = w_spatial[0]
    bs = b_spatial[0]
    wsp_p = jnp.tril(ws).astype(BF16)
    bsp_p = jnp.repeat(bs.T, 128, axis=1)
    w4 = jnp.tril(ws[:, :ST, :ST])
    seq_of = jnp.arange(128, dtype=jnp.int32) // ST
    same_seq = seq_of[:, None] == seq_of[None, :]
    wsp_s = jnp.where(same_seq, jnp.tile(w4, (1, 128 // ST, 128 // ST)), 0.0).astype(BF16)
    bsp_s = jnp.repeat(jnp.tile(bs[:, :ST].T, (128 // ST, 1)), 128, axis=1)

    w1, pe = cmp_w1[0].astype(BF16), cmp_pe[0]
    w1_rows = _block_diag2(w1)
    pe_rows = jnp.tile(pe, (1, 1, 2))
    w1_pages = _block_diag2(w1.transpose(0, 2, 1, 3)).reshape(2, HEAD_DIM // 2, 256, 128)
    pe_pages = jnp.tile(pe.transpose(0, 2, 1), (1, 1, 2)).reshape(2, HEAD_DIM // 2, 1, 256)
    pe_pages = jnp.pad(pe_pages, ((0, 0), (0, 0), (0, PE_ROWS - 1), (0, 0))).astype(BF16)
    b1t = jnp.tile(cmp_b1[0], (1, 2))[:, None, :]
    b2t = jnp.tile(cmp_b2[0], (1, 2))[:, None, :]
    w2bd = _block_diag2(cmp_w2[0].astype(BF16))

    wba = w_branch_a[0].astype(BF16)
    wbb = w_branch_b[0].astype(BF16)
    wo = w_out[0].astype(BF16)
    wup = w_up[0].astype(BF16)
    wdn = w_down[0].astype(BF16)
    cw = jnp.concatenate([w_conv[0], b_conv[0][None], jnp.zeros((4, F2), F32)], axis=0)

    mod = _mod_call(jnp.concatenate([c_prompt, c_sample], axis=0), w_ada[0], b_ada)
    mod_p = mod[:B].reshape(B, 6, 1, D_MODEL)
    mod_s = jnp.repeat(mod[B:].reshape(SB, 6, D_MODEL).transpose(1, 0, 2), ST, axis=1)[None]

    ya_p, qT_p, kv4T_p, winT_p, vselT_p, vwinT_p, kvc_p, ksel_p, kwin_p, gzT_p = _inproj_call(
        x_prompt, mod_p, g1, wa, lng, lnb, wsp_p, bsp_p, (wqT, wkvT, wgT), tm=512, prompt=True)
    xs2 = x_sample.reshape(1, SB * ST, D_MODEL)
    ya_s, q_s, kvT_s, kvc_s, gz_s, vn_s = _inproj_call(
        xs2, mod_s, g1, wa, lng, lnb, wsp_s, bsp_s, (wq, wkvT, wkvc, wg), tm=128, prompt=False)

    nb = T // BLOCK
    cmp_p = _cmp_rows_call(kvc_p.reshape(B * T, 256), pe_rows, w1_rows, b1t, w2bd, b2t, R=128)
    kc_p = cmp_p[0].reshape(B, nb, 128)
    vcT_p = cmp_p[1].reshape(B, nb, 128).transpose(0, 2, 1)
    yb_p = _p_attn_call(qT_p, ksel_p, kwin_p, vselT_p, vwinT_p, kc_p, vcT_p, gzT_p, tq=512, kc=512)

    cache_pages = jnp.transpose(cache_kv[0], (0, 2, 3, 4, 1)).reshape(n_pool, 512, page_size)
    pool_seq = _cmp_gather_call(page_table.reshape(-1), cache_pages, pe_pages, w1_pages, b1t, w2bd, b2t,
                                Pt=64)
    kv_rows_s = jnp.transpose(kvT_s[0], (1, 0)).reshape(SB, ST, KV_COLS)
    newblk = jnp.pad(jnp.transpose(kv_rows_s[:, :, :256], (0, 2, 1)), ((0, 0), (0, 0), (0, page_size - ST)))
    newc = _cmp_pages_call(newblk, pe_pages, w1_pages, b1t, w2bd, b2t, Pt=SB)

    q_r = jnp.pad(q_s.reshape(SB, ST, KV_HEADS, GQA, HEAD_DIM).transpose(0, 2, 3, 1, 4),
                  ((0, 0), (0, 0), (0, 0), (0, 8 - ST), (0, 0))).reshape(SB, KV_HEADS, 8 * GQA, HEAD_DIM)
    ocmp_s, imp_s = _s_cmp_call(q_r, pool_seq, newc.reshape(SB, 1, 512), n_pages=n_pages, past_len=past_len)
    idx = _s_select_call(imp_s.reshape(SB * KV_HEADS * 8, 2 * n_pages), n_pages=n_pages, n_pick=N_SELECT - 1)
    idx_flat = idx.reshape(SB, KV_HEADS, 8, 128)[:, :, :ST, :N_SELECT].reshape(-1)
    winT = jnp.transpose(state_kv_win[0], (0, 2, 3, 4, 1)).reshape(SB, 256, lbuf)
    gz_r = jnp.pad(gz_s[0, :, :24].reshape(SB, ST, 3, KV_HEADS, GQA).transpose(0, 3, 4, 1, 2),
                   ((0, 0), (0, 0), (0, 0), (0, 8 - ST), (0, 128 - 3))).reshape(SB, KV_HEADS, 8 * GQA, 128)
    cache_tiles = cache_pages.reshape(n_pool, 4, KV_HEADS, HEAD_DIM, page_size)
    yb_r, wnextT = _s_attn_call(page_table, idx_flat, q_r, kvT_s[0], winT, gz_r, ocmp_s, cache_tiles,
                        n_tok=ST, past_len=past_len)
    yb_s = yb_r.reshape(SB, KV_HEADS, GQA, 8, HEAD_DIM)[:, :, :, :ST].transpose(0, 3, 1, 2, 4)
    yb_s = yb_s.reshape(1, SB * ST, B_WIDTH).astype(BF16)

    y_p, up_p = _post_call(x_prompt, mod_p, ya_p, yb_p, g1, g2, gf, wm, wba, wbb, wo, wup, wdn, cw,
                           tm=512, seq_len=T)
    st = state_ffn_conv[0]
    zrow = jnp.zeros((SB, 1, F2), F32)
    p1 = jnp.concatenate([st[:, 1:2], jnp.tile(zrow, (1, ST - 1, 1))], axis=1).reshape(SB * ST, F2)
    p2 = jnp.concatenate([st, jnp.tile(zrow, (1, ST - 2, 1))], axis=1).reshape(SB * ST, F2)
    y_s, up_s = _post_call(xs2, mod_s, ya_s, yb_s, g1, g2, gf, wm, wba, wbb, wo, wup, wdn, cw,
                           prev=(p1, p2),
                           tm=SB * ST, seq_len=ST)

    kv_prompt = jnp.transpose(kv4T_p.reshape(B, 4, KV_HEADS, HEAD_DIM, T), (0, 4, 1, 2, 3))[None]
    win_prompt = jnp.transpose(winT_p[:, :, T - WINDOW:].reshape(B, 2, KV_HEADS, HEAD_DIM, WINDOW),
                               (0, 4, 1, 2, 3))[None]
    kv_sample = kv_rows_s[:, :, :512].reshape(SB, ST, 4, KV_HEADS, HEAD_DIM)[None]
    win_sample = jnp.transpose(wnextT.reshape(SB, 2, KV_HEADS, HEAD_DIM, lbuf), (0, 4, 1, 2, 3))[None]
    v_chunk = vn_s.reshape(SB, ST, A_WIDTH)[None]
    conv_prompt = up_p[:, 6:8][None]
    conv_sample = up_s.reshape(SB, ST, F2)[:, ST - 2:][None]
    return (y_p, y_s.reshape(SB, ST, D_MODEL), kv_prompt, kv_sample, win_prompt, win_sample, v_chunk,
            conv_prompt, conv_sample)
```

```python
import functools

import jax
import jax.numpy as jnp
from jax import lax
from jax.experimental import pallas as pl
from jax.experimental.pallas import tpu as pltpu

F32 = jnp.float32
BF16 = jnp.bfloat16

D_MODEL = 1024
A_WIDTH = 512
A_GROUPS = 4
CHUNK = 128
B_HEADS = 8
HEAD_DIM = 64
B_WIDTH = 512
KV_HEADS = 2
GQA = 4
BLOCK = 64
N_SELECT = 16
WINDOW = 512
N_KV_SLOTS = 6
KV_COLS = 768
D_FF = 2816
F2 = 2 * D_FF
CONV_W = 3
EPS = 1e-6
NEG = -1e30
FORCE_SCORE = 1e4

FF_CHUNK = 256
N_FF_CHUNKS = D_FF // FF_CHUNK
ROW_BLOCK = 64
PE_ROWS = 16
SKEW = 2
ACC_ROWS = HEAD_DIM + 16
LOG2E = 1.4426950408889634
VMEM_LIMIT = 56 * 1024 * 1024


def _dot(a, b):
    return jnp.dot(a, b, preferred_element_type=F32)


def _dot_nt(a, b):
    return lax.dot_general(a, b, (((1,), (1,)), ((), ())), preferred_element_type=F32)


def _rms_mod(x, g, scale, shift):
    y = x * lax.rsqrt(jnp.mean(x * x, axis=-1, keepdims=True) + EPS)
    return (y * g) * (1.0 + scale) + shift


def _mod_kernel(c_ref, w_ref, b_ref, o_ref):
    c = c_ref[...]
    s = c * jax.nn.sigmoid(c)
    o_ref[...] = _dot(s.astype(BF16), w_ref[...].astype(BF16)) + b_ref[...]


def _mod_call(c_all, w_ada, b_ada):
    n = c_all.shape[0]
    tn = 1536
    return pl.pallas_call(
        _mod_kernel,
        out_shape=jax.ShapeDtypeStruct((n, 6 * D_MODEL), F32),
        grid=(6 * D_MODEL // tn,),
        in_specs=[pl.BlockSpec((n, D_MODEL), lambda j: (0, 0)),
                  pl.BlockSpec((D_MODEL, tn), lambda j: (0, j)),
                  pl.BlockSpec((1, tn), lambda j: (0, j))],
        out_specs=pl.BlockSpec((n, tn), lambda j: (0, j)),
        compiler_params=pltpu.CompilerParams(dimension_semantics=("arbitrary",), vmem_limit_bytes=VMEM_LIMIT),
        name="mod",
    )(c_all, w_ada, b_ada)


def _inproj_kernel(x_ref, mod_ref, g1_ref, wa_ref, lng_ref, lnb_ref, wsp_ref, bsp_ref, *refs, tm, prompt):
    x = x_ref[0]
    hb = _rms_mod(x, g1_ref[...], mod_ref[0, 1], mod_ref[0, 0]).astype(BF16)

    ga = jax.nn.gelu(_dot(hb, wa_ref[...]))
    u = ga[:, :A_WIDTH]
    v = ga[:, A_WIDTH:]
    vc = v - jnp.mean(v, axis=-1, keepdims=True)
    vn = vc * lax.rsqrt(jnp.mean(vc * vc, axis=-1, keepdims=True) + EPS) * lng_ref[...] + lnb_ref[...]
    vnb = vn.astype(BF16)

    if prompt:
        (wqT_ref, wkvT_ref, wgT_ref,
         ya_ref, qT_ref, kv4T_ref, winT_ref, vselT_ref, vwinT_ref, kvc_ref, ksel_ref, kwin_ref, gzT_ref) = refs
    else:
        (wq_ref, wkvT_ref, wkvc_ref, wg_ref, ya_ref, q_ref, kvT_ref, kvc_ref, gz_ref, vn_ref) = refs
        vn_ref[0] = vn

    for c in range(tm // CHUNK):
        rows = slice(c * CHUNK, (c + 1) * CHUNK)
        s = jnp.concatenate(
            [_dot(wsp_ref[g], vnb[rows, g * 128:(g + 1) * 128]) for g in range(A_GROUPS)], axis=1)
        ya_ref[0, rows, :] = (u[rows] * (s + bsp_ref[...])).astype(BF16)

    kvT = _dot_nt(wkvT_ref[...], hb)
    if prompt:
        qT_ref[0] = (_dot_nt(wqT_ref[...], hb) * (HEAD_DIM ** -0.5 * LOG2E)).astype(BF16)
        kv4T_ref[0] = kvT[:4 * 128]
        winT_ref[0] = kvT[4 * 128:]
        vselT_ref[0] = kvT[3 * 128:4 * 128].astype(BF16)
        vwinT_ref[0] = kvT[5 * 128:].astype(BF16)
        kvc_ref[0] = kvT[:2 * 128].T
        ksel_ref[0] = kvT[2 * 128:3 * 128].T.astype(BF16)
        kwin_ref[0] = kvT[4 * 128:5 * 128].T.astype(BF16)
        gzT_ref[0] = jax.nn.sigmoid(_dot_nt(wgT_ref[...], hb))
    else:
        q_ref[0] = (_dot(hb, wq_ref[...]) * (HEAD_DIM ** -0.5)).astype(BF16)
        kvT_ref[0] = kvT
        kvc_ref[0] = _dot(hb, wkvc_ref[...])
        gz_ref[0] = jax.nn.sigmoid(_dot(hb, wg_ref[...]))


def _inproj_call(x, mod, g1, wa, lng, lnb, wsp, bsp, proj_w, *, tm, prompt):
    G, T, _ = x.shape
    R = mod.shape[2]
    nt = T // tm
    const2 = lambda b, i: (0, 0)
    const3 = lambda b, i: (0, 0, 0)
    rows = lambda w: pl.BlockSpec((1, tm, w), lambda b, i: (b, i, 0))
    cols = lambda h: pl.BlockSpec((1, h, tm), lambda b, i: (b, 0, i))
    if prompt:
        outs = [((G, T, A_WIDTH), BF16, rows(A_WIDTH)),
                ((G, B_WIDTH, T), BF16, cols(B_WIDTH)),
                ((G, 512, T), F32, cols(512)),
                ((G, 256, T), F32, cols(256)),
                ((G, 128, T), BF16, cols(128)),
                ((G, 128, T), BF16, cols(128)),
                ((G, T, 256), F32, rows(256)),
                ((G, T, 128), BF16, rows(128)),
                ((G, T, 128), BF16, rows(128)),
                ((G, 32, T), F32, cols(32))]
    else:
        outs = [((G, T, A_WIDTH), BF16, rows(A_WIDTH)),
                ((G, T, B_WIDTH), BF16, rows(B_WIDTH)),
                ((G, KV_COLS, T), F32, cols(KV_COLS)),
                ((G, T, 256), F32, rows(256)),
                ((G, T, 128), F32, rows(128)),
                ((G, T, A_WIDTH), F32, rows(A_WIDTH))]
    return pl.pallas_call(
        functools.partial(_inproj_kernel, tm=tm, prompt=prompt),
        out_shape=[jax.ShapeDtypeStruct(s, d) for s, d, _ in outs],
        grid=(G, nt),
        in_specs=[pl.BlockSpec((1, tm, D_MODEL), lambda b, i: (b, i, 0)),
                  pl.BlockSpec((1, 6, R, D_MODEL), lambda b, i: (b, 0, 0, 0)),
                  pl.BlockSpec((1, D_MODEL), const2),
                  pl.BlockSpec(wa.shape, const2),
                  pl.BlockSpec((1, A_WIDTH), const2),
                  pl.BlockSpec((1, A_WIDTH), const2),
                  pl.BlockSpec(wsp.shape, const3),
                  pl.BlockSpec(bsp.shape, const2)] + [pl.BlockSpec(w.shape, const2) for w in proj_w],
        out_specs=[sp for _, _, sp in outs],
        compiler_params=pltpu.CompilerParams(dimension_semantics=("arbitrary", "arbitrary"),
                                             vmem_limit_bytes=VMEM_LIMIT),
        name="inproj",
    )(x, mod, g1, wa, lng, lnb, wsp, bsp, *proj_w)


def _cmp_tail(acc, b1, w2, b2):
    return _dot(jax.nn.gelu(acc + b1).astype(BF16), w2) + b2


def _cmp_rows_kernel(x_ref, pe_ref, w1_ref, b1_ref, w2_ref, b2_ref, o_ref, *, R):
    acc = jnp.zeros((R, 128), F32)
    for jg in range(BLOCK // 8):
        xa = jnp.swapaxes(x_ref[:, jg * 8:(jg + 1) * 8, :], 0, 1)
        for jj in range(0, 8, 2):
            j = jg * 8 + jj
            xj = jnp.concatenate([xa[jj] + pe_ref[0, j:j + 1, :], xa[jj + 1] + pe_ref[0, j + 1:j + 2, :]], axis=1)
            acc = acc + _dot(xj.astype(BF16), w1_ref[0, j // 2])
    o_ref[0] = _cmp_tail(acc, b1_ref[0], w2_ref[0], b2_ref[0])


def _cmp_rows_call(kvc3d, pe, w1, b1, w2, b2, *, R):
    nrows = kvc3d.shape[0]
    return pl.pallas_call(
        functools.partial(_cmp_rows_kernel, R=R),
        out_shape=jax.ShapeDtypeStruct((2, nrows, 128), F32),
        grid=(2, nrows // R),
        in_specs=[pl.BlockSpec((R, BLOCK, 128), lambda s, r: (r, 0, s)),
                  pl.BlockSpec((1, BLOCK, 128), lambda s, r: (s, 0, 0)),
                  pl.BlockSpec((1, BLOCK // 2, 256, 128), lambda s, r: (s, 0, 0, 0)),
                  pl.BlockSpec((1, 1, 128), lambda s, r: (s, 0, 0)),
                  pl.BlockSpec((1, 128, 128), lambda s, r: (s, 0, 0)),
                  pl.BlockSpec((1, 1, 128), lambda s, r: (s, 0, 0))],
        out_specs=pl.BlockSpec((1, R, 128), lambda s, r: (s, r, 0)),
        compiler_params=pltpu.CompilerParams(dimension_semantics=("arbitrary", "arbitrary"),
                                             vmem_limit_bytes=VMEM_LIMIT),
        name="cmp_rows",
    )(kvc3d, pe, w1, b1, w2, b2)


def _compress_pages(load_rows, pe_ref, w1_ref, b1_ref, w2_ref, b2_ref, o_ref, Pt):
    for s in range(2):
        acc = jnp.zeros((2 * Pt + PE_ROWS, 128), F32)
        for dg in range(HEAD_DIM // 8):
            r0 = s * 128 + dg * 8
            xa = jnp.swapaxes(load_rows(r0), 0, 1).astype(BF16)
            xb = jnp.swapaxes(load_rows(HEAD_DIM + r0), 0, 1).astype(BF16)
            for dd in range(0, 8, 2):
                dp = dg * 4 + dd // 2
                xd = jnp.concatenate([jnp.concatenate([xa[dd], xa[dd + 1]], axis=1),
                                      jnp.concatenate([xb[dd], xb[dd + 1]], axis=1),
                                      pe_ref[s, dp]], axis=0)
                acc = acc + _dot(xd, w1_ref[s, dp])
        pre = acc[:2 * Pt] + acc[2 * Pt:2 * Pt + 1]
        r = _cmp_tail(pre, b1_ref[s], w2_ref[s], b2_ref[s])
        o_ref[:, (2 * s) * 128:(2 * s + 1) * 128] = r[:Pt]
        o_ref[:, (2 * s + 1) * 128:(2 * s + 2) * 128] = r[Pt:]


def _cmp_pages_kernel(x_ref, pe_ref, w1_ref, b1_ref, w2_ref, b2_ref, o_ref, *, Pt):
    _compress_pages(lambda r0: x_ref[:, r0:r0 + 8, :], pe_ref, w1_ref, b1_ref, w2_ref, b2_ref, o_ref, Pt)


def _cmp_weight_specs(ws, index):
    return [pl.BlockSpec(w.shape, functools.partial(index, (0,) * w.ndim)) for w in ws]


def _cmp_pages_call(pages, pe, w1, b1, w2, b2, *, Pt):
    P = pages.shape[0]
    ws = (pe, w1, b1, w2, b2)
    return pl.pallas_call(
        functools.partial(_cmp_pages_kernel, Pt=Pt),
        out_shape=jax.ShapeDtypeStruct((P, 512), F32),
        grid=(P // Pt,),
        in_specs=[pl.BlockSpec((Pt, 256, 128), lambda i: (i, 0, 0))] + _cmp_weight_specs(ws, lambda z, i: z),
        out_specs=pl.BlockSpec((Pt, 512), lambda i: (i, 0)),
        compiler_params=pltpu.CompilerParams(dimension_semantics=("arbitrary",), vmem_limit_bytes=VMEM_LIMIT),
        name="cmp_pages",
    )(pages, *ws)


def _cmp_gather_kernel(pt_ref, cache_ref, pe_ref, w1_ref, b1_ref, w2_ref, b2_ref, o_ref, xbuf, sem, *, Pt):
    i = pl.program_id(0)
    slot = i & 1

    def page_copy(step, sl, k):
        return pltpu.make_async_copy(cache_ref.at[pt_ref[step * Pt + k], pl.ds(0, 256), :],
                                     xbuf.at[sl, k], sem.at[sl])

    @pl.when(i == 0)
    def _():
        for k in range(Pt):
            page_copy(0, 0, k).start()

    @pl.when(i + 1 < pl.num_programs(0))
    def _():
        for k in range(Pt):
            page_copy(i + 1, 1 - slot, k).start()

    for k in range(Pt):
        page_copy(i, slot, k).wait()
    _compress_pages(lambda r0: xbuf[slot, :, r0:r0 + 8, :], pe_ref, w1_ref, b1_ref, w2_ref, b2_ref, o_ref, Pt)


def _cmp_gather_call(pt_flat, cache_pages, pe, w1, b1, w2, b2, *, Pt):
    n = pt_flat.shape[0]
    ws = (pe, w1, b1, w2, b2)
    return pl.pallas_call(
        functools.partial(_cmp_gather_kernel, Pt=Pt),
        out_shape=jax.ShapeDtypeStruct((n, 512), F32),
        grid_spec=pltpu.PrefetchScalarGridSpec(
            num_scalar_prefetch=1,
            grid=(n // Pt,),
            in_specs=[pl.BlockSpec(memory_space=pl.ANY)] + _cmp_weight_specs(ws, lambda z, i, pt: z),
            out_specs=pl.BlockSpec((Pt, 512), lambda i, pt: (i, 0)),
            scratch_shapes=[pltpu.VMEM((2, Pt, 256, 128), F32),
                            pltpu.SemaphoreType.DMA((2,))]),
        compiler_params=pltpu.CompilerParams(dimension_semantics=("arbitrary",), vmem_limit_bytes=VMEM_LIMIT),
        name="cmp_gather",
    )(pt_flat, cache_pages, *ws)


def _pair_schedule(T, tq, kc):
    js, cs = [], []
    for j in range(T // tq):
        for c in range(((j + 1) * tq - 1) // kc + 1):
            js.append(j)
            cs.append(c)
    return js, cs


def _p_attn_kernel(jt_ref, ct_ref, qT_ref, ksel_ref, vselT_ref, kwin_ref, kwinp_ref, vwinT_ref, vwinTp_ref,
                   kc_ref, vcT_ref, gzT_ref, yb_ref,
                   sel_s, neg_s, ocmp_s, m_s, acc_s, s_scr, s2_scr, p_scr, p2_scr, b_scr, k2_s, *, tq, kc, T, strip):
    p = pl.program_id(1)
    j = jt_ref[p]
    c = ct_ref[p]
    nb = T // BLOCK
    N = GQA * tq
    q0 = j * tq
    k0 = c * kc
    c_last = ((j + 1) * tq - 1) // kc

    def q_pad(h):
        qT = jnp.concatenate([qT_ref[0, (h * GQA + g) * HEAD_DIM:(h * GQA + g + 1) * HEAD_DIM, :]
                              for g in range(GQA)], axis=1)
        z = jnp.zeros_like(qT)
        return jnp.concatenate([qT if k == h else z for k in range(KV_HEADS)], axis=0)

    @pl.when(c == 0)
    def _():
        m_s[...] = jnp.full(m_s.shape, NEG, F32)
        acc_s[...] = jnp.zeros(acc_s.shape, F32)
        tok_n = q0 + (lax.broadcasted_iota(jnp.int32, (1, N), 1) & (tq - 1))
        tok_1 = q0 + lax.broadcasted_iota(jnp.int32, (1, tq), 1)
        blk = lax.broadcasted_iota(jnp.int32, (nb, 1), 0)
        avail = (blk + 1) * BLOCK <= tok_n + 1
        cur = tok_1 >> 6
        forced = (blk == 0) | (blk == cur) | (blk == cur - 1)
        future = blk > cur
        kcb = kc_ref[0].astype(BF16)
        for h in range(KV_HEADS):
            s = jnp.where(avail, _dot(kcb, q_pad(h)), NEG)
            m = jnp.max(s, axis=0, keepdims=True)
            e = jnp.where(avail, jnp.exp2(s - m), 0.0)
            pr = e / jnp.maximum(jnp.sum(e, axis=0, keepdims=True), 1e-30)
            ocmp_s[h] = _dot(vcT_ref[0, h * HEAD_DIM:(h + 1) * HEAD_DIM, :].astype(BF16), pr.astype(BF16))
            imp = pr[:, 0:tq]
            for g in range(1, GQA):
                imp = imp + pr[:, g * tq:(g + 1) * tq]
            imp = jnp.where(forced, FORCE_SCORE, jnp.where(future, -1.0, imp))
            groups = [imp[r:r + 8] for r in range(0, nb, 8)]
            ranks = [jnp.zeros((8, tq), F32) for _ in groups]
            for n in range(nb):
                vn = imp[n:n + 1, :]
                for k, grp in enumerate(groups):
                    if 8 * k > n:
                        ahead = vn >= grp
                    elif 8 * k + 7 <= n:
                        ahead = vn > grp
                    else:
                        ahead = (vn > grp) | ((vn == grp) & (blk[8 * k:8 * k + 8] > n))
                    ranks[k] = ranks[k] + ahead.astype(F32)
            sel = jnp.concatenate([(r < float(N_SELECT)).astype(F32) for r in ranks]
                                  + [jnp.zeros((128 - nb, tq), F32)], axis=0)
            sel_s[h] = sel.astype(BF16)
            neg_s[h] = ((sel - 1.0) * -NEG).astype(BF16)

    kpos = k0 + lax.broadcasted_iota(jnp.int32, (kc, tq), 0)
    tok = q0 + lax.broadcasted_iota(jnp.int32, (kc, tq), 1)

    strips = [slice(st * strip, (st + 1) * strip) for st in range(N // strip)]

    row_blocks = [slice(r, r + ROW_BLOCK) for r in range(0, kc, ROW_BLOCK)]

    ones_rows = (lax.broadcasted_iota(jnp.int32, (ACC_ROWS - HEAD_DIM, kc), 0) == 0).astype(BF16)

    streams = [(h, st) for st in range(len(strips)) for h in range(KV_HEADS)]

    def q_strip(h, st):
        g, t0 = divmod(st * strip, tq)
        r0 = (h * GQA + g) * HEAD_DIM
        qT = qT_ref[0, r0:r0 + HEAD_DIM, t0:t0 + strip]
        z = jnp.zeros_like(qT)
        return jnp.concatenate([qT if k == h else z for k in range(KV_HEADS)], axis=0)

    def with_ones(vT_ref, h):
        return jnp.concatenate([vT_ref[0, h * HEAD_DIM:(h + 1) * HEAD_DIM, :], ones_rows], axis=0)

    def run_skewed(stages):
        for k in range(len(streams) + SKEW * (len(stages) - 1)):
            for lag, stage in enumerate(stages):
                if 0 <= k - SKEW * lag < len(streams):
                    stage(*streams[k - SKEW * lag])

    def selected_step(diagonal):
        vT1 = [with_ones(vselT_ref, h) for h in range(KV_HEADS)]
        alpha = {}
        if diagonal:
            keys = ksel_ref[0]
            for h in range(KV_HEADS):
                b_scr[h] = jnp.where((_dot(expand, sel_s[h]) > 0.5) & causal, 0.0, NEG)
        else:
            k2_s[:, :128] = ksel_ref[0]
            k2_s[:, 128:] = expand
            keys = k2_s

        def n_keys(st):
            return min(kc, (st * strip) % tq + strip) if diagonal else kc

        def scores(h, st):
            nk = n_keys(st)
            qp = q_strip(h, st)
            if not diagonal:
                t0 = (st * strip) % tq
                qp = jnp.concatenate([qp, neg_s[h, :, t0:t0 + strip]], axis=0)
            s_scr[h, :nk, strips[st]] = _dot(keys[:nk, :], qp)

        def probs(h, st):
            ls = strips[st]
            t0 = (st * strip) % tq
            rbs = row_blocks[:n_keys(st) // ROW_BLOCK]
            m = None
            for rb in rbs:
                t = s_scr[h, rb, ls]
                if diagonal:
                    t = t + b_scr[h, rb, t0:t0 + strip]
                    s_scr[h, rb, ls] = t
                m = t if m is None else jnp.maximum(m, t)
            m_old = m_s[h, :, ls]
            m_new = jnp.maximum(m_old, jnp.max(m, axis=0, keepdims=True))
            alpha[(h, st)] = jnp.exp2(m_old - m_new)
            m_s[h, :, ls] = m_new
            for rb in rbs:
                p_scr[h, rb, ls] = jnp.exp2(s_scr[h, rb, ls] - m_new).astype(BF16)

        def weighted_values(h, st):
            ls = strips[st]
            nk = n_keys(st)
            acc_s[h, :, ls] = alpha[(h, st)] * acc_s[h, :, ls] + _dot(vT1[h][:, :nk], p_scr[h, :nk, ls])

        run_skewed((scores, probs, weighted_values))

    def window_step():
        no_prev = jnp.where(j == 0, NEG, 0.0)
        vT1 = [with_ones(vwinT_ref, h) for h in range(KV_HEADS)]
        vT1p = [with_ones(vwinTp_ref, h) for h in range(KV_HEADS)]

        def scores(h, st):
            qp = q_strip(h, st)
            s_scr[h, :, strips[st]] = _dot(kwin_ref[0], qp)
            s2_scr[h, :, strips[st]] = _dot(kwinp_ref[0], qp)

        def probs(h, st):
            ls = strips[st]
            t0 = (st * strip) % tq
            m = None
            for rb in row_blocks:
                cur = b_scr[KV_HEADS, rb, t0:t0 + strip] > 0.5
                t = jnp.where(cur, s_scr[h, rb, ls], s2_scr[h, rb, ls] + no_prev)
                s_scr[h, rb, ls] = t
                m = t if m is None else jnp.maximum(m, t)
            m_new = jnp.max(m, axis=0, keepdims=True)
            for rb in row_blocks:
                pe = jnp.exp2(s_scr[h, rb, ls] - m_new)
                pc = pe * b_scr[KV_HEADS, rb, t0:t0 + strip]
                p_scr[h, rb, ls] = pc.astype(BF16)
                p2_scr[h, rb, ls] = (pe - pc).astype(BF16)

        def weighted_values(h, st):
            ls = strips[st]
            acc_s[KV_HEADS + h, :, ls] = _dot(vT1[h], p_scr[h, :, ls]) + _dot(vT1p[h], p2_scr[h, :, ls])

        run_skewed((scores, probs, weighted_values))

    blk_of_key = (k0 >> 6) + (lax.broadcasted_iota(jnp.int32, (kc, 128), 0) >> 6)
    expand = (lax.broadcasted_iota(jnp.int32, (kc, 128), 1) == blk_of_key).astype(BF16)
    causal = kpos <= tok

    @pl.when(c < c_last)
    def _():
        selected_step(diagonal=False)

    @pl.when(c == c_last)
    def _():
        selected_step(diagonal=True)
        b_scr[KV_HEADS] = causal.astype(F32)
        window_step()

    @pl.when(c == c_last)
    def _():
        gz = gzT_ref[0]
        outs = []
        for h in range(KV_HEADS):
            a_sel = acc_s[h]
            a_win = acc_s[KV_HEADS + h]
            o_sel = a_sel[:HEAD_DIM] / jnp.maximum(a_sel[HEAD_DIM:HEAD_DIM + 1], 1e-30)
            o_win = a_win[:HEAD_DIM] / jnp.maximum(a_win[HEAD_DIM:HEAD_DIM + 1], 1e-30)
            o_cmp = ocmp_s[h]
            for g in range(GQA):
                col = h * GQA + g
                ls = slice(g * tq, (g + 1) * tq)
                outs.append(gz[col:col + 1] * o_cmp[:, ls] + gz[B_HEADS + col:B_HEADS + col + 1] * o_sel[:, ls]
                            + gz[2 * B_HEADS + col:2 * B_HEADS + col + 1] * o_win[:, ls])
        yb_ref[0] = jnp.concatenate(outs, axis=0).T.astype(BF16)


def _p_attn_call(qT, ksel, kwin, vselT, vwinT, kc_all, vcT, gzT, *, tq, kc):
    B, _, T = qT.shape
    nb = T // BLOCK
    N = GQA * tq
    assert tq == kc == WINDOW
    js, cs = _pair_schedule(T, tq, kc)
    jt = jnp.asarray(js, jnp.int32)
    ct = jnp.asarray(cs, jnp.int32)
    prev = lambda jt, p: jnp.maximum(jt[p] - 1, 0)
    return pl.pallas_call(
        functools.partial(_p_attn_kernel, tq=tq, kc=kc, T=T, strip=256),
        out_shape=jax.ShapeDtypeStruct((B, T, B_WIDTH), BF16),
        grid_spec=pltpu.PrefetchScalarGridSpec(
            num_scalar_prefetch=2,
            grid=(B, len(js)),
            in_specs=[pl.BlockSpec((1, B_WIDTH, tq), lambda b, p, jt, ct: (b, 0, jt[p])),
                      pl.BlockSpec((1, kc, 128), lambda b, p, jt, ct: (b, ct[p], 0)),
                      pl.BlockSpec((1, 128, kc), lambda b, p, jt, ct: (b, 0, ct[p])),
                      pl.BlockSpec((1, kc, 128), lambda b, p, jt, ct: (b, jt[p], 0)),
                      pl.BlockSpec((1, kc, 128), lambda b, p, jt, ct: (b, prev(jt, p), 0)),
                      pl.BlockSpec((1, 128, kc), lambda b, p, jt, ct: (b, 0, jt[p])),
                      pl.BlockSpec((1, 128, kc), lambda b, p, jt, ct: (b, 0, prev(jt, p))),
                      pl.BlockSpec((1, nb, 128), lambda b, p, jt, ct: (b, 0, 0)),
                      pl.BlockSpec((1, 128, nb), lambda b, p, jt, ct: (b, 0, 0)),
                      pl.BlockSpec((1, 32, tq), lambda b, p, jt, ct: (b, 0, jt[p]))],
            out_specs=pl.BlockSpec((1, tq, B_WIDTH), lambda b, p, jt, ct: (b, jt[p], 0)),
            scratch_shapes=[pltpu.VMEM((KV_HEADS, 128, tq), BF16),
                            pltpu.VMEM((KV_HEADS, 128, tq), BF16),
                            pltpu.VMEM((KV_HEADS, HEAD_DIM, N), F32),
                            pltpu.VMEM((2 * KV_HEADS, 1, N), F32),
                            pltpu.VMEM((2 * KV_HEADS, ACC_ROWS, N), F32),
                            pltpu.VMEM((KV_HEADS, kc, N), F32),
                            pltpu.VMEM((KV_HEADS, kc, N), F32),
                            pltpu.VMEM((KV_HEADS, kc, N), BF16),
                            pltpu.VMEM((KV_HEADS, kc, N), BF16),
                            pltpu.VMEM((KV_HEADS + 1, kc, tq), F32),
                            pltpu.VMEM((kc, 256), BF16)]),
        compiler_params=pltpu.CompilerParams(dimension_semantics=("arbitrary", "arbitrary"),
                                             vmem_limit_bytes=VMEM_LIMIT),
        name="p_attn",
    )(jt, ct, qT, ksel, vselT, kwin, kwin, vwinT, vwinT, kc_all, vcT, gzT)


def _s_cmp_kernel(q_ref, gath, newc_ref, ocmp_ref, imp_ref, *, n_pages, past_len):
    rows = 8 * GQA
    t_row = lax.broadcasted_iota(jnp.int32, (rows, 1), 0) & 7
    pos = past_len + t_row
    lane = lax.broadcasted_iota(jnp.int32, (1, 2 * n_pages), 1)
    blk = 2 * (lane & (n_pages - 1)) + (lane >> 7)
    new_blk = past_len // BLOCK
    avail = (blk + 1) * BLOCK <= pos + 1
    avail_new = (new_blk + 1) * BLOCK <= pos + 1
    for h in range(KV_HEADS):
        qh = q_ref[0, h]
        kc = [gath[:, (h * 2 + k) * 64:(h * 2 + k + 1) * 64].astype(BF16) for k in range(2)]
        vc = [gath[:, (2 + h) * 128 + k * 64:(2 + h) * 128 + (k + 1) * 64].astype(BF16) for k in range(2)]
        kc_new = newc_ref[0, :, h * 128:h * 128 + 64].astype(BF16).astype(F32)
        vc_new = newc_ref[0, :, (2 + h) * 128:(2 + h) * 128 + 64].astype(BF16).astype(F32)
        s = jnp.where(avail, jnp.concatenate([_dot_nt(qh, kc[0]), _dot_nt(qh, kc[1])], axis=1), NEG)
        s_new = jnp.where(avail_new, jnp.sum(qh.astype(F32) * kc_new, axis=-1, keepdims=True), NEG)
        m = jnp.maximum(jnp.max(s, axis=-1, keepdims=True), s_new)
        e = jnp.where(avail, jnp.exp(s - m), 0.0)
        e_new = jnp.where(avail_new, jnp.exp(s_new - m), 0.0)
        den = jnp.maximum(jnp.sum(e, axis=-1, keepdims=True) + e_new, 1e-30)
        p = e / den
        p_new = e_new / den
        pb = p.astype(BF16)
        ocmp_ref[0, h] = (_dot(pb[:, :n_pages], vc[0]) + _dot(pb[:, n_pages:], vc[1])
                          + p_new.astype(BF16).astype(F32) * vc_new)
        imp = p[0:8] + p[8:16] + p[16:24] + p[24:32]
        cur = (past_len + lax.broadcasted_iota(jnp.int32, (8, 1), 0)) >> 6
        forced = (blk == 0) | (blk == cur) | (blk == cur - 1)
        imp_ref[0, h] = jnp.where(forced, FORCE_SCORE, jnp.where(blk > cur, -1.0, imp))


def _s_cmp_call(q_r, pool_seq, newc, *, n_pages, past_len):
    nbatch = q_r.shape[0]
    return pl.pallas_call(
        functools.partial(_s_cmp_kernel, n_pages=n_pages, past_len=past_len),
        out_shape=[jax.ShapeDtypeStruct((nbatch, KV_HEADS, 32, HEAD_DIM), F32),
                   jax.ShapeDtypeStruct((nbatch, KV_HEADS, 8, 2 * n_pages), F32)],
        grid=(nbatch,),
        in_specs=[pl.BlockSpec((1, KV_HEADS, 32, HEAD_DIM), lambda b: (b, 0, 0, 0)),
                  pl.BlockSpec((n_pages, 512), lambda b: (b, 0)),
                  pl.BlockSpec((1, 1, 512), lambda b: (b, 0, 0))],
        out_specs=[pl.BlockSpec((1, KV_HEADS, 32, HEAD_DIM), lambda b: (b, 0, 0, 0)),
                   pl.BlockSpec((1, KV_HEADS, 8, 2 * n_pages), lambda b: (b, 0, 0, 0))],
        compiler_params=pltpu.CompilerParams(dimension_semantics=("arbitrary",), vmem_limit_bytes=VMEM_LIMIT),
        name="s_cmp",
    )(q_r, pool_seq, newc)


def _s_select_kernel(imp_ref, idx_ref, *, n_pages, n_pick):
    v = imp_ref[...]
    rows = v.shape[0]
    lane = lax.broadcasted_iota(jnp.int32, (1, 2 * n_pages), 1)
    blk = (2 * (lane & (n_pages - 1)) + (lane >> 7)).astype(F32)
    out_lane = lax.broadcasted_iota(jnp.int32, (1, 128), 1)
    out = jnp.full((rows, 128), float(2 * n_pages), F32)
    for k in range(n_pick):
        m = jnp.max(v, axis=-1, keepdims=True)
        pick = jnp.min(jnp.where(v == m, blk, 1e9), axis=-1, keepdims=True)
        out = jnp.where(out_lane == k, pick, out)
        v = jnp.where(blk == pick, NEG, v)
    idx_ref[...] = out.astype(jnp.int32)


def _s_select_call(imp2d, *, n_pages, n_pick):
    rows = imp2d.shape[0]
    return pl.pallas_call(
        functools.partial(_s_select_kernel, n_pages=n_pages, n_pick=n_pick),
        out_shape=jax.ShapeDtypeStruct((rows, 128), jnp.int32),
        grid=(1,),
        in_specs=[pl.BlockSpec(imp2d.shape, lambda i: (0, 0))],
        out_specs=pl.BlockSpec((rows, 128), lambda i: (0, 0)),
        compiler_params=pltpu.CompilerParams(dimension_semantics=("arbitrary",), vmem_limit_bytes=VMEM_LIMIT),
        name="s_select",
    )(imp2d)


def _s_attn_kernel(pt_ref, idx_ref, q_ref, kvn_ref, win_ref, gz_ref, ocmp_ref, cache_ref, yb_ref, wnext_ref,
                   kvbuf, sem, *, n_tok, n_gather, past_len):
    b = pl.program_id(0)
    span = n_gather * 128
    slot = b & 1

    def tile_copies(bb, sl, t, h, jj):
        n = idx_ref[((bb * KV_HEADS + h) * n_tok + t) * N_SELECT + jj]
        page = pt_ref[bb, n >> 1]
        return [pltpu.make_async_copy(cache_ref.at[page, 2 + kv, h],
                                      kvbuf.at[sl, h, t, kv, :, pl.ds(jj * 128, 128)], sem.at[sl])
                for kv in range(2)]

    def for_all_tiles(bb, sl, fn):
        for h in range(KV_HEADS):
            for t in range(n_tok):
                for jj in range(n_gather):
                    for cp in tile_copies(bb, sl, t, h, jj):
                        fn(cp)

    @pl.when(b == 0)
    def _():
        for_all_tiles(0, 0, lambda cp: cp.start())

    @pl.when(b + 1 < pl.num_programs(0))
    def _():
        for_all_tiles(b + 1, 1 - slot, lambda cp: cp.start())

    rows = 8 * GQA
    t_row = lax.broadcasted_iota(jnp.int32, (rows, 1), 0) & 7
    gz = gz_ref[0]

    lane_n = lax.broadcasted_iota(jnp.int32, (1, 128), 1)
    tok_shift = n_tok.bit_length() - 1
    new_ok = ((lane_n >> tok_shift) == b) & ((lane_n & (n_tok - 1)) <= t_row)
    lane_w = lax.broadcasted_iota(jnp.int32, (1, WINDOW), 1)
    win_ok = lane_w > t_row

    shifted = pltpu.roll(win_ref[0], WINDOW - n_tok, 1)
    moved = pltpu.roll(kvn_ref[4 * 128:, :], (128 - n_tok) - b * n_tok, 1)
    wnext_ref[0, :, :WINDOW - 128] = shifted[:, :WINDOW - 128]
    wnext_ref[0, :, WINDOW - 128:] = jnp.where(lane_n >= 128 - n_tok, moved, shifted[:, WINDOW - 128:])

    def softmax_pv(parts):
        m = None
        for sc, mk, _ in parts:
            mm = jnp.max(jnp.where(mk, sc, NEG), axis=-1, keepdims=True)
            m = mm if m is None else jnp.maximum(m, mm)
        den = jnp.zeros((rows, 1), F32)
        o = jnp.zeros((rows, HEAD_DIM), F32)
        for sc, mk, vts in parts:
            e = jnp.where(mk, jnp.exp(jnp.where(mk, sc, NEG) - m), 0.0)
            den = den + jnp.sum(e, axis=-1, keepdims=True)
            for rmask, vt in vts:
                er = e if rmask is None else jnp.where(rmask, e, 0.0)
                o = o + _dot_nt(er.astype(BF16), vt)
        return o / jnp.maximum(den, 1e-30)

    win_out = []
    for h in range(KV_HEADS):
        qh = q_ref[0, h]
        kw = win_ref[0, h * 64:(h + 1) * 64, :].astype(BF16)
        vw = win_ref[0, 128 + h * 64:128 + (h + 1) * 64, :].astype(BF16)
        kwn = kvn_ref[4 * 128 + h * 64:4 * 128 + (h + 1) * 64, :].astype(BF16)
        vwn = kvn_ref[5 * 128 + h * 64:5 * 128 + (h + 1) * 64, :].astype(BF16)
        win_out.append(softmax_pv([(_dot(qh, kw), win_ok, [(None, vw)]),
                                   (_dot(qh, kwn), new_ok, [(None, vwn)])]))

    for_all_tiles(b, slot, lambda cp: cp.wait())

    lane_s = lax.broadcasted_iota(jnp.int32, (1, span), 1)
    for h in range(KV_HEADS):
        qh = q_ref[0, h]
        sc = jnp.zeros((rows, span), F32)
        half = jnp.zeros((rows, span), jnp.int32)
        for t in range(n_tok):
            st = _dot(qh, kvbuf[slot, h, t, 0].astype(BF16))
            hrow = jnp.zeros((1, span), jnp.int32)
            for jj in range(n_gather):
                n = idx_ref[((b * KV_HEADS + h) * n_tok + t) * N_SELECT + jj]
                hrow = jnp.where((lane_s >> 7) == jj, n & 1, hrow)
            sc = jnp.where(t_row == t, st, sc)
            half = jnp.where(t_row == t, hrow, half)
        ok = ((lane_s >> 6) & 1) == half
        ksn = kvn_ref[2 * 128 + h * 64:2 * 128 + (h + 1) * 64, :].astype(BF16)
        vsn = kvn_ref[3 * 128 + h * 64:3 * 128 + (h + 1) * 64, :].astype(BF16)
        vts = [(t_row == t, kvbuf[slot, h, t, 1].astype(BF16)) for t in range(n_tok)]
        o_sel = softmax_pv([(sc, ok, vts), (_dot(qh, ksn), new_ok, [(None, vsn)])])
        yb_ref[0, h] = (gz[h, :, 0:1] * ocmp_ref[0, h] + gz[h, :, 1:2] * o_sel + gz[h, :, 2:3] * win_out[h])


def _s_attn_call(page_table, idx_flat, q_r, kvT_new, winT, gz_r, ocmp, cache_pages, *, n_tok, past_len):
    nbatch = page_table.shape[0]
    n_gather = N_SELECT - 1
    span = n_gather * 128
    blk4 = lambda b, pt, ix: (b, 0, 0, 0)
    return pl.pallas_call(
        functools.partial(_s_attn_kernel, n_tok=n_tok, n_gather=n_gather, past_len=past_len),
        out_shape=[jax.ShapeDtypeStruct((nbatch, KV_HEADS, 32, HEAD_DIM), F32),
                   jax.ShapeDtypeStruct((nbatch, 256, WINDOW), F32)],
        grid_spec=pltpu.PrefetchScalarGridSpec(
            num_scalar_prefetch=2,
            grid=(nbatch,),
            in_specs=[pl.BlockSpec((1, KV_HEADS, 32, HEAD_DIM), blk4),
                      pl.BlockSpec(kvT_new.shape, lambda b, pt, ix: (0, 0)),
                      pl.BlockSpec((1, 256, WINDOW), lambda b, pt, ix: (b, 0, 0)),
                      pl.BlockSpec((1, KV_HEADS, 32, 128), blk4),
                      pl.BlockSpec((1, KV_HEADS, 32, HEAD_DIM), blk4),
                      pl.BlockSpec(memory_space=pl.ANY)],
            out_specs=[pl.BlockSpec((1, KV_HEADS, 32, HEAD_DIM), blk4),
                       pl.BlockSpec((1, 256, WINDOW), lambda b, pt, ix: (b, 0, 0))],
            scratch_shapes=[pltpu.VMEM((2, KV_HEADS, n_tok, 2, HEAD_DIM, span), F32),
                            pltpu.SemaphoreType.DMA((2,))]),
        compiler_params=pltpu.CompilerParams(dimension_semantics=("arbitrary",), vmem_limit_bytes=VMEM_LIMIT),
        name="s_attn",
    )(page_table, idx_flat, q_r, kvT_new, winT, gz_r, ocmp, cache_pages)


def _post_kernel(*refs, tm, seq_len, has_prev):
    it = iter(refs)
    x_ref, mod_ref, ya_ref, yb_ref = next(it), next(it), next(it), next(it)
    g1_ref, g2_ref, gf_ref = next(it), next(it), next(it)
    wm_ref, wba_ref, wbb_ref, wo_ref, wup_ref, wdn_ref, cw_ref = (next(it) for _ in range(7))
    p1_ref = next(it) if has_prev else None
    p2_ref = next(it) if has_prev else None
    y_ref, up_ref = next(it), next(it)
    carry, h2_s, act_s = next(it), next(it), next(it)

    i = pl.program_id(1)
    x = x_ref[0]
    shift1, scale1, gate1 = mod_ref[0, 0], mod_ref[0, 1], mod_ref[0, 2]
    shift2, scale2, gate2 = mod_ref[0, 3], mod_ref[0, 4], mod_ref[0, 5]

    hb = _rms_mod(x, g1_ref[...], scale1, shift1).astype(BF16)
    gates = jax.nn.sigmoid(_dot(hb, wm_ref[...]))
    mix = (gates[:, :D_MODEL] * _dot(ya_ref[0], wba_ref[...])
           + gates[:, D_MODEL:] * _dot(yb_ref[0], wbb_ref[...]))
    x1 = x + gate1 * _dot(mix.astype(BF16), wo_ref[...])
    h2_s[...] = _rms_mod(x1, g2_ref[...], scale2, shift2).astype(BF16)

    if not has_prev:
        @pl.when(i == 0)
        def _():
            carry[...] = jnp.zeros(carry.shape, F32)

    row = lax.broadcasted_iota(jnp.int32, (tm, 1), 0)
    row8 = lax.broadcasted_iota(jnp.int32, (8, 1), 0)

    def conv_cols(col0):
        cols = slice(col0, col0 + FF_CHUNK)
        up = _dot(h2_s[...], wup_ref[:, cols])
        r1 = pltpu.roll(up, 1, 0)
        r2 = pltpu.roll(up, 2, 0)
        if has_prev:
            s1 = jnp.where((row & (seq_len - 1)) == 0, p1_ref[:, cols], r1)
            s2 = jnp.where((row & (seq_len - 1)) < 2, p2_ref[:, cols], r2)
            up_ref[:, cols] = up
        else:
            prev = carry[:, cols]
            t1 = jnp.where(row8 == 0, pltpu.roll(prev, 1, 0), r1[:8])
            t2 = jnp.where(row8 < 2, pltpu.roll(prev, 2, 0), r2[:8])
            s1 = jnp.concatenate([t1, r1[8:]], axis=0)
            s2 = jnp.concatenate([t2, r2[8:]], axis=0)
            carry[:, cols] = up[tm - 8:]
            up_ref[0, :, cols] = up[tm - 8:]
        cw = cw_ref[:, cols]
        return cw[3:4] + cw[0:1] * s2 + cw[1:2] * s1 + cw[2:3] * up

    for c in range(N_FF_CHUNKS):
        a = conv_cols(c * FF_CHUNK)
        gv = conv_cols(D_FF + c * FF_CHUNK)
        act_s[:, c * FF_CHUNK:(c + 1) * FF_CHUNK] = (jax.nn.gelu(a) * gv).astype(BF16)
    x2 = x1 + gate2 * _dot(act_s[...], wdn_ref[...])
    y_ref[0] = x2 * lax.rsqrt(jnp.mean(x2 * x2, axis=-1, keepdims=True) + EPS) * gf_ref[...]


def _post_call(x, mod, ya, yb, g1, g2, gf, wm, wba, wbb, wo, wup, wdn, cw, prev=None, *, tm, seq_len):
    G, T, _ = x.shape
    R = mod.shape[2]
    nt = T // tm
    has_prev = prev is not None
    single = pl.Buffered(1)
    c2 = lambda b, i: (0, 0)
    c3 = lambda b, i: (0, 0, 0)
    in_specs = [pl.BlockSpec((1, tm, D_MODEL), lambda b, i: (b, i, 0)),
                pl.BlockSpec((1, 6, R, D_MODEL), lambda b, i: (b, 0, 0, 0)),
                pl.BlockSpec((1, tm, A_WIDTH), lambda b, i: (b, i, 0)),
                pl.BlockSpec((1, tm, B_WIDTH), lambda b, i: (b, i, 0)),
                pl.BlockSpec((1, D_MODEL), c2),
                pl.BlockSpec((1, D_MODEL), c2),
                pl.BlockSpec((1, D_MODEL), c2),
                pl.BlockSpec(wm.shape, c2, pipeline_mode=single),
                pl.BlockSpec(wba.shape, c2, pipeline_mode=single),
                pl.BlockSpec(wbb.shape, c2, pipeline_mode=single),
                pl.BlockSpec(wo.shape, c2, pipeline_mode=single),
                pl.BlockSpec(wup.shape, c2, pipeline_mode=single),
                pl.BlockSpec(wdn.shape, c2, pipeline_mode=single),
                pl.BlockSpec(cw.shape, c2, pipeline_mode=single)]
    args = [x, mod, ya, yb, g1, g2, gf, wm, wba, wbb, wo, wup, wdn, cw]
    if has_prev:
        in_specs += [pl.BlockSpec(prev[0].shape, c2), pl.BlockSpec(prev[1].shape, c2)]
        args += list(prev)
        up_shape = jax.ShapeDtypeStruct((T, F2), F32)
        up_spec = pl.BlockSpec((tm, F2), lambda b, i: (i, 0))
    else:
        up_shape = jax.ShapeDtypeStruct((G, 8, F2), F32)
        up_spec = pl.BlockSpec((1, 8, F2), lambda b, i: (b, 0, 0))
    return pl.pallas_call(
        functools.partial(_post_kernel, tm=tm, seq_len=seq_len, has_prev=has_prev),
        out_shape=[jax.ShapeDtypeStruct((G, T, D_MODEL), F32), up_shape],
        grid=(G, nt),
        in_specs=in_specs,
        out_specs=[pl.BlockSpec((1, tm, D_MODEL), lambda b, i: (b, i, 0)), up_spec],
        scratch_shapes=[pltpu.VMEM((8, F2), F32),
                        pltpu.VMEM((tm, D_MODEL), BF16),
                        pltpu.VMEM((tm, D_FF), BF16)],
        compiler_params=pltpu.CompilerParams(dimension_semantics=("arbitrary", "arbitrary"),
                                             vmem_limit_bytes=VMEM_LIMIT),
        name="post",
    )(*args)


def _block_diag2(m):
    z = jnp.zeros_like(m)
    return jnp.concatenate([jnp.concatenate([m, z], axis=-1), jnp.concatenate([z, m], axis=-1)], axis=-2)


def kernel(x_prompt, x_sample, cache_kv, state_kv_win, state_ffn_conv, page_table, c_prompt, c_sample, w_ada, b_ada, g_norm1, w_in, ln_v_g, ln_v_b, w_spatial, b_spatial, cmp_pe, cmp_w1, cmp_b1, cmp_w2, cmp_b2, w_branch_a, w_branch_b, w_out, g_norm2, w_up, w_conv, b_conv, w_down, g_final):
    B, T, _ = x_prompt.shape
    SB, ST, _ = x_sample.shape
    n_pool, page_size = cache_kv.shape[1], cache_kv.shape[2]
    n_pages = page_table.shape[1]
    past_len = n_pages * page_size
    lbuf = state_kv_win.shape[2]
    assert cache_kv.shape[0] == 1 and page_size == 128 and lbuf == WINDOW and SB * ST == 128 and ST <= 8
    assert past_len % BLOCK == 0 and n_pages == 128 and T % 512 == 0 and ST & (ST - 1) == 0 and ST >= 2

    win = w_in[0]
    wa = win[:, :2 * A_WIDTH].astype(BF16)
    o = 2 * A_WIDTH
    wq = win[:, o:o + B_WIDTH].astype(BF16)
    o += B_WIDTH
    wkv = win[:, o:o + KV_COLS]
    wkvT = wkv.T.astype(BF16)
    wkvc = wkv[:, :256].astype(BF16)
    o += KV_COLS
    wg = jnp.pad(win[:, o:o + 3 * B_HEADS], ((0, 0), (0, 128 - 3 * B_HEADS))).astype(BF16)
    wgT = jnp.pad(win[:, o:o + 3 * B_HEADS].T, ((0, 32 - 3 * B_HEADS), (0, 0))).astype(BF16)
    wqT = wq.T
    o += 3 * B_HEADS
    wm = win[:, o:].astype(BF16)
    g1 = g_norm1[0][None]
    g2 = g_norm2[0][None]
    gf = g_final[None]
    lng = ln_v_g[0][None]
    lnb = ln_v_b[0][None]

    ws = w_spatial[0]
    bs = b_spatial[0]
    wsp_p = jnp.tril(ws).astype(BF16)
    bsp_p = jnp.repeat(bs.T, 128, axis=1)
    w4 = jnp.tril(ws[:, :ST, :ST])
    seq_of = jnp.arange(128, dtype=jnp.int32) // ST
    same_seq = seq_of[:, None] == seq_of[None, :]
    wsp_s = jnp.where(same_seq, jnp.tile(w4, (1, 128 // ST, 128 // ST)), 0.0).astype(BF16)
    bsp_s = jnp.repeat(jnp.tile(bs[:, :ST].T, (128 // ST, 1)), 128, axis=1)

    w1, pe = cmp_w1[0].astype(BF16), cmp_pe[0]
    w1_rows = _block_diag2(w1)
    pe_rows = jnp.tile(pe, (1, 1, 2))
    w1_pages = _block_diag2(w1.transpose(0, 2, 1, 3)).reshape(2, HEAD_DIM // 2, 256, 128)
    pe_pages = jnp.tile(pe.transpose(0, 2, 1), (1, 1, 2)).reshape(2, HEAD_DIM // 2, 1, 256)
    pe_pages = jnp.pad(pe_pages, ((0, 0), (0, 0), (0, PE_ROWS - 1), (0, 0))).astype(BF16)
    b1t = jnp.tile(cmp_b1[0], (1, 2))[:, None, :]
    b2t = jnp.tile(cmp_b2[0], (1, 2))[:, None, :]
    w2bd = _block_diag2(cmp_w2[0].astype(BF16))

    wba = w_branch_a[0].astype(BF16)
    wbb = w_branch_b[0].astype(BF16)
    wo = w_out[0].astype(BF16)
    wup = w_up[0].astype(BF16)
    wdn = w_down[0].astype(BF16)
    cw = jnp.concatenate([w_conv[0], b_conv[0][None], jnp.zeros((4, F2), F32)], axis=0)

    mod = _mod_call(jnp.concatenate([c_prompt, c_sample], axis=0), w_ada[0], b_ada)
    mod_p = mod[:B].reshape(B, 6, 1, D_MODEL)
    mod_s = jnp.repeat(mod[B:].reshape(SB, 6, D_MODEL).transpose(1, 0, 2), ST, axis=1)[None]

    ya_p, qT_p, kv4T_p, winT_p, vselT_p, vwinT_p, kvc_p, ksel_p, kwin_p, gzT_p = _inproj_call(
        x_prompt, mod_p, g1, wa, lng, lnb, wsp_p, bsp_p, (wqT, wkvT, wgT), tm=512, prompt=True)
    xs2 = x_sample.reshape(1, SB * ST, D_MODEL)
    ya_s, q_s, kvT_s, kvc_s, gz_s, vn_s = _inproj_call(
        xs2, mod_s, g1, wa, lng, lnb, wsp_s, bsp_s, (wq, wkvT, wkvc, wg), tm=128, prompt=False)

    nb = T // BLOCK
    cmp_p = _cmp_rows_call(kvc_p.reshape(B * nb, BLOCK, 256), pe_rows,
                           w1_rows.reshape(2, BLOCK // 2, 256, 128), b1t, w2bd, b2t, R=128)
    kc_p = cmp_p[0].reshape(B, nb, 128)
    vcT_p = cmp_p[1].reshape(B, nb, 128).transpose(0, 2, 1)
    yb_p = _p_attn_call(qT_p, ksel_p, kwin_p, vselT_p, vwinT_p, kc_p, vcT_p, gzT_p, tq=512, kc=512)

    cache_pages = jnp.transpose(cache_kv[0], (0, 2, 3, 4, 1)).reshape(n_pool, 512, page_size)
    pool_seq = _cmp_gather_call(page_table.reshape(-1), cache_pages, pe_pages, w1_pages, b1t, w2bd, b2t,
                                Pt=64)
    kv_rows_s = jnp.transpose(kvT_s[0], (1, 0)).reshape(SB, ST, KV_COLS)
    newblk = jnp.pad(jnp.transpose(kv_rows_s[:, :, :256], (0, 2, 1)), ((0, 0), (0, 0), (0, page_size - ST)))
    newc = _cmp_pages_call(newblk, pe_pages, w1_pages, b1t, w2bd, b2t, Pt=SB)

    q_r = jnp.pad(q_s.reshape(SB, ST, KV_HEADS, GQA, HEAD_DIM).transpose(0, 2, 3, 1, 4),
                  ((0, 0), (0, 0), (0, 0), (0, 8 - ST), (0, 0))).reshape(SB, KV_HEADS, 8 * GQA, HEAD_DIM)
    ocmp_s, imp_s = _s_cmp_call(q_r, pool_seq, newc.reshape(SB, 1, 512), n_pages=n_pages, past_len=past_len)
    idx = _s_select_call(imp_s.reshape(SB * KV_HEADS * 8, 2 * n_pages), n_pages=n_pages, n_pick=N_SELECT - 1)
    idx_flat = idx.reshape(SB, KV_HEADS, 8, 128)[:, :, :ST, :N_SELECT].reshape(-1)
    winT = jnp.transpose(state_kv_win[0], (0, 2, 3, 4, 1)).reshape(SB, 256, lbuf)
    gz_r = jnp.pad(gz_s[0, :, :24].reshape(SB, ST, 3, KV_HEADS, GQA).transpose(0, 3, 4, 1, 2),
                   ((0, 0), (0, 0), (0, 0), (0, 8 - ST), (0, 128 - 3))).reshape(SB, KV_HEADS, 8 * GQA, 128)
    cache_tiles = cache_pages.reshape(n_pool, 4, KV_HEADS, HEAD_DIM, page_size)
    yb_r, wnextT = _s_attn_call(page_table, idx_flat, q_r, kvT_s[0], winT, gz_r, ocmp_s, cache_tiles,
                        n_tok=ST, past_len=past_len)
    yb_s = yb_r.reshape(SB, KV_HEADS, GQA, 8, HEAD_DIM)[:, :, :, :ST].transpose(0, 3, 1, 2, 4)
    yb_s = yb_s.reshape(1, SB * ST, B_WIDTH).astype(BF16)

    y_p, up_p = _post_call(x_prompt, mod_p, ya_p, yb_p, g1, g2, gf, wm, wba, wbb, wo, wup, wdn, cw,
                           tm=512, seq_len=T)
    st = state_ffn_conv[0]
    zrow = jnp.zeros((SB, 1, F2), F32)
    p1 = jnp.concatenate([st[:, 1:2], jnp.tile(zrow, (1, ST - 1, 1))], axis=1).reshape(SB * ST, F2)
    p2 = jnp.concatenate([st, jnp.tile(zrow, (1, ST - 2, 1))], axis=1).reshape(SB * ST, F2)
    y_s, up_s = _post_call(xs2, mod_s, ya_s, yb_s, g1, g2, gf, wm, wba, wbb, wo, wup, wdn, cw,
                           prev=(p1, p2),
                           tm=SB * ST, seq_len=ST)

    kv_prompt = jnp.transpose(kv4T_p.reshape(B, 4, KV_HEADS, HEAD_DIM, T), (0, 4, 1, 2, 3))[None]
    win_prompt = jnp.transpose(winT_p[:, :, T - WINDOW:].reshape(B, 2, KV_HEADS, HEAD_DIM, WINDOW),
                               (0, 4, 1, 2, 3))[None]
    kv_sample = kv_rows_s[:, :, :512].reshape(SB, ST, 4, KV_HEADS, HEAD_DIM)[None]
    win_sample = jnp.transpose(wnextT.reshape(SB, 2, KV_HEADS, HEAD_DIM, lbuf), (0, 4, 1, 2, 3))[None]
    v_chunk = vn_s.reshape(SB, ST, A_WIDTH)[None]
    conv_prompt = up_p[:, 6:8][None]
    conv_sample = up_s.reshape(SB, ST, F2)[:, ST - 2:][None]
    return (y_p, y_s.reshape(SB, ST, D_MODEL), kv_prompt, kv_sample, win_prompt, win_sample, v_chunk,
            conv_prompt, conv_sample)
```
